```python
import math
import jax, jax.numpy as jnp
from jax import lax
import numpy as np

D_MODEL = 1024
BATCH = 4
SEQ = 4096
DEPTH = 1
DEC_BATCH = 32
DEC_SEQ = 8
PAST_LEN = 8192
PAGE_SIZE = 128

N_DA_HEADS = 4
DA_DK = 64
DA_DV = 128
DA_WIDTH = N_DA_HEADS * DA_DV
QK_WIDTH = N_DA_HEADS * 2 * DA_DK
N_SG_GROUPS = 4
SG_CH = (D_MODEL - DA_WIDTH) // N_SG_GROUPS
SG_WIDTH = N_SG_GROUPS * SG_CH
CHUNK = 128
IN_WIDTH = 2 * QK_WIDTH + DA_WIDTH + 2 * SG_WIDTH
MIX_WIDTH = DA_WIDTH + SG_WIDTH
N_BUCKETS = 32
MAX_DISTANCE = 128
Q_BLOCK = 128
N_EXPERT_GROUPS = 4
EXPERTS_PER_GROUP = 4
N_EXPERTS = N_EXPERT_GROUPS * EXPERTS_PER_GROUP
TOP_K_IN_GROUP = 2
D_EXPERT = 512
N_MOD = 6
EPS = 1e-6

kernel_name = "hymba_diffattn_gmlp_hmoe_step"


def rmsnorm(x, g=None):
    xf = x.astype(jnp.float32)
    y = (xf * lax.rsqrt(jnp.mean(xf * xf, axis=-1, keepdims=True) + EPS)).astype(x.dtype)
    return y if g is None else y * g


def layernorm(x, g, b):
    xf = x.astype(jnp.float32)
    mu = jnp.mean(xf, axis=-1, keepdims=True)
    var = jnp.mean(jnp.square(xf - mu), axis=-1, keepdims=True)
    return ((xf - mu) * lax.rsqrt(var + EPS)).astype(x.dtype) * g + b


def t5_bucket(n):
    n = jnp.maximum(n, 0)
    max_exact = N_BUCKETS // 2
    nf = jnp.maximum(n, 1).astype(jnp.float32)
    large = max_exact + (jnp.log(nf / max_exact) / math.log(MAX_DISTANCE / max_exact)
                         * (N_BUCKETS - max_exact)).astype(jnp.int32)
    large = jnp.minimum(large, N_BUCKETS - 1)
    return jnp.where(n < max_exact, n, large)


def diff_attn(q, k, v, q_pos, k_pos, lam, rel_bias):
    s = jnp.einsum('bqhmd,bkhmd->bhmqk', q, k,
                   preferred_element_type=jnp.float32) * (DA_DK ** -0.5)
    dist = q_pos[:, None] - k_pos[None, :]
    bias = jnp.transpose(rel_bias[t5_bucket(dist)], (2, 0, 1))
    s = s + bias[None, :, None].astype(jnp.float32)
    s = jnp.where((dist >= 0)[None, None, None], s, -jnp.inf)
    p = jax.nn.softmax(s, axis=-1)
    a = p[:, :, 0] - lam * p[:, :, 1]
    return jnp.einsum('bhqk,bkhd->bqhd', a.astype(v.dtype), v)


def attend_prompt(q, k, v, lam, rel_bias):
    B, T = q.shape[0], q.shape[1]
    nb = T // Q_BLOCK
    qb = q.reshape(B, nb, Q_BLOCK, N_DA_HEADS, 2, DA_DK).swapaxes(0, 1)
    k_pos = jnp.arange(T)

    def one(args):
        qi, i = args
        return diff_attn(qi, k, v, i * Q_BLOCK + jnp.arange(Q_BLOCK), k_pos, lam, rel_bias)

    o = lax.map(one, (qb, jnp.arange(nb)))
    return o.swapaxes(0, 1).reshape(B, T, N_DA_HEADS, DA_DV)


def attend_sample(q, k, v, past_k, past_v, lam, rel_bias):
    P = past_k.shape[1]
    Tq = q.shape[1]
    k_all = jnp.concatenate([past_k.astype(k.dtype), k], axis=1)
    v_all = jnp.concatenate([past_v.astype(v.dtype), v], axis=1)
    return diff_attn(q, k_all, v_all, P + jnp.arange(Tq), jnp.arange(P + Tq), lam, rel_bias)


def spatial_gate(vs, w_s, b_s):
    B, T, G, C = vs.shape
    nc = -(-T // CHUNK)
    vp = jnp.pad(vs, ((0, 0), (0, nc * CHUNK - T), (0, 0), (0, 0))).reshape(B, nc, CHUNK, G, C)
    s = jnp.einsum('gij,bnjgc->bnigc', jnp.tril(w_s), vp) + b_s.T[None, None, :, :, None]
    return s.reshape(B, nc * CHUNK, G, C)[:, :T]


def mix_tokens(h, attend, p, lam, lam_init):
    B, T, _ = h.shape
    z = h @ p['w_in']
    q, k, v, u, vs = jnp.split(z, [QK_WIDTH, 2 * QK_WIDTH, 2 * QK_WIDTH + DA_WIDTH,
                                   2 * QK_WIDTH + DA_WIDTH + SG_WIDTH], axis=-1)
    q = q.reshape(B, T, N_DA_HEADS, 2, DA_DK)
    k = k.reshape(B, T, N_DA_HEADS, 2, DA_DK)
    v = v.reshape(B, T, N_DA_HEADS, DA_DV)
    o = attend(q, k, v, lam)
    o = rmsnorm(o, p['g_subln']) * (1.0 - lam_init)
    vs = layernorm(vs.reshape(B, T, N_SG_GROUPS, SG_CH), p['g_sg_ln'], p['b_sg_ln'])
    sg = u.reshape(B, T, N_SG_GROUPS, SG_CH) * spatial_gate(vs, p['w_s'], p['b_s'])
    merged = jnp.concatenate([o.reshape(B, T, DA_WIDTH), sg.reshape(B, T, SG_WIDTH)], axis=-1)
    return merged @ p['w_o'], k.reshape(B, T, N_DA_HEADS, 2 * DA_DK), v, vs


def hier_moe(h, p):
    B, T, D = h.shape
    hf = h.reshape(-1, D)
    lg = (hf @ p['w_rg'] + p['b_rg']).astype(jnp.float32)
    pg = jax.nn.softmax(lg, axis=-1)
    gi = jnp.argmax(lg, axis=-1)
    gp = jnp.take_along_axis(pg, gi[:, None], axis=1)[:, 0]
    le = (hf @ p['w_re'] + p['b_re']).astype(jnp.float32).reshape(-1, N_EXPERT_GROUPS, EXPERTS_PER_GROUP)
    sel = jnp.take_along_axis(le, gi[:, None, None], axis=1)[:, 0]
    tv, ti = lax.top_k(sel, TOP_K_IN_GROUP)
    tw = jax.nn.softmax(tv, axis=-1) * gp[:, None]
    eid = gi[:, None] * EXPERTS_PER_GROUP + ti
    comb = jnp.sum(jax.nn.one_hot(eid, N_EXPERTS, dtype=jnp.float32) * tw[..., None], axis=1)
    comb = comb.astype(h.dtype)
    y = jnp.zeros_like(hf)
    for e in range(N_EXPERTS):
        he = jax.nn.silu(hf @ p['w_gate'][e]) * (hf @ p['w_up'][e])
        y = y + comb[:, e:e + 1] * (he @ p['w_down'][e])
    return y.reshape(B, T, D)


def decoder_layer(x, c, attend, p, lam, lam_init):
    mod = (jax.nn.silu(c) @ p['w_ada'] + p['b_ada'])[:, None, :]
    sh1, sc1, g1, sh2, sc2, g2 = jnp.split(mod, N_MOD, axis=-1)
    h = rmsnorm(x) * (1 + sc1) + sh1
    m, k, v, vs = mix_tokens(h, attend, p, lam, lam_init)
    x = x + g1 * m
    h = rmsnorm(x) * (1 + sc2) + sh2
    x = x + g2 * hier_moe(h, p)
    return x, k, v, vs


def setup_inputs(seed: int = 0) -> dict:
    key = jax.random.key(seed)
    ks = jax.random.split(key, 32)
    n_pages = PAST_LEN // PAGE_SIZE
    n_pool = (DEC_BATCH * n_pages * 5) // 4
    f32 = jnp.float32

    def nrm(k, shape, s):
        return jax.random.normal(k, shape, f32) * s

    page_table = jax.random.permutation(ks[6], n_pool)[:DEC_BATCH * n_pages]
    page_table = page_table.reshape(DEC_BATCH, n_pages).astype(jnp.int32)
    return {
        'x_prompt': nrm(ks[0], (BATCH, SEQ, D_MODEL), 1.0),
        'x_sample': nrm(ks[1], (DEC_BATCH, DEC_SEQ, D_MODEL), 1.0),
        'c_prompt': nrm(ks[2], (BATCH, D_MODEL), 1.0),
        'c_sample': nrm(ks[3], (DEC_BATCH, D_MODEL), 1.0),
        'cache_k': nrm(ks[4], (n_pool, DEPTH, PAGE_SIZE, N_DA_HEADS, 2 * DA_DK), 1.0),
        'cache_v': nrm(ks[5], (n_pool, DEPTH, PAGE_SIZE, N_DA_HEADS, DA_DV), 1.0),
        'page_table': page_table,
        'w_ada': nrm(ks[7], (DEPTH, D_MODEL, N_MOD * D_MODEL), 0.5 * D_MODEL ** -0.5),
        'b_ada': nrm(ks[8], (DEPTH, N_MOD * D_MODEL), 0.01),
        'w_in': nrm(ks[9], (DEPTH, D_MODEL, IN_WIDTH), D_MODEL ** -0.5),
        'w_o': nrm(ks[10], (DEPTH, MIX_WIDTH, D_MODEL), MIX_WIDTH ** -0.5),
        'lam_q1': nrm(ks[11], (DEPTH, DA_DK), 0.1),
        'lam_k1': nrm(ks[12], (DEPTH, DA_DK), 0.1),
        'lam_q2': nrm(ks[13], (DEPTH, DA_DK), 0.1),
        'lam_k2': nrm(ks[14], (DEPTH, DA_DK), 0.1),
        'g_subln': 1.0 + nrm(ks[15], (DEPTH, DA_DV), 0.02),
        'rel_bias': nrm(ks[16], (N_BUCKETS, N_DA_HEADS), 0.5),
        'g_sg_ln': 1.0 + nrm(ks[17], (DEPTH, N_SG_GROUPS, SG_CH), 0.02),
        'b_sg_ln': nrm(ks[18], (DEPTH, N_SG_GROUPS, SG_CH), 0.02),
        'w_s': nrm(ks[19], (DEPTH, N_SG_GROUPS, CHUNK, CHUNK), CHUNK ** -0.5),
        'b_s': 1.0 + nrm(ks[20], (DEPTH, N_SG_GROUPS, CHUNK), 0.1),
        'w_rg': nrm(ks[21], (DEPTH, D_MODEL, N_EXPERT_GROUPS), D_MODEL ** -0.5),
        'b_rg': nrm(ks[22], (DEPTH, N_EXPERT_GROUPS), 0.01),
        'w_re': nrm(ks[23], (DEPTH, D_MODEL, N_EXPERTS), D_MODEL ** -0.5),
        'b_re': nrm(ks[24], (DEPTH, N_EXPERTS), 0.01),
        'w_gate': nrm(ks[25], (DEPTH, N_EXPERTS, D_MODEL, D_EXPERT), D_MODEL ** -0.5),
        'w_up': nrm(ks[26], (DEPTH, N_EXPERTS, D_MODEL, D_EXPERT), D_MODEL ** -0.5),
        'w_down': nrm(ks[27], (DEPTH, N_EXPERTS, D_EXPERT, D_MODEL), D_EXPERT ** -0.5),
        'g_final': 1.0 + nrm(ks[28], (D_MODEL,), 0.02),
    }


def reference(x_prompt, x_sample, c_prompt, c_sample, cache_k, cache_v, page_table,
              w_ada, b_ada, w_in, w_o, lam_q1, lam_k1, lam_q2, lam_k2, g_subln, rel_bias,
              g_sg_ln, b_sg_ln, w_s, b_s, w_rg, b_rg, w_re, b_re, w_gate, w_up, w_down, g_final):
    dec_b, n_pages = page_table.shape
    past = n_pages * PAGE_SIZE
    xp, xs = x_prompt, x_sample
    kp_l, vp_l, ks_l, vs_l, sgv_l = [], [], [], [], []
    for l in range(DEPTH):
        p = {'w_ada': w_ada[l], 'b_ada': b_ada[l], 'w_in': w_in[l], 'w_o': w_o[l],
             'g_subln': g_subln[l], 'g_sg_ln': g_sg_ln[l], 'b_sg_ln': b_sg_ln[l],
             'w_s': w_s[l], 'b_s': b_s[l], 'w_rg': w_rg[l], 'b_rg': b_rg[l],
             'w_re': w_re[l], 'b_re': b_re[l], 'w_gate': w_gate[l], 'w_up': w_up[l],
             'w_down': w_down[l]}
        lam_init = 0.8 - 0.6 * math.exp(-0.3 * l)
        lam = (jnp.exp(jnp.sum(lam_q1[l].astype(jnp.float32) * lam_k1[l].astype(jnp.float32)))
               - jnp.exp(jnp.sum(lam_q2[l].astype(jnp.float32) * lam_k2[l].astype(jnp.float32)))
               + lam_init)
        past_k = cache_k[page_table, l].reshape(dec_b, past, N_DA_HEADS, 2, DA_DK)
        past_v = cache_v[page_table, l].reshape(dec_b, past, N_DA_HEADS, DA_DV)

        def att_p(q, k, v, lm):
            return attend_prompt(q, k, v, lm, rel_bias)

        def att_s(q, k, v, lm, pk=past_k, pv=past_v):
            return attend_sample(q, k, v, pk, pv, lm, rel_bias)

        xp, kp, vp, _ = decoder_layer(xp, c_prompt, att_p, p, lam, lam_init)
        xs, ksn, vsn, sgv = decoder_layer(xs, c_sample, att_s, p, lam, lam_init)
        kp_l.append(kp)
        vp_l.append(vp)
        ks_l.append(ksn)
        vs_l.append(vsn)
        sgv_l.append(sgv)
    y_prompt = rmsnorm(xp, g_final)
    y_sample = rmsnorm(xs, g_final)
    new_k_prompt = jnp.stack(kp_l, axis=1)
    new_v_prompt = jnp.stack(vp_l, axis=1)
    new_k_sample = jnp.stack(ks_l, axis=1)
    new_v_sample = jnp.stack(vs_l, axis=1)
    new_sgv_sample = jnp.stack(sgv_l, axis=1)
    return (y_prompt, y_sample, new_k_prompt, new_v_prompt, new_k_sample, new_v_sample, new_sgv_sample)
```

```python
import functools
import math

import numpy as np
import jax
import jax.numpy as jnp
from jax import lax
from jax.experimental import pallas as pl
from jax.experimental.pallas import tpu as pltpu

F32 = jnp.float32
BF16 = jnp.bfloat16

D_MODEL = 1024
N_HEADS = 4
DK = 64
HEAD_W = 128
QK_W = N_HEADS * HEAD_W
N_GROUPS_SG = 4
SG_CH = 128
IN_W = 2560
CHUNK = 128
PAGE = 128
N_BUCKETS = 32
MAX_DISTANCE = 128
N_EG = 4
EPG = 4
N_PAIRS = 6
N_CLASSES = N_EG * N_PAIRS
D_EXPERT = 512
EPS = 1e-6
LAM_INIT = 0.8 - 0.6 * math.exp(-0.3 * 0)
NEG = -1e30
LANES = 128
SUBLANES = 8

TM_TOK = 512
T_ATT = 512
PAGES_PER_STEP = 16
TM_E = 256
VMEM_LIMIT = 56 * 1024 * 1024


def _cparams(sem):
    return pltpu.CompilerParams(dimension_semantics=sem, vmem_limit_bytes=VMEM_LIMIT)


def _ada_body(c_ref, w_ref, b_ref, o_ref):
    c = c_ref[...]
    a = (c * jax.nn.sigmoid(c)).astype(BF16)
    o_ref[...] = jnp.dot(a, w_ref[...].astype(BF16), preferred_element_type=F32) + b_ref[...]


def _ada(c_all, w_ada, b_ada):
    m = c_all.shape[0]
    n = w_ada.shape[1]
    tn = 1536
    return pl.pallas_call(
        _ada_body,
        grid=(n // tn,),
        in_specs=[pl.BlockSpec((m, D_MODEL), lambda j: (0, 0)),
                  pl.BlockSpec((D_MODEL, tn), lambda j: (0, j)),
                  pl.BlockSpec((1, tn), lambda j: (0, j))],
        out_specs=pl.BlockSpec((m, tn), lambda j: (0, j)),
        out_shape=jax.ShapeDtypeStruct((m, n), F32),
        compiler_params=_cparams(("arbitrary",)),
        name="adaln",
    )(c_all, w_ada, b_ada.reshape(1, n))


def _stage_a_body(x_ref, mod_ref, w_in_ref, w_vt_ref, ws_ref, bs_ref, gln_ref, bln_ref,
                  *out_refs, chunk, sample):
    if sample:
        q_ref, kf_ref, vf_ref, sg_ref, vsn_ref = out_refs
    else:
        q_ref, kf_ref, kb_ref, vf_ref, vt_ref, sg_ref = out_refs
    x = x_ref[...]
    tm = x.shape[0]
    mod = mod_ref[0]
    sh1 = mod[:, 0:D_MODEL]
    sc1 = mod[:, D_MODEL:2 * D_MODEL]
    ms = jnp.mean(x * x, axis=-1, keepdims=True)
    h = (x * lax.rsqrt(ms + EPS)) * (1.0 + sc1) + sh1
    hb = h.astype(BF16)
    z = jnp.dot(hb, w_in_ref[...], preferred_element_type=F32)
    q = z[:, 0:QK_W] * (DK ** -0.5)
    k = z[:, QK_W:2 * QK_W]
    v = z[:, 2 * QK_W:3 * QK_W]
    kf_ref[...] = k
    vf_ref[...] = v
    if sample:
        q_ref[...] = q
    else:
        q_ref[...] = q.astype(BF16)
        kb_ref[...] = k.astype(BF16)
        vt = lax.dot_general(w_vt_ref[...], hb, (((1,), (1,)), ((), ())),
                             preferred_element_type=F32)
        vt_ref[0] = vt.astype(BF16)
    u = z[:, 3 * QK_W:4 * QK_W]
    vs = z[:, 4 * QK_W:5 * QK_W]
    for g in range(N_GROUPS_SG):
        lo, hi = g * SG_CH, (g + 1) * SG_CH
        vg = vs[:, lo:hi]
        mu = jnp.mean(vg, axis=-1, keepdims=True)
        dv = vg - mu
        var = jnp.mean(dv * dv, axis=-1, keepdims=True)
        vn = (dv * lax.rsqrt(var + EPS)) * gln_ref[g:g + 1, :] + bln_ref[g:g + 1, :]
        if sample:
            vsn_ref[:, lo:hi] = vn
        vnb = vn.astype(BF16)
        for c in range(tm // chunk):
            r0, r1 = c * chunk, (c + 1) * chunk
            s = jnp.dot(ws_ref[g], vnb[r0:r1], preferred_element_type=F32) + bs_ref[g]
            sg_ref[r0:r1, lo:hi] = (u[r0:r1, lo:hi] * s).astype(BF16)


def _stage_a(x, mod, w_in_b, w_vt_b, ws, bs, gln, bln, *, tm, chunk, sample):
    n = x.shape[0]
    nt = n // tm
    mrows = mod.shape[1]
    row = lambda w, dt: jax.ShapeDtypeStruct((n, w), dt)
    blk = lambda w: pl.BlockSpec((tm, w), lambda i: (i, 0))
    if sample:
        out_shape = [row(QK_W, F32), row(QK_W, F32), row(QK_W, F32), row(QK_W, BF16), row(QK_W, F32)]
        out_specs = [blk(QK_W)] * 5
        mod_map = lambda i: (0, 0, 0)
    else:
        out_shape = [row(QK_W, BF16), row(QK_W, F32), row(QK_W, BF16), row(QK_W, F32),
                     jax.ShapeDtypeStruct((nt, QK_W, tm), BF16), row(QK_W, BF16)]
        out_specs = [blk(QK_W), blk(QK_W), blk(QK_W), blk(QK_W),
                     pl.BlockSpec((1, QK_W, tm), lambda i: (i, 0, 0)), blk(QK_W)]
        tiles_per_batch = 4096 // tm
        mod_map = lambda i: (i // tiles_per_batch, 0, 0)
    full = lambda a: pl.BlockSpec(a.shape, lambda i: (0,) * a.ndim)
    return pl.pallas_call(
        functools.partial(_stage_a_body, chunk=chunk, sample=sample),
        grid=(nt,),
        in_specs=[blk(D_MODEL),
                  pl.BlockSpec((1, mrows, 6 * D_MODEL), mod_map),
                  full(w_in_b), full(w_vt_b), full(ws), full(bs), full(gln), full(bln)],
        out_specs=out_specs,
        out_shape=out_shape,
        compiler_params=_cparams(("arbitrary",)),
        name="stage_a_sample" if sample else "stage_a_prompt",
    )(x, mod, w_in_b, w_vt_b, ws, bs, gln, bln)


def _attn_body(lam_ref, q_ref, k_ref, vt_ref, bias_ref, g_ref, o_ref, m_ref, l_ref, acc_ref, *, t):
    qi = pl.program_id(2)
    q = q_ref[...]
    lane = lax.broadcasted_iota(jnp.int32, q.shape, 1)
    zero = jnp.zeros_like(q)
    qm = (jnp.where(lane < DK, q, zero), jnp.where(lane >= DK, q, zero))
    m_ref[...] = jnp.full(m_ref.shape, NEG, F32)
    l_ref[...] = jnp.zeros(l_ref.shape, F32)
    acc_ref[...] = jnp.zeros(acc_ref.shape, F32)

    def tile(j, bias):
        k = k_ref[pl.ds(pl.multiple_of(j * t, t), t), :]
        vt = vt_ref[j]
        for mi in range(2):
            s = lax.dot_general(k, qm[mi], (((1,), (1,)), ((), ())),
                                preferred_element_type=F32)
            if bias is not None:
                s = s + bias
            m_old = m_ref[mi]
            m_new = jnp.maximum(m_old, jnp.max(s, axis=0, keepdims=True))
            alpha = jnp.exp(m_old - m_new)
            p = jnp.exp(s - m_new)
            l_ref[mi] = alpha * l_ref[mi] + jnp.sum(p, axis=0, keepdims=True)
            acc_ref[mi] = alpha * acc_ref[mi] + jnp.dot(vt, p.astype(BF16),
                                                        preferred_element_type=F32)
            m_ref[mi] = m_new

    def plain(j, carry):
        tile(j, None)
        return carry

    lax.fori_loop(0, jnp.maximum(qi - 1, 0), plain, 0)

    @pl.when(qi >= 1)
    def _():
        tile(qi - 1, bias_ref[0, 0])

    tile(qi, bias_ref[0, 1])

    lam = lam_ref[0]
    o1 = acc_ref[0] * (1.0 / l_ref[0])
    o2 = acc_ref[1] * (1.0 / l_ref[1])
    o = o1 - lam * o2
    ms = jnp.mean(o * o, axis=0, keepdims=True)
    on = (o * lax.rsqrt(ms + EPS)) * g_ref[...] * (1.0 - LAM_INIT)
    o_ref[...] = on.T.astype(BF16)


def _attn_prompt(lam, q, kb, vt, bias_t, g_col, *, batch, seq, t):
    nq = seq // t
    n = batch * seq
    return pl.pallas_call(
        functools.partial(_attn_body, t=t),
        grid=(batch, N_HEADS, nq),
        in_specs=[pl.BlockSpec(memory_space=pltpu.SMEM),
                  pl.BlockSpec((t, HEAD_W), lambda b, h, i: (b * nq + i, h)),
                  pl.BlockSpec((seq, HEAD_W), lambda b, h, i: (b, h)),
                  pl.BlockSpec((nq, HEAD_W, t), lambda b, h, i: (b, h, 0)),
                  pl.BlockSpec((1, 2, t, t), lambda b, h, i: (h, 0, 0, 0)),
                  pl.BlockSpec((HEAD_W, 1), lambda b, h, i: (0, 0))],
        out_specs=pl.BlockSpec((t, HEAD_W), lambda b, h, i: (b * nq + i, h)),
        out_shape=jax.ShapeDtypeStruct((n, QK_W), BF16),
        scratch_shapes=[pltpu.VMEM((2, 1, t), F32), pltpu.VMEM((2, 1, t), F32),
                        pltpu.VMEM((2, HEAD_W, t), F32)],
        compiler_params=_cparams(("arbitrary", "arbitrary", "arbitrary")),
        name="attn_prompt",
    )(lam, q, kb, vt, bias_t, g_col)


def _sattn_body(pt_ref, lam_ref, q_ref, knew_ref, vnew_ref, bl_ref, bn_ref, g_ref, ck_ref, cv_ref,
                o_ref, kbuf, vbuf, sem, m_ref, l_ref, acc_ref, *, pages, n_steps, total):
    b = pl.program_id(0)
    s = pl.program_id(1)
    step = b * n_steps + s
    slot = step % 2

    def page_copies(step_idx, sl):
        base = step_idx * pages
        out = []
        for i in range(pages):
            page = pt_ref[base + i]
            dst = pl.ds(i * PAGE, PAGE)
            out.append(pltpu.make_async_copy(ck_ref.at[page], kbuf.at[sl, dst], sem.at[sl, 0]))
            out.append(pltpu.make_async_copy(cv_ref.at[page], vbuf.at[sl, dst], sem.at[sl, 1]))
        return out

    @pl.when(step == 0)
    def _():
        for c in page_copies(0, 0):
            c.start()

    @pl.when(step + 1 < total)
    def _():
        for c in page_copies(step + 1, 1 - slot):
            c.start()

    for c in page_copies(step, slot):
        c.wait()

    @pl.when(s == 0)
    def _():
        m_ref[...] = jnp.full(m_ref.shape, NEG, F32)
        l_ref[...] = jnp.zeros(l_ref.shape, F32)
        acc_ref[...] = jnp.zeros(acc_ref.shape, F32)

    q = q_ref[...]
    qt = jnp.concatenate([q] * (2 * N_HEADS), axis=0)
    row = lax.broadcasted_iota(jnp.int32, qt.shape, 0)
    col = lax.broadcasted_iota(jnp.int32, qt.shape, 1)
    qbd = jnp.where((col // DK) == (row // SUBLANES), qt, 0.0).astype(BF16)

    def update(kb, vb, bias):
        sc = lax.dot_general(qbd, kb, (((1,), (1,)), ((), ())),
                             preferred_element_type=F32) + bias
        m_old = m_ref[...]
        m_new = jnp.maximum(m_old, jnp.max(sc, axis=1, keepdims=True))
        alpha = jnp.exp(m_old - m_new)
        p = jnp.exp(sc - m_new)
        l_ref[...] = alpha * l_ref[...] + jnp.sum(p, axis=1, keepdims=True)
        acc_ref[...] = alpha * acc_ref[...] + jnp.dot(p.astype(BF16), vb,
                                                      preferred_element_type=F32)
        m_ref[...] = m_new

    is_last = s == n_steps - 1
    update(kbuf[slot].astype(BF16), vbuf[slot].astype(BF16),
           bl_ref[...] * is_last.astype(F32))

    @pl.when(is_last)
    def _():
        update(knew_ref[0], vnew_ref[0], bn_ref[...])
        lam = lam_ref[0]
        o_all = acc_ref[...] * (1.0 / l_ref[...])
        for h in range(N_HEADS):
            lo, hi = h * HEAD_W, (h + 1) * HEAD_W
            r = h * 2 * SUBLANES
            o = o_all[r:r + SUBLANES, lo:hi] - lam * o_all[r + SUBLANES:r + 2 * SUBLANES, lo:hi]
            ms = jnp.mean(o * o, axis=-1, keepdims=True)
            o_ref[:, lo:hi] = (o * lax.rsqrt(ms + EPS)) * g_ref[...] * (1.0 - LAM_INIT)


def _attn_sample(page_table_flat, lam, q_s, knew, vnew, bias_last, bias_new, g_row, cache_k, cache_v,
                 *, dec_b, n_pages):
    pages = PAGES_PER_STEP
    n_steps = n_pages // pages
    total = dec_b * n_steps
    nq = q_s.shape[0] // dec_b
    n_rows = 2 * N_HEADS * nq
    grid_spec = pltpu.PrefetchScalarGridSpec(
        num_scalar_prefetch=1,
        grid=(dec_b, n_steps),
        in_specs=[pl.BlockSpec(memory_space=pltpu.SMEM),
                  pl.BlockSpec((nq, QK_W), lambda b, s, pt: (b, 0)),
                  pl.BlockSpec((1, PAGE, QK_W), lambda b, s, pt: (b, 0, 0)),
                  pl.BlockSpec((1, PAGE, QK_W), lambda b, s, pt: (b, 0, 0)),
                  pl.BlockSpec(bias_last.shape, lambda b, s, pt: (0, 0)),
                  pl.BlockSpec(bias_new.shape, lambda b, s, pt: (0, 0)),
                  pl.BlockSpec((1, HEAD_W), lambda b, s, pt: (0, 0)),
                  pl.BlockSpec(memory_space=pl.ANY),
                  pl.BlockSpec(memory_space=pl.ANY)],
        out_specs=pl.BlockSpec((nq, QK_W), lambda b, s, pt: (b, 0)),
        scratch_shapes=[pltpu.VMEM((2, pages * PAGE, QK_W), F32),
                        pltpu.VMEM((2, pages * PAGE, QK_W), F32),
                        pltpu.SemaphoreType.DMA((2, 2)),
                        pltpu.VMEM((n_rows, 1), F32), pltpu.VMEM((n_rows, 1), F32),
                        pltpu.VMEM((n_rows, QK_W), F32)])
    return pl.pallas_call(
        functools.partial(_sattn_body, pages=pages, n_steps=n_steps, total=total),
        grid_spec=grid_spec,
        out_shape=jax.ShapeDtypeStruct(q_s.shape, F32),
        compiler_params=_cparams(("arbitrary", "arbitrary")),
        name="attn_sample",
    )(page_table_flat, lam, q_s, knew, vnew, bias_last, bias_new, g_row, cache_k, cache_v)


def _stage_c_body(o_ref, sg_ref, x_ref, mod_ref, wo_ref, wr_ref, br_ref, x1_ref, h2_ref, cls_ref, w_ref):
    x = x_ref[...]
    tm = x.shape[0]
    mod = mod_ref[0]
    g1 = mod[:, 2 * D_MODEL:3 * D_MODEL]
    sh2 = mod[:, 3 * D_MODEL:4 * D_MODEL]
    sc2 = mod[:, 4 * D_MODEL:5 * D_MODEL]
    mix = (jnp.dot(o_ref[...].astype(BF16), wo_ref[0:QK_W, :], preferred_element_type=F32)
           + jnp.dot(sg_ref[...], wo_ref[QK_W:2 * QK_W, :], preferred_element_type=F32))
    x1 = x + g1 * mix
    x1_ref[...] = x1
    ms = jnp.mean(x1 * x1, axis=-1, keepdims=True)
    h2 = (x1 * lax.rsqrt(ms + EPS)) * (1.0 + sc2) + sh2
    for c in range(D_MODEL // LANES):
        h2_ref[pl.ds(c, tm, stride=SUBLANES), :] = h2[:, c * LANES:(c + 1) * LANES]
    lg = lax.dot_general(wr_ref[...], h2.astype(BF16), (((1,), (1,)), ((), ())),
                         preferred_element_type=F32) + br_ref[...]
    gl = [lg[i:i + 1, :] for i in range(N_EG)]
    el = [lg[N_EG + i:N_EG + i + 1, :] for i in range(N_EG * EPG)]
    gmax = jnp.maximum(jnp.maximum(gl[0], gl[1]), jnp.maximum(gl[2], gl[3]))
    gi = jnp.where(gl[0] == gmax, 0, jnp.where(gl[1] == gmax, 1, jnp.where(gl[2] == gmax, 2, 3)))
    gsum = (jnp.exp(gl[0] - gmax) + jnp.exp(gl[1] - gmax)
            + jnp.exp(gl[2] - gmax) + jnp.exp(gl[3] - gmax))
    gp = 1.0 / gsum
    sel = [jnp.where(gi == 0, el[j], jnp.where(gi == 1, el[EPG + j],
                                               jnp.where(gi == 2, el[2 * EPG + j], el[3 * EPG + j])))
           for j in range(EPG)]
    v0 = jnp.maximum(jnp.maximum(sel[0], sel[1]), jnp.maximum(sel[2], sel[3]))
    i0 = jnp.where(sel[0] == v0, 0, jnp.where(sel[1] == v0, 1, jnp.where(sel[2] == v0, 2, 3)))
    rest = [jnp.where(i0 == j, -3e38, sel[j]) for j in range(EPG)]
    v1 = jnp.maximum(jnp.maximum(rest[0], rest[1]), jnp.maximum(rest[2], rest[3]))
    i1 = jnp.where(rest[0] == v1, 0, jnp.where(rest[1] == v1, 1, jnp.where(rest[2] == v1, 2, 3)))
    e1 = jnp.exp(v1 - v0)
    den = 1.0 / (1.0 + e1)
    tw0 = den * gp
    tw1 = e1 * den * gp
    first_low = i0 < i1
    lo = jnp.where(first_low, i0, i1)
    hi = jnp.where(first_low, i1, i0)
    w_lo = jnp.where(first_low, tw0, tw1)
    w_hi = jnp.where(first_low, tw1, tw0)
    pair = jnp.where(lo == 0, 0, jnp.where(lo == 1, 3, 5)) + hi - lo - 1
    cls = gi * N_PAIRS + pair
    cls_ref[...] = jnp.broadcast_to(cls, cls_ref.shape).astype(jnp.int32)
    w_ref[...] = jnp.concatenate([w_lo, w_hi, jnp.zeros((SUBLANES - 2, tm), F32)], axis=0)


def _stage_c(o, sg, x, mod, wo_b, wr_t, br, *, tm, sample):
    n = x.shape[0]
    nt = n // tm
    mrows = mod.shape[1]
    blk = lambda w: pl.BlockSpec((tm, w), lambda i: (i, 0))
    full = lambda a: pl.BlockSpec(a.shape, lambda i: (0,) * a.ndim)
    if sample:
        mod_map = lambda i: (0, 0, 0)
    else:
        tiles_per_batch = 4096 // tm
        mod_map = lambda i: (i // tiles_per_batch, 0, 0)
    return pl.pallas_call(
        _stage_c_body,
        grid=(nt,),
        in_specs=[blk(QK_W), blk(QK_W), blk(D_MODEL),
                  pl.BlockSpec((1, mrows, 6 * D_MODEL), mod_map),
                  full(wo_b), full(wr_t), full(br)],
        out_specs=[blk(D_MODEL),
                   pl.BlockSpec((tm * SUBLANES, LANES), lambda i: (i, 0)),
                   pl.BlockSpec((SUBLANES, tm), lambda i: (0, i)),
                   pl.BlockSpec((SUBLANES, tm), lambda i: (0, i))],
        out_shape=[jax.ShapeDtypeStruct((n, D_MODEL), F32),
                   jax.ShapeDtypeStruct((n * SUBLANES, LANES), F32),
                   jax.ShapeDtypeStruct((SUBLANES, n), jnp.int32),
                   jax.ShapeDtypeStruct((SUBLANES, n), F32)],
        compiler_params=_cparams(("arbitrary",)),
        name="stage_c_sample" if sample else "stage_c_prompt",
    )(o, sg, x, mod, wo_b, wr_t, br)


def _moe_body(ea_ref, eb_ref, nact_ref, npr_ref, nval_ref, src_ref, src_next_ref, dst_ref, wrow_ref,
              wga_ref, wgb_ref, wua_ref, wub_ref, wda_ref, wdb_ref, h2p_ref, h2s_ref, out_ref,
              xbuf, ybuf, gsem, ssem, *, tm, n_max):
    i = pl.program_id(0)
    nact = nact_ref[0]
    slot = i % 2

    def rows(ref, r):
        return ref.at[pl.ds(pl.multiple_of(r * SUBLANES, SUBLANES), SUBLANES)]

    def gather_start(idx_ref, tile, sl):
        def from_slab(h_ref):
            def body(r, carry):
                pltpu.make_async_copy(rows(h_ref, idx_ref[0, 0, r]), rows(xbuf.at[sl], r), gsem.at[sl]).start()
                return carry
            return body
        n_prompt_rows = npr_ref[tile]
        lax.fori_loop(0, n_prompt_rows, from_slab(h2p_ref), 0)
        lax.fori_loop(n_prompt_rows, tm, from_slab(h2s_ref), 0)

    def gather_wait(sl):
        pltpu.make_async_copy(h2p_ref.at[pl.ds(0, tm * SUBLANES)], xbuf.at[sl], gsem.at[sl]).wait()

    def scatter_start(tile, sl):
        def body(r, carry):
            pltpu.make_async_copy(rows(ybuf.at[sl], r), rows(out_ref, dst_ref[0, 0, r]), ssem.at[sl]).start()
            return carry
        lax.fori_loop(0, nval_ref[tile], body, 0)

    def scatter_wait(tile, sl):
        n = pl.multiple_of(nval_ref[tile] * SUBLANES, SUBLANES)
        pltpu.make_async_copy(ybuf.at[sl, pl.ds(0, n)], out_ref.at[pl.ds(0, n)], ssem.at[sl]).wait()

    @pl.when(i == 0)
    def _():
        gather_start(src_ref, 0, 0)

    @pl.when(i + 1 < nact)
    def _():
        gather_start(src_next_ref, i + 1, 1 - slot)

    @pl.when(i < nact)
    def _():
        gather_wait(slot)

        @pl.when(i >= 2)
        def _():
            scatter_wait(i - 2, slot)

        xs = xbuf.at[slot]
        x = jnp.concatenate([xs[pl.ds(c, tm, stride=SUBLANES), :] for c in range(D_MODEL // LANES)],
                            axis=1).astype(BF16)
        wrow = wrow_ref[...]

        def expert(wg_ref, wu_ref, wd_ref, wcol):
            gate = jnp.dot(x, wg_ref[0], preferred_element_type=F32)
            up = jnp.dot(x, wu_ref[0], preferred_element_type=F32)
            he = (gate * jax.nn.sigmoid(gate)) * up
            return wcol * jnp.dot(he.astype(BF16), wd_ref[0], preferred_element_type=F32)

        y = (expert(wga_ref, wua_ref, wda_ref, wrow[:, 0:1])
             + expert(wgb_ref, wub_ref, wdb_ref, wrow[:, 1:2]))
        ys = ybuf.at[slot]
        for c in range(D_MODEL // LANES):
            ys[pl.ds(c, tm, stride=SUBLANES), :] = y[:, c * LANES:(c + 1) * LANES]
        scatter_start(i, slot)

    @pl.when(i == n_max - 1)
    def _():
        scatter_wait(nact - 1, (nact - 1) % 2)

        @pl.when(nact >= 2)
        def _():
            scatter_wait(nact - 2, nact % 2)


def _moe(tile_ea, tile_eb, nact, n_prompt_rows, n_valid, src_idx, dst_idx, w_slot, wg_b, wu_b, wd_b,
         h2_p, h2_s, *, tm, n_max, out_rows):
    idx_spec = lambda f: pl.BlockSpec((1, 1, tm), f, memory_space=pltpu.SMEM)
    wspec_in = lambda sel: pl.BlockSpec((1, D_MODEL, D_EXPERT), sel)
    wspec_out = lambda sel: pl.BlockSpec((1, D_EXPERT, D_MODEL), sel)
    sel_a = lambda i, ea, eb, *_: (ea[i], 0, 0)
    sel_b = lambda i, ea, eb, *_: (eb[i], 0, 0)
    grid_spec = pltpu.PrefetchScalarGridSpec(
        num_scalar_prefetch=5,
        grid=(n_max,),
        in_specs=[idx_spec(lambda i, *_: (i, 0, 0)),
                  idx_spec(lambda i, *_: (jnp.minimum(i + 1, n_max - 1), 0, 0)),
                  idx_spec(lambda i, *_: (i, 0, 0)),
                  pl.BlockSpec((tm, 2), lambda i, *_: (i, 0)),
                  wspec_in(sel_a), wspec_in(sel_b), wspec_in(sel_a), wspec_in(sel_b),
                  wspec_out(sel_a), wspec_out(sel_b),
                  pl.BlockSpec(memory_space=pl.ANY),
                  pl.BlockSpec(memory_space=pl.ANY)],
        out_specs=pl.BlockSpec(memory_space=pl.ANY),
        scratch_shapes=[pltpu.VMEM((2, tm * SUBLANES, LANES), F32),
                        pltpu.VMEM((2, tm * SUBLANES, LANES), F32),
                        pltpu.SemaphoreType.DMA((2,)),
                        pltpu.SemaphoreType.DMA((2,))])
    return pl.pallas_call(
        functools.partial(_moe_body, tm=tm, n_max=n_max),
        grid_spec=grid_spec,
        out_shape=jax.ShapeDtypeStruct((out_rows * SUBLANES, LANES), F32),
        compiler_params=_cparams(("arbitrary",)),
        name="moe",
    )(tile_ea, tile_eb, nact, n_prompt_rows, n_valid, src_idx, src_idx, dst_idx, w_slot,
      wg_b, wg_b, wu_b, wu_b, wd_b, wd_b, h2_p, h2_s)


def _final_body(x1_ref, moe_ref, mod_ref, gf_ref, y_ref):
    x1 = x1_ref[...]
    tm = x1.shape[0]
    g2 = mod_ref[0][:, 5 * D_MODEL:6 * D_MODEL]
    moe = jnp.concatenate([moe_ref[pl.ds(c, tm, stride=SUBLANES), :] for c in range(D_MODEL // LANES)],
                          axis=1)
    x2 = x1 + g2 * moe
    ms = jnp.mean(x2 * x2, axis=-1, keepdims=True)
    y_ref[...] = (x2 * lax.rsqrt(ms + EPS)) * gf_ref[...]


def _final(x1, moe_out, mod, g_final, *, tm, sample, row_off):
    n = x1.shape[0]
    nt = n // tm
    mrows = mod.shape[1]
    if sample:
        mod_map = lambda i: (0, 0, 0)
    else:
        tiles_per_batch = 4096 // tm
        mod_map = lambda i: (i // tiles_per_batch, 0, 0)
    blk_off = row_off // tm
    return pl.pallas_call(
        _final_body,
        grid=(nt,),
        in_specs=[pl.BlockSpec((tm, D_MODEL), lambda i: (i, 0)),
                  pl.BlockSpec((tm * SUBLANES, LANES), lambda i: (i + blk_off, 0)),
                  pl.BlockSpec((1, mrows, 6 * D_MODEL), mod_map),
                  pl.BlockSpec((1, D_MODEL), lambda i: (0, 0))],
        out_specs=pl.BlockSpec((tm, D_MODEL), lambda i: (i, 0)),
        out_shape=jax.ShapeDtypeStruct((n, D_MODEL), F32),
        compiler_params=_cparams(("arbitrary",)),
        name="final_sample" if sample else "final_prompt",
    )(x1, moe_out, mod, g_final.reshape(1, D_MODEL))


def _bucket_table(n):
    d = np.arange(n)
    max_exact = N_BUCKETS // 2
    nf = np.maximum(d, 1).astype(np.float64)
    large = max_exact + (np.log(nf / max_exact) / math.log(MAX_DISTANCE / max_exact)
                         * (N_BUCKETS - max_exact)).astype(np.int64)
    large = np.minimum(large, N_BUCKETS - 1)
    return np.where(d < max_exact, d, large).astype(np.int32)


_PAIR_LO = np.array([0, 0, 0, 1, 1, 2], np.int32)
_PAIR_HI = np.array([1, 2, 3, 2, 3, 3], np.int32)


def kernel(x_prompt, x_sample, c_prompt, c_sample, cache_k, cache_v, page_table, w_ada, b_ada, w_in, w_o,
           lam_q1, lam_k1, lam_q2, lam_k2, g_subln, rel_bias, g_sg_ln, b_sg_ln, w_s, b_s, w_rg, b_rg,
           w_re, b_re, w_gate, w_up, w_down, g_final):
    batch, seq, _ = x_prompt.shape
    dec_b, dec_t, _ = x_sample.shape
    n_pages = page_table.shape[1]
    n_p = batch * seq
    n_s = dec_b * dec_t
    n_tot = n_p + n_s
    assert w_in.shape[0] == 1 and seq % T_ATT == 0 and n_pages % PAGES_PER_STEP == 0
    assert n_p % TM_TOK == 0 and n_p % n_s == 0 and n_tot % TM_E == 0 and dec_t == SUBLANES

    w_in_b = w_in[0].astype(BF16)
    w_vt_b = w_in[0][:, 2 * QK_W:3 * QK_W].T.astype(BF16)
    w_o_b = w_o[0].astype(BF16)
    wr_t = jnp.zeros((32, D_MODEL), F32).at[0:N_EG].set(w_rg[0].T).at[N_EG:N_EG + N_EG * EPG].set(w_re[0].T)
    wr_t = wr_t.astype(BF16)
    br = jnp.zeros((32, 1), F32).at[0:N_EG, 0].set(b_rg[0]).at[N_EG:N_EG + N_EG * EPG, 0].set(b_re[0])
    wg_b = w_gate[0].astype(BF16)
    wu_b = w_up[0].astype(BF16)
    wd_b = w_down[0].astype(BF16)
    ws_tril = jnp.tril(w_s[0])
    ws_p = ws_tril.astype(BF16)
    bs_p = b_s[0][:, :, None]
    eye = jnp.eye(dec_b, dtype=F32)
    ws_s = jnp.einsum('ab,gij->gaibj', eye, ws_tril[:, :dec_t, :dec_t]).reshape(
        N_GROUPS_SG, n_s, n_s).astype(BF16)
    bs_s = jnp.tile(b_s[0][:, :dec_t], (1, dec_b))[:, :, None]
    gln = g_sg_ln[0]
    bln = b_sg_ln[0]
    lam = (jnp.exp(jnp.sum(lam_q1[0] * lam_k1[0])) - jnp.exp(jnp.sum(lam_q2[0] * lam_k2[0]))
           + LAM_INIT).reshape(1).astype(F32)

    t = T_ATT
    btab = rel_bias[_bucket_table(2 * t + 2 * PAGE * PAGES_PER_STEP)] - rel_bias[N_BUCKETS - 1]
    kk = np.arange(t)[:, None]
    qq = np.arange(t)[None, :]
    d_sub = qq - kk + t
    d_diag = qq - kk
    bias_sub = btab[d_sub]
    bias_diag = jnp.where((d_diag >= 0)[:, :, None], btab[np.maximum(d_diag, 0)], NEG)
    bias_t = jnp.transpose(jnp.stack([bias_sub, bias_diag], axis=0), (3, 0, 1, 2))

    n_last = PAGES_PER_STEP * PAGE
    past = n_pages * PAGE
    qi = np.arange(dec_t)[:, None]
    d_last = past + qi - (past - n_last + np.arange(n_last)[None, :])
    bl = jnp.transpose(btab[d_last], (2, 0, 1))
    bias_last = jnp.broadcast_to(bl[:, None], (N_HEADS, 2, dec_t, n_last)).reshape(2 * N_HEADS * dec_t, n_last)
    jn = np.arange(PAGE)[None, :]
    d_new = qi - jn
    ok_new = (d_new >= 0) & (jn < dec_t)
    bn = jnp.where(ok_new[None], jnp.transpose(btab[np.maximum(d_new, 0)], (2, 0, 1)), NEG)
    bias_new = jnp.broadcast_to(bn[:, None], (N_HEADS, 2, dec_t, PAGE)).reshape(2 * N_HEADS * dec_t, PAGE)

    c_all = jnp.concatenate([c_prompt, c_sample, jnp.zeros((4, D_MODEL), F32)], axis=0)
    mod_all = _ada(c_all, w_ada[0], b_ada[0])
    mod_p = mod_all[:batch].reshape(batch, 1, 6 * D_MODEL)
    mod_s = jnp.repeat(mod_all[batch:batch + dec_b], dec_t, axis=0).reshape(1, n_s, 6 * D_MODEL)

    xp = x_prompt.reshape(n_p, D_MODEL)
    xs = x_sample.reshape(n_s, D_MODEL)

    q_p, kf_p, kb_p, vf_p, vt_p, sg_p = _stage_a(xp, mod_p, w_in_b, w_vt_b, ws_p, bs_p, gln, bln,
                                                 tm=TM_TOK, chunk=CHUNK, sample=False)
    q_s, kf_s, vf_s, sg_s, vsn_s = _stage_a(xs, mod_s, w_in_b, w_vt_b, ws_s, bs_s, gln, bln,
                                            tm=n_s, chunk=n_s, sample=True)

    g_col = g_subln[0].reshape(HEAD_W, 1)
    g_row = g_subln[0].reshape(1, HEAD_W)
    o_p = _attn_prompt(lam, q_p, kb_p, vt_p, bias_t, g_col, batch=batch, seq=seq, t=t)
    pad = ((0, 0), (0, PAGE - dec_t), (0, 0))
    knew = jnp.pad(kf_s.reshape(dec_b, dec_t, QK_W), pad).astype(BF16)
    vnew = jnp.pad(vf_s.reshape(dec_b, dec_t, QK_W), pad).astype(BF16)
    ck = cache_k[:, 0].reshape(cache_k.shape[0], PAGE, QK_W)
    cv = cache_v[:, 0].reshape(cache_v.shape[0], PAGE, QK_W)
    o_s = _attn_sample(page_table.reshape(-1), lam, q_s, knew, vnew, bias_last, bias_new, g_row, ck, cv,
                       dec_b=dec_b, n_pages=n_pages)

    x1_p, h2_p, cls_p, w_p = _stage_c(o_p, sg_p, xp, mod_p, w_o_b, wr_t, br, tm=TM_TOK, sample=False)
    x1_s, h2_s, cls_s, w_s2 = _stage_c(o_s, sg_s, xs, mod_s, w_o_b, wr_t, br, tm=n_s, sample=True)

    tm_e = TM_E
    n_max = n_tot // tm_e + N_CLASSES
    cls = jnp.concatenate([cls_p[0], cls_s[0]])
    w_tok = jnp.concatenate([w_p[0:2], w_s2[0:2]], axis=1)
    counts = jnp.sum((cls[:, None] == jnp.arange(N_CLASSES)[None, :]).astype(jnp.int32), axis=0)
    class_start = jnp.cumsum(counts) - counts
    ntile_c = (counts + tm_e - 1) // tm_e
    tile_end = jnp.cumsum(ntile_c)
    tile_start = tile_end - ntile_c
    nact = tile_end[-1]
    order = jnp.argsort(cls, stable=True).astype(jnp.int32)
    tile_ids = jnp.arange(n_max, dtype=jnp.int32)
    tile_cls = jnp.sum((tile_ids[:, None] >= tile_end[None, :]).astype(jnp.int32), axis=1)
    last_cls = jnp.sum((nact - 1 >= tile_end).astype(jnp.int32))
    tile_cls = jnp.where(tile_ids < nact, tile_cls, last_cls)
    grp = tile_cls // N_PAIRS
    pidx = tile_cls % N_PAIRS
    tile_ea = (grp * EPG + jnp.asarray(_PAIR_LO)[pidx]).astype(jnp.int32)
    tile_eb = (grp * EPG + jnp.asarray(_PAIR_HI)[pidx]).astype(jnp.int32)
    r = jnp.arange(tm_e, dtype=jnp.int32)[None, :]
    rank = (tile_ids[:, None] - tile_start[tile_cls][:, None]) * tm_e + r
    valid = (rank < counts[tile_cls][:, None]) & (tile_ids[:, None] < nact)
    tok = order[jnp.clip(class_start[tile_cls][:, None] + rank, 0, n_tot - 1)]
    from_prompt = valid & (tok < n_p)
    src = jnp.where(from_prompt, tok, jnp.where(valid, tok - n_p, 0))
    dst = jnp.where(valid, tok, 0)
    n_prompt_rows = jnp.sum(from_prompt.astype(jnp.int32), axis=1)
    n_valid = jnp.sum(valid.astype(jnp.int32), axis=1)
    w_slot = jnp.where(valid[:, :, None], w_tok.T[dst], 0.0).reshape(n_max * tm_e, 2)
    src_idx = src.reshape(n_max, 1, tm_e).astype(jnp.int32)
    dst_idx = dst.reshape(n_max, 1, tm_e).astype(jnp.int32)

    moe_out = _moe(tile_ea, tile_eb, nact.reshape(1).astype(jnp.int32), n_prompt_rows, n_valid,
                   src_idx, dst_idx, w_slot, wg_b, wu_b, wd_b, h2_p, h2_s,
                   tm=tm_e, n_max=n_max, out_rows=n_tot)

    y_p = _final(x1_p, moe_out, mod_p, g_final, tm=TM_TOK, sample=False, row_off=0)
    y_s = _final(x1_s, moe_out, mod_s, g_final, tm=n_s, sample=True, row_off=n_p)

    return (y_p.reshape(batch, seq, D_MODEL),
            y_s.reshape(dec_b, dec_t, D_MODEL),
            kf_p.reshape(batch, 1, seq, N_HEADS, HEAD_W),
            vf_p.reshape(batch, 1, seq, N_HEADS, HEAD_W),
            kf_s.reshape(dec_b, 1, dec_t, N_HEADS, HEAD_W),
            vf_s.reshape(dec_b, 1, dec_t, N_HEADS, HEAD_W),
            vsn_s.reshape(dec_b, 1, dec_t, N_GROUPS_SG, SG_CH))
```

```python
import functools
import math

import numpy as np
import jax
import jax.numpy as jnp
from jax import lax
from jax.experimental import pallas as pl
from jax.experimental.pallas import tpu as pltpu

F32 = jnp.float32
BF16 = jnp.bfloat16

D_MODEL = 1024
N_HEADS = 4
DK = 64
HEAD_W = 128
QK_W = N_HEADS * HEAD_W
N_GROUPS_SG = 4
SG_CH = 128
IN_W = 2560
CHUNK = 128
PAGE = 128
N_BUCKETS = 32
MAX_DISTANCE = 128
N_EG = 4
EPG = 4
N_PAIRS = 6
N_CLASSES = N_EG * N_PAIRS
D_EXPERT = 512
EPS = 1e-6
LAM_INIT = 0.8 - 0.6 * math.exp(-0.3 * 0)
NEG = -1e30
LANES = 128
SUBLANES = 8

TM_TOK = 512
T_ATT = 512
PAGES_PER_STEP = 16
TM_E = 256
VMEM_LIMIT = 56 * 1024 * 1024


def _cparams(sem):
    return pltpu.CompilerParams(dimension_semantics=sem, vmem_limit_bytes=VMEM_LIMIT)


def _ada_body(c_ref, w_ref, b_ref, o_ref):
    c = c_ref[...]
    a = (c * jax.nn.sigmoid(c)).astype(BF16)
    o_ref[...] = jnp.dot(a, w_ref[...].astype(BF16), preferred_element_type=F32) + b_ref[...]


def _ada(c_all, w_ada, b_ada):
    m = c_all.shape[0]
    n = w_ada.shape[1]
    tn = 1536
    return pl.pallas_call(
        _ada_body,
        grid=(n // tn,),
        in_specs=[pl.BlockSpec((m, D_MODEL), lambda j: (0, 0)),
                  pl.BlockSpec((D_MODEL, tn), lambda j: (0, j)),
                  pl.BlockSpec((1, tn), lambda j: (0, j))],
        out_specs=pl.BlockSpec((m, tn), lambda j: (0, j)),
        out_shape=jax.ShapeDtypeStruct((m, n), F32),
        compiler_params=_cparams(("arbitrary",)),
        name="adaln",
    )(c_all, w_ada, b_ada.reshape(1, n))


def _stage_a_body(x_ref, mod_ref, w_in_ref, w_vt_ref, ws_ref, bs_ref, gln_ref, bln_ref,
                  *out_refs, chunk, sample):
    if sample:
        q_ref, kf_ref, vf_ref, sg_ref, vsn_ref = out_refs
    else:
        q_ref, kf_ref, kb_ref, vf_ref, vt_ref, sg_ref = out_refs
    x = x_ref[...]
    tm = x.shape[0]
    mod = mod_ref[0]
    sh1 = mod[:, 0:D_MODEL]
    sc1 = mod[:, D_MODEL:2 * D_MODEL]
    ms = jnp.mean(x * x, axis=-1, keepdims=True)
    h = (x * lax.rsqrt(ms + EPS)) * (1.0 + sc1) + sh1
    hb = h.astype(BF16)
    z = jnp.dot(hb, w_in_ref[...], preferred_element_type=F32)
    q = z[:, 0:QK_W] * (DK ** -0.5)
    k = z[:, QK_W:2 * QK_W]
    v = z[:, 2 * QK_W:3 * QK_W]
    for hd in range(N_HEADS):
        kf_ref[pl.ds(hd, tm, stride=N_HEADS), :] = k[:, hd * HEAD_W:(hd + 1) * HEAD_W]
        vf_ref[pl.ds(hd, tm, stride=N_HEADS), :] = v[:, hd * HEAD_W:(hd + 1) * HEAD_W]
    if sample:
        q_ref[...] = q
    else:
        q_ref[...] = q.astype(BF16)
        kb_ref[...] = k.astype(BF16)
        vt = lax.dot_general(w_vt_ref[...], hb, (((1,), (1,)), ((), ())),
                             preferred_element_type=F32)
        vt_ref[0] = vt.astype(BF16)
    u = z[:, 3 * QK_W:4 * QK_W]
    vs = z[:, 4 * QK_W:5 * QK_W]
    for g in range(N_GROUPS_SG):
        lo, hi = g * SG_CH, (g + 1) * SG_CH
        vg = vs[:, lo:hi]
        mu = jnp.mean(vg, axis=-1, keepdims=True)
        dv = vg - mu
        var = jnp.mean(dv * dv, axis=-1, keepdims=True)
        vn = (dv * lax.rsqrt(var + EPS)) * gln_ref[g:g + 1, :] + bln_ref[g:g + 1, :]
        if sample:
            vsn_ref[:, lo:hi] = vn
        vnb = vn.astype(BF16)
        for c in range(tm // chunk):
            r0, r1 = c * chunk, (c + 1) * chunk
            s = jnp.dot(ws_ref[g], vnb[r0:r1], preferred_element_type=F32) + bs_ref[g]
            sg_ref[r0:r1, lo:hi] = (u[r0:r1, lo:hi] * s).astype(BF16)


def _stage_a(x, mod, w_in_b, w_vt_b, ws, bs, gln, bln, *, tm, chunk, sample):
    n = x.shape[0]
    nt = n // tm
    mrows = mod.shape[1]
    row = lambda w, dt: jax.ShapeDtypeStruct((n, w), dt)
    blk = lambda w: pl.BlockSpec((tm, w), lambda i: (i, 0))
    cache_shape = jax.ShapeDtypeStruct((n * N_HEADS, HEAD_W), F32)
    cache_blk = pl.BlockSpec((tm * N_HEADS, HEAD_W), lambda i: (i, 0))
    if sample:
        out_shape = [row(QK_W, F32), cache_shape, cache_shape, row(QK_W, BF16), row(QK_W, F32)]
        out_specs = [blk(QK_W), cache_blk, cache_blk, blk(QK_W), blk(QK_W)]
        mod_map = lambda i: (0, 0, 0)
    else:
        out_shape = [row(QK_W, BF16), cache_shape, row(QK_W, BF16), cache_shape,
                     jax.ShapeDtypeStruct((nt, QK_W, tm), BF16), row(QK_W, BF16)]
        out_specs = [blk(QK_W), cache_blk, blk(QK_W), cache_blk,
                     pl.BlockSpec((1, QK_W, tm), lambda i: (i, 0, 0)), blk(QK_W)]
        tiles_per_batch = 4096 // tm
        mod_map = lambda i: (i // tiles_per_batch, 0, 0)
    full = lambda a: pl.BlockSpec(a.shape, lambda i: (0,) * a.ndim)
    return pl.pallas_call(
        functools.partial(_stage_a_body, chunk=chunk, sample=sample),
        grid=(nt,),
        in_specs=[blk(D_MODEL),
                  pl.BlockSpec((1, mrows, 6 * D_MODEL), mod_map),
                  full(w_in_b), full(w_vt_b), full(ws), full(bs), full(gln), full(bln)],
        out_specs=out_specs,
        out_shape=out_shape,
        compiler_params=_cparams(("arbitrary",)),
        name="stage_a_sample" if sample else "stage_a_prompt",
    )(x, mod, w_in_b, w_vt_b, ws, bs, gln, bln)


def _attn_body(lam_ref, q_ref, k_ref, vt_ref, bias_ref, g_ref, o_ref, m_ref, l_ref, acc_ref, *, t):
    qi = pl.program_id(2)
    q = q_ref[...]
    lane = lax.broadcasted_iota(jnp.int32, q.shape, 1)
    zero = jnp.zeros_like(q)
    qm = (jnp.where(lane < DK, q, zero), jnp.where(lane >= DK, q, zero))
    m_ref[...] = jnp.full(m_ref.shape, NEG, F32)
    l_ref[...] = jnp.zeros(l_ref.shape, F32)
    acc_ref[...] = jnp.zeros(acc_ref.shape, F32)

    def tile(j, bias):
        k = k_ref[pl.ds(pl.multiple_of(j * t, t), t), :]
        vt = vt_ref[j]
        for mi in range(2):
            s = lax.dot_general(k, qm[mi], (((1,), (1,)), ((), ())),
                                preferred_element_type=F32)
            if bias is not None:
                s = s + bias
            m_old = m_ref[mi]
            m_new = jnp.maximum(m_old, jnp.max(s, axis=0, keepdims=True))
            alpha = jnp.exp(m_old - m_new)
            p = jnp.exp(s - m_new)
            l_ref[mi] = alpha * l_ref[mi] + jnp.sum(p, axis=0, keepdims=True)
            acc_ref[mi] = alpha * acc_ref[mi] + jnp.dot(vt, p.astype(BF16),
                                                        preferred_element_type=F32)
            m_ref[mi] = m_new

    def plain(j, carry):
        tile(j, None)
        return carry

    lax.fori_loop(0, jnp.maximum(qi - 1, 0), plain, 0)

    @pl.when(qi >= 1)
    def _():
        tile(qi - 1, bias_ref[0, 0])

    tile(qi, bias_ref[0, 1])

    lam = lam_ref[0]
    o1 = acc_ref[0] * (1.0 / l_ref[0])
    o2 = acc_ref[1] * (1.0 / l_ref[1])
    o = o1 - lam * o2
    ms = jnp.mean(o * o, axis=0, keepdims=True)
    on = (o * lax.rsqrt(ms + EPS)) * g_ref[...] * (1.0 - LAM_INIT)
    o_ref[...] = on.T.astype(BF16)


def _attn_prompt(lam, q, kb, vt, bias_t, g_col, *, batch, seq, t):
    nq = seq // t
    n = batch * seq
    return pl.pallas_call(
        functools.partial(_attn_body, t=t),
        grid=(batch, N_HEADS, nq),
        in_specs=[pl.BlockSpec(memory_space=pltpu.SMEM),
                  pl.BlockSpec((t, HEAD_W), lambda b, h, i: (b * nq + i, h)),
                  pl.BlockSpec((seq, HEAD_W), lambda b, h, i: (b, h)),
                  pl.BlockSpec((nq, HEAD_W, t), lambda b, h, i: (b, h, 0)),
                  pl.BlockSpec((1, 2, t, t), lambda b, h, i: (h, 0, 0, 0)),
                  pl.BlockSpec((HEAD_W, 1), lambda b, h, i: (0, 0))],
        out_specs=pl.BlockSpec((t, HEAD_W), lambda b, h, i: (b * nq + i, h)),
        out_shape=jax.ShapeDtypeStruct((n, QK_W), BF16),
        scratch_shapes=[pltpu.VMEM((2, 1, t), F32), pltpu.VMEM((2, 1, t), F32),
                        pltpu.VMEM((2, HEAD_W, t), F32)],
        compiler_params=_cparams(("arbitrary", "arbitrary", "arbitrary")),
        name="attn_prompt",
    )(lam, q, kb, vt, bias_t, g_col)


PAGE_ROWS = PAGE * N_HEADS


def _sattn_body(pt_ref, lam_ref, q_ref, knew_ref, vnew_ref, bl_ref, bn_ref, g_ref, ck_ref, cv_ref,
                o_ref, kbuf, vbuf, sem, mask_ref, m_ref, l_ref, acc_ref, *, pages, n_steps, total):
    b = pl.program_id(0)
    s = pl.program_id(1)
    step = b * n_steps + s
    slot = step % 2

    def page_copies(step_idx, sl):
        base = step_idx * pages
        out = []
        for i in range(pages):
            src = pl.ds(pl.multiple_of(pt_ref[base + i] * PAGE_ROWS, PAGE_ROWS), PAGE_ROWS)
            dst = pl.ds(i * PAGE_ROWS, PAGE_ROWS)
            out.append(pltpu.make_async_copy(ck_ref.at[src], kbuf.at[sl, dst], sem.at[sl, 0]))
            out.append(pltpu.make_async_copy(cv_ref.at[src], vbuf.at[sl, dst], sem.at[sl, 1]))
        return out

    @pl.when(step == 0)
    def _():
        for c in page_copies(0, 0):
            c.start()

    @pl.when(step + 1 < total)
    def _():
        for c in page_copies(step + 1, 1 - slot):
            c.start()

    for c in page_copies(step, slot):
        c.wait()

    @pl.when(s == 0)
    def _():
        m_ref[...] = jnp.full(m_ref.shape, NEG, F32)
        l_ref[...] = jnp.zeros(l_ref.shape, F32)
        acc_ref[...] = jnp.zeros(acc_ref.shape, F32)

    @pl.when(step == 0)
    def _():
        row = lax.broadcasted_iota(jnp.int32, mask_ref.shape, 0)
        col = lax.broadcasted_iota(jnp.int32, mask_ref.shape, 1)
        same_head = (col % N_HEADS) == (row // (2 * SUBLANES))
        mask_ref[...] = jnp.where(same_head, 0.0, NEG)

    q = q_ref[...]
    lane = lax.broadcasted_iota(jnp.int32, (SUBLANES, HEAD_W), 1)
    pieces = []
    for h in range(N_HEADS):
        qh = q[:, h * HEAD_W:(h + 1) * HEAD_W]
        pieces += [jnp.where(lane < DK, qh, 0.0), jnp.where(lane >= DK, qh, 0.0)]
    qm = jnp.concatenate(pieces, axis=0).astype(BF16)

    def update(kb, vb, bias):
        sc = lax.dot_general(qm, kb, (((1,), (1,)), ((), ())),
                             preferred_element_type=F32) + bias
        m_old = m_ref[...]
        m_new = jnp.maximum(m_old, jnp.max(sc, axis=1, keepdims=True))
        alpha = jnp.exp(m_old - m_new)
        p = jnp.exp(sc - m_new)
        l_ref[...] = alpha * l_ref[...] + jnp.sum(p, axis=1, keepdims=True)
        acc_ref[...] = alpha * acc_ref[...] + jnp.dot(p.astype(BF16), vb,
                                                      preferred_element_type=F32)
        m_ref[...] = m_new

    is_last = s == n_steps - 1
    update(kbuf[slot].astype(BF16), vbuf[slot].astype(BF16),
           mask_ref[...] + bl_ref[...] * is_last.astype(F32))

    @pl.when(is_last)
    def _():
        update(knew_ref[0], vnew_ref[0], bn_ref[...])
        lam = lam_ref[0]
        o_all = acc_ref[...] * (1.0 / l_ref[...])
        for h in range(N_HEADS):
            r = h * 2 * SUBLANES
            o = o_all[r:r + SUBLANES] - lam * o_all[r + SUBLANES:r + 2 * SUBLANES]
            ms = jnp.mean(o * o, axis=-1, keepdims=True)
            o_ref[:, h * HEAD_W:(h + 1) * HEAD_W] = ((o * lax.rsqrt(ms + EPS)) * g_ref[...]
                                                     * (1.0 - LAM_INIT))


def _attn_sample(page_table_flat, lam, q_s, knew, vnew, bias_last, bias_new, g_row, cache_k, cache_v,
                 *, dec_b, n_pages):
    pages = PAGES_PER_STEP
    n_steps = n_pages // pages
    total = dec_b * n_steps
    nq = q_s.shape[0] // dec_b
    n_rows = 2 * N_HEADS * nq
    step_rows = pages * PAGE_ROWS
    grid_spec = pltpu.PrefetchScalarGridSpec(
        num_scalar_prefetch=1,
        grid=(dec_b, n_steps),
        in_specs=[pl.BlockSpec(memory_space=pltpu.SMEM),
                  pl.BlockSpec((nq, QK_W), lambda b, s, pt: (b, 0)),
                  pl.BlockSpec((1, PAGE, HEAD_W), lambda b, s, pt: (b, 0, 0)),
                  pl.BlockSpec((1, PAGE, HEAD_W), lambda b, s, pt: (b, 0, 0)),
                  pl.BlockSpec(bias_last.shape, lambda b, s, pt: (0, 0)),
                  pl.BlockSpec(bias_new.shape, lambda b, s, pt: (0, 0)),
                  pl.BlockSpec((1, HEAD_W), lambda b, s, pt: (0, 0)),
                  pl.BlockSpec(memory_space=pl.ANY),
                  pl.BlockSpec(memory_space=pl.ANY)],
        out_specs=pl.BlockSpec((nq, QK_W), lambda b, s, pt: (b, 0)),
        scratch_shapes=[pltpu.VMEM((2, step_rows, HEAD_W), F32),
                        pltpu.VMEM((2, step_rows, HEAD_W), F32),
                        pltpu.SemaphoreType.DMA((2, 2)),
                        pltpu.VMEM((n_rows, step_rows), F32),
                        pltpu.VMEM((n_rows, 1), F32), pltpu.VMEM((n_rows, 1), F32),
                        pltpu.VMEM((n_rows, HEAD_W), F32)])
    return pl.pallas_call(
        functools.partial(_sattn_body, pages=pages, n_steps=n_steps, total=total),
        grid_spec=grid_spec,
        out_shape=jax.ShapeDtypeStruct(q_s.shape, F32),
        compiler_params=_cparams(("arbitrary", "arbitrary")),
        name="attn_sample",
    )(page_table_flat, lam, q_s, knew, vnew, bias_last, bias_new, g_row, cache_k, cache_v)


def _stage_c_body(o_ref, sg_ref, x_ref, mod_ref, wo_ref, wr_ref, br_ref, x1_ref, h2_ref, cls_ref, w_ref):
    x = x_ref[...]
    tm = x.shape[0]
    mod = mod_ref[0]
    g1 = mod[:, 2 * D_MODEL:3 * D_MODEL]
    sh2 = mod[:, 3 * D_MODEL:4 * D_MODEL]
    sc2 = mod[:, 4 * D_MODEL:5 * D_MODEL]
    mix = (jnp.dot(o_ref[...].astype(BF16), wo_ref[0:QK_W, :], preferred_element_type=F32)
           + jnp.dot(sg_ref[...], wo_ref[QK_W:2 * QK_W, :], preferred_element_type=F32))
    x1 = x + g1 * mix
    x1_ref[...] = x1
    ms = jnp.mean(x1 * x1, axis=-1, keepdims=True)
    h2 = (x1 * lax.rsqrt(ms + EPS)) * (1.0 + sc2) + sh2
    for c in range(D_MODEL // LANES):
        h2_ref[pl.ds(c, tm, stride=SUBLANES), :] = h2[:, c * LANES:(c + 1) * LANES]
    lg = lax.dot_general(wr_ref[...], h2.astype(BF16), (((1,), (1,)), ((), ())),
                         preferred_element_type=F32) + br_ref[...]
    gl = [lg[i:i + 1, :] for i in range(N_EG)]
    el = [lg[N_EG + i:N_EG + i + 1, :] for i in range(N_EG * EPG)]
    gmax = jnp.maximum(jnp.maximum(gl[0], gl[1]), jnp.maximum(gl[2], gl[3]))
    gi = jnp.where(gl[0] == gmax, 0, jnp.where(gl[1] == gmax, 1, jnp.where(gl[2] == gmax, 2, 3)))
    gsum = (jnp.exp(gl[0] - gmax) + jnp.exp(gl[1] - gmax)
            + jnp.exp(gl[2] - gmax) + jnp.exp(gl[3] - gmax))
    gp = 1.0 / gsum
    sel = [jnp.where(gi == 0, el[j], jnp.where(gi == 1, el[EPG + j],
                                               jnp.where(gi == 2, el[2 * EPG + j], el[3 * EPG + j])))
           for j in range(EPG)]
    v0 = jnp.maximum(jnp.maximum(sel[0], sel[1]), jnp.maximum(sel[2], sel[3]))
    i0 = jnp.where(sel[0] == v0, 0, jnp.where(sel[1] == v0, 1, jnp.where(sel[2] == v0, 2, 3)))
    rest = [jnp.where(i0 == j, -3e38, sel[j]) for j in range(EPG)]
    v1 = jnp.maximum(jnp.maximum(rest[0], rest[1]), jnp.maximum(rest[2], rest[3]))
    i1 = jnp.where(rest[0] == v1, 0, jnp.where(rest[1] == v1, 1, jnp.where(rest[2] == v1, 2, 3)))
    e1 = jnp.exp(v1 - v0)
    den = 1.0 / (1.0 + e1)
    tw0 = den * gp
    tw1 = e1 * den * gp
    first_low = i0 < i1
    lo = jnp.where(first_low, i0, i1)
    hi = jnp.where(first_low, i1, i0)
    w_lo = jnp.where(first_low, tw0, tw1)
    w_hi = jnp.where(first_low, tw1, tw0)
    pair = jnp.where(lo == 0, 0, jnp.where(lo == 1, 3, 5)) + hi - lo - 1
    cls = gi * N_PAIRS + pair
    cls_ref[...] = jnp.broadcast_to(cls, cls_ref.shape).astype(jnp.int32)
    w_ref[...] = jnp.concatenate([w_lo, w_hi, jnp.zeros((SUBLANES - 2, tm), F32)], axis=0)


def _stage_c(o, sg, x, mod, wo_b, wr_t, br, *, tm, sample):
    n = x.shape[0]
    nt = n // tm
    mrows = mod.shape[1]
    blk = lambda w: pl.BlockSpec((tm, w), lambda i: (i, 0))
    full = lambda a: pl.BlockSpec(a.shape, lambda i: (0,) * a.ndim)
    if sample:
        mod_map = lambda i: (0, 0, 0)
    else:
        tiles_per_batch = 4096 // tm
        mod_map = lambda i: (i // tiles_per_batch, 0, 0)
    return pl.pallas_call(
        _stage_c_body,
        grid=(nt,),
        in_specs=[blk(QK_W), blk(QK_W), blk(D_MODEL),
                  pl.BlockSpec((1, mrows, 6 * D_MODEL), mod_map),
                  full(wo_b), full(wr_t), full(br)],
        out_specs=[blk(D_MODEL),
                   pl.BlockSpec((tm * SUBLANES, LANES), lambda i: (i, 0)),
                   pl.BlockSpec((SUBLANES, tm), lambda i: (0, i)),
                   pl.BlockSpec((SUBLANES, tm), lambda i: (0, i))],
        out_shape=[jax.ShapeDtypeStruct((n, D_MODEL), F32),
                   jax.ShapeDtypeStruct((n * SUBLANES, LANES), F32),
                   jax.ShapeDtypeStruct((SUBLANES, n), jnp.int32),
                   jax.ShapeDtypeStruct((SUBLANES, n), F32)],
        compiler_params=_cparams(("arbitrary",)),
        name="stage_c_sample" if sample else "stage_c_prompt",
    )(o, sg, x, mod, wo_b, wr_t, br)


def _moe_body(ea_ref, eb_ref, nact_ref, npr_ref, nval_ref, src_ref, src_next_ref, dst_ref, wrow_ref,
              wga_ref, wgb_ref, wua_ref, wub_ref, wda_ref, wdb_ref, h2p_ref, h2s_ref, out_ref,
              xbuf, ybuf, gsem, ssem, *, tm, n_max):
    i = pl.program_id(0)
    nact = nact_ref[0]
    slot = i % 2

    def rows(ref, r):
        return ref.at[pl.ds(pl.multiple_of(r * SUBLANES, SUBLANES), SUBLANES)]

    def gather_start(idx_ref, tile, sl):
        def from_slab(h_ref):
            def body(r, carry):
                pltpu.make_async_copy(rows(h_ref, idx_ref[0, 0, r]), rows(xbuf.at[sl], r), gsem.at[sl]).start()
                return carry
            return body
        n_prompt_rows = npr_ref[tile]
        lax.fori_loop(0, n_prompt_rows, from_slab(h2p_ref), 0)
        lax.fori_loop(n_prompt_rows, tm, from_slab(h2s_ref), 0)

    def gather_wait(sl):
        pltpu.make_async_copy(h2p_ref.at[pl.ds(0, tm * SUBLANES)], xbuf.at[sl], gsem.at[sl]).wait()

    def scatter_start(tile, sl):
        def body(r, carry):
            pltpu.make_async_copy(rows(ybuf.at[sl], r), rows(out_ref, dst_ref[0, 0, r]), ssem.at[sl]).start()
            return carry
        lax.fori_loop(0, nval_ref[tile], body, 0)

    def scatter_wait(tile, sl):
        n = pl.multiple_of(nval_ref[tile] * SUBLANES, SUBLANES)
        pltpu.make_async_copy(ybuf.at[sl, pl.ds(0, n)], out_ref.at[pl.ds(0, n)], ssem.at[sl]).wait()

    @pl.when(i == 0)
    def _():
        gather_start(src_ref, 0, 0)

    @pl.when(i + 1 < nact)
    def _():
        gather_start(src_next_ref, i + 1, 1 - slot)

    @pl.when(i < nact)
    def _():
        gather_wait(slot)

        @pl.when(i >= 2)
        def _():
            scatter_wait(i - 2, slot)

        xs = xbuf.at[slot]
        x = jnp.concatenate([xs[pl.ds(c, tm, stride=SUBLANES), :] for c in range(D_MODEL // LANES)],
                            axis=1).astype(BF16)
        wrow = wrow_ref[...]

        def expert(wg_ref, wu_ref, wd_ref, wcol):
            gate = jnp.dot(x, wg_ref[0], preferred_element_type=F32)
            up = jnp.dot(x, wu_ref[0], preferred_element_type=F32)
            he = (gate * jax.nn.sigmoid(gate)) * up
            return wcol * jnp.dot(he.astype(BF16), wd_ref[0], preferred_element_type=F32)

        y = (expert(wga_ref, wua_ref, wda_ref, wrow[:, 0:1])
             + expert(wgb_ref, wub_ref, wdb_ref, wrow[:, 1:2]))
        ys = ybuf.at[slot]
        for c in range(D_MODEL // LANES):
            ys[pl.ds(c, tm, stride=SUBLANES), :] = y[:, c * LANES:(c + 1) * LANES]
        scatter_start(i, slot)

    @pl.when(i == n_max - 1)
    def _():
        scatter_wait(nact - 1, (nact - 1) % 2)

        @pl.when(nact >= 2)
        def _():
            scatter_wait(nact - 2, nact % 2)


def _moe(tile_ea, tile_eb, nact, n_prompt_rows, n_valid, src_idx, dst_idx, w_slot, wg_b, wu_b, wd_b,
         h2_p, h2_s, *, tm, n_max, out_rows):
    idx_spec = lambda f: pl.BlockSpec((1, 1, tm), f, memory_space=pltpu.SMEM)
    wspec_in = lambda sel: pl.BlockSpec((1, D_MODEL, D_EXPERT), sel)
    wspec_out = lambda sel: pl.BlockSpec((1, D_EXPERT, D_MODEL), sel)
    sel_a = lambda i, ea, eb, *_: (ea[i], 0, 0)
    sel_b = lambda i, ea, eb, *_: (eb[i], 0, 0)
    grid_spec = pltpu.PrefetchScalarGridSpec(
        num_scalar_prefetch=5,
        grid=(n_max,),
        in_specs=[idx_spec(lambda i, *_: (i, 0, 0)),
                  idx_spec(lambda i, *_: (jnp.minimum(i + 1, n_max - 1), 0, 0)),
                  idx_spec(lambda i, *_: (i, 0, 0)),
                  pl.BlockSpec((tm, 2), lambda i, *_: (i, 0)),
                  wspec_in(sel_a), wspec_in(sel_b), wspec_in(sel_a), wspec_in(sel_b),
                  wspec_out(sel_a), wspec_out(sel_b),
                  pl.BlockSpec(memory_space=pl.ANY),
                  pl.BlockSpec(memory_space=pl.ANY)],
        out_specs=pl.BlockSpec(memory_space=pl.ANY),
        scratch_shapes=[pltpu.VMEM((2, tm * SUBLANES, LANES), F32),
                        pltpu.VMEM((2, tm * SUBLANES, LANES), F32),
                        pltpu.SemaphoreType.DMA((2,)),
                        pltpu.SemaphoreType.DMA((2,))])
    return pl.pallas_call(
        functools.partial(_moe_body, tm=tm, n_max=n_max),
        grid_spec=grid_spec,
        out_shape=jax.ShapeDtypeStruct((out_rows * SUBLANES, LANES), F32),
        compiler_params=_cparams(("arbitrary",)),
        name="moe",
    )(tile_ea, tile_eb, nact, n_prompt_rows, n_valid, src_idx, src_idx, dst_idx, w_slot,
      wg_b, wg_b, wu_b, wu_b, wd_b, wd_b, h2_p, h2_s)


def _final_body(x1_ref, moe_ref, mod_ref, gf_ref, y_ref):
    x1 = x1_ref[...]
    tm = x1.shape[0]
    g2 = mod_ref[0][:, 5 * D_MODEL:6 * D_MODEL]
    moe = jnp.concatenate([moe_ref[pl.ds(c, tm, stride=SUBLANES), :] for c in range(D_MODEL // LANES)],
                          axis=1)
    x2 = x1 + g2 * moe
    ms = jnp.mean(x2 * x2, axis=-1, keepdims=True)
    y_ref[...] = (x2 * lax.rsqrt(ms + EPS)) * gf_ref[...]


def _final(x1, moe_out, mod, g_final, *, tm, sample, row_off):
    n = x1.shape[0]
    nt = n // tm
    mrows = mod.shape[1]
    if sample:
        mod_map = lambda i: (0, 0, 0)
    else:
        tiles_per_batch = 4096 // tm
        mod_map = lambda i: (i // tiles_per_batch, 0, 0)
    blk_off = row_off // tm
    return pl.pallas_call(
        _final_body,
        grid=(nt,),
        in_specs=[pl.BlockSpec((tm, D_MODEL), lambda i: (i, 0)),
                  pl.BlockSpec((tm * SUBLANES, LANES), lambda i: (i + blk_off, 0)),
                  pl.BlockSpec((1, mrows, 6 * D_MODEL), mod_map),
                  pl.BlockSpec((1, D_MODEL), lambda i: (0, 0))],
        out_specs=pl.BlockSpec((tm, D_MODEL), lambda i: (i, 0)),
        out_shape=jax.ShapeDtypeStruct((n, D_MODEL), F32),
        compiler_params=_cparams(("arbitrary",)),
        name="final_sample" if sample else "final_prompt",
    )(x1, moe_out, mod, g_final.reshape(1, D_MODEL))


def _bucket_table(n):
    d = np.arange(n)
    max_exact = N_BUCKETS // 2
    nf = np.maximum(d, 1).astype(np.float64)
    large = max_exact + (np.log(nf / max_exact) / math.log(MAX_DISTANCE / max_exact)
                         * (N_BUCKETS - max_exact)).astype(np.int64)
    large = np.minimum(large, N_BUCKETS - 1)
    return np.where(d < max_exact, d, large).astype(np.int32)


def _toeplitz(v, n_rows, n_cols):
    length = n_rows + n_cols - 1
    lead = v.shape[:-1]
    vp = jnp.concatenate([v, jnp.zeros(lead + (1,), v.dtype)], axis=-1)
    skew = jnp.tile(vp, (1,) * len(lead) + (n_rows,))[..., :n_rows * length].reshape(lead + (n_rows, length))
    return skew[..., n_rows - 1:n_rows - 1 + n_cols]


_PAIR_LO = np.array([0, 0, 0, 1, 1, 2], np.int32)
_PAIR_HI = np.array([1, 2, 3, 2, 3, 3], np.int32)


def kernel(x_prompt, x_sample, c_prompt, c_sample, cache_k, cache_v, page_table, w_ada, b_ada, w_in, w_o,
           lam_q1, lam_k1, lam_q2, lam_k2, g_subln, rel_bias, g_sg_ln, b_sg_ln, w_s, b_s, w_rg, b_rg,
           w_re, b_re, w_gate, w_up, w_down, g_final):
    batch, seq, _ = x_prompt.shape
    dec_b, dec_t, _ = x_sample.shape
    n_pages = page_table.shape[1]
    n_p = batch * seq
    n_s = dec_b * dec_t
    n_tot = n_p + n_s
    assert w_in.shape[0] == 1 and cache_k.shape[1] == 1 and seq % T_ATT == 0 and n_pages % PAGES_PER_STEP == 0
    assert n_p % TM_TOK == 0 and n_p % n_s == 0 and n_tot % TM_E == 0 and dec_t == SUBLANES

    w_in_b = w_in[0].astype(BF16)
    w_vt_b = w_in[0][:, 2 * QK_W:3 * QK_W].T.astype(BF16)
    w_o_b = w_o[0].astype(BF16)
    wr_t = jnp.zeros((32, D_MODEL), F32).at[0:N_EG].set(w_rg[0].T).at[N_EG:N_EG + N_EG * EPG].set(w_re[0].T)
    wr_t = wr_t.astype(BF16)
    br = jnp.zeros((32, 1), F32).at[0:N_EG, 0].set(b_rg[0]).at[N_EG:N_EG + N_EG * EPG, 0].set(b_re[0])
    wg_b = w_gate[0].astype(BF16)
    wu_b = w_up[0].astype(BF16)
    wd_b = w_down[0].astype(BF16)
    ws_tril = jnp.tril(w_s[0])
    ws_p = ws_tril.astype(BF16)
    bs_p = b_s[0][:, :, None]
    eye = jnp.eye(dec_b, dtype=F32)
    ws_s = jnp.einsum('ab,gij->gaibj', eye, ws_tril[:, :dec_t, :dec_t]).reshape(
        N_GROUPS_SG, n_s, n_s).astype(BF16)
    bs_s = jnp.tile(b_s[0][:, :dec_t], (1, dec_b))[:, :, None]
    gln = g_sg_ln[0]
    bln = b_sg_ln[0]
    lam = (jnp.exp(jnp.sum(lam_q1[0] * lam_k1[0])) - jnp.exp(jnp.sum(lam_q2[0] * lam_k2[0]))
           + LAM_INIT).reshape(1).astype(F32)

    t = T_ATT
    n_last = PAGES_PER_STEP * PAGE
    n_dist = max(2 * t, n_last + dec_t)
    onehot = np.eye(N_BUCKETS, dtype=np.float32)[_bucket_table(n_dist)]
    ft = jnp.dot(onehot, rel_bias - rel_bias[N_BUCKETS - 1], precision=lax.Precision.HIGHEST).T
    neg = lambda n: jnp.full((N_HEADS, n), NEG, F32)
    bias_diag = _toeplitz(jnp.concatenate([neg(t - 1), ft[:, 0:t]], axis=1), t, t)
    bias_sub = _toeplitz(ft[:, 1:2 * t], t, t)
    bias_t = jnp.stack([bias_sub, bias_diag], axis=1)
    bl = _toeplitz(jnp.flip(ft[:, 1:n_last + dec_t], axis=1), dec_t, n_last)
    head_eq = jnp.eye(N_HEADS, dtype=F32)
    bias_last = (bl[:, None, :, :, None] * head_eq[:, None, None, None, :])
    bias_last = jnp.broadcast_to(bias_last, (N_HEADS, 2, dec_t, n_last, N_HEADS)).reshape(
        2 * N_HEADS * dec_t, n_last * N_HEADS)
    bn = _toeplitz(jnp.concatenate([jnp.flip(ft[:, 0:dec_t], axis=1), neg(dec_t - 1)], axis=1),
                   dec_t, dec_t)
    bn = jnp.where(head_eq[:, None, None, :] > 0, bn[:, :, :, None], NEG)
    bn = jnp.broadcast_to(bn[:, None], (N_HEADS, 2, dec_t, dec_t, N_HEADS)).reshape(
        2 * N_HEADS * dec_t, dec_t * N_HEADS)
    bias_new = jnp.concatenate([bn, jnp.full((bn.shape[0], PAGE - bn.shape[1]), NEG, F32)], axis=1)

    c_all = jnp.concatenate([c_prompt, c_sample, jnp.zeros((4, D_MODEL), F32)], axis=0)
    mod_all = _ada(c_all, w_ada[0], b_ada[0])
    mod_p = mod_all[:batch].reshape(batch, 1, 6 * D_MODEL)
    mod_s = jnp.repeat(mod_all[batch:batch + dec_b], dec_t, axis=0).reshape(1, n_s, 6 * D_MODEL)

    xp = x_prompt.reshape(n_p, D_MODEL)
    xs = x_sample.reshape(n_s, D_MODEL)

    q_p, kf_p, kb_p, vf_p, vt_p, sg_p = _stage_a(xp, mod_p, w_in_b, w_vt_b, ws_p, bs_p, gln, bln,
                                                 tm=TM_TOK, chunk=CHUNK, sample=False)
    q_s, kf_s, vf_s, sg_s, vsn_s = _stage_a(xs, mod_s, w_in_b, w_vt_b, ws_s, bs_s, gln, bln,
                                            tm=n_s, chunk=n_s, sample=True)

    g_col = g_subln[0].reshape(HEAD_W, 1)
    g_row = g_subln[0].reshape(1, HEAD_W)
    o_p = _attn_prompt(lam, q_p, kb_p, vt_p, bias_t, g_col, batch=batch, seq=seq, t=t)
    pad = ((0, 0), (0, PAGE - dec_t * N_HEADS), (0, 0))
    knew = jnp.pad(kf_s.reshape(dec_b, dec_t * N_HEADS, HEAD_W), pad).astype(BF16)
    vnew = jnp.pad(vf_s.reshape(dec_b, dec_t * N_HEADS, HEAD_W), pad).astype(BF16)
    ck = cache_k.reshape(-1, HEAD_W)
    cv = cache_v.reshape(-1, HEAD_W)
    o_s = _attn_sample(page_table.reshape(-1), lam, q_s, knew, vnew, bias_last, bias_new, g_row, ck, cv,
                       dec_b=dec_b, n_pages=n_pages)

    x1_p, h2_p, cls_p, w_p = _stage_c(o_p, sg_p, xp, mod_p, w_o_b, wr_t, br, tm=TM_TOK, sample=False)
    x1_s, h2_s, cls_s, w_s2 = _stage_c(o_s, sg_s, xs, mod_s, w_o_b, wr_t, br, tm=n_s, sample=True)

    tm_e = TM_E
    n_max = n_tot // tm_e + N_CLASSES
    cls = jnp.concatenate([cls_p[0], cls_s[0]])
    w_tok = jnp.concatenate([w_p[0:2], w_s2[0:2]], axis=1)
    counts = jnp.sum((cls[:, None] == jnp.arange(N_CLASSES)[None, :]).astype(jnp.int32), axis=0)
    class_start = jnp.cumsum(counts) - counts
    ntile_c = (counts + tm_e - 1) // tm_e
    tile_end = jnp.cumsum(ntile_c)
    tile_start = tile_end - ntile_c
    nact = tile_end[-1]
    _, tok_sorted, wlo_sorted, whi_sorted = lax.sort(
        (cls, jnp.arange(n_tot, dtype=jnp.int32), w_tok[0], w_tok[1]), num_keys=1, is_stable=True)
    tile_ids = jnp.arange(n_max, dtype=jnp.int32)
    tile_cls = jnp.sum((tile_ids[:, None] >= tile_end[None, :]).astype(jnp.int32), axis=1)
    last_cls = jnp.sum((nact - 1 >= tile_end).astype(jnp.int32))
    tile_cls = jnp.where(tile_ids < nact, tile_cls, last_cls)
    grp = tile_cls // N_PAIRS
    pidx = tile_cls % N_PAIRS
    tile_ea = (grp * EPG + jnp.asarray(_PAIR_LO)[pidx]).astype(jnp.int32)
    tile_eb = (grp * EPG + jnp.asarray(_PAIR_HI)[pidx]).astype(jnp.int32)
    r = jnp.arange(tm_e, dtype=jnp.int32)[None, :]
    rank0 = (tile_ids - tile_start[tile_cls]) * tm_e
    n_valid = jnp.where(tile_ids < nact, jnp.clip(counts[tile_cls] - rank0, 0, tm_e), 0)
    win_off = jnp.clip(class_start[tile_cls] + rank0, 0, n_tot)
    window = lambda a: jax.vmap(lambda o: lax.dynamic_slice(jnp.pad(a, (0, tm_e)), (o,), (tm_e,)))(win_off)
    valid = r < n_valid[:, None]
    tok = window(tok_sorted)
    from_prompt = valid & (tok < n_p)
    src = jnp.where(from_prompt, tok, jnp.where(valid, tok - n_p, 0))
    dst = jnp.where(valid, tok, 0)
    n_prompt_rows = jnp.sum(from_prompt.astype(jnp.int32), axis=1)
    w_slot = jnp.where(valid[:, :, None], jnp.stack([window(wlo_sorted), window(whi_sorted)], axis=-1),
                       0.0).reshape(n_max * tm_e, 2)
    src_idx = src.reshape(n_max, 1, tm_e).astype(jnp.int32)
    dst_idx = dst.reshape(n_max, 1, tm_e).astype(jnp.int32)

    moe_out = _moe(tile_ea, tile_eb, nact.reshape(1).astype(jnp.int32), n_prompt_rows, n_valid,
                   src_idx, dst_idx, w_slot, wg_b, wu_b, wd_b, h2_p, h2_s,
                   tm=tm_e, n_max=n_max, out_rows=n_tot)

    y_p = _final(x1_p, moe_out, mod_p, g_final, tm=TM_TOK, sample=False, row_off=0)
    y_s = _final(x1_s, moe_out, mod_s, g_final, tm=n_s, sample=True, row_off=n_p)

    return (y_p.reshape(batch, seq, D_MODEL),
            y_s.reshape(dec_b, dec_t, D_MODEL),
            kf_p.reshape(batch, 1, seq, N_HEADS, HEAD_W),
            vf_p.reshape(batch, 1, seq, N_HEADS, HEAD_W),
            kf_s.reshape(dec_b, 1, dec_t, N_HEADS, HEAD_W),
            vf_s.reshape(dec_b, 1, dec_t, N_HEADS, HEAD_W),
            vsn_s.reshape(dec_b, 1, dec_t, N_GROUPS_SG, SG_CH))
```

```python
import functools
import math

import numpy as np
import jax
import jax.numpy as jnp
from jax import lax
from jax.experimental import pallas as pl
from jax.experimental.pallas import tpu as pltpu

F32 = jnp.float32
BF16 = jnp.bfloat16

D_MODEL = 1024
N_HEADS = 4
DK = 64
HEAD_W = 128
QK_W = N_HEADS * HEAD_W
N_GROUPS_SG = 4
SG_CH = 128
IN_W = 2560
CHUNK = 128
PAGE = 128
N_BUCKETS = 32
MAX_DISTANCE = 128
N_EG = 4
EPG = 4
N_PAIRS = 6
N_CLASSES = N_EG * N_PAIRS
D_EXPERT = 512
EPS = 1e-6
LAM_INIT = 0.8 - 0.6 * math.exp(-0.3 * 0)
NEG = -1e30
LOG2E = math.log2(math.e)
LANES = 128
SUBLANES = 8

TM_TOK = 512
T_ATT = 512
PAGES_PER_STEP = 16
TM_E = 256
VMEM_LIMIT = 56 * 1024 * 1024


def _cparams(sem):
    return pltpu.CompilerParams(dimension_semantics=sem, vmem_limit_bytes=VMEM_LIMIT)


def _ada_body(c_ref, w_ref, b_ref, o_ref):
    c = c_ref[...]
    a = (c * jax.nn.sigmoid(c)).astype(BF16)
    o_ref[...] = jnp.dot(a, w_ref[...].astype(BF16), preferred_element_type=F32) + b_ref[...]


def _ada(c_all, w_ada, b_ada):
    m = c_all.shape[0]
    n = w_ada.shape[1]
    tn = 1536
    return pl.pallas_call(
        _ada_body,
        grid=(n // tn,),
        in_specs=[pl.BlockSpec((m, D_MODEL), lambda j: (0, 0)),
                  pl.BlockSpec((D_MODEL, tn), lambda j: (0, j)),
                  pl.BlockSpec((1, tn), lambda j: (0, j))],
        out_specs=pl.BlockSpec((m, tn), lambda j: (0, j)),
        out_shape=jax.ShapeDtypeStruct((m, n), F32),
        compiler_params=_cparams(("arbitrary",)),
        name="adaln",
    )(c_all, w_ada, b_ada.reshape(1, n))


def _stage_a_body(x_ref, mod_ref, w_in_ref, w_vt_ref, ws_ref, bs_ref, gln_ref, bln_ref,
                  *out_refs, chunk, sample):
    if sample:
        q_ref, kf_ref, vf_ref, sg_ref, vsn_ref = out_refs
    else:
        q_ref, kf_ref, kb_ref, vf_ref, vt_ref, sg_ref = out_refs
    x = x_ref[...]
    tm = x.shape[0]
    mod = mod_ref[0]
    sh1 = mod[:, 0:D_MODEL]
    sc1 = mod[:, D_MODEL:2 * D_MODEL]
    ms = jnp.mean(x * x, axis=-1, keepdims=True)
    h = (x * lax.rsqrt(ms + EPS)) * (1.0 + sc1) + sh1
    hb = h.astype(BF16)
    z = jnp.dot(hb, w_in_ref[...], preferred_element_type=F32)
    q = z[:, 0:QK_W] * (DK ** -0.5)
    k = z[:, QK_W:2 * QK_W]
    v = z[:, 2 * QK_W:3 * QK_W]
    for hd in range(N_HEADS):
        kf_ref[pl.ds(hd, tm, stride=N_HEADS), :] = k[:, hd * HEAD_W:(hd + 1) * HEAD_W]
        vf_ref[pl.ds(hd, tm, stride=N_HEADS), :] = v[:, hd * HEAD_W:(hd + 1) * HEAD_W]
    if sample:
        q_ref[...] = q
    else:
        q_ref[...] = (q * LOG2E).astype(BF16)
        kb_ref[...] = k.astype(BF16)
        vt = lax.dot_general(w_vt_ref[...], hb, (((1,), (1,)), ((), ())),
                             preferred_element_type=F32)
        vt_ref[0] = vt.astype(BF16)
    u = z[:, 3 * QK_W:4 * QK_W]
    vs = z[:, 4 * QK_W:5 * QK_W]
    for g in range(N_GROUPS_SG):
        lo, hi = g * SG_CH, (g + 1) * SG_CH
        vg = vs[:, lo:hi]
        mu = jnp.mean(vg, axis=-1, keepdims=True)
        dv = vg - mu
        var = jnp.mean(dv * dv, axis=-1, keepdims=True)
        vn = (dv * lax.rsqrt(var + EPS)) * gln_ref[g:g + 1, :] + bln_ref[g:g + 1, :]
        if sample:
            vsn_ref[:, lo:hi] = vn
        vnb = vn.astype(BF16)
        for c in range(tm // chunk):
            r0, r1 = c * chunk, (c + 1) * chunk
            s = jnp.dot(ws_ref[g], vnb[r0:r1], preferred_element_type=F32) + bs_ref[g]
            sg_ref[r0:r1, lo:hi] = (u[r0:r1, lo:hi] * s).astype(BF16)


def _stage_a(x, mod, w_in_b, w_vt_b, ws, bs, gln, bln, *, tm, chunk, sample):
    n = x.shape[0]
    nt = n // tm
    mrows = mod.shape[1]
    row = lambda w, dt: jax.ShapeDtypeStruct((n, w), dt)
    blk = lambda w: pl.BlockSpec((tm, w), lambda i: (i, 0))
    cache_shape = jax.ShapeDtypeStruct((n * N_HEADS, HEAD_W), F32)
    cache_blk = pl.BlockSpec((tm * N_HEADS, HEAD_W), lambda i: (i, 0))
    if sample:
        out_shape = [row(QK_W, F32), cache_shape, cache_shape, row(QK_W, BF16), row(QK_W, F32)]
        out_specs = [blk(QK_W), cache_blk, cache_blk, blk(QK_W), blk(QK_W)]
        mod_map = lambda i: (0, 0, 0)
    else:
        out_shape = [row(QK_W, BF16), cache_shape, row(QK_W, BF16), cache_shape,
                     jax.ShapeDtypeStruct((nt, QK_W, tm), BF16), row(QK_W, BF16)]
        out_specs = [blk(QK_W), cache_blk, blk(QK_W), cache_blk,
                     pl.BlockSpec((1, QK_W, tm), lambda i: (i, 0, 0)), blk(QK_W)]
        tiles_per_batch = 4096 // tm
        mod_map = lambda i: (i // tiles_per_batch, 0, 0)
    full = lambda a: pl.BlockSpec(a.shape, lambda i: (0,) * a.ndim)
    return pl.pallas_call(
        functools.partial(_stage_a_body, chunk=chunk, sample=sample),
        grid=(nt,),
        in_specs=[blk(D_MODEL),
                  pl.BlockSpec((1, mrows, 6 * D_MODEL), mod_map),
                  full(w_in_b), full(w_vt_b), full(ws), full(bs), full(gln), full(bln)],
        out_specs=out_specs,
        out_shape=out_shape,
        compiler_params=_cparams(("arbitrary",)),
        name="stage_a_sample" if sample else "stage_a_prompt",
    )(x, mod, w_in_b, w_vt_b, ws, bs, gln, bln)


ATT_COLS = 256


def _attn_body(lam_ref, q_ref, k_ref, vt_ref, bias_ref, g_ref, o_ref, *scratch, t):
    n_chain = 2 * t // ATT_COLS
    q2_refs, m_refs, l_refs, acc_refs = (scratch[i * n_chain:(i + 1) * n_chain] for i in range(4))
    qi = pl.program_id(2)
    for c in range(n_chain):
        q0 = (c * ATT_COLS) % t
        q = q_ref[q0:q0 + ATT_COLS, :]
        lane = lax.broadcasted_iota(jnp.int32, q.shape, 1)
        keep = (lane < DK) if c < n_chain // 2 else (lane >= DK)
        q2_refs[c][...] = jnp.where(keep, q, jnp.zeros_like(q))
        m_refs[c][...] = jnp.full(m_refs[c].shape, NEG, F32)
        l_refs[c][...] = jnp.zeros(l_refs[c].shape, F32)
        acc_refs[c][...] = jnp.zeros(acc_refs[c].shape, F32)

    def tile(j, bias_idx):
        k = k_ref[pl.ds(pl.multiple_of(j * t, t), t), :]
        vt = vt_ref[j]
        scores = []
        for c in range(n_chain):
            s = lax.dot_general(k, q2_refs[c][...], (((1,), (1,)), ((), ())),
                                preferred_element_type=F32)
            if bias_idx is not None:
                q0 = (c * ATT_COLS) % t
                s = s + bias_ref[0, bias_idx, :, q0:q0 + ATT_COLS]
            scores.append(s)
        for c in range(n_chain):
            s = scores[c]
            m_old = m_refs[c][...]
            m_new = jnp.maximum(m_old, jnp.max(s, axis=0, keepdims=True))
            alpha = jnp.exp2(m_old - m_new)
            p = jnp.exp2(s - m_new)
            l_refs[c][...] = alpha * l_refs[c][...] + jnp.sum(p, axis=0, keepdims=True)
            acc_refs[c][...] = alpha * acc_refs[c][...] + jnp.dot(vt, p.astype(BF16),
                                                                  preferred_element_type=F32)
            m_refs[c][...] = m_new

    def plain(j, carry):
        tile(j, None)
        return carry

    lax.fori_loop(0, jnp.maximum(qi - 1, 0), plain, 0)

    @pl.when(qi >= 1)
    def _():
        tile(qi - 1, 0)

    tile(qi, 1)

    lam = lam_ref[0]
    o_all = jnp.concatenate([acc_refs[c][...] * (1.0 / l_refs[c][...]) for c in range(n_chain)],
                            axis=1)
    o = o_all[:, 0:t] - lam * o_all[:, t:2 * t]
    ms = jnp.mean(o * o, axis=0, keepdims=True)
    on = (o * lax.rsqrt(ms + EPS)) * g_ref[...] * (1.0 - LAM_INIT)
    o_ref[...] = on.T.astype(BF16)


def _attn_prompt(lam, q, kb, vt, bias_t, g_col, *, batch, seq, t):
    nq = seq // t
    n = batch * seq
    n_chain = 2 * t // ATT_COLS
    return pl.pallas_call(
        functools.partial(_attn_body, t=t),
        grid=(batch, N_HEADS, nq),
        in_specs=[pl.BlockSpec(memory_space=pltpu.SMEM),
                  pl.BlockSpec((t, HEAD_W), lambda b, h, i: (b * nq + i, h)),
                  pl.BlockSpec((seq, HEAD_W), lambda b, h, i: (b, h)),
                  pl.BlockSpec((nq, HEAD_W, t), lambda b, h, i: (b, h, 0)),
                  pl.BlockSpec((1, 2, t, t), lambda b, h, i: (h, 0, 0, 0)),
                  pl.BlockSpec((HEAD_W, 1), lambda b, h, i: (0, 0))],
        out_specs=pl.BlockSpec((t, HEAD_W), lambda b, h, i: (b * nq + i, h)),
        out_shape=jax.ShapeDtypeStruct((n, QK_W), BF16),
        scratch_shapes=([pltpu.VMEM((ATT_COLS, HEAD_W), BF16)] * n_chain
                        + [pltpu.VMEM((1, ATT_COLS), F32)] * (2 * n_chain)
                        + [pltpu.VMEM((HEAD_W, ATT_COLS), F32)] * n_chain),
        compiler_params=_cparams(("arbitrary", "arbitrary", "arbitrary")),
        name="attn_prompt",
    )(lam, q, kb, vt, bias_t, g_col)


PAGE_ROWS = PAGE * N_HEADS
TAIL_TOKENS = 2 * PAGE


def _sattn_body(pt_ref, lam_ref, q_ref, knew_ref, vnew_ref, bl_ref, bn_ref, g_ref, ck_ref, cv_ref,
                o_ref, kbuf, vbuf, sem, mask_ref, m_ref, l_ref, acc_ref, *, pages, n_steps, total):
    b = pl.program_id(0)
    s = pl.program_id(1)
    step = b * n_steps + s
    slot = step % 2

    def page_copies(step_idx, sl):
        base = step_idx * pages
        out = []
        for i in range(pages):
            src = pl.ds(pl.multiple_of(pt_ref[base + i] * PAGE_ROWS, PAGE_ROWS), PAGE_ROWS)
            dst = pl.ds(i * PAGE_ROWS, PAGE_ROWS)
            out.append(pltpu.make_async_copy(ck_ref.at[src], kbuf.at[sl, dst], sem.at[sl, 0]))
            out.append(pltpu.make_async_copy(cv_ref.at[src], vbuf.at[sl, dst], sem.at[sl, 1]))
        return out

    @pl.when(step == 0)
    def _():
        for c in page_copies(0, 0):
            c.start()

    @pl.when(step + 1 < total)
    def _():
        for c in page_copies(step + 1, 1 - slot):
            c.start()

    for c in page_copies(step, slot):
        c.wait()

    @pl.when(s == 0)
    def _():
        m_ref[...] = jnp.full(m_ref.shape, NEG, F32)
        l_ref[...] = jnp.zeros(l_ref.shape, F32)
        acc_ref[...] = jnp.zeros(acc_ref.shape, F32)

    @pl.when(step == 0)
    def _():
        row = lax.broadcasted_iota(jnp.int32, mask_ref.shape, 0)
        col = lax.broadcasted_iota(jnp.int32, mask_ref.shape, 1)
        same_head = (col % N_HEADS) == (row // (2 * SUBLANES))
        mask_ref[...] = jnp.where(same_head, 0.0, NEG)

    q = q_ref[...]
    lane = lax.broadcasted_iota(jnp.int32, (SUBLANES, HEAD_W), 1)
    pieces = []
    for h in range(N_HEADS):
        qh = q[:, h * HEAD_W:(h + 1) * HEAD_W]
        pieces += [jnp.where(lane < DK, qh, 0.0), jnp.where(lane >= DK, qh, 0.0)]
    qm = jnp.concatenate(pieces, axis=0).astype(BF16)

    def update(kb, vb, bias):
        sc = lax.dot_general(qm, kb, (((1,), (1,)), ((), ())),
                             preferred_element_type=F32) + bias
        m_old = m_ref[...]
        m_new = jnp.maximum(m_old, jnp.max(sc, axis=1, keepdims=True))
        alpha = jnp.exp(m_old - m_new)
        p = jnp.exp(sc - m_new)
        l_ref[...] = alpha * l_ref[...] + jnp.sum(p, axis=1, keepdims=True)
        acc_ref[...] = alpha * acc_ref[...] + jnp.dot(p.astype(BF16), vb,
                                                      preferred_element_type=F32)
        m_ref[...] = m_new

    is_last = s == n_steps - 1
    head_cols = mask_ref.shape[1] - bl_ref.shape[1]
    update(kbuf[slot].astype(BF16), vbuf[slot].astype(BF16),
           jnp.concatenate([mask_ref[:, :head_cols],
                            mask_ref[:, head_cols:] + bl_ref[...] * is_last.astype(F32)], axis=1))

    @pl.when(is_last)
    def _():
        update(knew_ref[0], vnew_ref[0], bn_ref[...])
        lam = lam_ref[0]
        o_all = acc_ref[...] * (1.0 / l_ref[...])
        for h in range(N_HEADS):
            r = h * 2 * SUBLANES
            o = o_all[r:r + SUBLANES] - lam * o_all[r + SUBLANES:r + 2 * SUBLANES]
            ms = jnp.mean(o * o, axis=-1, keepdims=True)
            o_ref[:, h * HEAD_W:(h + 1) * HEAD_W] = ((o * lax.rsqrt(ms + EPS)) * g_ref[...]
                                                     * (1.0 - LAM_INIT))


def _attn_sample(page_table_flat, lam, q_s, knew, vnew, bias_last, bias_new, g_row, cache_k, cache_v,
                 *, dec_b, n_pages):
    pages = PAGES_PER_STEP
    n_steps = n_pages // pages
    total = dec_b * n_steps
    nq = q_s.shape[0] // dec_b
    n_rows = 2 * N_HEADS * nq
    step_rows = pages * PAGE_ROWS
    grid_spec = pltpu.PrefetchScalarGridSpec(
        num_scalar_prefetch=1,
        grid=(dec_b, n_steps),
        in_specs=[pl.BlockSpec(memory_space=pltpu.SMEM),
                  pl.BlockSpec((nq, QK_W), lambda b, s, pt: (b, 0)),
                  pl.BlockSpec((1, PAGE, HEAD_W), lambda b, s, pt: (b, 0, 0)),
                  pl.BlockSpec((1, PAGE, HEAD_W), lambda b, s, pt: (b, 0, 0)),
                  pl.BlockSpec(bias_last.shape, lambda b, s, pt: (0, 0)),
                  pl.BlockSpec(bias_new.shape, lambda b, s, pt: (0, 0)),
                  pl.BlockSpec((1, HEAD_W), lambda b, s, pt: (0, 0)),
                  pl.BlockSpec(memory_space=pl.ANY),
                  pl.BlockSpec(memory_space=pl.ANY)],
        out_specs=pl.BlockSpec((nq, QK_W), lambda b, s, pt: (b, 0)),
        scratch_shapes=[pltpu.VMEM((2, step_rows, HEAD_W), F32),
                        pltpu.VMEM((2, step_rows, HEAD_W), F32),
                        pltpu.SemaphoreType.DMA((2, 2)),
                        pltpu.VMEM((n_rows, step_rows), F32),
                        pltpu.VMEM((n_rows, 1), F32), pltpu.VMEM((n_rows, 1), F32),
                        pltpu.VMEM((n_rows, HEAD_W), F32)])
    return pl.pallas_call(
        functools.partial(_sattn_body, pages=pages, n_steps=n_steps, total=total),
        grid_spec=grid_spec,
        out_shape=jax.ShapeDtypeStruct(q_s.shape, F32),
        compiler_params=_cparams(("arbitrary", "arbitrary")),
        name="attn_sample",
    )(page_table_flat, lam, q_s, knew, vnew, bias_last, bias_new, g_row, cache_k, cache_v)


def _stage_c_body(o_ref, sg_ref, x_ref, mod_ref, wo_ref, wr_ref, br_ref, x1_ref, h2_ref, cls_ref, w_ref):
    x = x_ref[...]
    tm = x.shape[0]
    mod = mod_ref[0]
    g1 = mod[:, 2 * D_MODEL:3 * D_MODEL]
    sh2 = mod[:, 3 * D_MODEL:4 * D_MODEL]
    sc2 = mod[:, 4 * D_MODEL:5 * D_MODEL]
    mix = (jnp.dot(o_ref[...].astype(BF16), wo_ref[0:QK_W, :], preferred_element_type=F32)
           + jnp.dot(sg_ref[...], wo_ref[QK_W:2 * QK_W, :], preferred_element_type=F32))
    x1 = x + g1 * mix
    x1_ref[...] = x1
    ms = jnp.mean(x1 * x1, axis=-1, keepdims=True)
    h2 = (x1 * lax.rsqrt(ms + EPS)) * (1.0 + sc2) + sh2
    for c in range(D_MODEL // LANES):
        h2_ref[pl.ds(c, tm, stride=SUBLANES), :] = h2[:, c * LANES:(c + 1) * LANES]
    lg = lax.dot_general(wr_ref[...], h2.astype(BF16), (((1,), (1,)), ((), ())),
                         preferred_element_type=F32) + br_ref[...]
    gl = [lg[i:i + 1, :] for i in range(N_EG)]
    el = [lg[N_EG + i:N_EG + i + 1, :] for i in range(N_EG * EPG)]
    gmax = jnp.maximum(jnp.maximum(gl[0], gl[1]), jnp.maximum(gl[2], gl[3]))
    gi = jnp.where(gl[0] == gmax, 0, jnp.where(gl[1] == gmax, 1, jnp.where(gl[2] == gmax, 2, 3)))
    gsum = (jnp.exp(gl[0] - gmax) + jnp.exp(gl[1] - gmax)
            + jnp.exp(gl[2] - gmax) + jnp.exp(gl[3] - gmax))
    gp = 1.0 / gsum
    sel = [jnp.where(gi == 0, el[j], jnp.where(gi == 1, el[EPG + j],
                                               jnp.where(gi == 2, el[2 * EPG + j], el[3 * EPG + j])))
           for j in range(EPG)]
    v0 = jnp.maximum(jnp.maximum(sel[0], sel[1]), jnp.maximum(sel[2], sel[3]))
    i0 = jnp.where(sel[0] == v0, 0, jnp.where(sel[1] == v0, 1, jnp.where(sel[2] == v0, 2, 3)))
    rest = [jnp.where(i0 == j, -3e38, sel[j]) for j in range(EPG)]
    v1 = jnp.maximum(jnp.maximum(rest[0], rest[1]), jnp.maximum(rest[2], rest[3]))
    i1 = jnp.where(rest[0] == v1, 0, jnp.where(rest[1] == v1, 1, jnp.where(rest[2] == v1, 2, 3)))
    e1 = jnp.exp(v1 - v0)
    den = 1.0 / (1.0 + e1)
    tw0 = den * gp
    tw1 = e1 * den * gp
    first_low = i0 < i1
    lo = jnp.where(first_low, i0, i1)
    hi = jnp.where(first_low, i1, i0)
    w_lo = jnp.where(first_low, tw0, tw1)
    w_hi = jnp.where(first_low, tw1, tw0)
    pair = jnp.where(lo == 0, 0, jnp.where(lo == 1, 3, 5)) + hi - lo - 1
    cls = gi * N_PAIRS + pair
    cls_ref[...] = jnp.broadcast_to(cls, cls_ref.shape).astype(jnp.int32)
    w_ref[...] = jnp.concatenate([w_lo, w_hi, jnp.zeros((SUBLANES - 2, tm), F32)], axis=0)


def _stage_c(o, sg, x, mod, wo_b, wr_t, br, *, tm, sample):
    n = x.shape[0]
    nt = n // tm
    mrows = mod.shape[1]
    blk = lambda w: pl.BlockSpec((tm, w), lambda i: (i, 0))
    full = lambda a: pl.BlockSpec(a.shape, lambda i: (0,) * a.ndim)
    if sample:
        mod_map = lambda i: (0, 0, 0)
    else:
        tiles_per_batch = 4096 // tm
        mod_map = lambda i: (i // tiles_per_batch, 0, 0)
    return pl.pallas_call(
        _stage_c_body,
        grid=(nt,),
        in_specs=[blk(QK_W), blk(QK_W), blk(D_MODEL),
                  pl.BlockSpec((1, mrows, 6 * D_MODEL), mod_map),
                  full(wo_b), full(wr_t), full(br)],
        out_specs=[blk(D_MODEL),
                   pl.BlockSpec((tm * SUBLANES, LANES), lambda i: (i, 0)),
                   pl.BlockSpec((SUBLANES, tm), lambda i: (0, i)),
                   pl.BlockSpec((SUBLANES, tm), lambda i: (0, i))],
        out_shape=[jax.ShapeDtypeStruct((n, D_MODEL), F32),
                   jax.ShapeDtypeStruct((n * SUBLANES, LANES), F32),
                   jax.ShapeDtypeStruct((SUBLANES, n), jnp.int32),
                   jax.ShapeDtypeStruct((SUBLANES, n), F32)],
        compiler_params=_cparams(("arbitrary",)),
        name="stage_c_sample" if sample else "stage_c_prompt",
    )(o, sg, x, mod, wo_b, wr_t, br)


def _moe_body(ea_ref, eb_ref, nact_ref, npr_ref, nval_ref, src_ref, src_next_ref, dst_ref, wrow_ref,
              wga_ref, wgb_ref, wua_ref, wub_ref, wda_ref, wdb_ref, h2p_ref, h2s_ref, out_ref,
              xbuf, ybuf, gsem, ssem, *, tm, n_max):
    i = pl.program_id(0)
    nact = nact_ref[0]
    slot = i % 2

    def rows(ref, r):
        return ref.at[pl.ds(pl.multiple_of(r * SUBLANES, SUBLANES), SUBLANES)]

    def gather_start(idx_ref, tile, sl):
        def from_slab(h_ref, priority):
            def body(r, carry):
                pltpu.make_async_copy(rows(h_ref, idx_ref[0, 0, r]), rows(xbuf.at[sl], r),
                                      gsem.at[sl]).start(priority=priority)
                return carry
            return body
        n_prompt_rows = npr_ref[tile]
        half = tm // 2
        lax.fori_loop(0, jnp.minimum(n_prompt_rows, half), from_slab(h2p_ref, 0), 0)
        lax.fori_loop(half, jnp.maximum(n_prompt_rows, half), from_slab(h2p_ref, 1), 0)
        lax.fori_loop(jnp.minimum(n_prompt_rows, half), half, from_slab(h2s_ref, 0), 0)
        lax.fori_loop(jnp.maximum(n_prompt_rows, half), tm, from_slab(h2s_ref, 1), 0)

    def gather_wait(sl):
        pltpu.make_async_copy(h2p_ref.at[pl.ds(0, tm * SUBLANES)], xbuf.at[sl], gsem.at[sl]).wait()

    def scatter_start(tile, sl):
        def to_out(priority):
            def body(r, carry):
                pltpu.make_async_copy(rows(ybuf.at[sl], r), rows(out_ref, dst_ref[0, 0, r]),
                                      ssem.at[sl]).start(priority=priority)
                return carry
            return body
        n_rows = nval_ref[tile]
        half = tm // 2
        lax.fori_loop(0, jnp.minimum(n_rows, half), to_out(0), 0)
        lax.fori_loop(half, jnp.maximum(n_rows, half), to_out(1), 0)

    def scatter_wait(tile, sl):
        n = pl.multiple_of(nval_ref[tile] * SUBLANES, SUBLANES)
        pltpu.make_async_copy(ybuf.at[sl, pl.ds(0, n)], out_ref.at[pl.ds(0, n)], ssem.at[sl]).wait()

    @pl.when(i == 0)
    def _():
        gather_start(src_ref, 0, 0)

    @pl.when(i + 1 < nact)
    def _():
        gather_start(src_next_ref, i + 1, 1 - slot)

    @pl.when(i < nact)
    def _():
        gather_wait(slot)

        @pl.when(i >= 2)
        def _():
            scatter_wait(i - 2, slot)

        xs = xbuf.at[slot]
        x = jnp.concatenate([xs[pl.ds(c, tm, stride=SUBLANES), :] for c in range(D_MODEL // LANES)],
                            axis=1).astype(BF16)
        wrow = wrow_ref[...]

        def expert(wg_ref, wu_ref, wd_ref, wcol):
            gate = jnp.dot(x, wg_ref[0], preferred_element_type=F32)
            up = jnp.dot(x, wu_ref[0], preferred_element_type=F32)
            he = (gate * jax.nn.sigmoid(gate)) * up
            return wcol * jnp.dot(he.astype(BF16), wd_ref[0], preferred_element_type=F32)

        y = (expert(wga_ref, wua_ref, wda_ref, wrow[:, 0:1])
             + expert(wgb_ref, wub_ref, wdb_ref, wrow[:, 1:2]))
        ys = ybuf.at[slot]
        for c in range(D_MODEL // LANES):
            ys[pl.ds(c, tm, stride=SUBLANES), :] = y[:, c * LANES:(c + 1) * LANES]
        scatter_start(i, slot)

    @pl.when(i == n_max - 1)
    def _():
        scatter_wait(nact - 1, (nact - 1) % 2)

        @pl.when(nact >= 2)
        def _():
            scatter_wait(nact - 2, nact % 2)


def _moe(tile_ea, tile_eb, nact, n_prompt_rows, n_valid, src_idx, dst_idx, w_slot, wg_b, wu_b, wd_b,
         h2_p, h2_s, *, tm, n_max, out_rows):
    idx_spec = lambda f: pl.BlockSpec((1, 1, tm), f, memory_space=pltpu.SMEM)
    wspec_in = lambda sel: pl.BlockSpec((1, D_MODEL, D_EXPERT), sel)
    wspec_out = lambda sel: pl.BlockSpec((1, D_EXPERT, D_MODEL), sel)
    sel_a = lambda i, ea, eb, *_: (ea[i], 0, 0)
    sel_b = lambda i, ea, eb, *_: (eb[i], 0, 0)
    grid_spec = pltpu.PrefetchScalarGridSpec(
        num_scalar_prefetch=5,
        grid=(n_max,),
        in_specs=[idx_spec(lambda i, *_: (i, 0, 0)),
                  idx_spec(lambda i, *_: (jnp.minimum(i + 1, n_max - 1), 0, 0)),
                  idx_spec(lambda i, *_: (i, 0, 0)),
                  pl.BlockSpec((tm, 2), lambda i, *_: (i, 0)),
                  wspec_in(sel_a), wspec_in(sel_b), wspec_in(sel_a), wspec_in(sel_b),
                  wspec_out(sel_a), wspec_out(sel_b),
                  pl.BlockSpec(memory_space=pl.ANY),
                  pl.BlockSpec(memory_space=pl.ANY)],
        out_specs=pl.BlockSpec(memory_space=pl.ANY),
        scratch_shapes=[pltpu.VMEM((2, tm * SUBLANES, LANES), F32),
                        pltpu.VMEM((2, tm * SUBLANES, LANES), F32),
                        pltpu.SemaphoreType.DMA((2,)),
                        pltpu.SemaphoreType.DMA((2,))])
    return pl.pallas_call(
        functools.partial(_moe_body, tm=tm, n_max=n_max),
        grid_spec=grid_spec,
        out_shape=jax.ShapeDtypeStruct((out_rows * SUBLANES, LANES), F32),
        compiler_params=_cparams(("arbitrary",)),
        name="moe",
    )(tile_ea, tile_eb, nact, n_prompt_rows, n_valid, src_idx, src_idx, dst_idx, w_slot,
      wg_b, wg_b, wu_b, wu_b, wd_b, wd_b, h2_p, h2_s)


def _final_body(x1_ref, moe_ref, mod_ref, gf_ref, y_ref):
    x1 = x1_ref[...]
    tm = x1.shape[0]
    g2 = mod_ref[0][:, 5 * D_MODEL:6 * D_MODEL]
    moe = jnp.concatenate([moe_ref[pl.ds(c, tm, stride=SUBLANES), :] for c in range(D_MODEL // LANES)],
                          axis=1)
    x2 = x1 + g2 * moe
    ms = jnp.mean(x2 * x2, axis=-1, keepdims=True)
    y_ref[...] = (x2 * lax.rsqrt(ms + EPS)) * gf_ref[...]


def _final(x1, moe_out, mod, g_final, *, tm, sample, row_off):
    n = x1.shape[0]
    nt = n // tm
    mrows = mod.shape[1]
    if sample:
        mod_map = lambda i: (0, 0, 0)
    else:
        tiles_per_batch = 4096 // tm
        mod_map = lambda i: (i // tiles_per_batch, 0, 0)
    blk_off = row_off // tm
    return pl.pallas_call(
        _final_body,
        grid=(nt,),
        in_specs=[pl.BlockSpec((tm, D_MODEL), lambda i: (i, 0)),
                  pl.BlockSpec((tm * SUBLANES, LANES), lambda i: (i + blk_off, 0)),
                  pl.BlockSpec((1, mrows, 6 * D_MODEL), mod_map),
                  pl.BlockSpec((1, D_MODEL), lambda i: (0, 0))],
        out_specs=pl.BlockSpec((tm, D_MODEL), lambda i: (i, 0)),
        out_shape=jax.ShapeDtypeStruct((n, D_MODEL), F32),
        compiler_params=_cparams(("arbitrary",)),
        name="final_sample" if sample else "final_prompt",
    )(x1, moe_out, mod, g_final.reshape(1, D_MODEL))


def _bucket_table(n):
    d = np.arange(n)
    max_exact = N_BUCKETS // 2
    nf = np.maximum(d, 1).astype(np.float64)
    large = max_exact + (np.log(nf / max_exact) / math.log(MAX_DISTANCE / max_exact)
                         * (N_BUCKETS - max_exact)).astype(np.int64)
    large = np.minimum(large, N_BUCKETS - 1)
    return np.where(d < max_exact, d, large).astype(np.int32)


def _toeplitz(v, n_rows, n_cols):
    length = n_rows + n_cols - 1
    lead = v.shape[:-1]
    vp = jnp.concatenate([v, jnp.zeros(lead + (1,), v.dtype)], axis=-1)
    skew = jnp.tile(vp, (1,) * len(lead) + (n_rows,))[..., :n_rows * length].reshape(lead + (n_rows, length))
    return skew[..., n_rows - 1:n_rows - 1 + n_cols]


_PAIR_LO = np.array([0, 0, 0, 1, 1, 2], np.int32)
_PAIR_HI = np.array([1, 2, 3, 2, 3, 3], np.int32)


def kernel(x_prompt, x_sample, c_prompt, c_sample, cache_k, cache_v, page_table, w_ada, b_ada, w_in, w_o,
           lam_q1, lam_k1, lam_q2, lam_k2, g_subln, rel_bias, g_sg_ln, b_sg_ln, w_s, b_s, w_rg, b_rg,
           w_re, b_re, w_gate, w_up, w_down, g_final):
    batch, seq, _ = x_prompt.shape
    dec_b, dec_t, _ = x_sample.shape
    n_pages = page_table.shape[1]
    n_p = batch * seq
    n_s = dec_b * dec_t
    n_tot = n_p + n_s
    assert w_in.shape[0] == 1 and cache_k.shape[1] == 1 and seq % T_ATT == 0 and n_pages % PAGES_PER_STEP == 0
    assert n_p % TM_TOK == 0 and n_p % n_s == 0 and n_tot % TM_E == 0 and dec_t == SUBLANES
    assert TAIL_TOKENS >= MAX_DISTANCE + dec_t and TAIL_TOKENS <= PAGES_PER_STEP * PAGE

    w_in_b = w_in[0].astype(BF16)
    w_vt_b = w_in[0][:, 2 * QK_W:3 * QK_W].T.astype(BF16)
    w_o_b = w_o[0].astype(BF16)
    wr_t = jnp.zeros((32, D_MODEL), F32).at[0:N_EG].set(w_rg[0].T).at[N_EG:N_EG + N_EG * EPG].set(w_re[0].T)
    wr_t = wr_t.astype(BF16)
    br = jnp.zeros((32, 1), F32).at[0:N_EG, 0].set(b_rg[0]).at[N_EG:N_EG + N_EG * EPG, 0].set(b_re[0])
    wg_b = w_gate[0].astype(BF16)
    wu_b = w_up[0].astype(BF16)
    wd_b = w_down[0].astype(BF16)
    ws_tril = jnp.tril(w_s[0])
    ws_p = ws_tril.astype(BF16)
    bs_p = b_s[0][:, :, None]
    eye = jnp.eye(dec_b, dtype=F32)
    ws_s = jnp.einsum('ab,gij->gaibj', eye, ws_tril[:, :dec_t, :dec_t]).reshape(
        N_GROUPS_SG, n_s, n_s).astype(BF16)
    bs_s = jnp.tile(b_s[0][:, :dec_t], (1, dec_b))[:, :, None]
    gln = g_sg_ln[0]
    bln = b_sg_ln[0]
    lam = (jnp.exp(jnp.sum(lam_q1[0] * lam_k1[0])) - jnp.exp(jnp.sum(lam_q2[0] * lam_k2[0]))
           + LAM_INIT).reshape(1).astype(F32)

    t = T_ATT
    n_dist = max(2 * t, TAIL_TOKENS + dec_t)
    onehot = np.eye(N_BUCKETS, dtype=np.float32)[_bucket_table(n_dist)]
    ft = jnp.dot(onehot, rel_bias - rel_bias[N_BUCKETS - 1], precision=lax.Precision.HIGHEST).T
    neg = lambda n: jnp.full((N_HEADS, n), NEG, F32)
    bias_diag = _toeplitz(jnp.concatenate([neg(t - 1), ft[:, 0:t]], axis=1), t, t)
    bias_sub = _toeplitz(ft[:, 1:2 * t], t, t)
    bias_t = jnp.stack([bias_sub, bias_diag], axis=1) * LOG2E
    bl = _toeplitz(jnp.flip(ft[:, 1:TAIL_TOKENS + dec_t], axis=1), dec_t, TAIL_TOKENS)
    head_eq = jnp.eye(N_HEADS, dtype=F32)
    bias_last = (bl[:, None, :, :, None] * head_eq[:, None, None, None, :])
    bias_last = jnp.broadcast_to(bias_last, (N_HEADS, 2, dec_t, TAIL_TOKENS, N_HEADS)).reshape(
        2 * N_HEADS * dec_t, TAIL_TOKENS * N_HEADS)
    bn = _toeplitz(jnp.concatenate([jnp.flip(ft[:, 0:dec_t], axis=1), neg(dec_t - 1)], axis=1),
                   dec_t, dec_t)
    bn = jnp.where(head_eq[:, None, None, :] > 0, bn[:, :, :, None], NEG)
    bn = jnp.broadcast_to(bn[:, None], (N_HEADS, 2, dec_t, dec_t, N_HEADS)).reshape(
        2 * N_HEADS * dec_t, dec_t * N_HEADS)
    bias_new = jnp.concatenate([bn, jnp.full((bn.shape[0], PAGE - bn.shape[1]), NEG, F32)], axis=1)

    c_all = jnp.concatenate([c_prompt, c_sample, jnp.zeros((4, D_MODEL), F32)], axis=0)
    mod_all = _ada(c_all, w_ada[0], b_ada[0])
    mod_p = mod_all[:batch].reshape(batch, 1, 6 * D_MODEL)
    mod_s = jnp.repeat(mod_all[batch:batch + dec_b], dec_t, axis=0).reshape(1, n_s, 6 * D_MODEL)

    xp = x_prompt.reshape(n_p, D_MODEL)
    xs = x_sample.reshape(n_s, D_MODEL)

    q_p, kf_p, kb_p, vf_p, vt_p, sg_p = _stage_a(xp, mod_p, w_in_b, w_vt_b, ws_p, bs_p, gln, bln,
                                                 tm=TM_TOK, chunk=CHUNK, sample=False)
    q_s, kf_s, vf_s, sg_s, vsn_s = _stage_a(xs, mod_s, w_in_b, w_vt_b, ws_s, bs_s, gln, bln,
                                            tm=n_s, chunk=n_s, sample=True)

    g_col = g_subln[0].reshape(HEAD_W, 1)
    g_row = g_subln[0].reshape(1, HEAD_W)
    o_p = _attn_prompt(lam, q_p, kb_p, vt_p, bias_t, g_col, batch=batch, seq=seq, t=t)
    pad = ((0, 0), (0, PAGE - dec_t * N_HEADS), (0, 0))
    knew = jnp.pad(kf_s.reshape(dec_b, dec_t * N_HEADS, HEAD_W), pad).astype(BF16)
    vnew = jnp.pad(vf_s.reshape(dec_b, dec_t * N_HEADS, HEAD_W), pad).astype(BF16)
    ck = cache_k.reshape(-1, HEAD_W)
    cv = cache_v.reshape(-1, HEAD_W)
    o_s = _attn_sample(page_table.reshape(-1), lam, q_s, knew, vnew, bias_last, bias_new, g_row, ck, cv,
                       dec_b=dec_b, n_pages=n_pages)

    x1_p, h2_p, cls_p, w_p = _stage_c(o_p, sg_p, xp, mod_p, w_o_b, wr_t, br, tm=TM_TOK, sample=False)
    x1_s, h2_s, cls_s, w_s2 = _stage_c(o_s, sg_s, xs, mod_s, w_o_b, wr_t, br, tm=n_s, sample=True)

    tm_e = TM_E
    n_max = n_tot // tm_e + N_CLASSES
    cls = jnp.concatenate([cls_p[0], cls_s[0]])
    w_tok = jnp.concatenate([w_p[0:2], w_s2[0:2]], axis=1)
    counts = jnp.sum((cls[:, None] == jnp.arange(N_CLASSES)[None, :]).astype(jnp.int32), axis=0)
    class_start = jnp.cumsum(counts) - counts
    ntile_c = (counts + tm_e - 1) // tm_e
    tile_end = jnp.cumsum(ntile_c)
    tile_start = tile_end - ntile_c
    nact = tile_end[-1]
    _, tok_sorted, wlo_sorted, whi_sorted = lax.sort(
        (cls, jnp.arange(n_tot, dtype=jnp.int32), w_tok[0], w_tok[1]), num_keys=1, is_stable=True)
    tile_ids = jnp.arange(n_max, dtype=jnp.int32)
    tile_cls = jnp.sum((tile_ids[:, None] >= tile_end[None, :]).astype(jnp.int32), axis=1)
    last_cls = jnp.sum((nact - 1 >= tile_end).astype(jnp.int32))
    tile_cls = jnp.where(tile_ids < nact, tile_cls, last_cls)
    grp = tile_cls // N_PAIRS
    pidx = tile_cls % N_PAIRS
    tile_ea = (grp * EPG + jnp.asarray(_PAIR_LO)[pidx]).astype(jnp.int32)
    tile_eb = (grp * EPG + jnp.asarray(_PAIR_HI)[pidx]).astype(jnp.int32)
    r = jnp.arange(tm_e, dtype=jnp.int32)[None, :]
    rank0 = (tile_ids - tile_start[tile_cls]) * tm_e
    n_valid = jnp.where(tile_ids < nact, jnp.clip(counts[tile_cls] - rank0, 0, tm_e), 0)
    win_idx = jnp.clip(class_start[tile_cls] + rank0, 0, n_tot)[:, None] + r
    window = lambda a: jnp.pad(a, (0, tm_e))[win_idx]
    valid = r < n_valid[:, None]
    tok = window(tok_sorted)
    from_prompt = valid & (tok < n_p)
    src = jnp.where(from_prompt, tok, jnp.where(valid, tok - n_p, 0))
    dst = jnp.where(valid, tok, 0)
    n_prompt_rows = jnp.sum(from_prompt.astype(jnp.int32), axis=1)
    w_slot = jnp.where(valid[:, :, None], jnp.stack([window(wlo_sorted), window(whi_sorted)], axis=-1),
                       0.0).reshape(n_max * tm_e, 2)
    src_idx = src.reshape(n_max, 1, tm_e).astype(jnp.int32)
    dst_idx = dst.reshape(n_max, 1, tm_e).astype(jnp.int32)

    moe_out = _moe(tile_ea, tile_eb, nact.reshape(1).astype(jnp.int32), n_prompt_rows, n_valid,
                   src_idx, dst_idx, w_slot, wg_b, wu_b, wd_b, h2_p, h2_s,
                   tm=tm_e, n_max=n_max, out_rows=n_tot)

    y_p = _final(x1_p, moe_out, mod_p, g_final, tm=TM_TOK, sample=False, row_off=0)
    y_s = _final(x1_s, moe_out, mod_s, g_final, tm=n_s, sample=True, row_off=n_p)

    return (y_p.reshape(batch, seq, D_MODEL),
            y_s.reshape(dec_b, dec_t, D_MODEL),
            kf_p.reshape(batch, 1, seq, N_HEADS, HEAD_W),
            vf_p.reshape(batch, 1, seq, N_HEADS, HEAD_W),
            kf_s.reshape(dec_b, 1, dec_t, N_HEADS, HEAD_W),
            vf_s.reshape(dec_b, 1, dec_t, N_HEADS, HEAD_W),
            vsn_s.reshape(dec_b, 1, dec_t, N_GROUPS_SG, SG_CH))
```

```python
import functools
import math

import numpy as np
import jax
import jax.numpy as jnp
from jax import lax
from jax.experimental import pallas as pl
from jax.experimental.pallas import tpu as pltpu

F32 = jnp.float32
BF16 = jnp.bfloat16

D_MODEL = 1024
N_HEADS = 4
DK = 64
HEAD_W = 128
QK_W = N_HEADS * HEAD_W
N_GROUPS_SG = 4
SG_CH = 128
IN_W = 2560
CHUNK = 128
PAGE = 128
N_BUCKETS = 32
MAX_DISTANCE = 128
N_EG = 4
EPG = 4
N_PAIRS = 6
N_CLASSES = N_EG * N_PAIRS
D_EXPERT = 512
EPS = 1e-6
LAM_INIT = 0.8 - 0.6 * math.exp(-0.3 * 0)
NEG = -1e30
LOG2E = math.log2(math.e)
LANES = 128
SUBLANES = 8

TM_TOK = 512
T_ATT = 512
PAGES_PER_STEP = 16
TM_E = 256
TM_D = 256
VMEM_LIMIT = 56 * 1024 * 1024


def _cparams(sem):
    return pltpu.CompilerParams(dimension_semantics=sem, vmem_limit_bytes=VMEM_LIMIT)


def _ada_body(c_ref, w_ref, b_ref, o_ref):
    c = c_ref[...]
    a = (c * jax.nn.sigmoid(c)).astype(BF16)
    o_ref[...] = jnp.dot(a, w_ref[...].astype(BF16), preferred_element_type=F32) + b_ref[...]


def _ada(c_all, w_ada, b_ada):
    m = c_all.shape[0]
    n = w_ada.shape[1]
    tn = 1536
    return pl.pallas_call(
        _ada_body,
        grid=(n // tn,),
        in_specs=[pl.BlockSpec((m, D_MODEL), lambda j: (0, 0)),
                  pl.BlockSpec((D_MODEL, tn), lambda j: (0, j)),
                  pl.BlockSpec((1, tn), lambda j: (0, j))],
        out_specs=pl.BlockSpec((m, tn), lambda j: (0, j)),
        out_shape=jax.ShapeDtypeStruct((m, n), F32),
        compiler_params=_cparams(("arbitrary",)),
        name="adaln",
    )(c_all, w_ada, b_ada.reshape(1, n))


def _stage_a_body(x_ref, mod_ref, w_in_ref, w_vt_ref, ws_ref, bs_ref, gln_ref, bln_ref,
                  *out_refs, chunk, sample):
    if sample:
        q_ref, kf_ref, vf_ref, sg_ref, vsn_ref = out_refs
    else:
        q_ref, kf_ref, kb_ref, vf_ref, vt_ref, sg_ref = out_refs
    x = x_ref[...]
    tm = x.shape[0]
    mod = mod_ref[0]
    sh1 = mod[:, 0:D_MODEL]
    sc1 = mod[:, D_MODEL:2 * D_MODEL]
    ms = jnp.mean(x * x, axis=-1, keepdims=True)
    h = (x * lax.rsqrt(ms + EPS)) * (1.0 + sc1) + sh1
    hb = h.astype(BF16)
    z = jnp.dot(hb, w_in_ref[...], preferred_element_type=F32)
    q = z[:, 0:QK_W] * (DK ** -0.5)
    k = z[:, QK_W:2 * QK_W]
    v = z[:, 2 * QK_W:3 * QK_W]
    for hd in range(N_HEADS):
        kf_ref[pl.ds(hd, tm, stride=N_HEADS), :] = k[:, hd * HEAD_W:(hd + 1) * HEAD_W]
        vf_ref[pl.ds(hd, tm, stride=N_HEADS), :] = v[:, hd * HEAD_W:(hd + 1) * HEAD_W]
    if sample:
        q_ref[...] = q
    else:
        q_ref[...] = (q * LOG2E).astype(BF16)
        kb_ref[...] = k.astype(BF16)
        vt = lax.dot_general(w_vt_ref[...], hb, (((1,), (1,)), ((), ())),
                             preferred_element_type=F32)
        vt_ref[0] = vt.astype(BF16)
    u = z[:, 3 * QK_W:4 * QK_W]
    vs = z[:, 4 * QK_W:5 * QK_W]
    for g in range(N_GROUPS_SG):
        lo, hi = g * SG_CH, (g + 1) * SG_CH
        vg = vs[:, lo:hi]
        mu = jnp.mean(vg, axis=-1, keepdims=True)
        dv = vg - mu
        var = jnp.mean(dv * dv, axis=-1, keepdims=True)
        vn = (dv * lax.rsqrt(var + EPS)) * gln_ref[g:g + 1, :] + bln_ref[g:g + 1, :]
        if sample:
            vsn_ref[:, lo:hi] = vn
        vnb = vn.astype(BF16)
        for c in range(tm // chunk):
            r0, r1 = c * chunk, (c + 1) * chunk
            s = jnp.dot(ws_ref[g], vnb[r0:r1], preferred_element_type=F32) + bs_ref[g]
            sg_ref[r0:r1, lo:hi] = (u[r0:r1, lo:hi] * s).astype(BF16)


def _stage_a(x, mod, w_in_b, w_vt_b, ws, bs, gln, bln, *, tm, chunk, sample):
    n = x.shape[0]
    nt = n // tm
    mrows = mod.shape[1]
    row = lambda w, dt: jax.ShapeDtypeStruct((n, w), dt)
    blk = lambda w: pl.BlockSpec((tm, w), lambda i: (i, 0))
    cache_shape = jax.ShapeDtypeStruct((n * N_HEADS, HEAD_W), F32)
    cache_blk = pl.BlockSpec((tm * N_HEADS, HEAD_W), lambda i: (i, 0))
    if sample:
        out_shape = [row(QK_W, F32), cache_shape, cache_shape, row(QK_W, BF16), row(QK_W, F32)]
        out_specs = [blk(QK_W), cache_blk, cache_blk, blk(QK_W), blk(QK_W)]
        mod_map = lambda i: (0, 0, 0)
    else:
        out_shape = [row(QK_W, BF16), cache_shape, row(QK_W, BF16), cache_shape,
                     jax.ShapeDtypeStruct((nt, QK_W, tm), BF16), row(QK_W, BF16)]
        out_specs = [blk(QK_W), cache_blk, blk(QK_W), cache_blk,
                     pl.BlockSpec((1, QK_W, tm), lambda i: (i, 0, 0)), blk(QK_W)]
        tiles_per_batch = 4096 // tm
        mod_map = lambda i: (i // tiles_per_batch, 0, 0)
    full = lambda a: pl.BlockSpec(a.shape, lambda i: (0,) * a.ndim)
    return pl.pallas_call(
        functools.partial(_stage_a_body, chunk=chunk, sample=sample),
        grid=(nt,),
        in_specs=[blk(D_MODEL),
                  pl.BlockSpec((1, mrows, 6 * D_MODEL), mod_map),
                  full(w_in_b), full(w_vt_b), full(ws), full(bs), full(gln), full(bln)],
        out_specs=out_specs,
        out_shape=out_shape,
        compiler_params=_cparams(("arbitrary",)),
        name="stage_a_sample" if sample else "stage_a_prompt",
    )(x, mod, w_in_b, w_vt_b, ws, bs, gln, bln)


ATT_COLS = 256


def _attn_body(lam_ref, q_ref, k_ref, vt_ref, bias_ref, g_ref, o_ref, *scratch, t):
    n_chain = 2 * t // ATT_COLS
    q2_refs, m_refs, l_refs, acc_refs = (scratch[i * n_chain:(i + 1) * n_chain] for i in range(4))
    qi = pl.program_id(2)
    for c in range(n_chain):
        q0 = (c * ATT_COLS) % t
        q = q_ref[q0:q0 + ATT_COLS, :]
        lane = lax.broadcasted_iota(jnp.int32, q.shape, 1)
        keep = (lane < DK) if c < n_chain // 2 else (lane >= DK)
        q2_refs[c][...] = jnp.where(keep, q, jnp.zeros_like(q))
        m_refs[c][...] = jnp.full(m_refs[c].shape, NEG, F32)
        l_refs[c][...] = jnp.zeros(l_refs[c].shape, F32)
        acc_refs[c][...] = jnp.zeros(acc_refs[c].shape, F32)

    def tile(j, bias_idx):
        k = k_ref[pl.ds(pl.multiple_of(j * t, t), t), :]
        vt = vt_ref[j]
        scores = []
        for c in range(n_chain):
            s = lax.dot_general(k, q2_refs[c][...], (((1,), (1,)), ((), ())),
                                preferred_element_type=F32)
            if bias_idx is not None:
                q0 = (c * ATT_COLS) % t
                s = s + bias_ref[0, bias_idx, :, q0:q0 + ATT_COLS]
            scores.append(s)
        for c in range(n_chain):
            s = scores[c]
            m_old = m_refs[c][...]
            m_new = jnp.maximum(m_old, jnp.max(s, axis=0, keepdims=True))
            alpha = jnp.exp2(m_old - m_new)
            p = jnp.exp2(s - m_new)
            l_refs[c][...] = alpha * l_refs[c][...] + jnp.sum(p, axis=0, keepdims=True)
            acc_refs[c][...] = alpha * acc_refs[c][...] + jnp.dot(vt, p.astype(BF16),
                                                                  preferred_element_type=F32)
            m_refs[c][...] = m_new

    def plain(j, carry):
        tile(j, None)
        return carry

    lax.fori_loop(0, jnp.maximum(qi - 1, 0), plain, 0)

    @pl.when(qi >= 1)
    def _():
        tile(qi - 1, 0)

    tile(qi, 1)

    lam = lam_ref[0]
    o_all = jnp.concatenate([acc_refs[c][...] * (1.0 / l_refs[c][...]) for c in range(n_chain)],
                            axis=1)
    o = o_all[:, 0:t] - lam * o_all[:, t:2 * t]
    ms = jnp.mean(o * o, axis=0, keepdims=True)
    on = (o * lax.rsqrt(ms + EPS)) * g_ref[...] * (1.0 - LAM_INIT)
    o_ref[...] = on.T.astype(BF16)


def _attn_prompt(lam, q, kb, vt, bias_t, g_col, *, batch, seq, t):
    nq = seq // t
    n = batch * seq
    n_chain = 2 * t // ATT_COLS
    return pl.pallas_call(
        functools.partial(_attn_body, t=t),
        grid=(batch, N_HEADS, nq),
        in_specs=[pl.BlockSpec(memory_space=pltpu.SMEM),
                  pl.BlockSpec((t, HEAD_W), lambda b, h, i: (b * nq + i, h)),
                  pl.BlockSpec((seq, HEAD_W), lambda b, h, i: (b, h)),
                  pl.BlockSpec((nq, HEAD_W, t), lambda b, h, i: (b, h, 0)),
                  pl.BlockSpec((1, 2, t, t), lambda b, h, i: (h, 0, 0, 0)),
                  pl.BlockSpec((HEAD_W, 1), lambda b, h, i: (0, 0))],
        out_specs=pl.BlockSpec((t, HEAD_W), lambda b, h, i: (b * nq + i, h)),
        out_shape=jax.ShapeDtypeStruct((n, QK_W), BF16),
        scratch_shapes=([pltpu.VMEM((ATT_COLS, HEAD_W), BF16)] * n_chain
                        + [pltpu.VMEM((1, ATT_COLS), F32)] * (2 * n_chain)
                        + [pltpu.VMEM((HEAD_W, ATT_COLS), F32)] * n_chain),
        compiler_params=_cparams(("arbitrary", "arbitrary", "arbitrary")),
        name="attn_prompt",
    )(lam, q, kb, vt, bias_t, g_col)


PAGE_ROWS = PAGE * N_HEADS
TAIL_TOKENS = 2 * PAGE


def _sattn_body(pt_ref, lam_ref, q_ref, knew_ref, vnew_ref, bl_ref, bn_ref, g_ref, ck_ref, cv_ref,
                o_ref, kbuf, vbuf, sem, mask_ref, m_ref, l_ref, acc_ref, *, pages, n_steps, total):
    b = pl.program_id(0)
    s = pl.program_id(1)
    step = b * n_steps + s
    slot = step % 2

    def page_copies(step_idx, sl):
        base = step_idx * pages
        out = []
        for i in range(pages):
            src = pl.ds(pl.multiple_of(pt_ref[base + i] * PAGE_ROWS, PAGE_ROWS), PAGE_ROWS)
            dst = pl.ds(i * PAGE_ROWS, PAGE_ROWS)
            out.append(pltpu.make_async_copy(ck_ref.at[src], kbuf.at[sl, dst], sem.at[sl, 0]))
            out.append(pltpu.make_async_copy(cv_ref.at[src], vbuf.at[sl, dst], sem.at[sl, 1]))
        return out

    @pl.when(step == 0)
    def _():
        for c in page_copies(0, 0):
            c.start()

    @pl.when(step + 1 < total)
    def _():
        for c in page_copies(step + 1, 1 - slot):
            c.start()

    for c in page_copies(step, slot):
        c.wait()

    @pl.when(s == 0)
    def _():
        m_ref[...] = jnp.full(m_ref.shape, NEG, F32)
        l_ref[...] = jnp.zeros(l_ref.shape, F32)
        acc_ref[...] = jnp.zeros(acc_ref.shape, F32)

    @pl.when(step == 0)
    def _():
        row = lax.broadcasted_iota(jnp.int32, mask_ref.shape, 0)
        col = lax.broadcasted_iota(jnp.int32, mask_ref.shape, 1)
        same_head = (col % N_HEADS) == (row // (2 * SUBLANES))
        mask_ref[...] = jnp.where(same_head, 0.0, NEG)

    q = q_ref[...]
    lane = lax.broadcasted_iota(jnp.int32, (SUBLANES, HEAD_W), 1)
    pieces = []
    for h in range(N_HEADS):
        qh = q[:, h * HEAD_W:(h + 1) * HEAD_W]
        pieces += [jnp.where(lane < DK, qh, 0.0), jnp.where(lane >= DK, qh, 0.0)]
    qm = jnp.concatenate(pieces, axis=0).astype(BF16)

    def update(kb, vb, bias):
        sc = lax.dot_general(qm, kb, (((1,), (1,)), ((), ())),
                             preferred_element_type=F32) + bias
        m_old = m_ref[...]
        m_new = jnp.maximum(m_old, jnp.max(sc, axis=1, keepdims=True))
        alpha = jnp.exp(m_old - m_new)
        p = jnp.exp(sc - m_new)
        l_ref[...] = alpha * l_ref[...] + jnp.sum(p, axis=1, keepdims=True)
        acc_ref[...] = alpha * acc_ref[...] + jnp.dot(p.astype(BF16), vb,
                                                      preferred_element_type=F32)
        m_ref[...] = m_new

    is_last = s == n_steps - 1
    head_cols = mask_ref.shape[1] - bl_ref.shape[1]
    update(kbuf[slot].astype(BF16), vbuf[slot].astype(BF16),
           jnp.concatenate([mask_ref[:, :head_cols],
                            mask_ref[:, head_cols:] + bl_ref[...] * is_last.astype(F32)], axis=1))

    @pl.when(is_last)
    def _():
        update(knew_ref[0], vnew_ref[0], bn_ref[...])
        lam = lam_ref[0]
        o_all = acc_ref[...] * (1.0 / l_ref[...])
        for h in range(N_HEADS):
            r = h * 2 * SUBLANES
            o = o_all[r:r + SUBLANES] - lam * o_all[r + SUBLANES:r + 2 * SUBLANES]
            ms = jnp.mean(o * o, axis=-1, keepdims=True)
            o_ref[:, h * HEAD_W:(h + 1) * HEAD_W] = ((o * lax.rsqrt(ms + EPS)) * g_ref[...]
                                                     * (1.0 - LAM_INIT))


def _attn_sample(page_table_flat, lam, q_s, knew, vnew, bias_last, bias_new, g_row, cache_k, cache_v,
                 *, dec_b, n_pages):
    pages = PAGES_PER_STEP
    n_steps = n_pages // pages
    total = dec_b * n_steps
    nq = q_s.shape[0] // dec_b
    n_rows = 2 * N_HEADS * nq
    step_rows = pages * PAGE_ROWS
    grid_spec = pltpu.PrefetchScalarGridSpec(
        num_scalar_prefetch=1,
        grid=(dec_b, n_steps),
        in_specs=[pl.BlockSpec(memory_space=pltpu.SMEM),
                  pl.BlockSpec((nq, QK_W), lambda b, s, pt: (b, 0)),
                  pl.BlockSpec((1, PAGE, HEAD_W), lambda b, s, pt: (b, 0, 0)),
                  pl.BlockSpec((1, PAGE, HEAD_W), lambda b, s, pt: (b, 0, 0)),
                  pl.BlockSpec(bias_last.shape, lambda b, s, pt: (0, 0)),
                  pl.BlockSpec(bias_new.shape, lambda b, s, pt: (0, 0)),
                  pl.BlockSpec((1, HEAD_W), lambda b, s, pt: (0, 0)),
                  pl.BlockSpec(memory_space=pl.ANY),
                  pl.BlockSpec(memory_space=pl.ANY)],
        out_specs=pl.BlockSpec((nq, QK_W), lambda b, s, pt: (b, 0)),
        scratch_shapes=[pltpu.VMEM((2, step_rows, HEAD_W), F32),
                        pltpu.VMEM((2, step_rows, HEAD_W), F32),
                        pltpu.SemaphoreType.DMA((2, 2)),
                        pltpu.VMEM((n_rows, step_rows), F32),
                        pltpu.VMEM((n_rows, 1), F32), pltpu.VMEM((n_rows, 1), F32),
                        pltpu.VMEM((n_rows, HEAD_W), F32)])
    return pl.pallas_call(
        functools.partial(_sattn_body, pages=pages, n_steps=n_steps, total=total),
        grid_spec=grid_spec,
        out_shape=jax.ShapeDtypeStruct(q_s.shape, F32),
        compiler_params=_cparams(("arbitrary", "arbitrary")),
        name="attn_sample",
    )(page_table_flat, lam, q_s, knew, vnew, bias_last, bias_new, g_row, cache_k, cache_v)


INFO_W_LO, INFO_W_HI, INFO_CLS = 0, 1, 2


def _stage_c_body(o_ref, sg_ref, x_ref, mod_ref, wo_ref, wr_ref, br_ref, x1_ref, h2_ref, cls_ref, info_ref):
    x = x_ref[...]
    tm = x.shape[0]
    mod = mod_ref[0]
    g1 = mod[:, 2 * D_MODEL:3 * D_MODEL]
    sh2 = mod[:, 3 * D_MODEL:4 * D_MODEL]
    sc2 = mod[:, 4 * D_MODEL:5 * D_MODEL]
    mix = (jnp.dot(o_ref[...].astype(BF16), wo_ref[0:QK_W, :], preferred_element_type=F32)
           + jnp.dot(sg_ref[...], wo_ref[QK_W:2 * QK_W, :], preferred_element_type=F32))
    x1 = x + g1 * mix
    x1_ref[...] = x1
    ms = jnp.mean(x1 * x1, axis=-1, keepdims=True)
    h2 = ((x1 * lax.rsqrt(ms + EPS)) * (1.0 + sc2) + sh2).astype(BF16)
    h2_ref[...] = h2
    lg = lax.dot_general(wr_ref[...], h2, (((1,), (1,)), ((), ())),
                         preferred_element_type=F32) + br_ref[...]
    gl = [lg[i:i + 1, :] for i in range(N_EG)]
    el = [lg[N_EG + i:N_EG + i + 1, :] for i in range(N_EG * EPG)]
    gmax = jnp.maximum(jnp.maximum(gl[0], gl[1]), jnp.maximum(gl[2], gl[3]))
    gi = jnp.where(gl[0] == gmax, 0, jnp.where(gl[1] == gmax, 1, jnp.where(gl[2] == gmax, 2, 3)))
    gsum = (jnp.exp(gl[0] - gmax) + jnp.exp(gl[1] - gmax)
            + jnp.exp(gl[2] - gmax) + jnp.exp(gl[3] - gmax))
    gp = 1.0 / gsum
    sel = [jnp.where(gi == 0, el[j], jnp.where(gi == 1, el[EPG + j],
                                               jnp.where(gi == 2, el[2 * EPG + j], el[3 * EPG + j])))
           for j in range(EPG)]
    v0 = jnp.maximum(jnp.maximum(sel[0], sel[1]), jnp.maximum(sel[2], sel[3]))
    i0 = jnp.where(sel[0] == v0, 0, jnp.where(sel[1] == v0, 1, jnp.where(sel[2] == v0, 2, 3)))
    rest = [jnp.where(i0 == j, -3e38, sel[j]) for j in range(EPG)]
    v1 = jnp.maximum(jnp.maximum(rest[0], rest[1]), jnp.maximum(rest[2], rest[3]))
    i1 = jnp.where(rest[0] == v1, 0, jnp.where(rest[1] == v1, 1, jnp.where(rest[2] == v1, 2, 3)))
    e1 = jnp.exp(v1 - v0)
    den = 1.0 / (1.0 + e1)
    tw0 = den * gp
    tw1 = e1 * den * gp
    first_low = i0 < i1
    lo = jnp.where(first_low, i0, i1)
    hi = jnp.where(first_low, i1, i0)
    w_lo = jnp.where(first_low, tw0, tw1)
    w_hi = jnp.where(first_low, tw1, tw0)
    pair = jnp.where(lo == 0, 0, jnp.where(lo == 1, 3, 5)) + hi - lo - 1
    cls = gi * N_PAIRS + pair
    cls_ref[...] = jnp.broadcast_to(cls, cls_ref.shape).astype(jnp.int32)
    row = lax.broadcasted_iota(jnp.int32, (LANES, tm), 0)
    rec = jnp.where(row == INFO_W_LO, w_lo, jnp.where(row == INFO_W_HI, w_hi,
                                                      jnp.where(row == INFO_CLS, cls.astype(F32), 0.0)))
    info_ref[...] = rec.T


def _stage_c(o, sg, x, mod, wo_b, wr_t, br, *, tm, sample):
    n = x.shape[0]
    nt = n // tm
    mrows = mod.shape[1]
    blk = lambda w: pl.BlockSpec((tm, w), lambda i: (i, 0))
    full = lambda a: pl.BlockSpec(a.shape, lambda i: (0,) * a.ndim)
    if sample:
        mod_map = lambda i: (0, 0, 0)
    else:
        tiles_per_batch = 4096 // tm
        mod_map = lambda i: (i // tiles_per_batch, 0, 0)
    return pl.pallas_call(
        _stage_c_body,
        grid=(nt,),
        in_specs=[blk(QK_W), blk(QK_W), blk(D_MODEL),
                  pl.BlockSpec((1, mrows, 6 * D_MODEL), mod_map),
                  full(wo_b), full(wr_t), full(br)],
        out_specs=[blk(D_MODEL), blk(D_MODEL),
                   pl.BlockSpec((SUBLANES, tm), lambda i: (0, i)),
                   blk(LANES)],
        out_shape=[jax.ShapeDtypeStruct((n, D_MODEL), F32),
                   jax.ShapeDtypeStruct((n, D_MODEL), BF16),
                   jax.ShapeDtypeStruct((SUBLANES, n), jnp.int32),
                   jax.ShapeDtypeStruct((n, LANES), F32)],
        compiler_params=_cparams(("arbitrary",)),
        name="stage_c_sample" if sample else "stage_c_prompt",
    )(o, sg, x, mod, wo_b, wr_t, br)


def rows8(ref, start, count):
    scale = lambda v: v * SUBLANES if isinstance(v, int) else pl.multiple_of(v * SUBLANES, SUBLANES)
    return ref.at[pl.ds(scale(start), scale(count))]


def _perm_t(cls_col):
    n = cls_col.shape[0]
    lane = lax.broadcasted_iota(jnp.int32, (n, LANES), 1).astype(F32)
    onehot = (lane == cls_col).astype(BF16)
    r = lax.broadcasted_iota(jnp.int32, (n, n), 0)
    c = lax.broadcasted_iota(jnp.int32, (n, n), 1)
    before = (c < r).astype(BF16)
    rank = jnp.dot(before, onehot, preferred_element_type=F32)
    cnt = jnp.sum(onehot.astype(F32), axis=0, keepdims=True)
    cr = lax.broadcasted_iota(jnp.int32, (LANES, LANES), 0)
    cc = lax.broadcasted_iota(jnp.int32, (LANES, LANES), 1)
    lower_cls = (cr < cc).astype(BF16)
    base = jnp.dot(jnp.broadcast_to(cnt, (SUBLANES, LANES)).astype(BF16), lower_cls,
                   preferred_element_type=F32)[0:1, :]
    pos = jnp.sum(onehot.astype(F32) * (base + rank), axis=1, keepdims=True)
    dest = lax.broadcasted_iota(jnp.int32, (n, n), 1).astype(F32)
    return (dest == pos).astype(F32)


def _dispatch_body(soff_ref, slen_ref, poff_ref, plen_ref, ptot_ref, nact_ref, hp_ref, hs_ref, ip_ref, is_ref,
                   xs_ref, buf, zbuf, sem, zsem, *, tm, n_tiles, n_slab_tiles):
    i = pl.program_id(0)
    slot = i % 2
    is_sample = i == n_tiles - 1

    def wait_tile(sl):
        pltpu.make_async_copy(buf.at[sl], buf.at[sl], sem.at[sl]).wait()

    @pl.when(i == 0)
    def _():
        zbuf[...] = jnp.zeros(zbuf.shape, F32)
        for c in range(N_CLASSES):
            @pl.when(plen_ref[c] > 0)
            def _():
                pltpu.make_async_copy(rows8(zbuf, 0, plen_ref[c]), rows8(xs_ref, poff_ref[c], plen_ref[c]),
                                      zsem).start()
        for j in range(n_slab_tiles - N_CLASSES, n_slab_tiles):
            @pl.when(j >= nact_ref[0])
            def _():
                pltpu.make_async_copy(zbuf, rows8(xs_ref, j * TM_E, TM_E), zsem).start()

    x = jnp.where(is_sample, hs_ref[...], hp_ref[...])
    info = jnp.where(is_sample, is_ref[...], ip_ref[...])
    perm = _perm_t(info[:, INFO_CLS:INFO_CLS + 1]).T.astype(BF16)
    xp = jnp.dot(perm, x, preferred_element_type=F32)

    @pl.when(i >= 2)
    def _():
        wait_tile(slot)

    bs = buf.at[slot]
    for c in range(D_MODEL // LANES):
        bs[pl.ds(c, tm, stride=SUBLANES), :] = xp[:, c * LANES:(c + 1) * LANES]
    local = 0
    for c in range(N_CLASSES):
        n_rows = slen_ref[i * N_CLASSES + c]

        @pl.when(n_rows > 0)
        def _():
            pltpu.make_async_copy(rows8(bs, local, n_rows),
                                  rows8(xs_ref, soff_ref[i * N_CLASSES + c], n_rows), sem.at[slot]).start()
        local = local + n_rows

    @pl.when(i == n_tiles - 1)
    def _():
        wait_tile(slot)
        if n_tiles >= 2:
            wait_tile(1 - slot)

        @pl.when(ptot_ref[0] > 0)
        def _():
            n = pl.multiple_of(ptot_ref[0] * SUBLANES, SUBLANES)
            pltpu.make_async_copy(xs_ref.at[pl.ds(0, n)], xs_ref.at[pl.ds(0, n)], zsem).wait()


def _dispatch(seg_off, seg_len, pad_off, pad_len, pad_tot, nact, h2_p, h2_s, info_p, info_s, *, tm, n_slots):
    n_prompt_tiles = h2_p.shape[0] // tm
    n_tiles = n_prompt_tiles + 1
    assert h2_s.shape[0] == tm
    last_p = n_prompt_tiles - 1
    grid_spec = pltpu.PrefetchScalarGridSpec(
        num_scalar_prefetch=6,
        grid=(n_tiles,),
        in_specs=[pl.BlockSpec((tm, D_MODEL), lambda i, *_: (jnp.minimum(i, last_p), 0)),
                  pl.BlockSpec((tm, D_MODEL), lambda i, *_: (0, 0)),
                  pl.BlockSpec((tm, LANES), lambda i, *_: (jnp.minimum(i, last_p), 0)),
                  pl.BlockSpec((tm, LANES), lambda i, *_: (0, 0))],
        out_specs=pl.BlockSpec(memory_space=pl.ANY),
        scratch_shapes=[pltpu.VMEM((2, tm * SUBLANES, LANES), F32),
                        pltpu.VMEM((TM_E * SUBLANES, LANES), F32),
                        pltpu.SemaphoreType.DMA((2,)),
                        pltpu.SemaphoreType.DMA(())])
    return pl.pallas_call(
        functools.partial(_dispatch_body, tm=tm, n_tiles=n_tiles, n_slab_tiles=n_slots // TM_E),
        grid_spec=grid_spec,
        out_shape=jax.ShapeDtypeStruct((n_slots * SUBLANES, LANES), F32),
        compiler_params=_cparams(("arbitrary",)),
        name="moe_dispatch",
    )(seg_off, seg_len, pad_off, pad_len, pad_tot, nact, h2_p, h2_s, info_p, info_s)


def _moe_body(ea_ref, eb_ref, nact_ref, x_ref, wga_ref, wgb_ref, wua_ref, wub_ref, wda_ref, wdb_ref,
              y_ref, *, tm):
    i = pl.program_id(0)

    @pl.when(i < nact_ref[0])
    def _():
        x = jnp.concatenate([x_ref[pl.ds(c, tm, stride=SUBLANES), :] for c in range(D_MODEL // LANES)],
                            axis=1).astype(BF16)

        def expert(wg_ref, wu_ref, wd_ref):
            gate = jnp.dot(x, wg_ref[0], preferred_element_type=F32)
            up = jnp.dot(x, wu_ref[0], preferred_element_type=F32)
            he = (gate * jax.nn.sigmoid(gate)) * up
            y = jnp.dot(he.astype(BF16), wd_ref[0], preferred_element_type=F32)
            return lax.bitcast_convert_type(y.astype(BF16).astype(F32), jnp.uint32)

        packed = expert(wga_ref, wua_ref, wda_ref) | (expert(wgb_ref, wub_ref, wdb_ref) >> 16)
        for c in range(D_MODEL // LANES):
            y_ref[pl.ds(c, tm, stride=SUBLANES), :] = packed[:, c * LANES:(c + 1) * LANES]

    @pl.when(i >= nact_ref[0])
    def _():
        y_ref[...] = jnp.zeros(y_ref.shape, y_ref.dtype)


def _moe(tile_ea, tile_eb, nact, x_sorted, wg_b, wu_b, wd_b, *, tm, n_max):
    wspec_in = lambda sel: pl.BlockSpec((1, D_MODEL, D_EXPERT), sel)
    wspec_out = lambda sel: pl.BlockSpec((1, D_EXPERT, D_MODEL), sel)
    sel_a = lambda i, ea, eb, na: (ea[i], 0, 0)
    sel_b = lambda i, ea, eb, na: (eb[i], 0, 0)
    rows_in = lambda i, ea, eb, na: (jnp.minimum(i, na[0] - 1), 0)
    grid_spec = pltpu.PrefetchScalarGridSpec(
        num_scalar_prefetch=3,
        grid=(n_max,),
        in_specs=[pl.BlockSpec((tm * SUBLANES, LANES), rows_in),
                  wspec_in(sel_a), wspec_in(sel_b), wspec_in(sel_a), wspec_in(sel_b),
                  wspec_out(sel_a), wspec_out(sel_b)],
        out_specs=pl.BlockSpec((tm * SUBLANES, LANES), lambda i, ea, eb, na: (i, 0)))
    return pl.pallas_call(
        functools.partial(_moe_body, tm=tm),
        grid_spec=grid_spec,
        out_shape=jax.ShapeDtypeStruct(x_sorted.shape, jnp.uint32),
        compiler_params=_cparams(("arbitrary",)),
        name="moe",
    )(tile_ea, tile_eb, nact, x_sorted, wg_b, wg_b, wu_b, wu_b, wd_b, wd_b)


def _final_body(soff_ref, slen_ref, x1_ref, info_ref, mod_ref, gf_ref, ys_ref, y_ref, buf, sem,
                *, tm, n_tiles, tile_base):
    i = pl.program_id(0)
    slot = i % 2

    def fetch(tile, sl):
        local = 0
        for c in range(N_CLASSES):
            k = (tile + tile_base) * N_CLASSES + c
            n_rows = slen_ref[k]

            @pl.when(n_rows > 0)
            def _():
                pltpu.make_async_copy(rows8(ys_ref, soff_ref[k], n_rows), rows8(buf.at[sl], local, n_rows),
                                      sem.at[sl]).start()
            local = local + n_rows

    @pl.when(i == 0)
    def _():
        fetch(0, 0)

    @pl.when(i + 1 < n_tiles)
    def _():
        fetch(i + 1, 1 - slot)

    pltpu.make_async_copy(buf.at[slot], buf.at[slot], sem.at[slot]).wait()
    bs = buf.at[slot]
    packed = jnp.concatenate([bs[pl.ds(c, tm, stride=SUBLANES), :] for c in range(D_MODEL // LANES)],
                             axis=1)
    ya = lax.bitcast_convert_type(packed & jnp.uint32(0xFFFF0000), F32).astype(BF16)
    yb = lax.bitcast_convert_type(packed << 16, F32).astype(BF16)
    info = info_ref[...]
    perm_t = _perm_t(info[:, INFO_CLS:INFO_CLS + 1]).astype(BF16)
    moe = (info[:, INFO_W_LO:INFO_W_LO + 1] * jnp.dot(perm_t, ya, preferred_element_type=F32)
           + info[:, INFO_W_HI:INFO_W_HI + 1] * jnp.dot(perm_t, yb, preferred_element_type=F32))
    x1 = x1_ref[...]
    g2 = mod_ref[0][:, 5 * D_MODEL:6 * D_MODEL]
    x2 = x1 + g2 * moe
    ms = jnp.mean(x2 * x2, axis=-1, keepdims=True)
    y_ref[...] = (x2 * lax.rsqrt(ms + EPS)) * gf_ref[...]


def _final(seg_off, seg_len, x1, info, mod, g_final, y_sorted, *, tm, sample, tile_base):
    n = x1.shape[0]
    nt = n // tm
    mrows = mod.shape[1]
    if sample:
        mod_map = lambda i, *_: (0, 0, 0)
    else:
        tiles_per_batch = 4096 // tm
        mod_map = lambda i, *_: (i // tiles_per_batch, 0, 0)
    grid_spec = pltpu.PrefetchScalarGridSpec(
        num_scalar_prefetch=2,
        grid=(nt,),
        in_specs=[pl.BlockSpec((tm, D_MODEL), lambda i, *_: (i, 0)),
                  pl.BlockSpec((tm, LANES), lambda i, *_: (i, 0)),
                  pl.BlockSpec((1, mrows, 6 * D_MODEL), mod_map),
                  pl.BlockSpec((1, D_MODEL), lambda i, *_: (0, 0)),
                  pl.BlockSpec(memory_space=pl.ANY)],
        out_specs=pl.BlockSpec((tm, D_MODEL), lambda i, *_: (i, 0)),
        scratch_shapes=[pltpu.VMEM((2, tm * SUBLANES, LANES), jnp.uint32),
                        pltpu.SemaphoreType.DMA((2,))])
    return pl.pallas_call(
        functools.partial(_final_body, tm=tm, n_tiles=nt, tile_base=tile_base),
        grid_spec=grid_spec,
        out_shape=jax.ShapeDtypeStruct((n, D_MODEL), F32),
        compiler_params=_cparams(("arbitrary",)),
        name="final_sample" if sample else "final_prompt",
    )(seg_off, seg_len, x1, info, mod, g_final.reshape(1, D_MODEL), y_sorted)


def _bucket_table(n):
    d = np.arange(n)
    max_exact = N_BUCKETS // 2
    nf = np.maximum(d, 1).astype(np.float64)
    large = max_exact + (np.log(nf / max_exact) / math.log(MAX_DISTANCE / max_exact)
                         * (N_BUCKETS - max_exact)).astype(np.int64)
    large = np.minimum(large, N_BUCKETS - 1)
    return np.where(d < max_exact, d, large).astype(np.int32)


def _toeplitz(v, n_rows, n_cols):
    length = n_rows + n_cols - 1
    lead = v.shape[:-1]
    vp = jnp.concatenate([v, jnp.zeros(lead + (1,), v.dtype)], axis=-1)
    skew = jnp.tile(vp, (1,) * len(lead) + (n_rows,))[..., :n_rows * length].reshape(lead + (n_rows, length))
    return skew[..., n_rows - 1:n_rows - 1 + n_cols]


def kernel(x_prompt, x_sample, c_prompt, c_sample, cache_k, cache_v, page_table, w_ada, b_ada, w_in, w_o,
           lam_q1, lam_k1, lam_q2, lam_k2, g_subln, rel_bias, g_sg_ln, b_sg_ln, w_s, b_s, w_rg, b_rg,
           w_re, b_re, w_gate, w_up, w_down, g_final):
    batch, seq, _ = x_prompt.shape
    dec_b, dec_t, _ = x_sample.shape
    n_pages = page_table.shape[1]
    n_p = batch * seq
    n_s = dec_b * dec_t
    n_tot = n_p + n_s
    assert w_in.shape[0] == 1 and cache_k.shape[1] == 1 and seq % T_ATT == 0 and n_pages % PAGES_PER_STEP == 0
    assert n_p % TM_TOK == 0 and n_p % n_s == 0 and n_tot % TM_E == 0 and dec_t == SUBLANES
    assert TAIL_TOKENS >= MAX_DISTANCE + dec_t and TAIL_TOKENS <= PAGES_PER_STEP * PAGE
    assert n_s == TM_D and n_p % TM_D == 0 and N_CLASSES <= LANES

    w_in_b = w_in[0].astype(BF16)
    w_vt_b = w_in[0][:, 2 * QK_W:3 * QK_W].T.astype(BF16)
    w_o_b = w_o[0].astype(BF16)
    wr_t = jnp.zeros((32, D_MODEL), F32).at[0:N_EG].set(w_rg[0].T).at[N_EG:N_EG + N_EG * EPG].set(w_re[0].T)
    wr_t = wr_t.astype(BF16)
    br = jnp.zeros((32, 1), F32).at[0:N_EG, 0].set(b_rg[0]).at[N_EG:N_EG + N_EG * EPG, 0].set(b_re[0])
    wg_b = w_gate[0].astype(BF16)
    wu_b = w_up[0].astype(BF16)
    wd_b = w_down[0].astype(BF16)
    ws_tril = jnp.tril(w_s[0])
    ws_p = ws_tril.astype(BF16)
    bs_p = b_s[0][:, :, None]
    eye = jnp.eye(dec_b, dtype=F32)
    ws_s = jnp.einsum('ab,gij->gaibj', eye, ws_tril[:, :dec_t, :dec_t]).reshape(
        N_GROUPS_SG, n_s, n_s).astype(BF16)
    bs_s = jnp.tile(b_s[0][:, :dec_t], (1, dec_b))[:, :, None]
    gln = g_sg_ln[0]
    bln = b_sg_ln[0]
    lam = (jnp.exp(jnp.sum(lam_q1[0] * lam_k1[0])) - jnp.exp(jnp.sum(lam_q2[0] * lam_k2[0]))
           + LAM_INIT).reshape(1).astype(F32)

    t = T_ATT
    n_dist = max(2 * t, TAIL_TOKENS + dec_t)
    onehot = np.eye(N_BUCKETS, dtype=np.float32)[_bucket_table(n_dist)]
    ft = jnp.dot(onehot, rel_bias - rel_bias[N_BUCKETS - 1], precision=lax.Precision.HIGHEST).T
    neg = lambda n: jnp.full((N_HEADS, n), NEG, F32)
    bias_diag = _toeplitz(jnp.concatenate([neg(t - 1), ft[:, 0:t]], axis=1), t, t)
    bias_sub = _toeplitz(ft[:, 1:2 * t], t, t)
    bias_t = jnp.stack([bias_sub, bias_diag], axis=1) * LOG2E
    bl = _toeplitz(jnp.flip(ft[:, 1:TAIL_TOKENS + dec_t], axis=1), dec_t, TAIL_TOKENS)
    head_eq = jnp.eye(N_HEADS, dtype=F32)
    bias_last = (bl[:, None, :, :, None] * head_eq[:, None, None, None, :])
    bias_last = jnp.broadcast_to(bias_last, (N_HEADS, 2, dec_t, TAIL_TOKENS, N_HEADS)).reshape(
        2 * N_HEADS * dec_t, TAIL_TOKENS * N_HEADS)
    bn = _toeplitz(jnp.concatenate([jnp.flip(ft[:, 0:dec_t], axis=1), neg(dec_t - 1)], axis=1),
                   dec_t, dec_t)
    bn = jnp.where(head_eq[:, None, None, :] > 0, bn[:, :, :, None], NEG)
    bn = jnp.broadcast_to(bn[:, None], (N_HEADS, 2, dec_t, dec_t, N_HEADS)).reshape(
        2 * N_HEADS * dec_t, dec_t * N_HEADS)
    bias_new = jnp.concatenate([bn, jnp.full((bn.shape[0], PAGE - bn.shape[1]), NEG, F32)], axis=1)

    c_all = jnp.concatenate([c_prompt, c_sample, jnp.zeros((4, D_MODEL), F32)], axis=0)
    mod_all = _ada(c_all, w_ada[0], b_ada[0])
    mod_p = mod_all[:batch].reshape(batch, 1, 6 * D_MODEL)
    mod_s = jnp.repeat(mod_all[batch:batch + dec_b], dec_t, axis=0).reshape(1, n_s, 6 * D_MODEL)

    xp = x_prompt.reshape(n_p, D_MODEL)
    xs = x_sample.reshape(n_s, D_MODEL)

    q_p, kf_p, kb_p, vf_p, vt_p, sg_p = _stage_a(xp, mod_p, w_in_b, w_vt_b, ws_p, bs_p, gln, bln,
                                                 tm=TM_TOK, chunk=CHUNK, sample=False)
    q_s, kf_s, vf_s, sg_s, vsn_s = _stage_a(xs, mod_s, w_in_b, w_vt_b, ws_s, bs_s, gln, bln,
                                            tm=n_s, chunk=n_s, sample=True)

    g_col = g_subln[0].reshape(HEAD_W, 1)
    g_row = g_subln[0].reshape(1, HEAD_W)
    o_p = _attn_prompt(lam, q_p, kb_p, vt_p, bias_t, g_col, batch=batch, seq=seq, t=t)
    pad = ((0, 0), (0, PAGE - dec_t * N_HEADS), (0, 0))
    knew = jnp.pad(kf_s.reshape(dec_b, dec_t * N_HEADS, HEAD_W), pad).astype(BF16)
    vnew = jnp.pad(vf_s.reshape(dec_b, dec_t * N_HEADS, HEAD_W), pad).astype(BF16)
    ck = cache_k.reshape(-1, HEAD_W)
    cv = cache_v.reshape(-1, HEAD_W)
    o_s = _attn_sample(page_table.reshape(-1), lam, q_s, knew, vnew, bias_last, bias_new, g_row, ck, cv,
                       dec_b=dec_b, n_pages=n_pages)

    x1_p, h2_p, cls_p, info_p = _stage_c(o_p, sg_p, xp, mod_p, w_o_b, wr_t, br, tm=TM_TOK, sample=False)
    x1_s, h2_s, cls_s, info_s = _stage_c(o_s, sg_s, xs, mod_s, w_o_b, wr_t, br, tm=n_s, sample=True)

    tm_e = TM_E
    tm_d = TM_D
    n_max = n_tot // tm_e + N_CLASSES
    n_dt = n_tot // tm_d
    cls = jnp.concatenate([cls_p[0], cls_s[0]]).reshape(n_dt, tm_d)
    classes = jnp.arange(N_CLASSES, dtype=jnp.int32)
    seg_len = jnp.sum((cls[:, :, None] == classes).astype(jnp.int32), axis=1)
    counts = jnp.sum(seg_len, axis=0)
    ntile_c = (counts + tm_e - 1) // tm_e
    tile_end = jnp.cumsum(ntile_c)
    class_base = (tile_end - ntile_c) * tm_e
    nact = tile_end[-1]
    seg_off = class_base[None, :] + jnp.cumsum(seg_len, axis=0) - seg_len
    pad_off = class_base + counts
    pad_len = ntile_c * tm_e - counts
    tile_ids = jnp.arange(n_max, dtype=jnp.int32)
    tile_cls = jnp.sum((tile_ids[:, None] >= tile_end[None, :]).astype(jnp.int32), axis=1)
    last_cls = jnp.sum((nact - 1 >= tile_end).astype(jnp.int32))
    tile_cls = jnp.where(tile_ids < nact, tile_cls, last_cls)
    grp = tile_cls // N_PAIRS
    pidx = tile_cls % N_PAIRS
    pair_lo = jnp.where(pidx < 3, 0, jnp.where(pidx < 5, 1, 2))
    pair_hi = jnp.where(pidx < 3, pidx + 1, jnp.where(pidx < 5, pidx - 1, 3))
    tile_ea = (grp * EPG + pair_lo).astype(jnp.int32)
    tile_eb = (grp * EPG + pair_hi).astype(jnp.int32)
    seg_off = seg_off.reshape(-1).astype(jnp.int32)
    seg_len = seg_len.reshape(-1)

    nact1 = nact.reshape(1).astype(jnp.int32)
    zero_rows = (jnp.sum(pad_len) + (n_max - nact) * tm_e).reshape(1).astype(jnp.int32)
    x_sorted = _dispatch(seg_off, seg_len, pad_off.astype(jnp.int32), pad_len.astype(jnp.int32), zero_rows,
                         nact1, h2_p, h2_s, info_p, info_s, tm=tm_d, n_slots=n_max * tm_e)
    y_sorted = _moe(tile_ea, tile_eb, nact1, x_sorted, wg_b, wu_b, wd_b, tm=tm_e, n_max=n_max)

    y_p = _final(seg_off, seg_len, x1_p, info_p, mod_p, g_final, y_sorted, tm=tm_d, sample=False, tile_base=0)
    y_s = _final(seg_off, seg_len, x1_s, info_s, mod_s, g_final, y_sorted, tm=tm_d, sample=True,
                 tile_base=n_p // tm_d)

    return (y_p.reshape(batch, seq, D_MODEL),
            y_s.reshape(dec_b, dec_t, D_MODEL),
            kf_p.reshape(batch, 1, seq, N_HEADS, HEAD_W),
            vf_p.reshape(batch, 1, seq, N_HEADS, HEAD_W),
            kf_s.reshape(dec_b, 1, dec_t, N_HEADS, HEAD_W),
            vf_s.reshape(dec_b, 1, dec_t, N_HEADS, HEAD_W),
            vsn_s.reshape(dec_b, 1, dec_t, N_GROUPS_SG, SG_CH))
```

```python
import functools
import math

import numpy as np
import jax
import jax.numpy as jnp
from jax import lax
from jax.experimental import pallas as pl
from jax.experimental.pallas import tpu as pltpu

F32 = jnp.float32
BF16 = jnp.bfloat16

D_MODEL = 1024
N_HEADS = 4
DK = 64
HEAD_W = 128
QK_W = N_HEADS * HEAD_W
N_GROUPS_SG = 4
SG_CH = 128
IN_W = 2560
CHUNK = 128
PAGE = 128
N_BUCKETS = 32
MAX_DISTANCE = 128
N_EG = 4
EPG = 4
N_PAIRS = 6
N_CLASSES = N_EG * N_PAIRS
D_EXPERT = 512
EPS = 1e-6
LAM_INIT = 0.8 - 0.6 * math.exp(-0.3 * 0)
NEG = -1e30
LOG2E = math.log2(math.e)
LANES = 128
SUBLANES = 8

TM_TOK = 512
T_ATT = 512
PAGES_PER_STEP = 16
TM_E = 256
TM_D = 256
VMEM_LIMIT = 56 * 1024 * 1024


def _cparams(sem):
    return pltpu.CompilerParams(dimension_semantics=sem, vmem_limit_bytes=VMEM_LIMIT)


def _ada_body(c_ref, w_ref, b_ref, o_ref):
    c = c_ref[...]
    a = (c * jax.nn.sigmoid(c)).astype(BF16)
    o_ref[...] = jnp.dot(a, w_ref[...].astype(BF16), preferred_element_type=F32) + b_ref[...]


def _ada(c_all, w_ada, b_ada):
    m = c_all.shape[0]
    n = w_ada.shape[1]
    tn = 1536
    return pl.pallas_call(
        _ada_body,
        grid=(n // tn,),
        in_specs=[pl.BlockSpec((m, D_MODEL), lambda j: (0, 0)),
                  pl.BlockSpec((D_MODEL, tn), lambda j: (0, j)),
                  pl.BlockSpec((1, tn), lambda j: (0, j))],
        out_specs=pl.BlockSpec((m, tn), lambda j: (0, j)),
        out_shape=jax.ShapeDtypeStruct((m, n), F32),
        compiler_params=_cparams(("arbitrary",)),
        name="adaln",
    )(c_all, w_ada, b_ada.reshape(1, n))


def _stage_a_body(x_ref, mod_ref, w_in_ref, w_vt_ref, ws_ref, bs_ref, gln_ref, bln_ref,
                  *out_refs, chunk, sample):
    if sample:
        q_ref, kf_ref, vf_ref, sg_ref, vsn_ref = out_refs
    else:
        q_ref, kf_ref, kb_ref, vf_ref, vt_ref, sg_ref = out_refs
    x = x_ref[...]
    tm = x.shape[0]
    mod = mod_ref[0]
    sh1 = mod[:, 0:D_MODEL]
    sc1 = mod[:, D_MODEL:2 * D_MODEL]
    ms = jnp.mean(x * x, axis=-1, keepdims=True)
    h = (x * lax.rsqrt(ms + EPS)) * (1.0 + sc1) + sh1
    hb = h.astype(BF16)
    z = jnp.dot(hb, w_in_ref[...], preferred_element_type=F32)
    q = z[:, 0:QK_W] * (DK ** -0.5)
    k = z[:, QK_W:2 * QK_W]
    v = z[:, 2 * QK_W:3 * QK_W]
    for hd in range(N_HEADS):
        kf_ref[pl.ds(hd, tm, stride=N_HEADS), :] = k[:, hd * HEAD_W:(hd + 1) * HEAD_W]
        vf_ref[pl.ds(hd, tm, stride=N_HEADS), :] = v[:, hd * HEAD_W:(hd + 1) * HEAD_W]
    if sample:
        q_ref[...] = q
    else:
        q_ref[...] = (q * LOG2E).astype(BF16)
        kb_ref[...] = k.astype(BF16)
        vt = lax.dot_general(w_vt_ref[...], hb, (((1,), (1,)), ((), ())),
                             preferred_element_type=F32)
        vt_ref[0] = vt.astype(BF16)
    u = z[:, 3 * QK_W:4 * QK_W]
    vs = z[:, 4 * QK_W:5 * QK_W]
    for g in range(N_GROUPS_SG):
        lo, hi = g * SG_CH, (g + 1) * SG_CH
        vg = vs[:, lo:hi]
        mu = jnp.mean(vg, axis=-1, keepdims=True)
        dv = vg - mu
        var = jnp.mean(dv * dv, axis=-1, keepdims=True)
        vn = (dv * lax.rsqrt(var + EPS)) * gln_ref[g:g + 1, :] + bln_ref[g:g + 1, :]
        if sample:
            vsn_ref[:, lo:hi] = vn
        vnb = vn.astype(BF16)
        for c in range(tm // chunk):
            r0, r1 = c * chunk, (c + 1) * chunk
            s = jnp.dot(ws_ref[g], vnb[r0:r1], preferred_element_type=F32) + bs_ref[g]
            sg_ref[r0:r1, lo:hi] = (u[r0:r1, lo:hi] * s).astype(BF16)


def _stage_a(x, mod, w_in_b, w_vt_b, ws, bs, gln, bln, *, tm, chunk, sample):
    n = x.shape[0]
    nt = n // tm
    mrows = mod.shape[1]
    row = lambda w, dt: jax.ShapeDtypeStruct((n, w), dt)
    blk = lambda w: pl.BlockSpec((tm, w), lambda i: (i, 0))
    cache_shape = jax.ShapeDtypeStruct((n * N_HEADS, HEAD_W), F32)
    cache_blk = pl.BlockSpec((tm * N_HEADS, HEAD_W), lambda i: (i, 0))
    if sample:
        out_shape = [row(QK_W, F32), cache_shape, cache_shape, row(QK_W, BF16), row(QK_W, F32)]
        out_specs = [blk(QK_W), cache_blk, cache_blk, blk(QK_W), blk(QK_W)]
        mod_map = lambda i: (0, 0, 0)
    else:
        out_shape = [row(QK_W, BF16), cache_shape, row(QK_W, BF16), cache_shape,
                     jax.ShapeDtypeStruct((nt, QK_W, tm), BF16), row(QK_W, BF16)]
        out_specs = [blk(QK_W), cache_blk, blk(QK_W), cache_blk,
                     pl.BlockSpec((1, QK_W, tm), lambda i: (i, 0, 0)), blk(QK_W)]
        tiles_per_batch = 4096 // tm
        mod_map = lambda i: (i // tiles_per_batch, 0, 0)
    full = lambda a: pl.BlockSpec(a.shape, lambda i: (0,) * a.ndim)
    return pl.pallas_call(
        functools.partial(_stage_a_body, chunk=chunk, sample=sample),
        grid=(nt,),
        in_specs=[blk(D_MODEL),
                  pl.BlockSpec((1, mrows, 6 * D_MODEL), mod_map),
                  full(w_in_b), full(w_vt_b), full(ws), full(bs), full(gln), full(bln)],
        out_specs=out_specs,
        out_shape=out_shape,
        compiler_params=_cparams(("arbitrary",)),
        name="stage_a_sample" if sample else "stage_a_prompt",
    )(x, mod, w_in_b, w_vt_b, ws, bs, gln, bln)


ATT_COLS = 256


def _attn_body(lam_ref, q_ref, k_ref, vt_ref, bias_ref, g_ref, o_ref, *scratch, t):
    n_chain = 2 * t // ATT_COLS
    q2_refs, m_refs, l_refs, acc_refs = (scratch[i * n_chain:(i + 1) * n_chain] for i in range(4))
    qi = pl.program_id(2)
    for c in range(n_chain):
        q0 = (c * ATT_COLS) % t
        q = q_ref[q0:q0 + ATT_COLS, :]
        lane = lax.broadcasted_iota(jnp.int32, q.shape, 1)
        keep = (lane < DK) if c < n_chain // 2 else (lane >= DK)
        q2_refs[c][...] = jnp.where(keep, q, jnp.zeros_like(q))
        m_refs[c][...] = jnp.full(m_refs[c].shape, NEG, F32)
        l_refs[c][...] = jnp.zeros(l_refs[c].shape, F32)
        acc_refs[c][...] = jnp.zeros(acc_refs[c].shape, F32)

    def tile(j, bias_idx):
        k = k_ref[pl.ds(pl.multiple_of(j * t, t), t), :]
        vt = vt_ref[j]
        scores = []
        for c in range(n_chain):
            s = lax.dot_general(k, q2_refs[c][...], (((1,), (1,)), ((), ())),
                                preferred_element_type=F32)
            if bias_idx is not None:
                q0 = (c * ATT_COLS) % t
                s = s + bias_ref[0, bias_idx, :, q0:q0 + ATT_COLS]
            scores.append(s)
        for c in range(n_chain):
            s = scores[c]
            m_old = m_refs[c][...]
            m_new = jnp.maximum(m_old, jnp.max(s, axis=0, keepdims=True))
            alpha = jnp.exp2(m_old - m_new)
            p = jnp.exp2(s - m_new)
            l_refs[c][...] = alpha * l_refs[c][...] + jnp.sum(p, axis=0, keepdims=True)
            acc_refs[c][...] = alpha * acc_refs[c][...] + jnp.dot(vt, p.astype(BF16),
                                                                  preferred_element_type=F32)
            m_refs[c][...] = m_new

    def plain(j, carry):
        tile(j, None)
        return carry

    lax.fori_loop(0, jnp.maximum(qi - 1, 0), plain, 0)

    @pl.when(qi >= 1)
    def _():
        tile(qi - 1, 0)

    tile(qi, 1)

    lam = lam_ref[0]
    o_all = jnp.concatenate([acc_refs[c][...] * (1.0 / l_refs[c][...]) for c in range(n_chain)],
                            axis=1)
    o = o_all[:, 0:t] - lam * o_all[:, t:2 * t]
    ms = jnp.mean(o * o, axis=0, keepdims=True)
    on = (o * lax.rsqrt(ms + EPS)) * g_ref[...] * (1.0 - LAM_INIT)
    o_ref[...] = on.T.astype(BF16)


def _attn_prompt(lam, q, kb, vt, bias_t, g_col, *, batch, seq, t):
    nq = seq // t
    n = batch * seq
    n_chain = 2 * t // ATT_COLS
    return pl.pallas_call(
        functools.partial(_attn_body, t=t),
        grid=(batch, N_HEADS, nq),
        in_specs=[pl.BlockSpec(memory_space=pltpu.SMEM),
                  pl.BlockSpec((t, HEAD_W), lambda b, h, i: (b * nq + i, h)),
                  pl.BlockSpec((seq, HEAD_W), lambda b, h, i: (b, h)),
                  pl.BlockSpec((nq, HEAD_W, t), lambda b, h, i: (b, h, 0)),
                  pl.BlockSpec((1, 2, t, t), lambda b, h, i: (h, 0, 0, 0)),
                  pl.BlockSpec((HEAD_W, 1), lambda b, h, i: (0, 0))],
        out_specs=pl.BlockSpec((t, HEAD_W), lambda b, h, i: (b * nq + i, h)),
        out_shape=jax.ShapeDtypeStruct((n, QK_W), BF16),
        scratch_shapes=([pltpu.VMEM((ATT_COLS, HEAD_W), BF16)] * n_chain
                        + [pltpu.VMEM((1, ATT_COLS), F32)] * (2 * n_chain)
                        + [pltpu.VMEM((HEAD_W, ATT_COLS), F32)] * n_chain),
        compiler_params=_cparams(("arbitrary", "arbitrary", "arbitrary")),
        name="attn_prompt",
    )(lam, q, kb, vt, bias_t, g_col)


PAGE_ROWS = PAGE * N_HEADS
TAIL_TOKENS = 2 * PAGE


def _sattn_body(pt_ref, lam_ref, q_ref, knew_ref, vnew_ref, bl_ref, bn_ref, g_ref, ck_ref, cv_ref,
                o_ref, kbuf, vbuf, sem, mask_ref, m_ref, l_ref, acc_ref, *, pages, n_steps, total):
    b = pl.program_id(0)
    s = pl.program_id(1)
    step = b * n_steps + s
    slot = step % 2

    def page_copies(step_idx, sl):
        base = step_idx * pages
        out = []
        for i in range(pages):
            src = pl.ds(pl.multiple_of(pt_ref[base + i] * PAGE_ROWS, PAGE_ROWS), PAGE_ROWS)
            dst = pl.ds(i * PAGE_ROWS, PAGE_ROWS)
            out.append(pltpu.make_async_copy(ck_ref.at[src], kbuf.at[sl, dst], sem.at[sl, 0]))
            out.append(pltpu.make_async_copy(cv_ref.at[src], vbuf.at[sl, dst], sem.at[sl, 1]))
        return out

    @pl.when(step == 0)
    def _():
        for c in page_copies(0, 0):
            c.start()

    @pl.when(step + 1 < total)
    def _():
        for c in page_copies(step + 1, 1 - slot):
            c.start()

    for c in page_copies(step, slot):
        c.wait()

    @pl.when(s == 0)
    def _():
        m_ref[...] = jnp.full(m_ref.shape, NEG, F32)
        l_ref[...] = jnp.zeros(l_ref.shape, F32)
        acc_ref[...] = jnp.zeros(acc_ref.shape, F32)

    @pl.when(step == 0)
    def _():
        row = lax.broadcasted_iota(jnp.int32, mask_ref.shape, 0)
        col = lax.broadcasted_iota(jnp.int32, mask_ref.shape, 1)
        same_head = (col % N_HEADS) == (row // (2 * SUBLANES))
        mask_ref[...] = jnp.where(same_head, 0.0, NEG)

    q = q_ref[...]
    lane = lax.broadcasted_iota(jnp.int32, (SUBLANES, HEAD_W), 1)
    pieces = []
    for h in range(N_HEADS):
        qh = q[:, h * HEAD_W:(h + 1) * HEAD_W]
        pieces += [jnp.where(lane < DK, qh, 0.0), jnp.where(lane >= DK, qh, 0.0)]
    qm = jnp.concatenate(pieces, axis=0).astype(BF16)

    def update(kb, vb, bias):
        sc = lax.dot_general(qm, kb, (((1,), (1,)), ((), ())),
                             preferred_element_type=F32) + bias
        m_old = m_ref[...]
        m_new = jnp.maximum(m_old, jnp.max(sc, axis=1, keepdims=True))
        alpha = jnp.exp(m_old - m_new)
        p = jnp.exp(sc - m_new)
        l_ref[...] = alpha * l_ref[...] + jnp.sum(p, axis=1, keepdims=True)
        acc_ref[...] = alpha * acc_ref[...] + jnp.dot(p.astype(BF16), vb,
                                                      preferred_element_type=F32)
        m_ref[...] = m_new

    is_last = s == n_steps - 1
    head_cols = mask_ref.shape[1] - bl_ref.shape[1]
    update(kbuf[slot].astype(BF16), vbuf[slot].astype(BF16),
           jnp.concatenate([mask_ref[:, :head_cols],
                            mask_ref[:, head_cols:] + bl_ref[...] * is_last.astype(F32)], axis=1))

    @pl.when(is_last)
    def _():
        update(knew_ref[0], vnew_ref[0], bn_ref[...])
        lam = lam_ref[0]
        o_all = acc_ref[...] * (1.0 / l_ref[...])
        for h in range(N_HEADS):
            r = h * 2 * SUBLANES
            o = o_all[r:r + SUBLANES] - lam * o_all[r + SUBLANES:r + 2 * SUBLANES]
            ms = jnp.mean(o * o, axis=-1, keepdims=True)
            o_ref[:, h * HEAD_W:(h + 1) * HEAD_W] = ((o * lax.rsqrt(ms + EPS)) * g_ref[...]
                                                     * (1.0 - LAM_INIT))


def _attn_sample(page_table_flat, lam, q_s, knew, vnew, bias_last, bias_new, g_row, cache_k, cache_v,
                 *, dec_b, n_pages):
    pages = PAGES_PER_STEP
    n_steps = n_pages // pages
    total = dec_b * n_steps
    nq = q_s.shape[0] // dec_b
    n_rows = 2 * N_HEADS * nq
    step_rows = pages * PAGE_ROWS
    grid_spec = pltpu.PrefetchScalarGridSpec(
        num_scalar_prefetch=1,
        grid=(dec_b, n_steps),
        in_specs=[pl.BlockSpec(memory_space=pltpu.SMEM),
                  pl.BlockSpec((nq, QK_W), lambda b, s, pt: (b, 0)),
                  pl.BlockSpec((1, PAGE, HEAD_W), lambda b, s, pt: (b, 0, 0)),
                  pl.BlockSpec((1, PAGE, HEAD_W), lambda b, s, pt: (b, 0, 0)),
                  pl.BlockSpec(bias_last.shape, lambda b, s, pt: (0, 0)),
                  pl.BlockSpec(bias_new.shape, lambda b, s, pt: (0, 0)),
                  pl.BlockSpec((1, HEAD_W), lambda b, s, pt: (0, 0)),
                  pl.BlockSpec(memory_space=pl.ANY),
                  pl.BlockSpec(memory_space=pl.ANY)],
        out_specs=pl.BlockSpec((nq, QK_W), lambda b, s, pt: (b, 0)),
        scratch_shapes=[pltpu.VMEM((2, step_rows, HEAD_W), F32),
                        pltpu.VMEM((2, step_rows, HEAD_W), F32),
                        pltpu.SemaphoreType.DMA((2, 2)),
                        pltpu.VMEM((n_rows, step_rows), F32),
                        pltpu.VMEM((n_rows, 1), F32), pltpu.VMEM((n_rows, 1), F32),
                        pltpu.VMEM((n_rows, HEAD_W), F32)])
    return pl.pallas_call(
        functools.partial(_sattn_body, pages=pages, n_steps=n_steps, total=total),
        grid_spec=grid_spec,
        out_shape=jax.ShapeDtypeStruct(q_s.shape, F32),
        compiler_params=_cparams(("arbitrary", "arbitrary")),
        name="attn_sample",
    )(page_table_flat, lam, q_s, knew, vnew, bias_last, bias_new, g_row, cache_k, cache_v)


INFO_W_LO, INFO_W_HI, INFO_CLS = 0, 1, 2


def _stage_c_body(o_ref, sg_ref, x_ref, mod_ref, wo_ref, wr_ref, br_ref, x1_ref, h2_ref, cls_ref, info_ref):
    x = x_ref[...]
    tm = x.shape[0]
    mod = mod_ref[0]
    g1 = mod[:, 2 * D_MODEL:3 * D_MODEL]
    sh2 = mod[:, 3 * D_MODEL:4 * D_MODEL]
    sc2 = mod[:, 4 * D_MODEL:5 * D_MODEL]
    mix = (jnp.dot(o_ref[...].astype(BF16), wo_ref[0:QK_W, :], preferred_element_type=F32)
           + jnp.dot(sg_ref[...], wo_ref[QK_W:2 * QK_W, :], preferred_element_type=F32))
    x1 = x + g1 * mix
    x1_ref[...] = x1
    ms = jnp.mean(x1 * x1, axis=-1, keepdims=True)
    h2 = ((x1 * lax.rsqrt(ms + EPS)) * (1.0 + sc2) + sh2).astype(BF16)
    h2_ref[...] = h2
    lg = lax.dot_general(wr_ref[...], h2, (((1,), (1,)), ((), ())),
                         preferred_element_type=F32) + br_ref[...]
    gl = [lg[i:i + 1, :] for i in range(N_EG)]
    el = [lg[N_EG + i:N_EG + i + 1, :] for i in range(N_EG * EPG)]
    gmax = jnp.maximum(jnp.maximum(gl[0], gl[1]), jnp.maximum(gl[2], gl[3]))
    gi = jnp.where(gl[0] == gmax, 0, jnp.where(gl[1] == gmax, 1, jnp.where(gl[2] == gmax, 2, 3)))
    gsum = (jnp.exp(gl[0] - gmax) + jnp.exp(gl[1] - gmax)
            + jnp.exp(gl[2] - gmax) + jnp.exp(gl[3] - gmax))
    gp = 1.0 / gsum
    sel = [jnp.where(gi == 0, el[j], jnp.where(gi == 1, el[EPG + j],
                                               jnp.where(gi == 2, el[2 * EPG + j], el[3 * EPG + j])))
           for j in range(EPG)]
    v0 = jnp.maximum(jnp.maximum(sel[0], sel[1]), jnp.maximum(sel[2], sel[3]))
    i0 = jnp.where(sel[0] == v0, 0, jnp.where(sel[1] == v0, 1, jnp.where(sel[2] == v0, 2, 3)))
    rest = [jnp.where(i0 == j, -3e38, sel[j]) for j in range(EPG)]
    v1 = jnp.maximum(jnp.maximum(rest[0], rest[1]), jnp.maximum(rest[2], rest[3]))
    i1 = jnp.where(rest[0] == v1, 0, jnp.where(rest[1] == v1, 1, jnp.where(rest[2] == v1, 2, 3)))
    e1 = jnp.exp(v1 - v0)
    den = 1.0 / (1.0 + e1)
    tw0 = den * gp
    tw1 = e1 * den * gp
    first_low = i0 < i1
    lo = jnp.where(first_low, i0, i1)
    hi = jnp.where(first_low, i1, i0)
    w_lo = jnp.where(first_low, tw0, tw1)
    w_hi = jnp.where(first_low, tw1, tw0)
    pair = jnp.where(lo == 0, 0, jnp.where(lo == 1, 3, 5)) + hi - lo - 1
    cls = gi * N_PAIRS + pair
    cls_ref[...] = jnp.broadcast_to(cls, cls_ref.shape).astype(jnp.int32)
    row = lax.broadcasted_iota(jnp.int32, (LANES, tm), 0)
    rec = jnp.where(row == INFO_W_LO, w_lo, jnp.where(row == INFO_W_HI, w_hi,
                                                      jnp.where(row == INFO_CLS, cls.astype(F32), 0.0)))
    info_ref[...] = rec.T


def _stage_c(o, sg, x, mod, wo_b, wr_t, br, *, tm, sample):
    n = x.shape[0]
    nt = n // tm
    mrows = mod.shape[1]
    blk = lambda w: pl.BlockSpec((tm, w), lambda i: (i, 0))
    full = lambda a: pl.BlockSpec(a.shape, lambda i: (0,) * a.ndim)
    if sample:
        mod_map = lambda i: (0, 0, 0)
    else:
        tiles_per_batch = 4096 // tm
        mod_map = lambda i: (i // tiles_per_batch, 0, 0)
    return pl.pallas_call(
        _stage_c_body,
        grid=(nt,),
        in_specs=[blk(QK_W), blk(QK_W), blk(D_MODEL),
                  pl.BlockSpec((1, mrows, 6 * D_MODEL), mod_map),
                  full(wo_b), full(wr_t), full(br)],
        out_specs=[blk(D_MODEL), blk(D_MODEL),
                   pl.BlockSpec((SUBLANES, tm), lambda i: (0, i)),
                   blk(LANES)],
        out_shape=[jax.ShapeDtypeStruct((n, D_MODEL), F32),
                   jax.ShapeDtypeStruct((n, D_MODEL), BF16),
                   jax.ShapeDtypeStruct((SUBLANES, n), jnp.int32),
                   jax.ShapeDtypeStruct((n, LANES), F32)],
        compiler_params=_cparams(("arbitrary",)),
        name="stage_c_sample" if sample else "stage_c_prompt",
    )(o, sg, x, mod, wo_b, wr_t, br)


def rows8(ref, start, count):
    scale = lambda v: v * SUBLANES if isinstance(v, int) else pl.multiple_of(v * SUBLANES, SUBLANES)
    return ref.at[pl.ds(scale(start), scale(count))]


def _perm_t(cls_col):
    n = cls_col.shape[0]
    lane = lax.broadcasted_iota(jnp.int32, (n, LANES), 1).astype(F32)
    onehot = (lane == cls_col).astype(BF16)
    r = lax.broadcasted_iota(jnp.int32, (n, n), 0)
    c = lax.broadcasted_iota(jnp.int32, (n, n), 1)
    before = (c < r).astype(BF16)
    rank = jnp.dot(before, onehot, preferred_element_type=F32)
    cnt = jnp.sum(onehot.astype(F32), axis=0, keepdims=True)
    cr = lax.broadcasted_iota(jnp.int32, (LANES, LANES), 0)
    cc = lax.broadcasted_iota(jnp.int32, (LANES, LANES), 1)
    lower_cls = (cr < cc).astype(BF16)
    base = jnp.dot(jnp.broadcast_to(cnt, (SUBLANES, LANES)).astype(BF16), lower_cls,
                   preferred_element_type=F32)[0:1, :]
    pos = jnp.sum(onehot.astype(F32) * (base + rank), axis=1, keepdims=True)
    dest = lax.broadcasted_iota(jnp.int32, (n, n), 1).astype(F32)
    return (dest == pos).astype(F32)


def _dispatch_body(soff_ref, slen_ref, poff_ref, plen_ref, ptot_ref, nact_ref, hp_ref, hs_ref, ip_ref, is_ref,
                   xs_ref, buf, zbuf, sem, zsem, *, tm, n_tiles, n_slab_tiles):
    i = pl.program_id(0)
    slot = i % 2
    is_sample = i == n_tiles - 1

    def wait_tile(sl):
        pltpu.make_async_copy(buf.at[sl], buf.at[sl], sem.at[sl]).wait()

    @pl.when(i == 0)
    def _():
        zbuf[...] = jnp.zeros(zbuf.shape, F32)
        for c in range(N_CLASSES):
            @pl.when(plen_ref[c] > 0)
            def _():
                pltpu.make_async_copy(rows8(zbuf, 0, plen_ref[c]), rows8(xs_ref, poff_ref[c], plen_ref[c]),
                                      zsem).start()
        for j in range(n_slab_tiles - N_CLASSES, n_slab_tiles):
            @pl.when(j >= nact_ref[0])
            def _():
                pltpu.make_async_copy(zbuf, rows8(xs_ref, j * TM_E, TM_E), zsem).start()

    x = jnp.where(is_sample, hs_ref[...], hp_ref[...])
    info = jnp.where(is_sample, is_ref[...], ip_ref[...])
    perm = _perm_t(info[:, INFO_CLS:INFO_CLS + 1]).T.astype(BF16)
    xp = jnp.dot(perm, x, preferred_element_type=F32)

    @pl.when(i >= 2)
    def _():
        wait_tile(slot)

    bs = buf.at[slot]
    for c in range(D_MODEL // LANES):
        bs[pl.ds(c, tm, stride=SUBLANES), :] = xp[:, c * LANES:(c + 1) * LANES]
    local = 0
    for c in range(N_CLASSES):
        n_rows = slen_ref[i * N_CLASSES + c]

        @pl.when(n_rows > 0)
        def _():
            pltpu.make_async_copy(rows8(bs, local, n_rows),
                                  rows8(xs_ref, soff_ref[i * N_CLASSES + c], n_rows), sem.at[slot]).start()
        local = local + n_rows

    @pl.when(i == n_tiles - 1)
    def _():
        wait_tile(slot)
        if n_tiles >= 2:
            wait_tile(1 - slot)

        @pl.when(ptot_ref[0] > 0)
        def _():
            n = pl.multiple_of(ptot_ref[0] * SUBLANES, SUBLANES)
            pltpu.make_async_copy(xs_ref.at[pl.ds(0, n)], xs_ref.at[pl.ds(0, n)], zsem).wait()


def _dispatch(seg_off, seg_len, pad_off, pad_len, pad_tot, nact, h2_p, h2_s, info_p, info_s, *, tm, n_slots):
    n_prompt_tiles = h2_p.shape[0] // tm
    n_tiles = n_prompt_tiles + 1
    assert h2_s.shape[0] == tm
    last_p = n_prompt_tiles - 1
    grid_spec = pltpu.PrefetchScalarGridSpec(
        num_scalar_prefetch=6,
        grid=(n_tiles,),
        in_specs=[pl.BlockSpec((tm, D_MODEL), lambda i, *_: (jnp.minimum(i, last_p), 0)),
                  pl.BlockSpec((tm, D_MODEL), lambda i, *_: (0, 0)),
                  pl.BlockSpec((tm, LANES), lambda i, *_: (jnp.minimum(i, last_p), 0)),
                  pl.BlockSpec((tm, LANES), lambda i, *_: (0, 0))],
        out_specs=pl.BlockSpec(memory_space=pl.ANY),
        scratch_shapes=[pltpu.VMEM((2, tm * SUBLANES, LANES), F32),
                        pltpu.VMEM((TM_E * SUBLANES, LANES), F32),
                        pltpu.SemaphoreType.DMA((2,)),
                        pltpu.SemaphoreType.DMA(())])
    return pl.pallas_call(
        functools.partial(_dispatch_body, tm=tm, n_tiles=n_tiles, n_slab_tiles=n_slots // TM_E),
        grid_spec=grid_spec,
        out_shape=jax.ShapeDtypeStruct((n_slots * SUBLANES, LANES), F32),
        compiler_params=_cparams(("arbitrary",)),
        name="moe_dispatch",
    )(seg_off, seg_len, pad_off, pad_len, pad_tot, nact, h2_p, h2_s, info_p, info_s)


def _moe_body(ea_ref, eb_ref, nact_ref, x_ref, wga_ref, wgb_ref, wua_ref, wub_ref, wda_ref, wdb_ref,
              ya_ref, yb_ref, *, tm):
    i = pl.program_id(0)

    @pl.when(i < nact_ref[0])
    def _():
        x = jnp.concatenate([x_ref[pl.ds(c, tm, stride=SUBLANES), :] for c in range(D_MODEL // LANES)],
                            axis=1).astype(BF16)

        def expert(wg_ref, wu_ref, wd_ref):
            gate = jnp.dot(x, wg_ref[0], preferred_element_type=F32)
            up = jnp.dot(x, wu_ref[0], preferred_element_type=F32)
            he = (gate * jax.nn.sigmoid(gate)) * up
            return jnp.dot(he.astype(BF16), wd_ref[0], preferred_element_type=F32)

        ya = expert(wga_ref, wua_ref, wda_ref)
        yb = expert(wgb_ref, wub_ref, wdb_ref)
        for c in range(D_MODEL // LANES):
            ya_ref[pl.ds(c, tm, stride=SUBLANES), :] = ya[:, c * LANES:(c + 1) * LANES]
            yb_ref[pl.ds(c, tm, stride=SUBLANES), :] = yb[:, c * LANES:(c + 1) * LANES]

    @pl.when(i >= nact_ref[0])
    def _():
        ya_ref[...] = jnp.zeros(ya_ref.shape, ya_ref.dtype)
        yb_ref[...] = jnp.zeros(yb_ref.shape, yb_ref.dtype)


def _moe(tile_ea, tile_eb, nact, x_sorted, wg_b, wu_b, wd_b, *, tm, n_max):
    wspec_in = lambda sel: pl.BlockSpec((1, D_MODEL, D_EXPERT), sel)
    wspec_out = lambda sel: pl.BlockSpec((1, D_EXPERT, D_MODEL), sel)
    sel_a = lambda i, ea, eb, na: (ea[i], 0, 0)
    sel_b = lambda i, ea, eb, na: (eb[i], 0, 0)
    rows_in = lambda i, ea, eb, na: (jnp.minimum(i, na[0] - 1), 0)
    grid_spec = pltpu.PrefetchScalarGridSpec(
        num_scalar_prefetch=3,
        grid=(n_max,),
        in_specs=[pl.BlockSpec((tm * SUBLANES, LANES), rows_in),
                  wspec_in(sel_a), wspec_in(sel_b), wspec_in(sel_a), wspec_in(sel_b),
                  wspec_out(sel_a), wspec_out(sel_b)],
        out_specs=[pl.BlockSpec((tm * SUBLANES, LANES), lambda i, ea, eb, na: (i, 0))] * 2)
    return pl.pallas_call(
        functools.partial(_moe_body, tm=tm),
        grid_spec=grid_spec,
        out_shape=[jax.ShapeDtypeStruct(x_sorted.shape, F32)] * 2,
        compiler_params=_cparams(("arbitrary",)),
        name="moe",
    )(tile_ea, tile_eb, nact, x_sorted, wg_b, wg_b, wu_b, wu_b, wd_b, wd_b)


def _final_body(soff_ref, slen_ref, x1_ref, info_ref, mod_ref, gf_ref, ya_ref, yb_ref, y_ref, buf, sem,
                *, tm, n_tiles, tile_base):
    i = pl.program_id(0)
    slot = i % 2
    slabs = (ya_ref, yb_ref)

    def fetch(tile, sl):
        local = 0
        for c in range(N_CLASSES):
            k = (tile + tile_base) * N_CLASSES + c
            n_rows = slen_ref[k]

            @pl.when(n_rows > 0)
            def _():
                for e in range(2):
                    pltpu.make_async_copy(rows8(slabs[e], soff_ref[k], n_rows),
                                          rows8(buf.at[sl, e], local, n_rows), sem.at[sl, e]).start()
            local = local + n_rows

    @pl.when(i == 0)
    def _():
        fetch(0, 0)

    @pl.when(i + 1 < n_tiles)
    def _():
        fetch(i + 1, 1 - slot)

    info = info_ref[...]
    perm_t = _perm_t(info[:, INFO_CLS:INFO_CLS + 1]).astype(BF16)
    moe = None
    for e, lane_w in enumerate((INFO_W_LO, INFO_W_HI)):
        pltpu.make_async_copy(buf.at[slot, e], buf.at[slot, e], sem.at[slot, e]).wait()
        bs = buf.at[slot, e]
        ye = jnp.concatenate([bs[pl.ds(c, tm, stride=SUBLANES), :] for c in range(D_MODEL // LANES)],
                             axis=1)
        term = info[:, lane_w:lane_w + 1] * jnp.dot(perm_t, ye.astype(BF16), preferred_element_type=F32)
        moe = term if moe is None else moe + term
    x1 = x1_ref[...]
    g2 = mod_ref[0][:, 5 * D_MODEL:6 * D_MODEL]
    x2 = x1 + g2 * moe
    ms = jnp.mean(x2 * x2, axis=-1, keepdims=True)
    y_ref[...] = (x2 * lax.rsqrt(ms + EPS)) * gf_ref[...]


def _final(seg_off, seg_len, x1, info, mod, g_final, ya_sorted, yb_sorted, *, tm, sample, tile_base):
    n = x1.shape[0]
    nt = n // tm
    mrows = mod.shape[1]
    if sample:
        mod_map = lambda i, *_: (0, 0, 0)
    else:
        tiles_per_batch = 4096 // tm
        mod_map = lambda i, *_: (i // tiles_per_batch, 0, 0)
    grid_spec = pltpu.PrefetchScalarGridSpec(
        num_scalar_prefetch=2,
        grid=(nt,),
        in_specs=[pl.BlockSpec((tm, D_MODEL), lambda i, *_: (i, 0)),
                  pl.BlockSpec((tm, LANES), lambda i, *_: (i, 0)),
                  pl.BlockSpec((1, mrows, 6 * D_MODEL), mod_map),
                  pl.BlockSpec((1, D_MODEL), lambda i, *_: (0, 0)),
                  pl.BlockSpec(memory_space=pl.ANY),
                  pl.BlockSpec(memory_space=pl.ANY)],
        out_specs=pl.BlockSpec((tm, D_MODEL), lambda i, *_: (i, 0)),
        scratch_shapes=[pltpu.VMEM((2, 2, tm * SUBLANES, LANES), F32),
                        pltpu.SemaphoreType.DMA((2, 2))])
    return pl.pallas_call(
        functools.partial(_final_body, tm=tm, n_tiles=nt, tile_base=tile_base),
        grid_spec=grid_spec,
        out_shape=jax.ShapeDtypeStruct((n, D_MODEL), F32),
        compiler_params=_cparams(("arbitrary",)),
        name="final_sample" if sample else "final_prompt",
    )(seg_off, seg_len, x1, info, mod, g_final.reshape(1, D_MODEL), ya_sorted, yb_sorted)


def _bucket_table(n):
    d = np.arange(n)
    max_exact = N_BUCKETS // 2
    nf = np.maximum(d, 1).astype(np.float64)
    large = max_exact + (np.log(nf / max_exact) / math.log(MAX_DISTANCE / max_exact)
                         * (N_BUCKETS - max_exact)).astype(np.int64)
    large = np.minimum(large, N_BUCKETS - 1)
    return np.where(d < max_exact, d, large).astype(np.int32)


def _toeplitz(v, n_rows, n_cols):
    length = n_rows + n_cols - 1
    lead = v.shape[:-1]
    vp = jnp.concatenate([v, jnp.zeros(lead + (1,), v.dtype)], axis=-1)
    skew = jnp.tile(vp, (1,) * len(lead) + (n_rows,))[..., :n_rows * length].reshape(lead + (n_rows, length))
    return skew[..., n_rows - 1:n_rows - 1 + n_cols]


def kernel(x_prompt, x_sample, c_prompt, c_sample, cache_k, cache_v, page_table, w_ada, b_ada, w_in, w_o,
           lam_q1, lam_k1, lam_q2, lam_k2, g_subln, rel_bias, g_sg_ln, b_sg_ln, w_s, b_s, w_rg, b_rg,
           w_re, b_re, w_gate, w_up, w_down, g_final):
    batch, seq, _ = x_prompt.shape
    dec_b, dec_t, _ = x_sample.shape
    n_pages = page_table.shape[1]
    n_p = batch * seq
    n_s = dec_b * dec_t
    n_tot = n_p + n_s
    assert w_in.shape[0] == 1 and cache_k.shape[1] == 1 and seq % T_ATT == 0 and n_pages % PAGES_PER_STEP == 0
    assert n_p % TM_TOK == 0 and n_p % n_s == 0 and n_tot % TM_E == 0 and dec_t == SUBLANES
    assert TAIL_TOKENS >= MAX_DISTANCE + dec_t and TAIL_TOKENS <= PAGES_PER_STEP * PAGE
    assert n_s == TM_D and n_p % TM_D == 0 and N_CLASSES <= LANES

    w_in_b = w_in[0].astype(BF16)
    w_vt_b = w_in[0][:, 2 * QK_W:3 * QK_W].T.astype(BF16)
    w_o_b = w_o[0].astype(BF16)
    wr_t = jnp.zeros((32, D_MODEL), F32).at[0:N_EG].set(w_rg[0].T).at[N_EG:N_EG + N_EG * EPG].set(w_re[0].T)
    wr_t = wr_t.astype(BF16)
    br = jnp.zeros((32, 1), F32).at[0:N_EG, 0].set(b_rg[0]).at[N_EG:N_EG + N_EG * EPG, 0].set(b_re[0])
    wg_b = w_gate[0].astype(BF16)
    wu_b = w_up[0].astype(BF16)
    wd_b = w_down[0].astype(BF16)
    ws_tril = jnp.tril(w_s[0])
    ws_p = ws_tril.astype(BF16)
    bs_p = b_s[0][:, :, None]
    eye = jnp.eye(dec_b, dtype=F32)
    ws_s = jnp.einsum('ab,gij->gaibj', eye, ws_tril[:, :dec_t, :dec_t]).reshape(
        N_GROUPS_SG, n_s, n_s).astype(BF16)
    bs_s = jnp.tile(b_s[0][:, :dec_t], (1, dec_b))[:, :, None]
    gln = g_sg_ln[0]
    bln = b_sg_ln[0]
    lam = (jnp.exp(jnp.sum(lam_q1[0] * lam_k1[0])) - jnp.exp(jnp.sum(lam_q2[0] * lam_k2[0]))
           + LAM_INIT).reshape(1).astype(F32)

    t = T_ATT
    n_dist = max(2 * t, TAIL_TOKENS + dec_t)
    onehot = np.eye(N_BUCKETS, dtype=np.float32)[_bucket_table(n_dist)]
    ft = jnp.dot(onehot, rel_bias - rel_bias[N_BUCKETS - 1], precision=lax.Precision.HIGHEST).T
    neg = lambda n: jnp.full((N_HEADS, n), NEG, F32)
    bias_diag = _toeplitz(jnp.concatenate([neg(t - 1), ft[:, 0:t]], axis=1), t, t)
    bias_sub = _toeplitz(ft[:, 1:2 * t], t, t)
    bias_t = jnp.stack([bias_sub, bias_diag], axis=1) * LOG2E
    bl = _toeplitz(jnp.flip(ft[:, 1:TAIL_TOKENS + dec_t], axis=1), dec_t, TAIL_TOKENS)
    head_eq = jnp.eye(N_HEADS, dtype=F32)
    bias_last = (bl[:, None, :, :, None] * head_eq[:, None, None, None, :])
    bias_last = jnp.broadcast_to(bias_last, (N_HEADS, 2, dec_t, TAIL_TOKENS, N_HEADS)).reshape(
        2 * N_HEADS * dec_t, TAIL_TOKENS * N_HEADS)
    bn = _toeplitz(jnp.concatenate([jnp.flip(ft[:, 0:dec_t], axis=1), neg(dec_t - 1)], axis=1),
                   dec_t, dec_t)
    bn = jnp.where(head_eq[:, None, None, :] > 0, bn[:, :, :, None], NEG)
    bn = jnp.broadcast_to(bn[:, None], (N_HEADS, 2, dec_t, dec_t, N_HEADS)).reshape(
        2 * N_HEADS * dec_t, dec_t * N_HEADS)
    bias_new = jnp.concatenate([bn, jnp.full((bn.shape[0], PAGE - bn.shape[1]), NEG, F32)], axis=1)

    c_all = jnp.concatenate([c_prompt, c_sample, jnp.zeros((4, D_MODEL), F32)], axis=0)
    mod_all = _ada(c_all, w_ada[0], b_ada[0])
    mod_p = mod_all[:batch].reshape(batch, 1, 6 * D_MODEL)
    mod_s = jnp.repeat(mod_all[batch:batch + dec_b], dec_t, axis=0).reshape(1, n_s, 6 * D_MODEL)

    xp = x_prompt.reshape(n_p, D_MODEL)
    xs = x_sample.reshape(n_s, D_MODEL)

    q_p, kf_p, kb_p, vf_p, vt_p, sg_p = _stage_a(xp, mod_p, w_in_b, w_vt_b, ws_p, bs_p, gln, bln,
                                                 tm=TM_TOK, chunk=CHUNK, sample=False)
    q_s, kf_s, vf_s, sg_s, vsn_s = _stage_a(xs, mod_s, w_in_b, w_vt_b, ws_s, bs_s, gln, bln,
                                            tm=n_s, chunk=n_s, sample=True)

    g_col = g_subln[0].reshape(HEAD_W, 1)
    g_row = g_subln[0].reshape(1, HEAD_W)
    o_p = _attn_prompt(lam, q_p, kb_p, vt_p, bias_t, g_col, batch=batch, seq=seq, t=t)
    pad = ((0, 0), (0, PAGE - dec_t * N_HEADS), (0, 0))
    knew = jnp.pad(kf_s.reshape(dec_b, dec_t * N_HEADS, HEAD_W), pad).astype(BF16)
    vnew = jnp.pad(vf_s.reshape(dec_b, dec_t * N_HEADS, HEAD_W), pad).astype(BF16)
    ck = cache_k.reshape(-1, HEAD_W)
    cv = cache_v.reshape(-1, HEAD_W)
    o_s = _attn_sample(page_table.reshape(-1), lam, q_s, knew, vnew, bias_last, bias_new, g_row, ck, cv,
                       dec_b=dec_b, n_pages=n_pages)

    x1_p, h2_p, cls_p, info_p = _stage_c(o_p, sg_p, xp, mod_p, w_o_b, wr_t, br, tm=TM_TOK, sample=False)
    x1_s, h2_s, cls_s, info_s = _stage_c(o_s, sg_s, xs, mod_s, w_o_b, wr_t, br, tm=n_s, sample=True)

    tm_e = TM_E
    tm_d = TM_D
    n_max = n_tot // tm_e + N_CLASSES
    n_dt = n_tot // tm_d
    cls = jnp.concatenate([cls_p[0], cls_s[0]]).reshape(n_dt, tm_d)
    classes = jnp.arange(N_CLASSES, dtype=jnp.int32)
    seg_len = jnp.sum((cls[:, :, None] == classes).astype(jnp.int32), axis=1)
    counts = jnp.sum(seg_len, axis=0)
    ntile_c = (counts + tm_e - 1) // tm_e
    tile_end = jnp.cumsum(ntile_c)
    class_base = (tile_end - ntile_c) * tm_e
    nact = tile_end[-1]
    seg_off = class_base[None, :] + jnp.cumsum(seg_len, axis=0) - seg_len
    pad_off = class_base + counts
    pad_len = ntile_c * tm_e - counts
    tile_ids = jnp.arange(n_max, dtype=jnp.int32)
    tile_cls = jnp.sum((tile_ids[:, None] >= tile_end[None, :]).astype(jnp.int32), axis=1)
    last_cls = jnp.sum((nact - 1 >= tile_end).astype(jnp.int32))
    tile_cls = jnp.where(tile_ids < nact, tile_cls, last_cls)
    grp = tile_cls // N_PAIRS
    pidx = tile_cls % N_PAIRS
    pair_lo = jnp.where(pidx < 3, 0, jnp.where(pidx < 5, 1, 2))
    pair_hi = jnp.where(pidx < 3, pidx + 1, jnp.where(pidx < 5, pidx - 1, 3))
    tile_ea = (grp * EPG + pair_lo).astype(jnp.int32)
    tile_eb = (grp * EPG + pair_hi).astype(jnp.int32)
    seg_off = seg_off.reshape(-1).astype(jnp.int32)
    seg_len = seg_len.reshape(-1)

    nact1 = nact.reshape(1).astype(jnp.int32)
    zero_rows = (jnp.sum(pad_len) + (n_max - nact) * tm_e).reshape(1).astype(jnp.int32)
    x_sorted = _dispatch(seg_off, seg_len, pad_off.astype(jnp.int32), pad_len.astype(jnp.int32), zero_rows,
                         nact1, h2_p, h2_s, info_p, info_s, tm=tm_d, n_slots=n_max * tm_e)
    ya_sorted, yb_sorted = _moe(tile_ea, tile_eb, nact1, x_sorted, wg_b, wu_b, wd_b, tm=tm_e, n_max=n_max)

    y_p = _final(seg_off, seg_len, x1_p, info_p, mod_p, g_final, ya_sorted, yb_sorted,
                 tm=tm_d, sample=False, tile_base=0)
    y_s = _final(seg_off, seg_len, x1_s, info_s, mod_s, g_final, ya_sorted, yb_sorted,
                 tm=tm_d, sample=True, tile_base=n_p // tm_d)

    return (y_p.reshape(batch, seq, D_MODEL),
            y_s.reshape(dec_b, dec_t, D_MODEL),
            kf_p.reshape(batch, 1, seq, N_HEADS, HEAD_W),
            vf_p.reshape(batch, 1, seq, N_HEADS, HEAD_W),
            kf_s.reshape(dec_b, 1, dec_t, N_HEADS, HEAD_W),
            vf_s.reshape(dec_b, 1, dec_t, N_HEADS, HEAD_W),
            vsn_s.reshape(dec_b, 1, dec_t, N_GROUPS_SG, SG_CH))
```

```python
import functools
import math

import numpy as np
import jax
import jax.numpy as jnp
from jax import lax
from jax.experimental import pallas as pl
from jax.experimental.pallas import tpu as pltpu

F32 = jnp.float32
BF16 = jnp.bfloat16

D_MODEL = 1024
N_HEADS = 4
DK = 64
HEAD_W = 128
QK_W = N_HEADS * HEAD_W
N_GROUPS_SG = 4
SG_CH = 128
IN_W = 2560
CHUNK = 128
PAGE = 128
N_BUCKETS = 32
MAX_DISTANCE = 128
N_EG = 4
EPG = 4
N_PAIRS = 6
N_CLASSES = N_EG * N_PAIRS
D_EXPERT = 512
EPS = 1e-6
LAM_INIT = 0.8 - 0.6 * math.exp(-0.3 * 0)
NEG = -1e30
LOG2E = math.log2(math.e)
LANES = 128
SUBLANES = 8

TM_TOK = 512
T_ATT = 512
PAGES_PER_STEP = 16
TM_E = 256
TM_D = 256
VMEM_LIMIT = 56 * 1024 * 1024


def _cparams(sem):
    return pltpu.CompilerParams(dimension_semantics=sem, vmem_limit_bytes=VMEM_LIMIT)


def _ada_body(c_ref, w_ref, b_ref, o_ref):
    c = c_ref[...]
    a = (c * jax.nn.sigmoid(c)).astype(BF16)
    o_ref[...] = jnp.dot(a, w_ref[...].astype(BF16), preferred_element_type=F32) + b_ref[...]


def _ada(c_all, w_ada, b_ada):
    m = c_all.shape[0]
    n = w_ada.shape[1]
    tn = 1536
    return pl.pallas_call(
        _ada_body,
        grid=(n // tn,),
        in_specs=[pl.BlockSpec((m, D_MODEL), lambda j: (0, 0)),
                  pl.BlockSpec((D_MODEL, tn), lambda j: (0, j)),
                  pl.BlockSpec((1, tn), lambda j: (0, j))],
        out_specs=pl.BlockSpec((m, tn), lambda j: (0, j)),
        out_shape=jax.ShapeDtypeStruct((m, n), F32),
        compiler_params=_cparams(("arbitrary",)),
        name="adaln",
    )(c_all, w_ada, b_ada.reshape(1, n))


def _stage_a_body(x_ref, mod_ref, w_in_ref, w_vt_ref, ws_ref, bs_ref, gln_ref, bln_ref,
                  *out_refs, chunk, sample):
    if sample:
        q_ref, kf_ref, vf_ref, sg_ref, vsn_ref = out_refs
    else:
        q_ref, kf_ref, kb_ref, vf_ref, vt_ref, sg_ref = out_refs
    x = x_ref[...]
    tm = x.shape[0]
    mod = mod_ref[0]
    sh1 = mod[:, 0:D_MODEL]
    sc1 = mod[:, D_MODEL:2 * D_MODEL]
    ms = jnp.mean(x * x, axis=-1, keepdims=True)
    h = (x * lax.rsqrt(ms + EPS)) * (1.0 + sc1) + sh1
    hb = h.astype(BF16)
    z = jnp.dot(hb, w_in_ref[...], preferred_element_type=F32)
    q = z[:, 0:QK_W] * (DK ** -0.5)
    k = z[:, QK_W:2 * QK_W]
    v = z[:, 2 * QK_W:3 * QK_W]
    for hd in range(N_HEADS):
        kf_ref[pl.ds(hd, tm, stride=N_HEADS), :] = k[:, hd * HEAD_W:(hd + 1) * HEAD_W]
        vf_ref[pl.ds(hd, tm, stride=N_HEADS), :] = v[:, hd * HEAD_W:(hd + 1) * HEAD_W]
    if sample:
        q_ref[...] = q
    else:
        q_ref[...] = (q * LOG2E).astype(BF16)
        kb_ref[...] = k.astype(BF16)
        vt = lax.dot_general(w_vt_ref[...], hb, (((1,), (1,)), ((), ())),
                             preferred_element_type=F32)
        vt_ref[0] = vt.astype(BF16)
    u = z[:, 3 * QK_W:4 * QK_W]
    vs = z[:, 4 * QK_W:5 * QK_W]
    for g in range(N_GROUPS_SG):
        lo, hi = g * SG_CH, (g + 1) * SG_CH
        vg = vs[:, lo:hi]
        mu = jnp.mean(vg, axis=-1, keepdims=True)
        dv = vg - mu
        var = jnp.mean(dv * dv, axis=-1, keepdims=True)
        vn = (dv * lax.rsqrt(var + EPS)) * gln_ref[g:g + 1, :] + bln_ref[g:g + 1, :]
        if sample:
            vsn_ref[:, lo:hi] = vn
        vnb = vn.astype(BF16)
        for c in range(tm // chunk):
            r0, r1 = c * chunk, (c + 1) * chunk
            s = jnp.dot(ws_ref[g], vnb[r0:r1], preferred_element_type=F32) + bs_ref[g]
            sg_ref[r0:r1, lo:hi] = (u[r0:r1, lo:hi] * s).astype(BF16)


def _stage_a(x, mod, w_in_b, w_vt_b, ws, bs, gln, bln, *, tm, chunk, sample):
    n = x.shape[0]
    nt = n // tm
    mrows = mod.shape[1]
    row = lambda w, dt: jax.ShapeDtypeStruct((n, w), dt)
    blk = lambda w: pl.BlockSpec((tm, w), lambda i: (i, 0))
    cache_shape = jax.ShapeDtypeStruct((n * N_HEADS, HEAD_W), F32)
    cache_blk = pl.BlockSpec((tm * N_HEADS, HEAD_W), lambda i: (i, 0))
    if sample:
        out_shape = [row(QK_W, F32), cache_shape, cache_shape, row(QK_W, BF16), row(QK_W, F32)]
        out_specs = [blk(QK_W), cache_blk, cache_blk, blk(QK_W), blk(QK_W)]
        mod_map = lambda i: (0, 0, 0)
    else:
        out_shape = [row(QK_W, BF16), cache_shape, row(QK_W, BF16), cache_shape,
                     jax.ShapeDtypeStruct((nt, QK_W, tm), BF16), row(QK_W, BF16)]
        out_specs = [blk(QK_W), cache_blk, blk(QK_W), cache_blk,
                     pl.BlockSpec((1, QK_W, tm), lambda i: (i, 0, 0)), blk(QK_W)]
        tiles_per_batch = 4096 // tm
        mod_map = lambda i: (i // tiles_per_batch, 0, 0)
    full = lambda a: pl.BlockSpec(a.shape, lambda i: (0,) * a.ndim)
    return pl.pallas_call(
        functools.partial(_stage_a_body, chunk=chunk, sample=sample),
        grid=(nt,),
        in_specs=[blk(D_MODEL),
                  pl.BlockSpec((1, mrows, 6 * D_MODEL), mod_map),
                  full(w_in_b), full(w_vt_b), full(ws), full(bs), full(gln), full(bln)],
        out_specs=out_specs,
        out_shape=out_shape,
        compiler_params=_cparams(("arbitrary",)),
        name="stage_a_sample" if sample else "stage_a_prompt",
    )(x, mod, w_in_b, w_vt_b, ws, bs, gln, bln)


ATT_COLS = 256


def _attn_body(lam_ref, q_ref, k_ref, vt_ref, bias_ref, g_ref, o_ref, *scratch, t):
    n_chain = 2 * t // ATT_COLS
    q2_refs, m_refs, l_refs, acc_refs = (scratch[i * n_chain:(i + 1) * n_chain] for i in range(4))
    qi = pl.program_id(2)
    for c in range(n_chain):
        q0 = (c * ATT_COLS) % t
        q = q_ref[q0:q0 + ATT_COLS, :]
        lane = lax.broadcasted_iota(jnp.int32, q.shape, 1)
        keep = (lane < DK) if c < n_chain // 2 else (lane >= DK)
        q2_refs[c][...] = jnp.where(keep, q, jnp.zeros_like(q))
        m_refs[c][...] = jnp.full(m_refs[c].shape, NEG, F32)
        l_refs[c][...] = jnp.zeros(l_refs[c].shape, F32)
        acc_refs[c][...] = jnp.zeros(acc_refs[c].shape, F32)

    def tiles(*work):
        scores = []
        for j, bias_idx in work:
            k = k_ref[pl.ds(pl.multiple_of(j * t, t), t), :]
            for c in range(n_chain):
                s = lax.dot_general(k, q2_refs[c][...], (((1,), (1,)), ((), ())),
                                    preferred_element_type=F32)
                if bias_idx is not None:
                    q0 = (c * ATT_COLS) % t
                    s = s + bias_ref[0, bias_idx, :, q0:q0 + ATT_COLS]
                scores.append(s)
        for w, (j, _) in enumerate(work):
            vt = vt_ref[j]
            for c in range(n_chain):
                s = scores[w * n_chain + c]
                m_old = m_refs[c][...]
                m_new = jnp.maximum(m_old, jnp.max(s, axis=0, keepdims=True))
                alpha = jnp.exp2(m_old - m_new)
                p = jnp.exp2(s - m_new)
                l_refs[c][...] = alpha * l_refs[c][...] + jnp.sum(p, axis=0, keepdims=True)
                acc_refs[c][...] = alpha * acc_refs[c][...] + jnp.dot(vt, p.astype(BF16),
                                                                      preferred_element_type=F32)
                m_refs[c][...] = m_new

    n_plain = jnp.maximum(qi - 1, 0)

    def plain_pair(jj, carry):
        tiles((2 * jj, None), (2 * jj + 1, None))
        return carry

    lax.fori_loop(0, n_plain // 2, plain_pair, 0)

    @pl.when(n_plain % 2 == 1)
    def _():
        tiles((n_plain - 1, None))

    @pl.when(qi >= 1)
    def _():
        tiles((qi - 1, 0), (qi, 1))

    @pl.when(qi == 0)
    def _():
        tiles((0, 1))

    lam = lam_ref[0]
    o_all = jnp.concatenate([acc_refs[c][...] * (1.0 / l_refs[c][...]) for c in range(n_chain)],
                            axis=1)
    o = o_all[:, 0:t] - lam * o_all[:, t:2 * t]
    ms = jnp.mean(o * o, axis=0, keepdims=True)
    on = (o * lax.rsqrt(ms + EPS)) * g_ref[...] * (1.0 - LAM_INIT)
    o_ref[...] = on.T.astype(BF16)


def _attn_prompt(lam, q, kb, vt, bias_t, g_col, *, batch, seq, t):
    nq = seq // t
    n = batch * seq
    n_chain = 2 * t // ATT_COLS
    return pl.pallas_call(
        functools.partial(_attn_body, t=t),
        grid=(batch, N_HEADS, nq),
        in_specs=[pl.BlockSpec(memory_space=pltpu.SMEM),
                  pl.BlockSpec((t, HEAD_W), lambda b, h, i: (b * nq + i, h)),
                  pl.BlockSpec((seq, HEAD_W), lambda b, h, i: (b, h)),
                  pl.BlockSpec((nq, HEAD_W, t), lambda b, h, i: (b, h, 0)),
                  pl.BlockSpec((1, 2, t, t), lambda b, h, i: (h, 0, 0, 0)),
                  pl.BlockSpec((HEAD_W, 1), lambda b, h, i: (0, 0))],
        out_specs=pl.BlockSpec((t, HEAD_W), lambda b, h, i: (b * nq + i, h)),
        out_shape=jax.ShapeDtypeStruct((n, QK_W), BF16),
        scratch_shapes=([pltpu.VMEM((ATT_COLS, HEAD_W), BF16)] * n_chain
                        + [pltpu.VMEM((1, ATT_COLS), F32)] * (2 * n_chain)
                        + [pltpu.VMEM((HEAD_W, ATT_COLS), F32)] * n_chain),
        compiler_params=_cparams(("arbitrary", "arbitrary", "arbitrary")),
        name="attn_prompt",
    )(lam, q, kb, vt, bias_t, g_col)


PAGE_ROWS = PAGE * N_HEADS
TAIL_TOKENS = 2 * PAGE


def _sattn_body(pt_ref, lam_ref, q_ref, knew_ref, vnew_ref, bl_ref, bn_ref, g_ref, ck_ref, cv_ref,
                o_ref, kbuf, vbuf, sem, mask_ref, m_ref, l_ref, acc_ref, *, pages, n_steps, total):
    b = pl.program_id(0)
    s = pl.program_id(1)
    step = b * n_steps + s
    slot = step % 2

    def page_copies(step_idx, sl):
        base = step_idx * pages
        out = []
        for i in range(pages):
            src = pl.ds(pl.multiple_of(pt_ref[base + i] * PAGE_ROWS, PAGE_ROWS), PAGE_ROWS)
            dst = pl.ds(i * PAGE_ROWS, PAGE_ROWS)
            out.append(pltpu.make_async_copy(ck_ref.at[src], kbuf.at[sl, dst], sem.at[sl, 0]))
            out.append(pltpu.make_async_copy(cv_ref.at[src], vbuf.at[sl, dst], sem.at[sl, 1]))
        return out

    @pl.when(step == 0)
    def _():
        for c in page_copies(0, 0):
            c.start()

    @pl.when(step + 1 < total)
    def _():
        for c in page_copies(step + 1, 1 - slot):
            c.start()

    for c in page_copies(step, slot):
        c.wait()

    @pl.when(s == 0)
    def _():
        m_ref[...] = jnp.full(m_ref.shape, NEG, F32)
        l_ref[...] = jnp.zeros(l_ref.shape, F32)
        acc_ref[...] = jnp.zeros(acc_ref.shape, F32)

    @pl.when(step == 0)
    def _():
        row = lax.broadcasted_iota(jnp.int32, mask_ref.shape, 0)
        col = lax.broadcasted_iota(jnp.int32, mask_ref.shape, 1)
        same_head = (col % N_HEADS) == (row // (2 * SUBLANES))
        mask_ref[...] = jnp.where(same_head, 0.0, NEG)

    q = q_ref[...]
    lane = lax.broadcasted_iota(jnp.int32, (SUBLANES, HEAD_W), 1)
    pieces = []
    for h in range(N_HEADS):
        qh = q[:, h * HEAD_W:(h + 1) * HEAD_W]
        pieces += [jnp.where(lane < DK, qh, 0.0), jnp.where(lane >= DK, qh, 0.0)]
    qm = jnp.concatenate(pieces, axis=0).astype(BF16)

    def update(kb, vb, bias):
        sc = lax.dot_general(qm, kb, (((1,), (1,)), ((), ())),
                             preferred_element_type=F32) + bias
        m_old = m_ref[...]
        m_new = jnp.maximum(m_old, jnp.max(sc, axis=1, keepdims=True))
        alpha = jnp.exp(m_old - m_new)
        p = jnp.exp(sc - m_new)
        l_ref[...] = alpha * l_ref[...] + jnp.sum(p, axis=1, keepdims=True)
        acc_ref[...] = alpha * acc_ref[...] + jnp.dot(p.astype(BF16), vb,
                                                      preferred_element_type=F32)
        m_ref[...] = m_new

    is_last = s == n_steps - 1
    head_cols = mask_ref.shape[1] - bl_ref.shape[1]
    update(kbuf[slot].astype(BF16), vbuf[slot].astype(BF16),
           jnp.concatenate([mask_ref[:, :head_cols],
                            mask_ref[:, head_cols:] + bl_ref[...] * is_last.astype(F32)], axis=1))

    @pl.when(is_last)
    def _():
        update(knew_ref[0], vnew_ref[0], bn_ref[...])
        lam = lam_ref[0]
        o_all = acc_ref[...] * (1.0 / l_ref[...])
        for h in range(N_HEADS):
            r = h * 2 * SUBLANES
            o = o_all[r:r + SUBLANES] - lam * o_all[r + SUBLANES:r + 2 * SUBLANES]
            ms = jnp.mean(o * o, axis=-1, keepdims=True)
            o_ref[:, h * HEAD_W:(h + 1) * HEAD_W] = ((o * lax.rsqrt(ms + EPS)) * g_ref[...]
                                                     * (1.0 - LAM_INIT))


def _attn_sample(page_table_flat, lam, q_s, knew, vnew, bias_last, bias_new, g_row, cache_k, cache_v,
                 *, dec_b, n_pages):
    pages = PAGES_PER_STEP
    n_steps = n_pages // pages
    total = dec_b * n_steps
    nq = q_s.shape[0] // dec_b
    n_rows = 2 * N_HEADS * nq
    step_rows = pages * PAGE_ROWS
    grid_spec = pltpu.PrefetchScalarGridSpec(
        num_scalar_prefetch=1,
        grid=(dec_b, n_steps),
        in_specs=[pl.BlockSpec(memory_space=pltpu.SMEM),
                  pl.BlockSpec((nq, QK_W), lambda b, s, pt: (b, 0)),
                  pl.BlockSpec((1, PAGE, HEAD_W), lambda b, s, pt: (b, 0, 0)),
                  pl.BlockSpec((1, PAGE, HEAD_W), lambda b, s, pt: (b, 0, 0)),
                  pl.BlockSpec(bias_last.shape, lambda b, s, pt: (0, 0)),
                  pl.BlockSpec(bias_new.shape, lambda b, s, pt: (0, 0)),
                  pl.BlockSpec((1, HEAD_W), lambda b, s, pt: (0, 0)),
                  pl.BlockSpec(memory_space=pl.ANY),
                  pl.BlockSpec(memory_space=pl.ANY)],
        out_specs=pl.BlockSpec((nq, QK_W), lambda b, s, pt: (b, 0)),
        scratch_shapes=[pltpu.VMEM((2, step_rows, HEAD_W), F32),
                        pltpu.VMEM((2, step_rows, HEAD_W), F32),
                        pltpu.SemaphoreType.DMA((2, 2)),
                        pltpu.VMEM((n_rows, step_rows), F32),
                        pltpu.VMEM((n_rows, 1), F32), pltpu.VMEM((n_rows, 1), F32),
                        pltpu.VMEM((n_rows, HEAD_W), F32)])
    return pl.pallas_call(
        functools.partial(_sattn_body, pages=pages, n_steps=n_steps, total=total),
        grid_spec=grid_spec,
        out_shape=jax.ShapeDtypeStruct(q_s.shape, F32),
        compiler_params=_cparams(("arbitrary", "arbitrary")),
        name="attn_sample",
    )(page_table_flat, lam, q_s, knew, vnew, bias_last, bias_new, g_row, cache_k, cache_v)


INFO_W_LO, INFO_W_HI, INFO_CLS = 0, 1, 2


def _stage_c_body(o_ref, sg_ref, x_ref, mod_ref, wo_ref, wr_ref, br_ref, x1_ref, h2_ref, cls_ref, info_ref):
    x = x_ref[...]
    tm = x.shape[0]
    mod = mod_ref[0]
    g1 = mod[:, 2 * D_MODEL:3 * D_MODEL]
    sh2 = mod[:, 3 * D_MODEL:4 * D_MODEL]
    sc2 = mod[:, 4 * D_MODEL:5 * D_MODEL]
    mix = (jnp.dot(o_ref[...].astype(BF16), wo_ref[0:QK_W, :], preferred_element_type=F32)
           + jnp.dot(sg_ref[...], wo_ref[QK_W:2 * QK_W, :], preferred_element_type=F32))
    x1 = x + g1 * mix
    x1_ref[...] = x1
    ms = jnp.mean(x1 * x1, axis=-1, keepdims=True)
    h2 = ((x1 * lax.rsqrt(ms + EPS)) * (1.0 + sc2) + sh2).astype(BF16)
    h2_ref[...] = h2
    lg = lax.dot_general(wr_ref[...], h2, (((1,), (1,)), ((), ())),
                         preferred_element_type=F32) + br_ref[...]
    gl = [lg[i:i + 1, :] for i in range(N_EG)]
    el = [lg[N_EG + i:N_EG + i + 1, :] for i in range(N_EG * EPG)]
    gmax = jnp.maximum(jnp.maximum(gl[0], gl[1]), jnp.maximum(gl[2], gl[3]))
    gi = jnp.where(gl[0] == gmax, 0, jnp.where(gl[1] == gmax, 1, jnp.where(gl[2] == gmax, 2, 3)))
    gsum = (jnp.exp(gl[0] - gmax) + jnp.exp(gl[1] - gmax)
            + jnp.exp(gl[2] - gmax) + jnp.exp(gl[3] - gmax))
    gp = 1.0 / gsum
    sel = [jnp.where(gi == 0, el[j], jnp.where(gi == 1, el[EPG + j],
                                               jnp.where(gi == 2, el[2 * EPG + j], el[3 * EPG + j])))
           for j in range(EPG)]
    v0 = jnp.maximum(jnp.maximum(sel[0], sel[1]), jnp.maximum(sel[2], sel[3]))
    i0 = jnp.where(sel[0] == v0, 0, jnp.where(sel[1] == v0, 1, jnp.where(sel[2] == v0, 2, 3)))
    rest = [jnp.where(i0 == j, -3e38, sel[j]) for j in range(EPG)]
    v1 = jnp.maximum(jnp.maximum(rest[0], rest[1]), jnp.maximum(rest[2], rest[3]))
    i1 = jnp.where(rest[0] == v1, 0, jnp.where(rest[1] == v1, 1, jnp.where(rest[2] == v1, 2, 3)))
    e1 = jnp.exp(v1 - v0)
    den = 1.0 / (1.0 + e1)
    tw0 = den * gp
    tw1 = e1 * den * gp
    first_low = i0 < i1
    lo = jnp.where(first_low, i0, i1)
    hi = jnp.where(first_low, i1, i0)
    w_lo = jnp.where(first_low, tw0, tw1)
    w_hi = jnp.where(first_low, tw1, tw0)
    pair = jnp.where(lo == 0, 0, jnp.where(lo == 1, 3, 5)) + hi - lo - 1
    cls = gi * N_PAIRS + pair
    cls_ref[...] = jnp.broadcast_to(cls, cls_ref.shape).astype(jnp.int32)
    row = lax.broadcasted_iota(jnp.int32, (LANES, tm), 0)
    rec = jnp.where(row == INFO_W_LO, w_lo, jnp.where(row == INFO_W_HI, w_hi,
                                                      jnp.where(row == INFO_CLS, cls.astype(F32), 0.0)))
    info_ref[...] = rec.T


def _stage_c(o, sg, x, mod, wo_b, wr_t, br, *, tm, sample):
    n = x.shape[0]
    nt = n // tm
    mrows = mod.shape[1]
    blk = lambda w: pl.BlockSpec((tm, w), lambda i: (i, 0))
    full = lambda a: pl.BlockSpec(a.shape, lambda i: (0,) * a.ndim)
    if sample:
        mod_map = lambda i: (0, 0, 0)
    else:
        tiles_per_batch = 4096 // tm
        mod_map = lambda i: (i // tiles_per_batch, 0, 0)
    return pl.pallas_call(
        _stage_c_body,
        grid=(nt,),
        in_specs=[blk(QK_W), blk(QK_W), blk(D_MODEL),
                  pl.BlockSpec((1, mrows, 6 * D_MODEL), mod_map),
                  full(wo_b), full(wr_t), full(br)],
        out_specs=[blk(D_MODEL), blk(D_MODEL),
                   pl.BlockSpec((SUBLANES, tm), lambda i: (0, i)),
                   blk(LANES)],
        out_shape=[jax.ShapeDtypeStruct((n, D_MODEL), F32),
                   jax.ShapeDtypeStruct((n, D_MODEL), BF16),
                   jax.ShapeDtypeStruct((SUBLANES, n), jnp.int32),
                   jax.ShapeDtypeStruct((n, LANES), F32)],
        compiler_params=_cparams(("arbitrary",)),
        name="stage_c_sample" if sample else "stage_c_prompt",
    )(o, sg, x, mod, wo_b, wr_t, br)


def rows8(ref, start, count):
    scale = lambda v: v * SUBLANES if isinstance(v, int) else pl.multiple_of(v * SUBLANES, SUBLANES)
    return ref.at[pl.ds(scale(start), scale(count))]


def _perm_t(cls_col):
    n = cls_col.shape[0]
    lane = lax.broadcasted_iota(jnp.int32, (n, LANES), 1).astype(F32)
    onehot = (lane == cls_col).astype(BF16)
    r = lax.broadcasted_iota(jnp.int32, (n, n), 0)
    c = lax.broadcasted_iota(jnp.int32, (n, n), 1)
    before = (c < r).astype(BF16)
    rank = jnp.dot(before, onehot, preferred_element_type=F32)
    cnt = jnp.sum(onehot.astype(F32), axis=0, keepdims=True)
    cr = lax.broadcasted_iota(jnp.int32, (LANES, LANES), 0)
    cc = lax.broadcasted_iota(jnp.int32, (LANES, LANES), 1)
    lower_cls = (cr < cc).astype(BF16)
    base = jnp.dot(jnp.broadcast_to(cnt, (SUBLANES, LANES)).astype(BF16), lower_cls,
                   preferred_element_type=F32)[0:1, :]
    pos = jnp.sum(onehot.astype(F32) * (base + rank), axis=1, keepdims=True)
    dest = lax.broadcasted_iota(jnp.int32, (n, n), 1).astype(F32)
    return (dest == pos).astype(F32)


def _dispatch_body(soff_ref, slen_ref, poff_ref, plen_ref, ptot_ref, nact_ref, hp_ref, hs_ref, ip_ref, is_ref,
                   xs_ref, buf, zbuf, sem, zsem, *, tm, n_tiles, n_slab_tiles):
    i = pl.program_id(0)
    slot = i % 2
    is_sample = i == n_tiles - 1

    def wait_tile(sl):
        pltpu.make_async_copy(buf.at[sl], buf.at[sl], sem.at[sl]).wait()

    @pl.when(i == 0)
    def _():
        zbuf[...] = jnp.zeros(zbuf.shape, F32)
        for c in range(N_CLASSES):
            @pl.when(plen_ref[c] > 0)
            def _():
                pltpu.make_async_copy(rows8(zbuf, 0, plen_ref[c]), rows8(xs_ref, poff_ref[c], plen_ref[c]),
                                      zsem).start()
        for j in range(n_slab_tiles - N_CLASSES, n_slab_tiles):
            @pl.when(j >= nact_ref[0])
            def _():
                pltpu.make_async_copy(zbuf, rows8(xs_ref, j * TM_E, TM_E), zsem).start()

    x = jnp.where(is_sample, hs_ref[...], hp_ref[...])
    info = jnp.where(is_sample, is_ref[...], ip_ref[...])
    perm = _perm_t(info[:, INFO_CLS:INFO_CLS + 1]).T.astype(BF16)
    xp = jnp.dot(perm, x, preferred_element_type=F32)

    @pl.when(i >= 2)
    def _():
        wait_tile(slot)

    bs = buf.at[slot]
    for c in range(D_MODEL // LANES):
        bs[pl.ds(c, tm, stride=SUBLANES), :] = xp[:, c * LANES:(c + 1) * LANES]
    local = 0
    for c in range(N_CLASSES):
        n_rows = slen_ref[i * N_CLASSES + c]

        @pl.when(n_rows > 0)
        def _():
            pltpu.make_async_copy(rows8(bs, local, n_rows),
                                  rows8(xs_ref, soff_ref[i * N_CLASSES + c], n_rows), sem.at[slot]).start()
        local = local + n_rows

    @pl.when(i == n_tiles - 1)
    def _():
        wait_tile(slot)
        if n_tiles >= 2:
            wait_tile(1 - slot)

        @pl.when(ptot_ref[0] > 0)
        def _():
            n = pl.multiple_of(ptot_ref[0] * SUBLANES, SUBLANES)
            pltpu.make_async_copy(xs_ref.at[pl.ds(0, n)], xs_ref.at[pl.ds(0, n)], zsem).wait()


def _dispatch(seg_off, seg_len, pad_off, pad_len, pad_tot, nact, h2_p, h2_s, info_p, info_s, *, tm, n_slots):
    n_prompt_tiles = h2_p.shape[0] // tm
    n_tiles = n_prompt_tiles + 1
    assert h2_s.shape[0] == tm
    last_p = n_prompt_tiles - 1
    grid_spec = pltpu.PrefetchScalarGridSpec(
        num_scalar_prefetch=6,
        grid=(n_tiles,),
        in_specs=[pl.BlockSpec((tm, D_MODEL), lambda i, *_: (jnp.minimum(i, last_p), 0)),
                  pl.BlockSpec((tm, D_MODEL), lambda i, *_: (0, 0)),
                  pl.BlockSpec((tm, LANES), lambda i, *_: (jnp.minimum(i, last_p), 0)),
                  pl.BlockSpec((tm, LANES), lambda i, *_: (0, 0))],
        out_specs=pl.BlockSpec(memory_space=pl.ANY),
        scratch_shapes=[pltpu.VMEM((2, tm * SUBLANES, LANES), F32),
                        pltpu.VMEM((TM_E * SUBLANES, LANES), F32),
                        pltpu.SemaphoreType.DMA((2,)),
                        pltpu.SemaphoreType.DMA(())])
    return pl.pallas_call(
        functools.partial(_dispatch_body, tm=tm, n_tiles=n_tiles, n_slab_tiles=n_slots // TM_E),
        grid_spec=grid_spec,
        out_shape=jax.ShapeDtypeStruct((n_slots * SUBLANES, LANES), F32),
        compiler_params=_cparams(("arbitrary",)),
        name="moe_dispatch",
    )(seg_off, seg_len, pad_off, pad_len, pad_tot, nact, h2_p, h2_s, info_p, info_s)


def _moe_body(ea_ref, eb_ref, nact_ref, x_ref, wga_ref, wgb_ref, wua_ref, wub_ref, wda_ref, wdb_ref,
              ya_ref, yb_ref, *, tm):
    i = pl.program_id(0)

    @pl.when(i < nact_ref[0])
    def _():
        x = jnp.concatenate([x_ref[pl.ds(c, tm, stride=SUBLANES), :] for c in range(D_MODEL // LANES)],
                            axis=1).astype(BF16)

        def expert(wg_ref, wu_ref, wd_ref):
            gate = jnp.dot(x, wg_ref[0], preferred_element_type=F32)
            up = jnp.dot(x, wu_ref[0], preferred_element_type=F32)
            he = (gate * jax.nn.sigmoid(gate)) * up
            return jnp.dot(he.astype(BF16), wd_ref[0], preferred_element_type=F32)

        ya = expert(wga_ref, wua_ref, wda_ref)
        yb = expert(wgb_ref, wub_ref, wdb_ref)
        for c in range(D_MODEL // LANES):
            ya_ref[pl.ds(c, tm, stride=SUBLANES), :] = ya[:, c * LANES:(c + 1) * LANES]
            yb_ref[pl.ds(c, tm, stride=SUBLANES), :] = yb[:, c * LANES:(c + 1) * LANES]

    @pl.when(i >= nact_ref[0])
    def _():
        ya_ref[...] = jnp.zeros(ya_ref.shape, ya_ref.dtype)
        yb_ref[...] = jnp.zeros(yb_ref.shape, yb_ref.dtype)


def _moe(tile_ea, tile_eb, nact, x_sorted, wg_b, wu_b, wd_b, *, tm, n_max):
    wspec_in = lambda sel: pl.BlockSpec((1, D_MODEL, D_EXPERT), sel)
    wspec_out = lambda sel: pl.BlockSpec((1, D_EXPERT, D_MODEL), sel)
    sel_a = lambda i, ea, eb, na: (ea[i], 0, 0)
    sel_b = lambda i, ea, eb, na: (eb[i], 0, 0)
    rows_in = lambda i, ea, eb, na: (jnp.minimum(i, na[0] - 1), 0)
    grid_spec = pltpu.PrefetchScalarGridSpec(
        num_scalar_prefetch=3,
        grid=(n_max,),
        in_specs=[pl.BlockSpec((tm * SUBLANES, LANES), rows_in),
                  wspec_in(sel_a), wspec_in(sel_b), wspec_in(sel_a), wspec_in(sel_b),
                  wspec_out(sel_a), wspec_out(sel_b)],
        out_specs=[pl.BlockSpec((tm * SUBLANES, LANES), lambda i, ea, eb, na: (i, 0))] * 2)
    return pl.pallas_call(
        functools.partial(_moe_body, tm=tm),
        grid_spec=grid_spec,
        out_shape=[jax.ShapeDtypeStruct(x_sorted.shape, F32)] * 2,
        compiler_params=_cparams(("arbitrary",)),
        name="moe",
    )(tile_ea, tile_eb, nact, x_sorted, wg_b, wg_b, wu_b, wu_b, wd_b, wd_b)


def _final_body(soff_ref, slen_ref, x1_ref, info_ref, mod_ref, gf_ref, ya_ref, yb_ref, y_ref, buf, sem,
                *, tm, n_tiles, tile_base):
    i = pl.program_id(0)
    slot = i % 2
    slabs = (ya_ref, yb_ref)

    def fetch(tile, sl):
        local = 0
        for c in range(N_CLASSES):
            k = (tile + tile_base) * N_CLASSES + c
            n_rows = slen_ref[k]

            @pl.when(n_rows > 0)
            def _():
                for e in range(2):
                    pltpu.make_async_copy(rows8(slabs[e], soff_ref[k], n_rows),
                                          rows8(buf.at[sl, e], local, n_rows), sem.at[sl, e]).start()
            local = local + n_rows

    @pl.when(i == 0)
    def _():
        fetch(0, 0)

    @pl.when(i + 1 < n_tiles)
    def _():
        fetch(i + 1, 1 - slot)

    info = info_ref[...]
    perm_t = _perm_t(info[:, INFO_CLS:INFO_CLS + 1]).astype(BF16)
    moe = None
    for e, lane_w in enumerate((INFO_W_LO, INFO_W_HI)):
        pltpu.make_async_copy(buf.at[slot, e], buf.at[slot, e], sem.at[slot, e]).wait()
        bs = buf.at[slot, e]
        ye = jnp.concatenate([bs[pl.ds(c, tm, stride=SUBLANES), :] for c in range(D_MODEL // LANES)],
                             axis=1)
        term = info[:, lane_w:lane_w + 1] * jnp.dot(perm_t, ye.astype(BF16), preferred_element_type=F32)
        moe = term if moe is None else moe + term
    x1 = x1_ref[...]
    g2 = mod_ref[0][:, 5 * D_MODEL:6 * D_MODEL]
    x2 = x1 + g2 * moe
    ms = jnp.mean(x2 * x2, axis=-1, keepdims=True)
    y_ref[...] = (x2 * lax.rsqrt(ms + EPS)) * gf_ref[...]


def _final(seg_off, seg_len, x1, info, mod, g_final, ya_sorted, yb_sorted, *, tm, sample, tile_base):
    n = x1.shape[0]
    nt = n // tm
    mrows = mod.shape[1]
    if sample:
        mod_map = lambda i, *_: (0, 0, 0)
    else:
        tiles_per_batch = 4096 // tm
        mod_map = lambda i, *_: (i // tiles_per_batch, 0, 0)
    grid_spec = pltpu.PrefetchScalarGridSpec(
        num_scalar_prefetch=2,
        grid=(nt,),
        in_specs=[pl.BlockSpec((tm, D_MODEL), lambda i, *_: (i, 0)),
                  pl.BlockSpec((tm, LANES), lambda i, *_: (i, 0)),
                  pl.BlockSpec((1, mrows, 6 * D_MODEL), mod_map),
                  pl.BlockSpec((1, D_MODEL), lambda i, *_: (0, 0)),
                  pl.BlockSpec(memory_space=pl.ANY),
                  pl.BlockSpec(memory_space=pl.ANY)],
        out_specs=pl.BlockSpec((tm, D_MODEL), lambda i, *_: (i, 0)),
        scratch_shapes=[pltpu.VMEM((2, 2, tm * SUBLANES, LANES), F32),
                        pltpu.SemaphoreType.DMA((2, 2))])
    return pl.pallas_call(
        functools.partial(_final_body, tm=tm, n_tiles=nt, tile_base=tile_base),
        grid_spec=grid_spec,
        out_shape=jax.ShapeDtypeStruct((n, D_MODEL), F32),
        compiler_params=_cparams(("arbitrary",)),
        name="final_sample" if sample else "final_prompt",
    )(seg_off, seg_len, x1, info, mod, g_final.reshape(1, D_MODEL), ya_sorted, yb_sorted)


def _bucket_table(n):
    d = np.arange(n)
    max_exact = N_BUCKETS // 2
    nf = np.maximum(d, 1).astype(np.float64)
    large = max_exact + (np.log(nf / max_exact) / math.log(MAX_DISTANCE / max_exact)
                         * (N_BUCKETS - max_exact)).astype(np.int64)
    large = np.minimum(large, N_BUCKETS - 1)
    return np.where(d < max_exact, d, large).astype(np.int32)


def _toeplitz(v, n_rows, n_cols):
    length = n_rows + n_cols - 1
    lead = v.shape[:-1]
    vp = jnp.concatenate([v, jnp.zeros(lead + (1,), v.dtype)], axis=-1)
    skew = jnp.tile(vp, (1,) * len(lead) + (n_rows,))[..., :n_rows * length].reshape(lead + (n_rows, length))
    return skew[..., n_rows - 1:n_rows - 1 + n_cols]


def kernel(x_prompt, x_sample, c_prompt, c_sample, cache_k, cache_v, page_table, w_ada, b_ada, w_in, w_o,
           lam_q1, lam_k1, lam_q2, lam_k2, g_subln, rel_bias, g_sg_ln, b_sg_ln, w_s, b_s, w_rg, b_rg,
           w_re, b_re, w_gate, w_up, w_down, g_final):
    batch, seq, _ = x_prompt.shape
    dec_b, dec_t, _ = x_sample.shape
    n_pages = page_table.shape[1]
    n_p = batch * seq
    n_s = dec_b * dec_t
    n_tot = n_p + n_s
    assert w_in.shape[0] == 1 and cache_k.shape[1] == 1 and seq % T_ATT == 0 and n_pages % PAGES_PER_STEP == 0
    assert n_p % TM_TOK == 0 and n_p % n_s == 0 and n_tot % TM_E == 0 and dec_t == SUBLANES
    assert TAIL_TOKENS >= MAX_DISTANCE + dec_t and TAIL_TOKENS <= PAGES_PER_STEP * PAGE
    assert n_s == TM_D and n_p % TM_D == 0 and N_CLASSES <= LANES

    w_in_b = w_in[0].astype(BF16)
    w_vt_b = w_in[0][:, 2 * QK_W:3 * QK_W].T.astype(BF16)
    w_o_b = w_o[0].astype(BF16)
    wr_t = jnp.zeros((32, D_MODEL), F32).at[0:N_EG].set(w_rg[0].T).at[N_EG:N_EG + N_EG * EPG].set(w_re[0].T)
    wr_t = wr_t.astype(BF16)
    br = jnp.zeros((32, 1), F32).at[0:N_EG, 0].set(b_rg[0]).at[N_EG:N_EG + N_EG * EPG, 0].set(b_re[0])
    wg_b = w_gate[0].astype(BF16)
    wu_b = w_up[0].astype(BF16)
    wd_b = w_down[0].astype(BF16)
    ws_tril = jnp.tril(w_s[0])
    ws_p = ws_tril.astype(BF16)
    bs_p = b_s[0][:, :, None]
    eye = jnp.eye(dec_b, dtype=F32)
    ws_s = jnp.einsum('ab,gij->gaibj', eye, ws_tril[:, :dec_t, :dec_t]).reshape(
        N_GROUPS_SG, n_s, n_s).astype(BF16)
    bs_s = jnp.tile(b_s[0][:, :dec_t], (1, dec_b))[:, :, None]
    gln = g_sg_ln[0]
    bln = b_sg_ln[0]
    lam = (jnp.exp(jnp.sum(lam_q1[0] * lam_k1[0])) - jnp.exp(jnp.sum(lam_q2[0] * lam_k2[0]))
           + LAM_INIT).reshape(1).astype(F32)

    t = T_ATT
    n_dist = max(2 * t, TAIL_TOKENS + dec_t)
    onehot = np.eye(N_BUCKETS, dtype=np.float32)[_bucket_table(n_dist)]
    ft = jnp.dot(onehot, rel_bias - rel_bias[N_BUCKETS - 1], precision=lax.Precision.HIGHEST).T
    neg = lambda n: jnp.full((N_HEADS, n), NEG, F32)
    bias_diag = _toeplitz(jnp.concatenate([neg(t - 1), ft[:, 0:t]], axis=1), t, t)
    bias_sub = _toeplitz(ft[:, 1:2 * t], t, t)
    bias_t = jnp.stack([bias_sub, bias_diag], axis=1) * LOG2E
    bl = _toeplitz(jnp.flip(ft[:, 1:TAIL_TOKENS + dec_t], axis=1), dec_t, TAIL_TOKENS)
    head_eq = jnp.eye(N_HEADS, dtype=F32)
    bias_last = (bl[:, None, :, :, None] * head_eq[:, None, None, None, :])
    bias_last = jnp.broadcast_to(bias_last, (N_HEADS, 2, dec_t, TAIL_TOKENS, N_HEADS)).reshape(
        2 * N_HEADS * dec_t, TAIL_TOKENS * N_HEADS)
    bn = _toeplitz(jnp.concatenate([jnp.flip(ft[:, 0:dec_t], axis=1), neg(dec_t - 1)], axis=1),
                   dec_t, dec_t)
    bn = jnp.where(head_eq[:, None, None, :] > 0, bn[:, :, :, None], NEG)
    bn = jnp.broadcast_to(bn[:, None], (N_HEADS, 2, dec_t, dec_t, N_HEADS)).reshape(
        2 * N_HEADS * dec_t, dec_t * N_HEADS)
    bias_new = jnp.concatenate([bn, jnp.full((bn.shape[0], PAGE - bn.shape[1]), NEG, F32)], axis=1)

    c_all = jnp.concatenate([c_prompt, c_sample, jnp.zeros((4, D_MODEL), F32)], axis=0)
    mod_all = _ada(c_all, w_ada[0], b_ada[0])
    mod_p = mod_all[:batch].reshape(batch, 1, 6 * D_MODEL)
    mod_s = jnp.repeat(mod_all[batch:batch + dec_b], dec_t, axis=0).reshape(1, n_s, 6 * D_MODEL)

    xp = x_prompt.reshape(n_p, D_MODEL)
    xs = x_sample.reshape(n_s, D_MODEL)

    q_p, kf_p, kb_p, vf_p, vt_p, sg_p = _stage_a(xp, mod_p, w_in_b, w_vt_b, ws_p, bs_p, gln, bln,
                                                 tm=TM_TOK, chunk=CHUNK, sample=False)
    q_s, kf_s, vf_s, sg_s, vsn_s = _stage_a(xs, mod_s, w_in_b, w_vt_b, ws_s, bs_s, gln, bln,
                                            tm=n_s, chunk=n_s, sample=True)

    g_col = g_subln[0].reshape(HEAD_W, 1)
    g_row = g_subln[0].reshape(1, HEAD_W)
    o_p = _attn_prompt(lam, q_p, kb_p, vt_p, bias_t, g_col, batch=batch, seq=seq, t=t)
    pad = ((0, 0), (0, PAGE - dec_t * N_HEADS), (0, 0))
    knew = jnp.pad(kf_s.reshape(dec_b, dec_t * N_HEADS, HEAD_W), pad).astype(BF16)
    vnew = jnp.pad(vf_s.reshape(dec_b, dec_t * N_HEADS, HEAD_W), pad).astype(BF16)
    ck = cache_k.reshape(-1, HEAD_W)
    cv = cache_v.reshape(-1, HEAD_W)
    o_s = _attn_sample(page_table.reshape(-1), lam, q_s, knew, vnew, bias_last, bias_new, g_row, ck, cv,
                       dec_b=dec_b, n_pages=n_pages)

    x1_p, h2_p, cls_p, info_p = _stage_c(o_p, sg_p, xp, mod_p, w_o_b, wr_t, br, tm=TM_TOK, sample=False)
    x1_s, h2_s, cls_s, info_s = _stage_c(o_s, sg_s, xs, mod_s, w_o_b, wr_t, br, tm=n_s, sample=True)

    tm_e = TM_E
    tm_d = TM_D
    n_max = n_tot // tm_e + N_CLASSES
    n_dt = n_tot // tm_d
    cls = jnp.concatenate([cls_p[0], cls_s[0]]).reshape(n_dt, tm_d)
    classes = jnp.arange(N_CLASSES, dtype=jnp.int32)
    seg_len = jnp.sum((cls[:, :, None] == classes).astype(jnp.int32), axis=1)
    counts = jnp.sum(seg_len, axis=0)
    ntile_c = (counts + tm_e - 1) // tm_e
    tile_end = jnp.cumsum(ntile_c)
    class_base = (tile_end - ntile_c) * tm_e
    nact = tile_end[-1]
    seg_off = class_base[None, :] + jnp.cumsum(seg_len, axis=0) - seg_len
    pad_off = class_base + counts
    pad_len = ntile_c * tm_e - counts
    tile_ids = jnp.arange(n_max, dtype=jnp.int32)
    tile_cls = jnp.sum((tile_ids[:, None] >= tile_end[None, :]).astype(jnp.int32), axis=1)
    last_cls = jnp.sum((nact - 1 >= tile_end).astype(jnp.int32))
    tile_cls = jnp.where(tile_ids < nact, tile_cls, last_cls)
    grp = tile_cls // N_PAIRS
    pidx = tile_cls % N_PAIRS
    pair_lo = jnp.where(pidx < 3, 0, jnp.where(pidx < 5, 1, 2))
    pair_hi = jnp.where(pidx < 3, pidx + 1, jnp.where(pidx < 5, pidx - 1, 3))
    tile_ea = (grp * EPG + pair_lo).astype(jnp.int32)
    tile_eb = (grp * EPG + pair_hi).astype(jnp.int32)
    seg_off = seg_off.reshape(-1).astype(jnp.int32)
    seg_len = seg_len.reshape(-1)

    nact1 = nact.reshape(1).astype(jnp.int32)
    zero_rows = (jnp.sum(pad_len) + (n_max - nact) * tm_e).reshape(1).astype(jnp.int32)
    x_sorted = _dispatch(seg_off, seg_len, pad_off.astype(jnp.int32), pad_len.astype(jnp.int32), zero_rows,
                         nact1, h2_p, h2_s, info_p, info_s, tm=tm_d, n_slots=n_max * tm_e)
    ya_sorted, yb_sorted = _moe(tile_ea, tile_eb, nact1, x_sorted, wg_b, wu_b, wd_b, tm=tm_e, n_max=n_max)

    y_p = _final(seg_off, seg_len, x1_p, info_p, mod_p, g_final, ya_sorted, yb_sorted,
                 tm=tm_d, sample=False, tile_base=0)
    y_s = _final(seg_off, seg_len, x1_s, info_s, mod_s, g_final, ya_sorted, yb_sorted,
                 tm=tm_d, sample=True, tile_base=n_p // tm_d)

    return (y_p.reshape(batch, seq, D_MODEL),
            y_s.reshape(dec_b, dec_t, D_MODEL),
            kf_p.reshape(batch, 1, seq, N_HEADS, HEAD_W),
            vf_p.reshape(batch, 1, seq, N_HEADS, HEAD_W),
            kf_s.reshape(dec_b, 1, dec_t, N_HEADS, HEAD_W),
            vf_s.reshape(dec_b, 1, dec_t, N_HEADS, HEAD_W),
            vsn_s.reshape(dec_b, 1, dec_t, N_GROUPS_SG, SG_CH))
```

```python
import functools
import math

import numpy as np
import jax
import jax.numpy as jnp
from jax import lax
from jax.experimental import pallas as pl
from jax.experimental.pallas import tpu as pltpu

F32 = jnp.float32
BF16 = jnp.bfloat16

D_MODEL = 1024
N_HEADS = 4
DK = 64
HEAD_W = 128
QK_W = N_HEADS * HEAD_W
N_GROUPS_SG = 4
SG_CH = 128
IN_W = 2560
CHUNK = 128
PAGE = 128
N_BUCKETS = 32
MAX_DISTANCE = 128
N_EG = 4
EPG = 4
N_PAIRS = 6
N_CLASSES = N_EG * N_PAIRS
D_EXPERT = 512
EPS = 1e-6
LAM_INIT = 0.8 - 0.6 * math.exp(-0.3 * 0)
NEG = -1e30
LOG2E = math.log2(math.e)
LANES = 128
SUBLANES = 8

TM_TOK = 512
TQ_ATT = 1024
PAGES_PER_STEP = 16
TM_E = 256
TM_D = 256
VMEM_LIMIT = 56 * 1024 * 1024


def _cparams(sem):
    return pltpu.CompilerParams(dimension_semantics=sem, vmem_limit_bytes=VMEM_LIMIT)


def _ada_body(c_ref, w_ref, b_ref, o_ref):
    c = c_ref[...]
    a = (c * jax.nn.sigmoid(c)).astype(BF16)
    o_ref[...] = jnp.dot(a, w_ref[...].astype(BF16), preferred_element_type=F32) + b_ref[...]


def _ada(c_all, w_ada, b_ada):
    m = c_all.shape[0]
    n = w_ada.shape[1]
    tn = 1536
    return pl.pallas_call(
        _ada_body,
        grid=(n // tn,),
        in_specs=[pl.BlockSpec((m, D_MODEL), lambda j: (0, 0)),
                  pl.BlockSpec((D_MODEL, tn), lambda j: (0, j)),
                  pl.BlockSpec((1, tn), lambda j: (0, j))],
        out_specs=pl.BlockSpec((m, tn), lambda j: (0, j)),
        out_shape=jax.ShapeDtypeStruct((m, n), F32),
        compiler_params=_cparams(("arbitrary",)),
        name="adaln",
    )(c_all, w_ada, b_ada.reshape(1, n))


def _stage_a_body(x_ref, mod_ref, w_in_ref, w_vt_ref, ws_ref, bs_ref, gln_ref, bln_ref,
                  *out_refs, chunk, sample):
    if sample:
        q_ref, kf_ref, vf_ref, sg_ref, vsn_ref = out_refs
    else:
        q_ref, kf_ref, kb_ref, vf_ref, vt_ref, sg_ref = out_refs
    x = x_ref[...]
    tm = x.shape[0]
    mod = mod_ref[0]
    sh1 = mod[:, 0:D_MODEL]
    sc1 = mod[:, D_MODEL:2 * D_MODEL]
    ms = jnp.mean(x * x, axis=-1, keepdims=True)
    h = (x * lax.rsqrt(ms + EPS)) * (1.0 + sc1) + sh1
    hb = h.astype(BF16)
    z = jnp.dot(hb, w_in_ref[...], preferred_element_type=F32)
    q = z[:, 0:QK_W] * (DK ** -0.5)
    k = z[:, QK_W:2 * QK_W]
    v = z[:, 2 * QK_W:3 * QK_W]
    for hd in range(N_HEADS):
        kf_ref[pl.ds(hd, tm, stride=N_HEADS), :] = k[:, hd * HEAD_W:(hd + 1) * HEAD_W]
        vf_ref[pl.ds(hd, tm, stride=N_HEADS), :] = v[:, hd * HEAD_W:(hd + 1) * HEAD_W]
    if sample:
        q_ref[...] = q
    else:
        q_ref[...] = (q * LOG2E).astype(BF16)
        kb_ref[...] = k.astype(BF16)
        vt = lax.dot_general(w_vt_ref[...], hb, (((1,), (1,)), ((), ())),
                             preferred_element_type=F32)
        vt_ref[0] = vt.astype(BF16)
    u = z[:, 3 * QK_W:4 * QK_W]
    vs = z[:, 4 * QK_W:5 * QK_W]
    for g in range(N_GROUPS_SG):
        lo, hi = g * SG_CH, (g + 1) * SG_CH
        vg = vs[:, lo:hi]
        mu = jnp.mean(vg, axis=-1, keepdims=True)
        dv = vg - mu
        var = jnp.mean(dv * dv, axis=-1, keepdims=True)
        vn = (dv * lax.rsqrt(var + EPS)) * gln_ref[g:g + 1, :] + bln_ref[g:g + 1, :]
        if sample:
            vsn_ref[:, lo:hi] = vn
        vnb = vn.astype(BF16)
        for c in range(tm // chunk):
            r0, r1 = c * chunk, (c + 1) * chunk
            s = jnp.dot(ws_ref[g], vnb[r0:r1], preferred_element_type=F32) + bs_ref[g]
            sg_ref[r0:r1, lo:hi] = (u[r0:r1, lo:hi] * s).astype(BF16)


def _stage_a(x, mod, w_in_b, w_vt_b, ws, bs, gln, bln, *, tm, chunk, sample):
    n = x.shape[0]
    nt = n // tm
    mrows = mod.shape[1]
    row = lambda w, dt: jax.ShapeDtypeStruct((n, w), dt)
    blk = lambda w: pl.BlockSpec((tm, w), lambda i: (i, 0))
    cache_shape = jax.ShapeDtypeStruct((n * N_HEADS, HEAD_W), F32)
    cache_blk = pl.BlockSpec((tm * N_HEADS, HEAD_W), lambda i: (i, 0))
    if sample:
        out_shape = [row(QK_W, F32), cache_shape, cache_shape, row(QK_W, BF16), row(QK_W, F32)]
        out_specs = [blk(QK_W), cache_blk, cache_blk, blk(QK_W), blk(QK_W)]
        mod_map = lambda i: (0, 0, 0)
    else:
        out_shape = [row(QK_W, BF16), cache_shape, row(QK_W, BF16), cache_shape,
                     jax.ShapeDtypeStruct((nt, QK_W, tm), BF16), row(QK_W, BF16)]
        out_specs = [blk(QK_W), cache_blk, blk(QK_W), cache_blk,
                     pl.BlockSpec((1, QK_W, tm), lambda i: (i, 0, 0)), blk(QK_W)]
        tiles_per_batch = 4096 // tm
        mod_map = lambda i: (i // tiles_per_batch, 0, 0)
    full = lambda a: pl.BlockSpec(a.shape, lambda i: (0,) * a.ndim)
    return pl.pallas_call(
        functools.partial(_stage_a_body, chunk=chunk, sample=sample),
        grid=(nt,),
        in_specs=[blk(D_MODEL),
                  pl.BlockSpec((1, mrows, 6 * D_MODEL), mod_map),
                  full(w_in_b), full(w_vt_b), full(ws), full(bs), full(gln), full(bln)],
        out_specs=out_specs,
        out_shape=out_shape,
        compiler_params=_cparams(("arbitrary",)),
        name="stage_a_sample" if sample else "stage_a_prompt",
    )(x, mod, w_in_b, w_vt_b, ws, bs, gln, bln)


ATT_COLS = 256
BIAS_BLOCK = MAX_DISTANCE


def _attn_body(lam_ref, q_ref, k_ref, vt_ref, bias_ref, g_ref, o_ref, *scratch, tq, tk):
    n_chain = 2 * tq // ATT_COLS
    q2_refs, m_refs, l_refs, acc_refs = (scratch[i * n_chain:(i + 1) * n_chain] for i in range(4))
    qi = pl.program_id(2)
    for c in range(n_chain):
        q0 = (c * ATT_COLS) % tq
        q = q_ref[q0:q0 + ATT_COLS, :]
        lane = lax.broadcasted_iota(jnp.int32, q.shape, 1)
        keep = (lane < DK) if c < n_chain // 2 else (lane >= DK)
        q2_refs[c][...] = jnp.where(keep, q, jnp.zeros_like(q))
        m_refs[c][...] = jnp.full(m_refs[c].shape, NEG, F32)
        l_refs[c][...] = jnp.zeros(l_refs[c].shape, F32)
        acc_refs[c][...] = jnp.zeros(acc_refs[c].shape, F32)

    blk = BIAS_BLOCK

    kblocks = tk // blk

    def block_kinds(rel, q0):
        return [[(q0 // blk + b) - (rel + a) for b in range(ATT_COLS // blk)] for a in range(kblocks)]

    def keys_needed(rel, q0):
        if rel is None:
            return kblocks
        return sum(1 for row in block_kinds(rel, q0) if max(row) >= 0)

    def with_bias(s, rel, q0):
        if rel is None:
            return s
        kinds = block_kinds(rel, q0)[:s.shape[0] // blk]
        if all(d >= 2 for row in kinds for d in row):
            return s
        nxt, near = bias_ref[0, 0], bias_ref[0, 1]
        pick = lambda d: (jnp.full((blk, blk), NEG, F32) if d < 0 else near if d == 0 else nxt if d == 1
                          else jnp.zeros((blk, blk), F32))
        rows = []
        for a, row in enumerate(kinds):
            s_row = s[a * blk:(a + 1) * blk]
            if any(d < 2 for d in row):
                s_row = s_row + jnp.concatenate([pick(d) for d in row], axis=1)
            rows.append(s_row)
        return jnp.concatenate(rows, axis=0)

    def tiles(*work):
        scores = []
        for j, rel in work:
            for c in range(n_chain):
                nk = keys_needed(rel, (c * ATT_COLS) % tq) * blk
                if nk == 0:
                    scores.append(None)
                    continue
                k = k_ref[pl.ds(pl.multiple_of(j * tk, tk), nk), :]
                s = lax.dot_general(k, q2_refs[c][...], (((1,), (1,)), ((), ())),
                                    preferred_element_type=F32)
                scores.append(with_bias(s, rel, (c * ATT_COLS) % tq))
        for w, (j, _) in enumerate(work):
            for c in range(n_chain):
                s = scores[w * n_chain + c]
                if s is None:
                    continue
                vt = vt_ref[j, :, 0:s.shape[0]]
                m_old = m_refs[c][...]
                m_new = jnp.maximum(m_old, jnp.max(s, axis=0, keepdims=True))
                alpha = jnp.exp2(m_old - m_new)
                p = jnp.exp2(s - m_new)
                l_refs[c][...] = alpha * l_refs[c][...] + jnp.sum(p, axis=0, keepdims=True)
                acc_refs[c][...] = alpha * acc_refs[c][...] + jnp.dot(vt, p.astype(BF16),
                                                                      preferred_element_type=F32)
                m_refs[c][...] = m_new

    ratio = tq // tk
    first_diag = qi * ratio
    n_plain = jnp.maximum(first_diag - 1, 0)

    def plain_pair(jj, carry):
        tiles((2 * jj, None), (2 * jj + 1, None))
        return carry

    lax.fori_loop(0, n_plain // 2, plain_pair, 0)

    @pl.when(qi >= 1)
    def _():
        if ratio % 2 == 0:
            tiles((first_diag - 2, None), (first_diag - 1, -kblocks))
        else:
            @pl.when(n_plain % 2 == 1)
            def _():
                tiles((n_plain - 1, None))
            tiles((first_diag - 1, -kblocks))

    tiles(*[(first_diag + r, r * kblocks) for r in range(ratio)])

    lam = lam_ref[0]
    o_all = jnp.concatenate([acc_refs[c][...] * (1.0 / l_refs[c][...]) for c in range(n_chain)],
                            axis=1)
    o = o_all[:, 0:tq] - lam * o_all[:, tq:2 * tq]
    ms = jnp.mean(o * o, axis=0, keepdims=True)
    on = (o * lax.rsqrt(ms + EPS)) * g_ref[...] * (1.0 - LAM_INIT)
    o_ref[...] = on.T.astype(BF16)


def _attn_prompt(lam, q, kb, vt, bias_t, g_col, *, batch, seq, tq, tk):
    nq = seq // tq
    nk = seq // tk
    n = batch * seq
    n_chain = 2 * tq // ATT_COLS
    assert vt.shape == (batch * nk, QK_W, tk) and tq % tk == 0 and tk % BIAS_BLOCK == 0
    return pl.pallas_call(
        functools.partial(_attn_body, tq=tq, tk=tk),
        grid=(batch, N_HEADS, nq),
        in_specs=[pl.BlockSpec(memory_space=pltpu.SMEM),
                  pl.BlockSpec((tq, HEAD_W), lambda b, h, i: (b * nq + i, h)),
                  pl.BlockSpec((seq, HEAD_W), lambda b, h, i: (b, h)),
                  pl.BlockSpec((nk, HEAD_W, tk), lambda b, h, i: (b, h, 0)),
                  pl.BlockSpec((1, 2, BIAS_BLOCK, BIAS_BLOCK), lambda b, h, i: (h, 0, 0, 0)),
                  pl.BlockSpec((HEAD_W, 1), lambda b, h, i: (0, 0))],
        out_specs=pl.BlockSpec((tq, HEAD_W), lambda b, h, i: (b * nq + i, h)),
        out_shape=jax.ShapeDtypeStruct((n, QK_W), BF16),
        scratch_shapes=([pltpu.VMEM((ATT_COLS, HEAD_W), BF16)] * n_chain
                        + [pltpu.VMEM((1, ATT_COLS), F32)] * (2 * n_chain)
                        + [pltpu.VMEM((HEAD_W, ATT_COLS), F32)] * n_chain),
        compiler_params=_cparams(("arbitrary", "arbitrary", "arbitrary")),
        name="attn_prompt",
    )(lam, q, kb, vt, bias_t, g_col)


PAGE_ROWS = PAGE * N_HEADS
TAIL_TOKENS = 2 * PAGE


def _sattn_body(pt_ref, lam_ref, q_ref, knew_ref, vnew_ref, bl_ref, bn_ref, g_ref, ck_ref, cv_ref,
                o_ref, kbuf, vbuf, sem, mask_ref, m_ref, l_ref, acc_ref, *, pages, n_steps, total):
    b = pl.program_id(0)
    s = pl.program_id(1)
    step = b * n_steps + s
    slot = step % 2

    def page_copies(step_idx, sl):
        base = step_idx * pages
        out = []
        for i in range(pages):
            src = pl.ds(pl.multiple_of(pt_ref[base + i] * PAGE_ROWS, PAGE_ROWS), PAGE_ROWS)
            dst = pl.ds(i * PAGE_ROWS, PAGE_ROWS)
            out.append(pltpu.make_async_copy(ck_ref.at[src], kbuf.at[sl, dst], sem.at[sl, 0]))
            out.append(pltpu.make_async_copy(cv_ref.at[src], vbuf.at[sl, dst], sem.at[sl, 1]))
        return out

    def start_all(copies):
        for n, c in enumerate(copies):
            c.start(priority=n % 2)

    @pl.when(step == 0)
    def _():
        start_all(page_copies(0, 0))

    @pl.when(step + 1 < total)
    def _():
        start_all(page_copies(step + 1, 1 - slot))

    for c in page_copies(step, slot):
        c.wait()

    @pl.when(s == 0)
    def _():
        m_ref[...] = jnp.full(m_ref.shape, NEG, F32)
        l_ref[...] = jnp.zeros(l_ref.shape, F32)
        acc_ref[...] = jnp.zeros(acc_ref.shape, F32)

    @pl.when(step == 0)
    def _():
        row = lax.broadcasted_iota(jnp.int32, mask_ref.shape, 0)
        col = lax.broadcasted_iota(jnp.int32, mask_ref.shape, 1)
        same_head = (col % N_HEADS) == (row // (2 * SUBLANES))
        mask_ref[...] = jnp.where(same_head, 0.0, NEG)

    q = q_ref[...]
    lane = lax.broadcasted_iota(jnp.int32, (SUBLANES, HEAD_W), 1)
    pieces = []
    for h in range(N_HEADS):
        qh = q[:, h * HEAD_W:(h + 1) * HEAD_W]
        pieces += [jnp.where(lane < DK, qh, 0.0), jnp.where(lane >= DK, qh, 0.0)]
    qm = jnp.concatenate(pieces, axis=0).astype(BF16)

    def update(kb, vb, bias):
        sc = lax.dot_general(qm, kb, (((1,), (1,)), ((), ())),
                             preferred_element_type=F32) + bias
        m_old = m_ref[...]
        m_new = jnp.maximum(m_old, jnp.max(sc, axis=1, keepdims=True))
        alpha = jnp.exp(m_old - m_new)
        p = jnp.exp(sc - m_new)
        l_ref[...] = alpha * l_ref[...] + jnp.sum(p, axis=1, keepdims=True)
        acc_ref[...] = alpha * acc_ref[...] + jnp.dot(p.astype(BF16), vb,
                                                      preferred_element_type=F32)
        m_ref[...] = m_new

    is_last = s == n_steps - 1
    head_cols = mask_ref.shape[1] - bl_ref.shape[1]
    update(kbuf[slot].astype(BF16), vbuf[slot].astype(BF16),
           jnp.concatenate([mask_ref[:, :head_cols],
                            mask_ref[:, head_cols:] + bl_ref[...] * is_last.astype(F32)], axis=1))

    @pl.when(is_last)
    def _():
        update(knew_ref[0], vnew_ref[0], bn_ref[...])
        lam = lam_ref[0]
        o_all = acc_ref[...] * (1.0 / l_ref[...])
        for h in range(N_HEADS):
            r = h * 2 * SUBLANES
            o = o_all[r:r + SUBLANES] - lam * o_all[r + SUBLANES:r + 2 * SUBLANES]
            ms = jnp.mean(o * o, axis=-1, keepdims=True)
            o_ref[:, h * HEAD_W:(h + 1) * HEAD_W] = ((o * lax.rsqrt(ms + EPS)) * g_ref[...]
                                                     * (1.0 - LAM_INIT))


def _attn_sample(page_table_flat, lam, q_s, knew, vnew, bias_last, bias_new, g_row, cache_k, cache_v,
                 *, dec_b, n_pages):
    pages = PAGES_PER_STEP
    n_steps = n_pages // pages
    total = dec_b * n_steps
    nq = q_s.shape[0] // dec_b
    n_rows = 2 * N_HEADS * nq
    step_rows = pages * PAGE_ROWS
    grid_spec = pltpu.PrefetchScalarGridSpec(
        num_scalar_prefetch=1,
        grid=(dec_b, n_steps),
        in_specs=[pl.BlockSpec(memory_space=pltpu.SMEM),
                  pl.BlockSpec((nq, QK_W), lambda b, s, pt: (b, 0)),
                  pl.BlockSpec((1, PAGE, HEAD_W), lambda b, s, pt: (b, 0, 0)),
                  pl.BlockSpec((1, PAGE, HEAD_W), lambda b, s, pt: (b, 0, 0)),
                  pl.BlockSpec(bias_last.shape, lambda b, s, pt: (0, 0)),
                  pl.BlockSpec(bias_new.shape, lambda b, s, pt: (0, 0)),
                  pl.BlockSpec((1, HEAD_W), lambda b, s, pt: (0, 0)),
                  pl.BlockSpec(memory_space=pl.ANY),
                  pl.BlockSpec(memory_space=pl.ANY)],
        out_specs=pl.BlockSpec((nq, QK_W), lambda b, s, pt: (b, 0)),
        scratch_shapes=[pltpu.VMEM((2, step_rows, HEAD_W), F32),
                        pltpu.VMEM((2, step_rows, HEAD_W), F32),
                        pltpu.SemaphoreType.DMA((2, 2)),
                        pltpu.VMEM((n_rows, step_rows), F32),
                        pltpu.VMEM((n_rows, 1), F32), pltpu.VMEM((n_rows, 1), F32),
                        pltpu.VMEM((n_rows, HEAD_W), F32)])
    return pl.pallas_call(
        functools.partial(_sattn_body, pages=pages, n_steps=n_steps, total=total),
        grid_spec=grid_spec,
        out_shape=jax.ShapeDtypeStruct(q_s.shape, F32),
        compiler_params=_cparams(("arbitrary", "arbitrary")),
        name="attn_sample",
    )(page_table_flat, lam, q_s, knew, vnew, bias_last, bias_new, g_row, cache_k, cache_v)


INFO_W_LO, INFO_W_HI, INFO_CLS = 0, 1, 2


def _stage_c_body(o_ref, sg_ref, x_ref, mod_ref, wo_ref, wr_ref, br_ref, x1_ref, h2_ref, cls_ref, info_ref):
    x = x_ref[...]
    tm = x.shape[0]
    mod = mod_ref[0]
    g1 = mod[:, 2 * D_MODEL:3 * D_MODEL]
    sh2 = mod[:, 3 * D_MODEL:4 * D_MODEL]
    sc2 = mod[:, 4 * D_MODEL:5 * D_MODEL]
    mix = (jnp.dot(o_ref[...].astype(BF16), wo_ref[0:QK_W, :], preferred_element_type=F32)
           + jnp.dot(sg_ref[...], wo_ref[QK_W:2 * QK_W, :], preferred_element_type=F32))
    x1 = x + g1 * mix
    x1_ref[...] = x1
    ms = jnp.mean(x1 * x1, axis=-1, keepdims=True)
    h2 = ((x1 * lax.rsqrt(ms + EPS)) * (1.0 + sc2) + sh2).astype(BF16)
    h2_ref[...] = h2
    lg = lax.dot_general(wr_ref[...], h2, (((1,), (1,)), ((), ())),
                         preferred_element_type=F32) + br_ref[...]
    gl = [lg[i:i + 1, :] for i in range(N_EG)]
    el = [lg[N_EG + i:N_EG + i + 1, :] for i in range(N_EG * EPG)]
    gmax = jnp.maximum(jnp.maximum(gl[0], gl[1]), jnp.maximum(gl[2], gl[3]))
    gi = jnp.where(gl[0] == gmax, 0, jnp.where(gl[1] == gmax, 1, jnp.where(gl[2] == gmax, 2, 3)))
    gsum = (jnp.exp(gl[0] - gmax) + jnp.exp(gl[1] - gmax)
            + jnp.exp(gl[2] - gmax) + jnp.exp(gl[3] - gmax))
    gp = 1.0 / gsum
    sel = [jnp.where(gi == 0, el[j], jnp.where(gi == 1, el[EPG + j],
                                               jnp.where(gi == 2, el[2 * EPG + j], el[3 * EPG + j])))
           for j in range(EPG)]
    v0 = jnp.maximum(jnp.maximum(sel[0], sel[1]), jnp.maximum(sel[2], sel[3]))
    i0 = jnp.where(sel[0] == v0, 0, jnp.where(sel[1] == v0, 1, jnp.where(sel[2] == v0, 2, 3)))
    rest = [jnp.where(i0 == j, -3e38, sel[j]) for j in range(EPG)]
    v1 = jnp.maximum(jnp.maximum(rest[0], rest[1]), jnp.maximum(rest[2], rest[3]))
    i1 = jnp.where(rest[0] == v1, 0, jnp.where(rest[1] == v1, 1, jnp.where(rest[2] == v1, 2, 3)))
    e1 = jnp.exp(v1 - v0)
    den = 1.0 / (1.0 + e1)
    tw0 = den * gp
    tw1 = e1 * den * gp
    first_low = i0 < i1
    lo = jnp.where(first_low, i0, i1)
    hi = jnp.where(first_low, i1, i0)
    w_lo = jnp.where(first_low, tw0, tw1)
    w_hi = jnp.where(first_low, tw1, tw0)
    pair = jnp.where(lo == 0, 0, jnp.where(lo == 1, 3, 5)) + hi - lo - 1
    cls = gi * N_PAIRS + pair
    cls_ref[...] = jnp.broadcast_to(cls, cls_ref.shape).astype(jnp.int32)
    row = lax.broadcasted_iota(jnp.int32, (LANES, tm), 0)
    rec = jnp.where(row == INFO_W_LO, w_lo, jnp.where(row == INFO_W_HI, w_hi,
                                                      jnp.where(row == INFO_CLS, cls.astype(F32), 0.0)))
    info_ref[...] = rec.T


def _stage_c(o, sg, x, mod, wo_b, wr_t, br, *, tm, sample):
    n = x.shape[0]
    nt = n // tm
    mrows = mod.shape[1]
    blk = lambda w: pl.BlockSpec((tm, w), lambda i: (i, 0))
    full = lambda a: pl.BlockSpec(a.shape, lambda i: (0,) * a.ndim)
    if sample:
        mod_map = lambda i: (0, 0, 0)
    else:
        tiles_per_batch = 4096 // tm
        mod_map = lambda i: (i // tiles_per_batch, 0, 0)
    return pl.pallas_call(
        _stage_c_body,
        grid=(nt,),
        in_specs=[blk(QK_W), blk(QK_W), blk(D_MODEL),
                  pl.BlockSpec((1, mrows, 6 * D_MODEL), mod_map),
                  full(wo_b), full(wr_t), full(br)],
        out_specs=[blk(D_MODEL), blk(D_MODEL),
                   pl.BlockSpec((SUBLANES, tm), lambda i: (0, i)),
                   blk(LANES)],
        out_shape=[jax.ShapeDtypeStruct((n, D_MODEL), F32),
                   jax.ShapeDtypeStruct((n, D_MODEL), BF16),
                   jax.ShapeDtypeStruct((SUBLANES, n), jnp.int32),
                   jax.ShapeDtypeStruct((n, LANES), F32)],
        compiler_params=_cparams(("arbitrary",)),
        name="stage_c_sample" if sample else "stage_c_prompt",
    )(o, sg, x, mod, wo_b, wr_t, br)


def rows8(ref, start, count):
    scale = lambda v: v * SUBLANES if isinstance(v, int) else pl.multiple_of(v * SUBLANES, SUBLANES)
    return ref.at[pl.ds(scale(start), scale(count))]


def _perm_t(cls_col):
    n = cls_col.shape[0]
    lane = lax.broadcasted_iota(jnp.int32, (n, LANES), 1).astype(F32)
    onehot = (lane == cls_col).astype(BF16)
    r = lax.broadcasted_iota(jnp.int32, (n, n), 0)
    c = lax.broadcasted_iota(jnp.int32, (n, n), 1)
    before = (c < r).astype(BF16)
    rank = jnp.dot(before, onehot, preferred_element_type=F32)
    cnt = jnp.sum(onehot.astype(F32), axis=0, keepdims=True)
    cr = lax.broadcasted_iota(jnp.int32, (LANES, LANES), 0)
    cc = lax.broadcasted_iota(jnp.int32, (LANES, LANES), 1)
    lower_cls = (cr < cc).astype(BF16)
    base = jnp.dot(jnp.broadcast_to(cnt, (SUBLANES, LANES)).astype(BF16), lower_cls,
                   preferred_element_type=F32)[0:1, :]
    pos = jnp.sum(onehot.astype(F32) * (base + rank), axis=1, keepdims=True)
    dest = lax.broadcasted_iota(jnp.int32, (n, n), 1).astype(F32)
    return (dest == pos).astype(F32)


def _dispatch_body(soff_ref, slen_ref, poff_ref, plen_ref, ptot_ref, nact_ref, hp_ref, hs_ref, ip_ref, is_ref,
                   xs_ref, buf, zbuf, sem, zsem, *, tm, n_tiles, n_slab_tiles):
    i = pl.program_id(0)
    slot = i % 2
    is_sample = i == n_tiles - 1

    def wait_tile(sl):
        pltpu.make_async_copy(buf.at[sl], buf.at[sl], sem.at[sl]).wait()

    @pl.when(i == 0)
    def _():
        zbuf[...] = jnp.zeros(zbuf.shape, F32)
        for c in range(N_CLASSES):
            @pl.when(plen_ref[c] > 0)
            def _():
                pltpu.make_async_copy(rows8(zbuf, 0, plen_ref[c]), rows8(xs_ref, poff_ref[c], plen_ref[c]),
                                      zsem).start()
        for j in range(n_slab_tiles - N_CLASSES, n_slab_tiles):
            @pl.when(j >= nact_ref[0])
            def _():
                pltpu.make_async_copy(zbuf, rows8(xs_ref, j * TM_E, TM_E), zsem).start()

    x = jnp.where(is_sample, hs_ref[...], hp_ref[...])
    info = jnp.where(is_sample, is_ref[...], ip_ref[...])
    perm = _perm_t(info[:, INFO_CLS:INFO_CLS + 1]).T.astype(BF16)
    xp = jnp.dot(perm, x, preferred_element_type=F32)

    @pl.when(i >= 2)
    def _():
        wait_tile(slot)

    bs = buf.at[slot]
    for c in range(D_MODEL // LANES):
        bs[pl.ds(c, tm, stride=SUBLANES), :] = xp[:, c * LANES:(c + 1) * LANES]
    local = 0
    for c in range(N_CLASSES):
        n_rows = slen_ref[i * N_CLASSES + c]

        @pl.when(n_rows > 0)
        def _():
            pltpu.make_async_copy(rows8(bs, local, n_rows),
                                  rows8(xs_ref, soff_ref[i * N_CLASSES + c], n_rows), sem.at[slot]).start()
        local = local + n_rows

    @pl.when(i == n_tiles - 1)
    def _():
        wait_tile(slot)
        if n_tiles >= 2:
            wait_tile(1 - slot)

        @pl.when(ptot_ref[0] > 0)
        def _():
            n = pl.multiple_of(ptot_ref[0] * SUBLANES, SUBLANES)
            pltpu.make_async_copy(xs_ref.at[pl.ds(0, n)], xs_ref.at[pl.ds(0, n)], zsem).wait()


def _dispatch(seg_off, seg_len, pad_off, pad_len, pad_tot, nact, h2_p, h2_s, info_p, info_s, *, tm, n_slots):
    n_prompt_tiles = h2_p.shape[0] // tm
    n_tiles = n_prompt_tiles + 1
    assert h2_s.shape[0] == tm
    last_p = n_prompt_tiles - 1
    grid_spec = pltpu.PrefetchScalarGridSpec(
        num_scalar_prefetch=6,
        grid=(n_tiles,),
        in_specs=[pl.BlockSpec((tm, D_MODEL), lambda i, *_: (jnp.minimum(i, last_p), 0)),
                  pl.BlockSpec((tm, D_MODEL), lambda i, *_: (0, 0)),
                  pl.BlockSpec((tm, LANES), lambda i, *_: (jnp.minimum(i, last_p), 0)),
                  pl.BlockSpec((tm, LANES), lambda i, *_: (0, 0))],
        out_specs=pl.BlockSpec(memory_space=pl.ANY),
        scratch_shapes=[pltpu.VMEM((2, tm * SUBLANES, LANES), F32),
                        pltpu.VMEM((TM_E * SUBLANES, LANES), F32),
                        pltpu.SemaphoreType.DMA((2,)),
                        pltpu.SemaphoreType.DMA(())])
    return pl.pallas_call(
        functools.partial(_dispatch_body, tm=tm, n_tiles=n_tiles, n_slab_tiles=n_slots // TM_E),
        grid_spec=grid_spec,
        out_shape=jax.ShapeDtypeStruct((n_slots * SUBLANES, LANES), F32),
        compiler_params=_cparams(("arbitrary",)),
        name="moe_dispatch",
    )(seg_off, seg_len, pad_off, pad_len, pad_tot, nact, h2_p, h2_s, info_p, info_s)


def _moe_body(ea_ref, eb_ref, nact_ref, x_ref, wga_ref, wgb_ref, wua_ref, wub_ref, wda_ref, wdb_ref,
              ya_ref, yb_ref, *, tm):
    i = pl.program_id(0)

    @pl.when(i < nact_ref[0])
    def _():
        x = jnp.concatenate([x_ref[pl.ds(c, tm, stride=SUBLANES), :] for c in range(D_MODEL // LANES)],
                            axis=1).astype(BF16)

        def expert(wg_ref, wu_ref, wd_ref):
            gate = jnp.dot(x, wg_ref[0], preferred_element_type=F32)
            up = jnp.dot(x, wu_ref[0], preferred_element_type=F32)
            he = (gate * jax.nn.sigmoid(gate)) * up
            return jnp.dot(he.astype(BF16), wd_ref[0], preferred_element_type=F32)

        ya = expert(wga_ref, wua_ref, wda_ref)
        yb = expert(wgb_ref, wub_ref, wdb_ref)
        for c in range(D_MODEL // LANES):
            ya_ref[pl.ds(c, tm, stride=SUBLANES), :] = ya[:, c * LANES:(c + 1) * LANES]
            yb_ref[pl.ds(c, tm, stride=SUBLANES), :] = yb[:, c * LANES:(c + 1) * LANES]

    @pl.when(i >= nact_ref[0])
    def _():
        ya_ref[...] = jnp.zeros(ya_ref.shape, ya_ref.dtype)
        yb_ref[...] = jnp.zeros(yb_ref.shape, yb_ref.dtype)


def _moe(tile_ea, tile_eb, nact, x_sorted, wg_b, wu_b, wd_b, *, tm, n_max):
    wspec_in = lambda sel: pl.BlockSpec((1, D_MODEL, D_EXPERT), sel)
    wspec_out = lambda sel: pl.BlockSpec((1, D_EXPERT, D_MODEL), sel)
    sel_a = lambda i, ea, eb, na: (ea[i], 0, 0)
    sel_b = lambda i, ea, eb, na: (eb[i], 0, 0)
    rows_in = lambda i, ea, eb, na: (jnp.minimum(i, na[0] - 1), 0)
    grid_spec = pltpu.PrefetchScalarGridSpec(
        num_scalar_prefetch=3,
        grid=(n_max,),
        in_specs=[pl.BlockSpec((tm * SUBLANES, LANES), rows_in),
                  wspec_in(sel_a), wspec_in(sel_b), wspec_in(sel_a), wspec_in(sel_b),
                  wspec_out(sel_a), wspec_out(sel_b)],
        out_specs=[pl.BlockSpec((tm * SUBLANES, LANES), lambda i, ea, eb, na: (i, 0))] * 2)
    return pl.pallas_call(
        functools.partial(_moe_body, tm=tm),
        grid_spec=grid_spec,
        out_shape=[jax.ShapeDtypeStruct(x_sorted.shape, F32)] * 2,
        compiler_params=_cparams(("arbitrary",)),
        name="moe",
    )(tile_ea, tile_eb, nact, x_sorted, wg_b, wg_b, wu_b, wu_b, wd_b, wd_b)


def _final_body(soff_ref, slen_ref, x1_ref, info_ref, mod_ref, gf_ref, ya_ref, yb_ref, y_ref, buf, sem,
                *, tm, n_tiles, tile_base):
    i = pl.program_id(0)
    slot = i % 2
    slabs = (ya_ref, yb_ref)

    def fetch(tile, sl):
        local = 0
        for c in range(N_CLASSES):
            k = (tile + tile_base) * N_CLASSES + c
            n_rows = slen_ref[k]

            @pl.when(n_rows > 0)
            def _():
                for e in range(2):
                    pltpu.make_async_copy(rows8(slabs[e], soff_ref[k], n_rows),
                                          rows8(buf.at[sl, e], local, n_rows), sem.at[sl, e]).start()
            local = local + n_rows

    @pl.when(i == 0)
    def _():
        fetch(0, 0)

    @pl.when(i + 1 < n_tiles)
    def _():
        fetch(i + 1, 1 - slot)

    info = info_ref[...]
    perm_t = _perm_t(info[:, INFO_CLS:INFO_CLS + 1]).astype(BF16)
    moe = None
    for e, lane_w in enumerate((INFO_W_LO, INFO_W_HI)):
        pltpu.make_async_copy(buf.at[slot, e], buf.at[slot, e], sem.at[slot, e]).wait()
        bs = buf.at[slot, e]
        ye = jnp.concatenate([bs[pl.ds(c, tm, stride=SUBLANES), :] for c in range(D_MODEL // LANES)],
                             axis=1)
        term = info[:, lane_w:lane_w + 1] * jnp.dot(perm_t, ye.astype(BF16), preferred_element_type=F32)
        moe = term if moe is None else moe + term
    x1 = x1_ref[...]
    g2 = mod_ref[0][:, 5 * D_MODEL:6 * D_MODEL]
    x2 = x1 + g2 * moe
    ms = jnp.mean(x2 * x2, axis=-1, keepdims=True)
    y_ref[...] = (x2 * lax.rsqrt(ms + EPS)) * gf_ref[...]


def _final(seg_off, seg_len, x1, info, mod, g_final, ya_sorted, yb_sorted, *, tm, sample, tile_base):
    n = x1.shape[0]
    nt = n // tm
    mrows = mod.shape[1]
    if sample:
        mod_map = lambda i, *_: (0, 0, 0)
    else:
        tiles_per_batch = 4096 // tm
        mod_map = lambda i, *_: (i // tiles_per_batch, 0, 0)
    grid_spec = pltpu.PrefetchScalarGridSpec(
        num_scalar_prefetch=2,
        grid=(nt,),
        in_specs=[pl.BlockSpec((tm, D_MODEL), lambda i, *_: (i, 0)),
                  pl.BlockSpec((tm, LANES), lambda i, *_: (i, 0)),
                  pl.BlockSpec((1, mrows, 6 * D_MODEL), mod_map),
                  pl.BlockSpec((1, D_MODEL), lambda i, *_: (0, 0)),
                  pl.BlockSpec(memory_space=pl.ANY),
                  pl.BlockSpec(memory_space=pl.ANY)],
        out_specs=pl.BlockSpec((tm, D_MODEL), lambda i, *_: (i, 0)),
        scratch_shapes=[pltpu.VMEM((2, 2, tm * SUBLANES, LANES), F32),
                        pltpu.SemaphoreType.DMA((2, 2))])
    return pl.pallas_call(
        functools.partial(_final_body, tm=tm, n_tiles=nt, tile_base=tile_base),
        grid_spec=grid_spec,
        out_shape=jax.ShapeDtypeStruct((n, D_MODEL), F32),
        compiler_params=_cparams(("arbitrary",)),
        name="final_sample" if sample else "final_prompt",
    )(seg_off, seg_len, x1, info, mod, g_final.reshape(1, D_MODEL), ya_sorted, yb_sorted)


def _bucket_table(n):
    d = np.arange(n)
    max_exact = N_BUCKETS // 2
    nf = np.maximum(d, 1).astype(np.float64)
    large = max_exact + (np.log(nf / max_exact) / math.log(MAX_DISTANCE / max_exact)
                         * (N_BUCKETS - max_exact)).astype(np.int64)
    large = np.minimum(large, N_BUCKETS - 1)
    return np.where(d < max_exact, d, large).astype(np.int32)


def _toeplitz(v, n_rows, n_cols):
    length = n_rows + n_cols - 1
    lead = v.shape[:-1]
    vp = jnp.concatenate([v, jnp.zeros(lead + (1,), v.dtype)], axis=-1)
    skew = jnp.tile(vp, (1,) * len(lead) + (n_rows,))[..., :n_rows * length].reshape(lead + (n_rows, length))
    return skew[..., n_rows - 1:n_rows - 1 + n_cols]


def kernel(x_prompt, x_sample, c_prompt, c_sample, cache_k, cache_v, page_table, w_ada, b_ada, w_in, w_o,
           lam_q1, lam_k1, lam_q2, lam_k2, g_subln, rel_bias, g_sg_ln, b_sg_ln, w_s, b_s, w_rg, b_rg,
           w_re, b_re, w_gate, w_up, w_down, g_final):
    batch, seq, _ = x_prompt.shape
    dec_b, dec_t, _ = x_sample.shape
    n_pages = page_table.shape[1]
    n_p = batch * seq
    n_s = dec_b * dec_t
    n_tot = n_p + n_s
    assert w_in.shape[0] == 1 and cache_k.shape[1] == 1 and seq % TQ_ATT == 0 and n_pages % PAGES_PER_STEP == 0
    assert n_p % TM_TOK == 0 and n_p % n_s == 0 and n_tot % TM_E == 0 and dec_t == SUBLANES
    assert TAIL_TOKENS >= MAX_DISTANCE + dec_t and TAIL_TOKENS <= PAGES_PER_STEP * PAGE
    assert n_s == TM_D and n_p % TM_D == 0 and N_CLASSES <= LANES

    w_in_b = w_in[0].astype(BF16)
    w_vt_b = w_in[0][:, 2 * QK_W:3 * QK_W].T.astype(BF16)
    w_o_b = w_o[0].astype(BF16)
    wr_t = jnp.zeros((32, D_MODEL), F32).at[0:N_EG].set(w_rg[0].T).at[N_EG:N_EG + N_EG * EPG].set(w_re[0].T)
    wr_t = wr_t.astype(BF16)
    br = jnp.zeros((32, 1), F32).at[0:N_EG, 0].set(b_rg[0]).at[N_EG:N_EG + N_EG * EPG, 0].set(b_re[0])
    wg_b = w_gate[0].astype(BF16)
    wu_b = w_up[0].astype(BF16)
    wd_b = w_down[0].astype(BF16)
    ws_tril = jnp.tril(w_s[0])
    ws_p = ws_tril.astype(BF16)
    bs_p = b_s[0][:, :, None]
    same_seq = np.kron(np.eye(dec_b, dtype=np.float32), np.ones((dec_t, dec_t), np.float32))
    ws_s = (jnp.tile(ws_tril[:, :dec_t, :dec_t], (1, dec_b, dec_b)) * same_seq).astype(BF16)
    bs_s = jnp.tile(b_s[0][:, :dec_t], (1, dec_b))[:, :, None]
    gln = g_sg_ln[0]
    bln = b_sg_ln[0]
    lam = (jnp.exp(jnp.sum(lam_q1[0] * lam_k1[0])) - jnp.exp(jnp.sum(lam_q2[0] * lam_k2[0]))
           + LAM_INIT).reshape(1).astype(F32)

    blk = BIAS_BLOCK
    n_dist = max(2 * blk, TAIL_TOKENS + dec_t)
    onehot = np.eye(N_BUCKETS, dtype=np.float32)[_bucket_table(n_dist)]
    ft = jnp.dot(onehot, rel_bias - rel_bias[N_BUCKETS - 1], precision=lax.Precision.HIGHEST).T
    neg = lambda n: jnp.full((N_HEADS, n), NEG, F32)
    bias_near = _toeplitz(jnp.concatenate([neg(blk - 1), ft[:, 0:blk]], axis=1), blk, blk)
    bias_next = _toeplitz(ft[:, 1:2 * blk], blk, blk)
    bias_t = jnp.stack([bias_next, bias_near], axis=1) * LOG2E
    bl = _toeplitz(jnp.flip(ft[:, 1:TAIL_TOKENS + dec_t], axis=1), dec_t, TAIL_TOKENS)
    head_eq = jnp.eye(N_HEADS, dtype=F32)
    bias_last = (bl[:, None, :, :, None] * head_eq[:, None, None, None, :])
    bias_last = jnp.broadcast_to(bias_last, (N_HEADS, 2, dec_t, TAIL_TOKENS, N_HEADS)).reshape(
        2 * N_HEADS * dec_t, TAIL_TOKENS * N_HEADS)
    bn = _toeplitz(jnp.concatenate([jnp.flip(ft[:, 0:dec_t], axis=1), neg(dec_t - 1)], axis=1),
                   dec_t, dec_t)
    bn = jnp.where(head_eq[:, None, None, :] > 0, bn[:, :, :, None], NEG)
    bn = jnp.broadcast_to(bn[:, None], (N_HEADS, 2, dec_t, dec_t, N_HEADS)).reshape(
        2 * N_HEADS * dec_t, dec_t * N_HEADS)
    bias_new = jnp.concatenate([bn, jnp.full((bn.shape[0], PAGE - bn.shape[1]), NEG, F32)], axis=1)

    c_all = jnp.concatenate([c_prompt, c_sample, jnp.zeros((4, D_MODEL), F32)], axis=0)
    mod_all = _ada(c_all, w_ada[0], b_ada[0])
    mod_p = mod_all[:batch].reshape(batch, 1, 6 * D_MODEL)
    mod_s = jnp.repeat(mod_all[batch:batch + dec_b], dec_t, axis=0).reshape(1, n_s, 6 * D_MODEL)

    xp = x_prompt.reshape(n_p, D_MODEL)
    xs = x_sample.reshape(n_s, D_MODEL)

    q_p, kf_p, kb_p, vf_p, vt_p, sg_p = _stage_a(xp, mod_p, w_in_b, w_vt_b, ws_p, bs_p, gln, bln,
                                                 tm=TM_TOK, chunk=CHUNK, sample=False)
    q_s, kf_s, vf_s, sg_s, vsn_s = _stage_a(xs, mod_s, w_in_b, w_vt_b, ws_s, bs_s, gln, bln,
                                            tm=n_s, chunk=n_s, sample=True)

    g_col = g_subln[0].reshape(HEAD_W, 1)
    g_row = g_subln[0].reshape(1, HEAD_W)
    o_p = _attn_prompt(lam, q_p, kb_p, vt_p, bias_t, g_col, batch=batch, seq=seq, tq=TQ_ATT, tk=TM_TOK)
    pad = ((0, 0), (0, PAGE - dec_t * N_HEADS), (0, 0))
    knew = jnp.pad(kf_s.reshape(dec_b, dec_t * N_HEADS, HEAD_W), pad).astype(BF16)
    vnew = jnp.pad(vf_s.reshape(dec_b, dec_t * N_HEADS, HEAD_W), pad).astype(BF16)
    ck = cache_k.reshape(-1, HEAD_W)
    cv = cache_v.reshape(-1, HEAD_W)
    o_s = _attn_sample(page_table.reshape(-1), lam, q_s, knew, vnew, bias_last, bias_new, g_row, ck, cv,
                       dec_b=dec_b, n_pages=n_pages)

    x1_p, h2_p, cls_p, info_p = _stage_c(o_p, sg_p, xp, mod_p, w_o_b, wr_t, br, tm=TM_TOK, sample=False)
    x1_s, h2_s, cls_s, info_s = _stage_c(o_s, sg_s, xs, mod_s, w_o_b, wr_t, br, tm=n_s, sample=True)

    tm_e = TM_E
    tm_d = TM_D
    n_max = n_tot // tm_e + N_CLASSES
    n_dt = n_tot // tm_d
    cls = jnp.concatenate([cls_p[0], cls_s[0]]).reshape(n_dt, tm_d)
    classes = jnp.arange(N_CLASSES, dtype=jnp.int32)
    seg_len = jnp.sum((cls[:, :, None] == classes).astype(jnp.int32), axis=1)
    counts = jnp.sum(seg_len, axis=0)
    ntile_c = (counts + tm_e - 1) // tm_e
    tile_end = jnp.cumsum(ntile_c)
    class_base = (tile_end - ntile_c) * tm_e
    nact = tile_end[-1]
    seg_off = class_base[None, :] + jnp.cumsum(seg_len, axis=0) - seg_len
    pad_off = class_base + counts
    pad_len = ntile_c * tm_e - counts
    tile_ids = jnp.arange(n_max, dtype=jnp.int32)
    tile_cls = jnp.sum((tile_ids[:, None] >= tile_end[None, :]).astype(jnp.int32), axis=1)
    last_cls = jnp.sum((nact - 1 >= tile_end).astype(jnp.int32))
    tile_cls = jnp.where(tile_ids < nact, tile_cls, last_cls)
    grp = tile_cls // N_PAIRS
    pidx = tile_cls % N_PAIRS
    pair_lo = jnp.where(pidx < 3, 0, jnp.where(pidx < 5, 1, 2))
    pair_hi = jnp.where(pidx < 3, pidx + 1, jnp.where(pidx < 5, pidx - 1, 3))
    tile_ea = (grp * EPG + pair_lo).astype(jnp.int32)
    tile_eb = (grp * EPG + pair_hi).astype(jnp.int32)
    seg_off = seg_off.reshape(-1).astype(jnp.int32)
    seg_len = seg_len.reshape(-1)

    nact1 = nact.reshape(1).astype(jnp.int32)
    zero_rows = (jnp.sum(pad_len) + (n_max - nact) * tm_e).reshape(1).astype(jnp.int32)
    x_sorted = _dispatch(seg_off, seg_len, pad_off.astype(jnp.int32), pad_len.astype(jnp.int32), zero_rows,
                         nact1, h2_p, h2_s, info_p, info_s, tm=tm_d, n_slots=n_max * tm_e)
    ya_sorted, yb_sorted = _moe(tile_ea, tile_eb, nact1, x_sorted, wg_b, wu_b, wd_b, tm=tm_e, n_max=n_max)

    y_p = _final(seg_off, seg_len, x1_p, info_p, mod_p, g_final, ya_sorted, yb_sorted,
                 tm=tm_d, sample=False, tile_base=0)
    y_s = _final(seg_off, seg_len, x1_s, info_s, mod_s, g_final, ya_sorted, yb_sorted,
                 tm=tm_d, sample=True, tile_base=n_p // tm_d)

    return (y_p.reshape(batch, seq, D_MODEL),
            y_s.reshape(dec_b, dec_t, D_MODEL),
            kf_p.reshape(batch, 1, seq, N_HEADS, HEAD_W),
            vf_p.reshape(batch, 1, seq, N_HEADS, HEAD_W),
            kf_s.reshape(dec_b, 1, dec_t, N_HEADS, HEAD_W),
            vf_s.reshape(dec_b, 1, dec_t, N_HEADS, HEAD_W),
            vsn_s.reshape(dec_b, 1, dec_t, N_GROUPS_SG, SG_CH))
```

```python
import functools
import math

import numpy as np
import jax
import jax.numpy as jnp
from jax import lax
from jax.experimental import pallas as pl
from jax.experimental.pallas import tpu as pltpu

F32 = jnp.float32
BF16 = jnp.bfloat16

D_MODEL = 1024
N_HEADS = 4
DK = 64
HEAD_W = 128
QK_W = N_HEADS * HEAD_W
N_GROUPS_SG = 4
SG_CH = 128
IN_W = 2560
CHUNK = 128
PAGE = 128
N_BUCKETS = 32
MAX_DISTANCE = 128
N_EG = 4
EPG = 4
N_PAIRS = 6
N_CLASSES = N_EG * N_PAIRS
D_EXPERT = 512
EPS = 1e-6
LAM_INIT = 0.8 - 0.6 * math.exp(-0.3 * 0)
NEG = -1e30
LOG2E = math.log2(math.e)
LANES = 128
SUBLANES = 8

TM_TOK = 512
TQ_ATT = 2048
PAGES_PER_STEP = 16
TM_E = 256
TM_D = 256
VMEM_LIMIT = 56 * 1024 * 1024


def _cparams(sem):
    return pltpu.CompilerParams(dimension_semantics=sem, vmem_limit_bytes=VMEM_LIMIT)


def _ada_body(c_ref, w_ref, b_ref, o_ref):
    c = c_ref[...]
    a = (c * jax.nn.sigmoid(c)).astype(BF16)
    o_ref[...] = jnp.dot(a, w_ref[...].astype(BF16), preferred_element_type=F32) + b_ref[...]


def _ada(c_all, w_ada, b_ada):
    m = c_all.shape[0]
    n = w_ada.shape[1]
    tn = 1536
    return pl.pallas_call(
        _ada_body,
        grid=(n // tn,),
        in_specs=[pl.BlockSpec((m, D_MODEL), lambda j: (0, 0)),
                  pl.BlockSpec((D_MODEL, tn), lambda j: (0, j)),
                  pl.BlockSpec((1, tn), lambda j: (0, j))],
        out_specs=pl.BlockSpec((m, tn), lambda j: (0, j)),
        out_shape=jax.ShapeDtypeStruct((m, n), F32),
        compiler_params=_cparams(("arbitrary",)),
        name="adaln",
    )(c_all, w_ada, b_ada.reshape(1, n))


def _stage_a_body(x_ref, mod_ref, w_in_ref, w_vt_ref, ws_ref, bs_ref, gln_ref, bln_ref,
                  *out_refs, chunk, sample):
    if sample:
        q_ref, kf_ref, vf_ref, sg_ref, vsn_ref = out_refs
    else:
        q_ref, kf_ref, kb_ref, vf_ref, vt_ref, sg_ref = out_refs
    x = x_ref[...]
    tm = x.shape[0]
    mod = mod_ref[0]
    sh1 = mod[:, 0:D_MODEL]
    sc1 = mod[:, D_MODEL:2 * D_MODEL]
    ms = jnp.mean(x * x, axis=-1, keepdims=True)
    h = (x * lax.rsqrt(ms + EPS)) * (1.0 + sc1) + sh1
    hb = h.astype(BF16)
    z = jnp.dot(hb, w_in_ref[...], preferred_element_type=F32)
    q = z[:, 0:QK_W] * (DK ** -0.5)
    k = z[:, QK_W:2 * QK_W]
    v = z[:, 2 * QK_W:3 * QK_W]
    for hd in range(N_HEADS):
        kf_ref[pl.ds(hd, tm, stride=N_HEADS), :] = k[:, hd * HEAD_W:(hd + 1) * HEAD_W]
        vf_ref[pl.ds(hd, tm, stride=N_HEADS), :] = v[:, hd * HEAD_W:(hd + 1) * HEAD_W]
    if sample:
        q_ref[...] = q
    else:
        q_ref[...] = (q * LOG2E).astype(BF16)
        kb_ref[...] = k.astype(BF16)
        vt = lax.dot_general(w_vt_ref[...], hb, (((1,), (1,)), ((), ())),
                             preferred_element_type=F32)
        vt_ref[0] = vt.astype(BF16)
    u = z[:, 3 * QK_W:4 * QK_W]
    vs = z[:, 4 * QK_W:5 * QK_W]
    for g in range(N_GROUPS_SG):
        lo, hi = g * SG_CH, (g + 1) * SG_CH
        vg = vs[:, lo:hi]
        mu = jnp.mean(vg, axis=-1, keepdims=True)
        dv = vg - mu
        var = jnp.mean(dv * dv, axis=-1, keepdims=True)
        vn = (dv * lax.rsqrt(var + EPS)) * gln_ref[g:g + 1, :] + bln_ref[g:g + 1, :]
        if sample:
            vsn_ref[:, lo:hi] = vn
        vnb = vn.astype(BF16)
        for c in range(tm // chunk):
            r0, r1 = c * chunk, (c + 1) * chunk
            s = jnp.dot(ws_ref[g], vnb[r0:r1], preferred_element_type=F32) + bs_ref[g]
            sg_ref[r0:r1, lo:hi] = (u[r0:r1, lo:hi] * s).astype(BF16)


def _stage_a(x, mod, w_in_b, w_vt_b, ws, bs, gln, bln, *, tm, chunk, sample):
    n = x.shape[0]
    nt = n // tm
    mrows = mod.shape[1]
    row = lambda w, dt: jax.ShapeDtypeStruct((n, w), dt)
    blk = lambda w: pl.BlockSpec((tm, w), lambda i: (i, 0))
    cache_shape = jax.ShapeDtypeStruct((n * N_HEADS, HEAD_W), F32)
    cache_blk = pl.BlockSpec((tm * N_HEADS, HEAD_W), lambda i: (i, 0))
    if sample:
        out_shape = [row(QK_W, F32), cache_shape, cache_shape, row(QK_W, BF16), row(QK_W, F32)]
        out_specs = [blk(QK_W), cache_blk, cache_blk, blk(QK_W), blk(QK_W)]
        mod_map = lambda i: (0, 0, 0)
    else:
        out_shape = [row(QK_W, BF16), cache_shape, row(QK_W, BF16), cache_shape,
                     jax.ShapeDtypeStruct((nt, QK_W, tm), BF16), row(QK_W, BF16)]
        out_specs = [blk(QK_W), cache_blk, blk(QK_W), cache_blk,
                     pl.BlockSpec((1, QK_W, tm), lambda i: (i, 0, 0)), blk(QK_W)]
        tiles_per_batch = 4096 // tm
        mod_map = lambda i: (i // tiles_per_batch, 0, 0)
    full = lambda a: pl.BlockSpec(a.shape, lambda i: (0,) * a.ndim)
    return pl.pallas_call(
        functools.partial(_stage_a_body, chunk=chunk, sample=sample),
        grid=(nt,),
        in_specs=[blk(D_MODEL),
                  pl.BlockSpec((1, mrows, 6 * D_MODEL), mod_map),
                  full(w_in_b), full(w_vt_b), full(ws), full(bs), full(gln), full(bln)],
        out_specs=out_specs,
        out_shape=out_shape,
        compiler_params=_cparams(("arbitrary",)),
        name="stage_a_sample" if sample else "stage_a_prompt",
    )(x, mod, w_in_b, w_vt_b, ws, bs, gln, bln)


ATT_COLS = 256
BIAS_BLOCK = MAX_DISTANCE


def _attn_body(lam_ref, q_ref, k_ref, vt_ref, bias_ref, g_ref, o_ref, *scratch, tq, tk):
    n_chain = 2 * tq // ATT_COLS
    q2_refs, m_refs, l_refs, acc_refs = (scratch[i * n_chain:(i + 1) * n_chain] for i in range(4))
    qi = pl.program_id(2)
    for c in range(n_chain):
        q0 = (c * ATT_COLS) % tq
        q = q_ref[q0:q0 + ATT_COLS, :]
        lane = lax.broadcasted_iota(jnp.int32, q.shape, 1)
        keep = (lane < DK) if c < n_chain // 2 else (lane >= DK)
        q2_refs[c][...] = jnp.where(keep, q, jnp.zeros_like(q))
        m_refs[c][...] = jnp.full(m_refs[c].shape, NEG, F32)
        l_refs[c][...] = jnp.zeros(l_refs[c].shape, F32)
        acc_refs[c][...] = jnp.zeros(acc_refs[c].shape, F32)

    blk = BIAS_BLOCK

    kblocks = tk // blk

    def block_kinds(rel, q0):
        return [[(q0 // blk + b) - (rel + a) for b in range(ATT_COLS // blk)] for a in range(kblocks)]

    def keys_needed(rel, q0):
        if rel is None:
            return kblocks
        return sum(1 for row in block_kinds(rel, q0) if max(row) >= 0)

    def with_bias(s, rel, q0):
        if rel is None:
            return s
        kinds = block_kinds(rel, q0)[:s.shape[0] // blk]
        if all(d >= 2 for row in kinds for d in row):
            return s
        nxt, near = bias_ref[0, 0], bias_ref[0, 1]
        pick = lambda d: (jnp.full((blk, blk), NEG, F32) if d < 0 else near if d == 0 else nxt if d == 1
                          else jnp.zeros((blk, blk), F32))
        rows = []
        for a, row in enumerate(kinds):
            s_row = s[a * blk:(a + 1) * blk]
            if any(d < 2 for d in row):
                s_row = s_row + jnp.concatenate([pick(d) for d in row], axis=1)
            rows.append(s_row)
        return jnp.concatenate(rows, axis=0)

    def tiles(*work):
        scores = []
        for j, rel in work:
            for c in range(n_chain):
                nk = keys_needed(rel, (c * ATT_COLS) % tq) * blk
                if nk == 0:
                    scores.append(None)
                    continue
                k = k_ref[pl.ds(pl.multiple_of(j * tk, tk), nk), :]
                s = lax.dot_general(k, q2_refs[c][...], (((1,), (1,)), ((), ())),
                                    preferred_element_type=F32)
                scores.append(with_bias(s, rel, (c * ATT_COLS) % tq))
        for w, (j, _) in enumerate(work):
            for c in range(n_chain):
                s = scores[w * n_chain + c]
                if s is None:
                    continue
                vt = vt_ref[j, :, 0:s.shape[0]]
                m_old = m_refs[c][...]
                m_new = jnp.maximum(m_old, jnp.max(s, axis=0, keepdims=True))
                alpha = jnp.exp2(m_old - m_new)
                p = jnp.exp2(s - m_new)
                l_refs[c][...] = alpha * l_refs[c][...] + jnp.sum(p, axis=0, keepdims=True)
                acc_refs[c][...] = alpha * acc_refs[c][...] + jnp.dot(vt, p.astype(BF16),
                                                                      preferred_element_type=F32)
                m_refs[c][...] = m_new

    ratio = tq // tk
    first_diag = qi * ratio
    n_plain = jnp.maximum(first_diag - 1, 0)

    def plain_pair(jj, carry):
        tiles((2 * jj, None), (2 * jj + 1, None))
        return carry

    lax.fori_loop(0, n_plain // 2, plain_pair, 0)

    @pl.when(qi >= 1)
    def _():
        if ratio % 2 == 0:
            tiles((first_diag - 2, None), (first_diag - 1, -kblocks))
        else:
            @pl.when(n_plain % 2 == 1)
            def _():
                tiles((n_plain - 1, None))
            tiles((first_diag - 1, -kblocks))

    tiles(*[(first_diag + r, r * kblocks) for r in range(ratio)])

    lam = lam_ref[0]
    o_all = jnp.concatenate([acc_refs[c][...] * (1.0 / l_refs[c][...]) for c in range(n_chain)],
                            axis=1)
    o = o_all[:, 0:tq] - lam * o_all[:, tq:2 * tq]
    ms = jnp.mean(o * o, axis=0, keepdims=True)
    on = (o * lax.rsqrt(ms + EPS)) * g_ref[...] * (1.0 - LAM_INIT)
    o_ref[...] = on.T.astype(BF16)


def _attn_prompt(lam, q, kb, vt, bias_t, g_col, *, batch, seq, tq, tk):
    nq = seq // tq
    nk = seq // tk
    n = batch * seq
    n_chain = 2 * tq // ATT_COLS
    assert vt.shape == (batch * nk, QK_W, tk) and tq % tk == 0 and tk % BIAS_BLOCK == 0
    return pl.pallas_call(
        functools.partial(_attn_body, tq=tq, tk=tk),
        grid=(batch, N_HEADS, nq),
        in_specs=[pl.BlockSpec(memory_space=pltpu.SMEM),
                  pl.BlockSpec((tq, HEAD_W), lambda b, h, i: (b * nq + i, h)),
                  pl.BlockSpec((seq, HEAD_W), lambda b, h, i: (b, h)),
                  pl.BlockSpec((nk, HEAD_W, tk), lambda b, h, i: (b, h, 0)),
                  pl.BlockSpec((1, 2, BIAS_BLOCK, BIAS_BLOCK), lambda b, h, i: (h, 0, 0, 0)),
                  pl.BlockSpec((HEAD_W, 1), lambda b, h, i: (0, 0))],
        out_specs=pl.BlockSpec((tq, HEAD_W), lambda b, h, i: (b * nq + i, h)),
        out_shape=jax.ShapeDtypeStruct((n, QK_W), BF16),
        scratch_shapes=([pltpu.VMEM((ATT_COLS, HEAD_W), BF16)] * n_chain
                        + [pltpu.VMEM((1, ATT_COLS), F32)] * (2 * n_chain)
                        + [pltpu.VMEM((HEAD_W, ATT_COLS), F32)] * n_chain),
        compiler_params=_cparams(("arbitrary", "arbitrary", "arbitrary")),
        name="attn_prompt",
    )(lam, q, kb, vt, bias_t, g_col)


PAGE_ROWS = PAGE * N_HEADS
TAIL_TOKENS = 2 * PAGE


def _sattn_body(pt_ref, lam_ref, q_ref, knew_ref, vnew_ref, bl_ref, bn_ref, g_ref, ck_ref, cv_ref,
                o_ref, kbuf, vbuf, sem, mask_ref, m_ref, l_ref, acc_ref, *, pages, n_steps, total):
    b = pl.program_id(0)
    s = pl.program_id(1)
    step = b * n_steps + s
    slot = step % 2

    def page_copies(step_idx, sl):
        base = step_idx * pages
        out = []
        for i in range(pages):
            src = pl.ds(pl.multiple_of(pt_ref[base + i] * PAGE_ROWS, PAGE_ROWS), PAGE_ROWS)
            dst = pl.ds(i * PAGE_ROWS, PAGE_ROWS)
            out.append(pltpu.make_async_copy(ck_ref.at[src], kbuf.at[sl, dst], sem.at[sl, 0]))
            out.append(pltpu.make_async_copy(cv_ref.at[src], vbuf.at[sl, dst], sem.at[sl, 1]))
        return out

    def start_all(copies):
        for n, c in enumerate(copies):
            c.start(priority=n % 2)

    @pl.when(step == 0)
    def _():
        start_all(page_copies(0, 0))

    @pl.when(step + 1 < total)
    def _():
        start_all(page_copies(step + 1, 1 - slot))

    for c in page_copies(step, slot):
        c.wait()

    @pl.when(s == 0)
    def _():
        m_ref[...] = jnp.full(m_ref.shape, NEG, F32)
        l_ref[...] = jnp.zeros(l_ref.shape, F32)
        acc_ref[...] = jnp.zeros(acc_ref.shape, F32)

    @pl.when(step == 0)
    def _():
        row = lax.broadcasted_iota(jnp.int32, mask_ref.shape, 0)
        col = lax.broadcasted_iota(jnp.int32, mask_ref.shape, 1)
        same_head = (col % N_HEADS) == (row // (2 * SUBLANES))
        mask_ref[...] = jnp.where(same_head, 0.0, NEG)

    q = q_ref[...]
    lane = lax.broadcasted_iota(jnp.int32, (SUBLANES, HEAD_W), 1)
    pieces = []
    for h in range(N_HEADS):
        qh = q[:, h * HEAD_W:(h + 1) * HEAD_W]
        pieces += [jnp.where(lane < DK, qh, 0.0), jnp.where(lane >= DK, qh, 0.0)]
    qm = jnp.concatenate(pieces, axis=0).astype(BF16)

    def update(kb, vb, bias):
        sc = lax.dot_general(qm, kb, (((1,), (1,)), ((), ())),
                             preferred_element_type=F32) + bias
        m_old = m_ref[...]
        m_new = jnp.maximum(m_old, jnp.max(sc, axis=1, keepdims=True))
        alpha = jnp.exp(m_old - m_new)
        p = jnp.exp(sc - m_new)
        l_ref[...] = alpha * l_ref[...] + jnp.sum(p, axis=1, keepdims=True)
        acc_ref[...] = alpha * acc_ref[...] + jnp.dot(p.astype(BF16), vb,
                                                      preferred_element_type=F32)
        m_ref[...] = m_new

    is_last = s == n_steps - 1
    head_cols = mask_ref.shape[1] - bl_ref.shape[1]
    update(kbuf[slot].astype(BF16), vbuf[slot].astype(BF16),
           jnp.concatenate([mask_ref[:, :head_cols],
                            mask_ref[:, head_cols:] + bl_ref[...] * is_last.astype(F32)], axis=1))

    @pl.when(is_last)
    def _():
        update(knew_ref[0], vnew_ref[0], bn_ref[...])
        lam = lam_ref[0]
        o_all = acc_ref[...] * (1.0 / l_ref[...])
        for h in range(N_HEADS):
            r = h * 2 * SUBLANES
            o = o_all[r:r + SUBLANES] - lam * o_all[r + SUBLANES:r + 2 * SUBLANES]
            ms = jnp.mean(o * o, axis=-1, keepdims=True)
            o_ref[:, h * HEAD_W:(h + 1) * HEAD_W] = ((o * lax.rsqrt(ms + EPS)) * g_ref[...]
                                                     * (1.0 - LAM_INIT))


def _attn_sample(page_table_flat, lam, q_s, knew, vnew, bias_last, bias_new, g_row, cache_k, cache_v,
                 *, dec_b, n_pages):
    pages = PAGES_PER_STEP
    n_steps = n_pages // pages
    total = dec_b * n_steps
    nq = q_s.shape[0] // dec_b
    n_rows = 2 * N_HEADS * nq
    step_rows = pages * PAGE_ROWS
    grid_spec = pltpu.PrefetchScalarGridSpec(
        num_scalar_prefetch=1,
        grid=(dec_b, n_steps),
        in_specs=[pl.BlockSpec(memory_space=pltpu.SMEM),
                  pl.BlockSpec((nq, QK_W), lambda b, s, pt: (b, 0)),
                  pl.BlockSpec((1, PAGE, HEAD_W), lambda b, s, pt: (b, 0, 0)),
                  pl.BlockSpec((1, PAGE, HEAD_W), lambda b, s, pt: (b, 0, 0)),
                  pl.BlockSpec(bias_last.shape, lambda b, s, pt: (0, 0)),
                  pl.BlockSpec(bias_new.shape, lambda b, s, pt: (0, 0)),
                  pl.BlockSpec((1, HEAD_W), lambda b, s, pt: (0, 0)),
                  pl.BlockSpec(memory_space=pl.ANY),
                  pl.BlockSpec(memory_space=pl.ANY)],
        out_specs=pl.BlockSpec((nq, QK_W), lambda b, s, pt: (b, 0)),
        scratch_shapes=[pltpu.VMEM((2, step_rows, HEAD_W), F32),
                        pltpu.VMEM((2, step_rows, HEAD_W), F32),
                        pltpu.SemaphoreType.DMA((2, 2)),
                        pltpu.VMEM((n_rows, step_rows), F32),
                        pltpu.VMEM((n_rows, 1), F32), pltpu.VMEM((n_rows, 1), F32),
                        pltpu.VMEM((n_rows, HEAD_W), F32)])
    return pl.pallas_call(
        functools.partial(_sattn_body, pages=pages, n_steps=n_steps, total=total),
        grid_spec=grid_spec,
        out_shape=jax.ShapeDtypeStruct(q_s.shape, F32),
        compiler_params=_cparams(("arbitrary", "arbitrary")),
        name="attn_sample",
    )(page_table_flat, lam, q_s, knew, vnew, bias_last, bias_new, g_row, cache_k, cache_v)


INFO_W_LO, INFO_W_HI, INFO_CLS = 0, 1, 2


def _stage_c_body(o_ref, sg_ref, x_ref, mod_ref, wo_ref, wr_ref, br_ref, x1_ref, h2_ref, cls_ref, info_ref):
    x = x_ref[...]
    tm = x.shape[0]
    mod = mod_ref[0]
    g1 = mod[:, 2 * D_MODEL:3 * D_MODEL]
    sh2 = mod[:, 3 * D_MODEL:4 * D_MODEL]
    sc2 = mod[:, 4 * D_MODEL:5 * D_MODEL]
    mix = (jnp.dot(o_ref[...].astype(BF16), wo_ref[0:QK_W, :], preferred_element_type=F32)
           + jnp.dot(sg_ref[...], wo_ref[QK_W:2 * QK_W, :], preferred_element_type=F32))
    x1 = x + g1 * mix
    x1_ref[...] = x1
    ms = jnp.mean(x1 * x1, axis=-1, keepdims=True)
    h2 = ((x1 * lax.rsqrt(ms + EPS)) * (1.0 + sc2) + sh2).astype(BF16)
    h2_ref[...] = h2
    lg = lax.dot_general(wr_ref[...], h2, (((1,), (1,)), ((), ())),
                         preferred_element_type=F32) + br_ref[...]
    gl = [lg[i:i + 1, :] for i in range(N_EG)]
    el = [lg[N_EG + i:N_EG + i + 1, :] for i in range(N_EG * EPG)]
    gmax = jnp.maximum(jnp.maximum(gl[0], gl[1]), jnp.maximum(gl[2], gl[3]))
    gi = jnp.where(gl[0] == gmax, 0, jnp.where(gl[1] == gmax, 1, jnp.where(gl[2] == gmax, 2, 3)))
    gsum = (jnp.exp(gl[0] - gmax) + jnp.exp(gl[1] - gmax)
            + jnp.exp(gl[2] - gmax) + jnp.exp(gl[3] - gmax))
    gp = 1.0 / gsum
    sel = [jnp.where(gi == 0, el[j], jnp.where(gi == 1, el[EPG + j],
                                               jnp.where(gi == 2, el[2 * EPG + j], el[3 * EPG + j])))
           for j in range(EPG)]
    v0 = jnp.maximum(jnp.maximum(sel[0], sel[1]), jnp.maximum(sel[2], sel[3]))
    i0 = jnp.where(sel[0] == v0, 0, jnp.where(sel[1] == v0, 1, jnp.where(sel[2] == v0, 2, 3)))
    rest = [jnp.where(i0 == j, -3e38, sel[j]) for j in range(EPG)]
    v1 = jnp.maximum(jnp.maximum(rest[0], rest[1]), jnp.maximum(rest[2], rest[3]))
    i1 = jnp.where(rest[0] == v1, 0, jnp.where(rest[1] == v1, 1, jnp.where(rest[2] == v1, 2, 3)))
    e1 = jnp.exp(v1 - v0)
    den = 1.0 / (1.0 + e1)
    tw0 = den * gp
    tw1 = e1 * den * gp
    first_low = i0 < i1
    lo = jnp.where(first_low, i0, i1)
    hi = jnp.where(first_low, i1, i0)
    w_lo = jnp.where(first_low, tw0, tw1)
    w_hi = jnp.where(first_low, tw1, tw0)
    pair = jnp.where(lo == 0, 0, jnp.where(lo == 1, 3, 5)) + hi - lo - 1
    cls = gi * N_PAIRS + pair
    cls_ref[...] = jnp.broadcast_to(cls, cls_ref.shape).astype(jnp.int32)
    row = lax.broadcasted_iota(jnp.int32, (LANES, tm), 0)
    rec = jnp.where(row == INFO_W_LO, w_lo, jnp.where(row == INFO_W_HI, w_hi,
                                                      jnp.where(row == INFO_CLS, cls.astype(F32), 0.0)))
    info_ref[...] = rec.T


def _stage_c(o, sg, x, mod, wo_b, wr_t, br, *, tm, sample):
    n = x.shape[0]
    nt = n // tm
    mrows = mod.shape[1]
    blk = lambda w: pl.BlockSpec((tm, w), lambda i: (i, 0))
    full = lambda a: pl.BlockSpec(a.shape, lambda i: (0,) * a.ndim)
    if sample:
        mod_map = lambda i: (0, 0, 0)
    else:
        tiles_per_batch = 4096 // tm
        mod_map = lambda i: (i // tiles_per_batch, 0, 0)
    return pl.pallas_call(
        _stage_c_body,
        grid=(nt,),
        in_specs=[blk(QK_W), blk(QK_W), blk(D_MODEL),
                  pl.BlockSpec((1, mrows, 6 * D_MODEL), mod_map),
                  full(wo_b), full(wr_t), full(br)],
        out_specs=[blk(D_MODEL), blk(D_MODEL),
                   pl.BlockSpec((SUBLANES, tm), lambda i: (0, i)),
                   blk(LANES)],
        out_shape=[jax.ShapeDtypeStruct((n, D_MODEL), F32),
                   jax.ShapeDtypeStruct((n, D_MODEL), BF16),
                   jax.ShapeDtypeStruct((SUBLANES, n), jnp.int32),
                   jax.ShapeDtypeStruct((n, LANES), F32)],
        compiler_params=_cparams(("arbitrary",)),
        name="stage_c_sample" if sample else "stage_c_prompt",
    )(o, sg, x, mod, wo_b, wr_t, br)


def rows8(ref, start, count):
    scale = lambda v: v * SUBLANES if isinstance(v, int) else pl.multiple_of(v * SUBLANES, SUBLANES)
    return ref.at[pl.ds(scale(start), scale(count))]


def _perm_t(cls_col):
    n = cls_col.shape[0]
    lane = lax.broadcasted_iota(jnp.int32, (n, LANES), 1).astype(F32)
    onehot = (lane == cls_col).astype(BF16)
    r = lax.broadcasted_iota(jnp.int32, (n, n), 0)
    c = lax.broadcasted_iota(jnp.int32, (n, n), 1)
    before = (c < r).astype(BF16)
    rank = jnp.dot(before, onehot, preferred_element_type=F32)
    cnt = jnp.sum(onehot.astype(F32), axis=0, keepdims=True)
    cr = lax.broadcasted_iota(jnp.int32, (LANES, LANES), 0)
    cc = lax.broadcasted_iota(jnp.int32, (LANES, LANES), 1)
    lower_cls = (cr < cc).astype(BF16)
    base = jnp.dot(jnp.broadcast_to(cnt, (SUBLANES, LANES)).astype(BF16), lower_cls,
                   preferred_element_type=F32)[0:1, :]
    pos = jnp.sum(onehot.astype(F32) * (base + rank), axis=1, keepdims=True)
    dest = lax.broadcasted_iota(jnp.int32, (n, n), 1).astype(F32)
    return (dest == pos).astype(F32)


def _dispatch_body(soff_ref, slen_ref, poff_ref, plen_ref, ptot_ref, nact_ref, hp_ref, hs_ref, ip_ref, is_ref,
                   xs_ref, buf, zbuf, sem, zsem, *, tm, n_tiles, n_slab_tiles):
    i = pl.program_id(0)
    slot = i % 2
    is_sample = i == n_tiles - 1

    def wait_tile(sl):
        pltpu.make_async_copy(buf.at[sl], buf.at[sl], sem.at[sl]).wait()

    @pl.when(i == 0)
    def _():
        zbuf[...] = jnp.zeros(zbuf.shape, F32)
        for c in range(N_CLASSES):
            @pl.when(plen_ref[c] > 0)
            def _():
                pltpu.make_async_copy(rows8(zbuf, 0, plen_ref[c]), rows8(xs_ref, poff_ref[c], plen_ref[c]),
                                      zsem).start()
        for j in range(n_slab_tiles - N_CLASSES, n_slab_tiles):
            @pl.when(j >= nact_ref[0])
            def _():
                pltpu.make_async_copy(zbuf, rows8(xs_ref, j * TM_E, TM_E), zsem).start()

    x = jnp.where(is_sample, hs_ref[...], hp_ref[...])
    info = jnp.where(is_sample, is_ref[...], ip_ref[...])
    perm = _perm_t(info[:, INFO_CLS:INFO_CLS + 1]).T.astype(BF16)
    xp = jnp.dot(perm, x, preferred_element_type=F32)

    @pl.when(i >= 2)
    def _():
        wait_tile(slot)

    bs = buf.at[slot]
    for c in range(D_MODEL // LANES):
        bs[pl.ds(c, tm, stride=SUBLANES), :] = xp[:, c * LANES:(c + 1) * LANES]
    local = 0
    for c in range(N_CLASSES):
        n_rows = slen_ref[i * N_CLASSES + c]

        @pl.when(n_rows > 0)
        def _():
            pltpu.make_async_copy(rows8(bs, local, n_rows),
                                  rows8(xs_ref, soff_ref[i * N_CLASSES + c], n_rows), sem.at[slot]).start()
        local = local + n_rows

    @pl.when(i == n_tiles - 1)
    def _():
        wait_tile(slot)
        if n_tiles >= 2:
            wait_tile(1 - slot)

        @pl.when(ptot_ref[0] > 0)
        def _():
            n = pl.multiple_of(ptot_ref[0] * SUBLANES, SUBLANES)
            pltpu.make_async_copy(xs_ref.at[pl.ds(0, n)], xs_ref.at[pl.ds(0, n)], zsem).wait()


def _dispatch(seg_off, seg_len, pad_off, pad_len, pad_tot, nact, h2_p, h2_s, info_p, info_s, *, tm, n_slots):
    n_prompt_tiles = h2_p.shape[0] // tm
    n_tiles = n_prompt_tiles + 1
    assert h2_s.shape[0] == tm
    last_p = n_prompt_tiles - 1
    grid_spec = pltpu.PrefetchScalarGridSpec(
        num_scalar_prefetch=6,
        grid=(n_tiles,),
        in_specs=[pl.BlockSpec((tm, D_MODEL), lambda i, *_: (jnp.minimum(i, last_p), 0)),
                  pl.BlockSpec((tm, D_MODEL), lambda i, *_: (0, 0)),
                  pl.BlockSpec((tm, LANES), lambda i, *_: (jnp.minimum(i, last_p), 0)),
                  pl.BlockSpec((tm, LANES), lambda i, *_: (0, 0))],
        out_specs=pl.BlockSpec(memory_space=pl.ANY),
        scratch_shapes=[pltpu.VMEM((2, tm * SUBLANES, LANES), F32),
                        pltpu.VMEM((TM_E * SUBLANES, LANES), F32),
                        pltpu.SemaphoreType.DMA((2,)),
                        pltpu.SemaphoreType.DMA(())])
    return pl.pallas_call(
        functools.partial(_dispatch_body, tm=tm, n_tiles=n_tiles, n_slab_tiles=n_slots // TM_E),
        grid_spec=grid_spec,
        out_shape=jax.ShapeDtypeStruct((n_slots * SUBLANES, LANES), F32),
        compiler_params=_cparams(("arbitrary",)),
        name="moe_dispatch",
    )(seg_off, seg_len, pad_off, pad_len, pad_tot, nact, h2_p, h2_s, info_p, info_s)


def _moe_body(ea_ref, eb_ref, nact_ref, x_ref, wga_ref, wgb_ref, wua_ref, wub_ref, wda_ref, wdb_ref,
              ya_ref, yb_ref, *, tm):
    i = pl.program_id(0)

    @pl.when(i < nact_ref[0])
    def _():
        x = jnp.concatenate([x_ref[pl.ds(c, tm, stride=SUBLANES), :] for c in range(D_MODEL // LANES)],
                            axis=1).astype(BF16)

        def expert(wg_ref, wu_ref, wd_ref):
            gate = jnp.dot(x, wg_ref[0], preferred_element_type=F32)
            up = jnp.dot(x, wu_ref[0], preferred_element_type=F32)
            he = (gate * jax.nn.sigmoid(gate)) * up
            return jnp.dot(he.astype(BF16), wd_ref[0], preferred_element_type=F32)

        ya = expert(wga_ref, wua_ref, wda_ref)
        yb = expert(wgb_ref, wub_ref, wdb_ref)
        for c in range(D_MODEL // LANES):
            ya_ref[pl.ds(c, tm, stride=SUBLANES), :] = ya[:, c * LANES:(c + 1) * LANES]
            yb_ref[pl.ds(c, tm, stride=SUBLANES), :] = yb[:, c * LANES:(c + 1) * LANES]

    @pl.when(i >= nact_ref[0])
    def _():
        ya_ref[...] = jnp.zeros(ya_ref.shape, ya_ref.dtype)
        yb_ref[...] = jnp.zeros(yb_ref.shape, yb_ref.dtype)


def _moe(tile_ea, tile_eb, nact, x_sorted, wg_b, wu_b, wd_b, *, tm, n_max):
    wspec_in = lambda sel: pl.BlockSpec((1, D_MODEL, D_EXPERT), sel)
    wspec_out = lambda sel: pl.BlockSpec((1, D_EXPERT, D_MODEL), sel)
    sel_a = lambda i, ea, eb, na: (ea[i], 0, 0)
    sel_b = lambda i, ea, eb, na: (eb[i], 0, 0)
    rows_in = lambda i, ea, eb, na: (jnp.minimum(i, na[0] - 1), 0)
    grid_spec = pltpu.PrefetchScalarGridSpec(
        num_scalar_prefetch=3,
        grid=(n_max,),
        in_specs=[pl.BlockSpec((tm * SUBLANES, LANES), rows_in),
                  wspec_in(sel_a), wspec_in(sel_b), wspec_in(sel_a), wspec_in(sel_b),
                  wspec_out(sel_a), wspec_out(sel_b)],
        out_specs=[pl.BlockSpec((tm * SUBLANES, LANES), lambda i, ea, eb, na: (i, 0))] * 2)
    return pl.pallas_call(
        functools.partial(_moe_body, tm=tm),
        grid_spec=grid_spec,
        out_shape=[jax.ShapeDtypeStruct(x_sorted.shape, F32)] * 2,
        compiler_params=_cparams(("arbitrary",)),
        name="moe",
    )(tile_ea, tile_eb, nact, x_sorted, wg_b, wg_b, wu_b, wu_b, wd_b, wd_b)


def _final_body(soff_ref, slen_ref, x1_ref, info_ref, mod_ref, gf_ref, ya_ref, yb_ref, y_ref, buf, sem,
                *, tm, n_tiles, tile_base):
    i = pl.program_id(0)
    slot = i % 2
    slabs = (ya_ref, yb_ref)

    def fetch(tile, sl):
        local = 0
        for c in range(N_CLASSES):
            k = (tile + tile_base) * N_CLASSES + c
            n_rows = slen_ref[k]

            @pl.when(n_rows > 0)
            def _():
                for e in range(2):
                    pltpu.make_async_copy(rows8(slabs[e], soff_ref[k], n_rows),
                                          rows8(buf.at[sl, e], local, n_rows), sem.at[sl, e]).start()
            local = local + n_rows

    @pl.when(i == 0)
    def _():
        fetch(0, 0)

    @pl.when(i + 1 < n_tiles)
    def _():
        fetch(i + 1, 1 - slot)

    info = info_ref[...]
    perm_t = _perm_t(info[:, INFO_CLS:INFO_CLS + 1]).astype(BF16)
    moe = None
    for e, lane_w in enumerate((INFO_W_LO, INFO_W_HI)):
        pltpu.make_async_copy(buf.at[slot, e], buf.at[slot, e], sem.at[slot, e]).wait()
        bs = buf.at[slot, e]
        ye = jnp.concatenate([bs[pl.ds(c, tm, stride=SUBLANES), :] for c in range(D_MODEL // LANES)],
                             axis=1)
        term = info[:, lane_w:lane_w + 1] * jnp.dot(perm_t, ye.astype(BF16), preferred_element_type=F32)
        moe = term if moe is None else moe + term
    x1 = x1_ref[...]
    g2 = mod_ref[0][:, 5 * D_MODEL:6 * D_MODEL]
    x2 = x1 + g2 * moe
    ms = jnp.mean(x2 * x2, axis=-1, keepdims=True)
    y_ref[...] = (x2 * lax.rsqrt(ms + EPS)) * gf_ref[...]


def _final(seg_off, seg_len, x1, info, mod, g_final, ya_sorted, yb_sorted, *, tm, sample, tile_base):
    n = x1.shape[0]
    nt = n // tm
    mrows = mod.shape[1]
    if sample:
        mod_map = lambda i, *_: (0, 0, 0)
    else:
        tiles_per_batch = 4096 // tm
        mod_map = lambda i, *_: (i // tiles_per_batch, 0, 0)
    grid_spec = pltpu.PrefetchScalarGridSpec(
        num_scalar_prefetch=2,
        grid=(nt,),
        in_specs=[pl.BlockSpec((tm, D_MODEL), lambda i, *_: (i, 0)),
                  pl.BlockSpec((tm, LANES), lambda i, *_: (i, 0)),
                  pl.BlockSpec((1, mrows, 6 * D_MODEL), mod_map),
                  pl.BlockSpec((1, D_MODEL), lambda i, *_: (0, 0)),
                  pl.BlockSpec(memory_space=pl.ANY),
                  pl.BlockSpec(memory_space=pl.ANY)],
        out_specs=pl.BlockSpec((tm, D_MODEL), lambda i, *_: (i, 0)),
        scratch_shapes=[pltpu.VMEM((2, 2, tm * SUBLANES, LANES), F32),
                        pltpu.SemaphoreType.DMA((2, 2))])
    return pl.pallas_call(
        functools.partial(_final_body, tm=tm, n_tiles=nt, tile_base=tile_base),
        grid_spec=grid_spec,
        out_shape=jax.ShapeDtypeStruct((n, D_MODEL), F32),
        compiler_params=_cparams(("arbitrary",)),
        name="final_sample" if sample else "final_prompt",
    )(seg_off, seg_len, x1, info, mod, g_final.reshape(1, D_MODEL), ya_sorted, yb_sorted)


def _bucket_table(n):
    d = np.arange(n)
    max_exact = N_BUCKETS // 2
    nf = np.maximum(d, 1).astype(np.float64)
    large = max_exact + (np.log(nf / max_exact) / math.log(MAX_DISTANCE / max_exact)
                         * (N_BUCKETS - max_exact)).astype(np.int64)
    large = np.minimum(large, N_BUCKETS - 1)
    return np.where(d < max_exact, d, large).astype(np.int32)


def _toeplitz(v, n_rows, n_cols):
    length = n_rows + n_cols - 1
    lead = v.shape[:-1]
    vp = jnp.concatenate([v, jnp.zeros(lead + (1,), v.dtype)], axis=-1)
    skew = jnp.tile(vp, (1,) * len(lead) + (n_rows,))[..., :n_rows * length].reshape(lead + (n_rows, length))
    return skew[..., n_rows - 1:n_rows - 1 + n_cols]


def kernel(x_prompt, x_sample, c_prompt, c_sample, cache_k, cache_v, page_table, w_ada, b_ada, w_in, w_o,
           lam_q1, lam_k1, lam_q2, lam_k2, g_subln, rel_bias, g_sg_ln, b_sg_ln, w_s, b_s, w_rg, b_rg,
           w_re, b_re, w_gate, w_up, w_down, g_final):
    batch, seq, _ = x_prompt.shape
    dec_b, dec_t, _ = x_sample.shape
    n_pages = page_table.shape[1]
    n_p = batch * seq
    n_s = dec_b * dec_t
    n_tot = n_p + n_s
    assert w_in.shape[0] == 1 and cache_k.shape[1] == 1 and seq % TQ_ATT == 0 and n_pages % PAGES_PER_STEP == 0
    assert n_p % TM_TOK == 0 and n_p % n_s == 0 and n_tot % TM_E == 0 and dec_t == SUBLANES
    assert TAIL_TOKENS >= MAX_DISTANCE + dec_t and TAIL_TOKENS <= PAGES_PER_STEP * PAGE
    assert n_s == TM_D and n_p % TM_D == 0 and N_CLASSES <= LANES

    w_in_b = w_in[0].astype(BF16)
    w_vt_b = w_in[0][:, 2 * QK_W:3 * QK_W].T.astype(BF16)
    w_o_b = w_o[0].astype(BF16)
    wr_t = jnp.zeros((32, D_MODEL), F32).at[0:N_EG].set(w_rg[0].T).at[N_EG:N_EG + N_EG * EPG].set(w_re[0].T)
    wr_t = wr_t.astype(BF16)
    br = jnp.zeros((32, 1), F32).at[0:N_EG, 0].set(b_rg[0]).at[N_EG:N_EG + N_EG * EPG, 0].set(b_re[0])
    wg_b = w_gate[0].astype(BF16)
    wu_b = w_up[0].astype(BF16)
    wd_b = w_down[0].astype(BF16)
    ws_tril = jnp.tril(w_s[0])
    ws_p = ws_tril.astype(BF16)
    bs_p = b_s[0][:, :, None]
    same_seq = np.kron(np.eye(dec_b, dtype=np.float32), np.ones((dec_t, dec_t), np.float32))
    ws_s = (jnp.tile(ws_tril[:, :dec_t, :dec_t], (1, dec_b, dec_b)) * same_seq).astype(BF16)
    bs_s = jnp.tile(b_s[0][:, :dec_t], (1, dec_b))[:, :, None]
    gln = g_sg_ln[0]
    bln = b_sg_ln[0]
    lam = (jnp.exp(jnp.sum(lam_q1[0] * lam_k1[0])) - jnp.exp(jnp.sum(lam_q2[0] * lam_k2[0]))
           + LAM_INIT).reshape(1).astype(F32)

    blk = BIAS_BLOCK
    n_dist = max(2 * blk, TAIL_TOKENS + dec_t)
    onehot = np.eye(N_BUCKETS, dtype=np.float32)[_bucket_table(n_dist)]
    ft = jnp.dot(onehot, rel_bias - rel_bias[N_BUCKETS - 1], precision=lax.Precision.HIGHEST).T
    neg = lambda n: jnp.full((N_HEADS, n), NEG, F32)
    bias_near = _toeplitz(jnp.concatenate([neg(blk - 1), ft[:, 0:blk]], axis=1), blk, blk)
    bias_next = _toeplitz(ft[:, 1:2 * blk], blk, blk)
    bias_t = jnp.stack([bias_next, bias_near], axis=1) * LOG2E
    bl = _toeplitz(jnp.flip(ft[:, 1:TAIL_TOKENS + dec_t], axis=1), dec_t, TAIL_TOKENS)
    head_eq = jnp.eye(N_HEADS, dtype=F32)
    bias_last = (bl[:, None, :, :, None] * head_eq[:, None, None, None, :])
    bias_last = jnp.broadcast_to(bias_last, (N_HEADS, 2, dec_t, TAIL_TOKENS, N_HEADS)).reshape(
        2 * N_HEADS * dec_t, TAIL_TOKENS * N_HEADS)
    bn = _toeplitz(jnp.concatenate([jnp.flip(ft[:, 0:dec_t], axis=1), neg(dec_t - 1)], axis=1),
                   dec_t, dec_t)
    bn = jnp.where(head_eq[:, None, None, :] > 0, bn[:, :, :, None], NEG)
    bn = jnp.broadcast_to(bn[:, None], (N_HEADS, 2, dec_t, dec_t, N_HEADS)).reshape(
        2 * N_HEADS * dec_t, dec_t * N_HEADS)
    bias_new = jnp.concatenate([bn, jnp.full((bn.shape[0], PAGE - bn.shape[1]), NEG, F32)], axis=1)

    c_all = jnp.concatenate([c_prompt, c_sample, jnp.zeros((4, D_MODEL), F32)], axis=0)
    mod_all = _ada(c_all, w_ada[0], b_ada[0])
    mod_p = mod_all[:batch].reshape(batch, 1, 6 * D_MODEL)
    mod_s = jnp.repeat(mod_all[batch:batch + dec_b], dec_t, axis=0).reshape(1, n_s, 6 * D_MODEL)

    xp = x_prompt.reshape(n_p, D_MODEL)
    xs = x_sample.reshape(n_s, D_MODEL)

    q_p, kf_p, kb_p, vf_p, vt_p, sg_p = _stage_a(xp, mod_p, w_in_b, w_vt_b, ws_p, bs_p, gln, bln,
                                                 tm=TM_TOK, chunk=CHUNK, sample=False)
    q_s, kf_s, vf_s, sg_s, vsn_s = _stage_a(xs, mod_s, w_in_b, w_vt_b, ws_s, bs_s, gln, bln,
                                            tm=n_s, chunk=n_s, sample=True)

    g_col = g_subln[0].reshape(HEAD_W, 1)
    g_row = g_subln[0].reshape(1, HEAD_W)
    o_p = _attn_prompt(lam, q_p, kb_p, vt_p, bias_t, g_col, batch=batch, seq=seq, tq=TQ_ATT, tk=TM_TOK)
    pad = ((0, 0), (0, PAGE - dec_t * N_HEADS), (0, 0))
    knew = jnp.pad(kf_s.reshape(dec_b, dec_t * N_HEADS, HEAD_W), pad).astype(BF16)
    vnew = jnp.pad(vf_s.reshape(dec_b, dec_t * N_HEADS, HEAD_W), pad).astype(BF16)
    ck = cache_k.reshape(-1, HEAD_W)
    cv = cache_v.reshape(-1, HEAD_W)
    o_s = _attn_sample(page_table.reshape(-1), lam, q_s, knew, vnew, bias_last, bias_new, g_row, ck, cv,
                       dec_b=dec_b, n_pages=n_pages)

    x1_p, h2_p, cls_p, info_p = _stage_c(o_p, sg_p, xp, mod_p, w_o_b, wr_t, br, tm=TM_TOK, sample=False)
    x1_s, h2_s, cls_s, info_s = _stage_c(o_s, sg_s, xs, mod_s, w_o_b, wr_t, br, tm=n_s, sample=True)

    tm_e = TM_E
    tm_d = TM_D
    n_max = n_tot // tm_e + N_CLASSES
    n_dt = n_tot // tm_d
    cls = jnp.concatenate([cls_p[0], cls_s[0]]).reshape(n_dt, tm_d)
    classes = jnp.arange(N_CLASSES, dtype=jnp.int32)
    seg_len = jnp.sum((cls[:, :, None] == classes).astype(jnp.int32), axis=1)
    counts = jnp.sum(seg_len, axis=0)
    ntile_c = (counts + tm_e - 1) // tm_e
    tile_end = jnp.cumsum(ntile_c)
    class_base = (tile_end - ntile_c) * tm_e
    nact = tile_end[-1]
    seg_off = class_base[None, :] + jnp.cumsum(seg_len, axis=0) - seg_len
    pad_off = class_base + counts
    pad_len = ntile_c * tm_e - counts
    tile_ids = jnp.arange(n_max, dtype=jnp.int32)
    tile_cls = jnp.sum((tile_ids[:, None] >= tile_end[None, :]).astype(jnp.int32), axis=1)
    last_cls = jnp.sum((nact - 1 >= tile_end).astype(jnp.int32))
    tile_cls = jnp.where(tile_ids < nact, tile_cls, last_cls)
    grp = tile_cls // N_PAIRS
    pidx = tile_cls % N_PAIRS
    pair_lo = jnp.where(pidx < 3, 0, jnp.where(pidx < 5, 1, 2))
    pair_hi = jnp.where(pidx < 3, pidx + 1, jnp.where(pidx < 5, pidx - 1, 3))
    tile_ea = (grp * EPG + pair_lo).astype(jnp.int32)
    tile_eb = (grp * EPG + pair_hi).astype(jnp.int32)
    seg_off = seg_off.reshape(-1).astype(jnp.int32)
    seg_len = seg_len.reshape(-1)

    nact1 = nact.reshape(1).astype(jnp.int32)
    zero_rows = (jnp.sum(pad_len) + (n_max - nact) * tm_e).reshape(1).astype(jnp.int32)
    x_sorted = _dispatch(seg_off, seg_len, pad_off.astype(jnp.int32), pad_len.astype(jnp.int32), zero_rows,
                         nact1, h2_p, h2_s, info_p, info_s, tm=tm_d, n_slots=n_max * tm_e)
    ya_sorted, yb_sorted = _moe(tile_ea, tile_eb, nact1, x_sorted, wg_b, wu_b, wd_b, tm=tm_e, n_max=n_max)

    y_p = _final(seg_off, seg_len, x1_p, info_p, mod_p, g_final, ya_sorted, yb_sorted,
                 tm=tm_d, sample=False, tile_base=0)
    y_s = _final(seg_off, seg_len, x1_s, info_s, mod_s, g_final, ya_sorted, yb_sorted,
                 tm=tm_d, sample=True, tile_base=n_p // tm_d)

    return (y_p.reshape(batch, seq, D_MODEL),
            y_s.reshape(dec_b, dec_t, D_MODEL),
            kf_p.reshape(batch, 1, seq, N_HEADS, HEAD_W),
            vf_p.reshape(batch, 1, seq, N_HEADS, HEAD_W),
            kf_s.reshape(dec_b, 1, dec_t, N_HEADS, HEAD_W),
            vf_s.reshape(dec_b, 1, dec_t, N_HEADS, HEAD_W),
            vsn_s.reshape(dec_b, 1, dec_t, N_GROUPS_SG, SG_CH))
```

```python
import functools
import math

import numpy as np
import jax
import jax.numpy as jnp
from jax import lax
from jax.experimental import pallas as pl
from jax.experimental.pallas import tpu as pltpu

F32 = jnp.float32
BF16 = jnp.bfloat16

D_MODEL = 1024
N_HEADS = 4
DK = 64
HEAD_W = 128
QK_W = N_HEADS * HEAD_W
N_GROUPS_SG = 4
SG_CH = 128
IN_W = 2560
CHUNK = 128
PAGE = 128
N_BUCKETS = 32
MAX_DISTANCE = 128
N_EG = 4
EPG = 4
N_PAIRS = 6
N_CLASSES = N_EG * N_PAIRS
D_EXPERT = 512
EPS = 1e-6
LAM_INIT = 0.8 - 0.6 * math.exp(-0.3 * 0)
NEG = -1e30
LOG2E = math.log2(math.e)
LANES = 128
SUBLANES = 8

TM_TOK = 512
STAGE_A_PARTS = 4
TQ_ATT = 2048
PAGES_PER_STEP = 16
TM_E = 256
TM_D = 256
N_SUB = 2
VMEM_LIMIT = 56 * 1024 * 1024


def _cparams(sem):
    return pltpu.CompilerParams(dimension_semantics=sem, vmem_limit_bytes=VMEM_LIMIT)


def _ada_body(c_ref, w_ref, b_ref, o_ref):
    c = c_ref[...]
    a = (c * jax.nn.sigmoid(c)).astype(BF16)
    o_ref[...] = jnp.dot(a, w_ref[...].astype(BF16), preferred_element_type=F32) + b_ref[...]


def _ada(c_all, w_ada, b_ada):
    m = c_all.shape[0]
    n = w_ada.shape[1]
    tn = 1536
    return pl.pallas_call(
        _ada_body,
        grid=(n // tn,),
        in_specs=[pl.BlockSpec((m, D_MODEL), lambda j: (0, 0)),
                  pl.BlockSpec((D_MODEL, tn), lambda j: (0, j)),
                  pl.BlockSpec((1, tn), lambda j: (0, j))],
        out_specs=pl.BlockSpec((m, tn), lambda j: (0, j)),
        out_shape=jax.ShapeDtypeStruct((m, n), F32),
        compiler_params=_cparams(("arbitrary",)),
        name="adaln",
    )(c_all, w_ada, b_ada.reshape(1, n))


def _stage_a_body(x_ref, mod_ref, w_in_ref, ws_ref, bs_ref, gln_ref, bln_ref,
                  *out_refs, chunk, sample):
    if sample:
        q_ref, kf_ref, vf_ref, sg_ref, vsn_ref = out_refs
    else:
        q_ref, kf_ref, kb_ref, vf_ref, vt_ref, sg_ref = out_refs
    x = x_ref[...]
    tm = x.shape[0]
    mod = mod_ref[0]
    sh1 = mod[:, 0:D_MODEL]
    sc1 = mod[:, D_MODEL:2 * D_MODEL]
    ms = jnp.mean(x * x, axis=-1, keepdims=True)
    h = (x * lax.rsqrt(ms + EPS)) * (1.0 + sc1) + sh1
    hb = h.astype(BF16)
    n_parts = 1 if sample else STAGE_A_PARTS
    rows_p = tm // n_parts
    zs = [jnp.dot(hb[p * rows_p:(p + 1) * rows_p], w_in_ref[...], preferred_element_type=F32)
          for p in range(n_parts)]
    for p, z in enumerate(zs):
        a = p * rows_p
        q = z[:, 0:QK_W] * (DK ** -0.5)
        k = z[:, QK_W:2 * QK_W]
        v = z[:, 2 * QK_W:3 * QK_W]
        for hd in range(N_HEADS):
            kf_ref[pl.ds(a * N_HEADS + hd, rows_p, stride=N_HEADS), :] = k[:, hd * HEAD_W:(hd + 1) * HEAD_W]
            vf_ref[pl.ds(a * N_HEADS + hd, rows_p, stride=N_HEADS), :] = v[:, hd * HEAD_W:(hd + 1) * HEAD_W]
        if sample:
            q_ref[...] = q
        else:
            q_ref[a:a + rows_p, :] = (q * LOG2E).astype(BF16)
            kb_ref[a:a + rows_p, :] = k.astype(BF16)
            vt_ref[0, :, a:a + rows_p] = v.T.astype(BF16)
        u = z[:, 3 * QK_W:4 * QK_W]
        vs = z[:, 4 * QK_W:5 * QK_W]
        for g in range(N_GROUPS_SG):
            lo, hi = g * SG_CH, (g + 1) * SG_CH
            vg = vs[:, lo:hi]
            mu = jnp.mean(vg, axis=-1, keepdims=True)
            dv = vg - mu
            var = jnp.mean(dv * dv, axis=-1, keepdims=True)
            vn = (dv * lax.rsqrt(var + EPS)) * gln_ref[g:g + 1, :] + bln_ref[g:g + 1, :]
            if sample:
                vsn_ref[:, lo:hi] = vn
            vnb = vn.astype(BF16)
            for c in range(rows_p // chunk):
                r0, r1 = c * chunk, (c + 1) * chunk
                s = jnp.dot(ws_ref[g], vnb[r0:r1], preferred_element_type=F32) + bs_ref[g]
                sg_ref[a + r0:a + r1, lo:hi] = (u[r0:r1, lo:hi] * s).astype(BF16)


def _stage_a(x, mod, w_in_b, ws, bs, gln, bln, *, tm, chunk, sample):
    n = x.shape[0]
    nt = n // tm
    mrows = mod.shape[1]
    row = lambda w, dt: jax.ShapeDtypeStruct((n, w), dt)
    blk = lambda w: pl.BlockSpec((tm, w), lambda i: (i, 0))
    cache_shape = jax.ShapeDtypeStruct((n * N_HEADS, HEAD_W), F32)
    cache_blk = pl.BlockSpec((tm * N_HEADS, HEAD_W), lambda i: (i, 0))
    if sample:
        out_shape = [row(QK_W, F32), cache_shape, cache_shape, row(QK_W, BF16), row(QK_W, F32)]
        out_specs = [blk(QK_W), cache_blk, cache_blk, blk(QK_W), blk(QK_W)]
        mod_map = lambda i: (0, 0, 0)
    else:
        out_shape = [row(QK_W, BF16), cache_shape, row(QK_W, BF16), cache_shape,
                     jax.ShapeDtypeStruct((nt, QK_W, tm), BF16), row(QK_W, BF16)]
        out_specs = [blk(QK_W), cache_blk, blk(QK_W), cache_blk,
                     pl.BlockSpec((1, QK_W, tm), lambda i: (i, 0, 0)), blk(QK_W)]
        tiles_per_batch = 4096 // tm
        mod_map = lambda i: (i // tiles_per_batch, 0, 0)
    full = lambda a: pl.BlockSpec(a.shape, lambda i: (0,) * a.ndim)
    return pl.pallas_call(
        functools.partial(_stage_a_body, chunk=chunk, sample=sample),
        grid=(nt,),
        in_specs=[blk(D_MODEL),
                  pl.BlockSpec((1, mrows, 6 * D_MODEL), mod_map),
                  full(w_in_b), full(ws), full(bs), full(gln), full(bln)],
        out_specs=out_specs,
        out_shape=out_shape,
        compiler_params=_cparams(("arbitrary",)),
        name="stage_a_sample" if sample else "stage_a_prompt",
    )(x, mod, w_in_b, ws, bs, gln, bln)


ATT_COLS = 256
BIAS_BLOCK = MAX_DISTANCE


def _attn_body(lam_ref, q_ref, k_ref, vt_ref, bias_ref, g_ref, o_ref, *scratch, tq, tk):
    n_chain = 2 * tq // ATT_COLS
    q2_refs, m_refs, l_refs, acc_refs = (scratch[i * n_chain:(i + 1) * n_chain] for i in range(4))
    qi = pl.program_id(2)
    for c in range(n_chain):
        q0 = (c * ATT_COLS) % tq
        q = q_ref[q0:q0 + ATT_COLS, :]
        lane = lax.broadcasted_iota(jnp.int32, q.shape, 1)
        keep = (lane < DK) if c < n_chain // 2 else (lane >= DK)
        q2_refs[c][...] = jnp.where(keep, q, jnp.zeros_like(q))
        m_refs[c][...] = jnp.full(m_refs[c].shape, NEG, F32)
        l_refs[c][...] = jnp.zeros(l_refs[c].shape, F32)
        acc_refs[c][...] = jnp.zeros(acc_refs[c].shape, F32)

    blk = BIAS_BLOCK

    kblocks = tk // blk

    def block_kinds(rel, q0):
        return [[(q0 // blk + b) - (rel + a) for b in range(ATT_COLS // blk)] for a in range(kblocks)]

    def keys_needed(rel, q0):
        if rel is None:
            return kblocks
        return sum(1 for row in block_kinds(rel, q0) if max(row) >= 0)

    def with_bias(s, rel, q0):
        if rel is None:
            return s
        kinds = block_kinds(rel, q0)[:s.shape[0] // blk]
        if all(d >= 2 for row in kinds for d in row):
            return s
        nxt, near = bias_ref[0, 0], bias_ref[0, 1]
        pick = lambda d: (jnp.full((blk, blk), NEG, F32) if d < 0 else near if d == 0 else nxt if d == 1
                          else jnp.zeros((blk, blk), F32))
        rows = []
        for a, row in enumerate(kinds):
            s_row = s[a * blk:(a + 1) * blk]
            if any(d < 2 for d in row):
                s_row = s_row + jnp.concatenate([pick(d) for d in row], axis=1)
            rows.append(s_row)
        return jnp.concatenate(rows, axis=0)

    def tiles(*work):
        scores = []
        for j, rel in work:
            for c in range(n_chain):
                nk = keys_needed(rel, (c * ATT_COLS) % tq) * blk
                if nk == 0:
                    scores.append(None)
                    continue
                k = k_ref[pl.ds(pl.multiple_of(j * tk, tk), nk), :]
                s = lax.dot_general(k, q2_refs[c][...], (((1,), (1,)), ((), ())),
                                    preferred_element_type=F32)
                scores.append(with_bias(s, rel, (c * ATT_COLS) % tq))
        for w, (j, _) in enumerate(work):
            for c in range(n_chain):
                s = scores[w * n_chain + c]
                if s is None:
                    continue
                vt = vt_ref[j, :, 0:s.shape[0]]
                m_old = m_refs[c][...]
                m_new = jnp.maximum(m_old, jnp.max(s, axis=0, keepdims=True))
                alpha = jnp.exp2(m_old - m_new)
                p = jnp.exp2(s - m_new)
                l_refs[c][...] = alpha * l_refs[c][...] + jnp.sum(p, axis=0, keepdims=True)
                acc_refs[c][...] = alpha * acc_refs[c][...] + jnp.dot(vt, p.astype(BF16),
                                                                      preferred_element_type=F32)
                m_refs[c][...] = m_new

    ratio = tq // tk
    first_diag = qi * ratio
    n_plain = jnp.maximum(first_diag - 1, 0)

    def plain_pair(jj, carry):
        tiles((2 * jj, None), (2 * jj + 1, None))
        return carry

    lax.fori_loop(0, n_plain // 2, plain_pair, 0)

    @pl.when(qi >= 1)
    def _():
        if ratio % 2 == 0:
            tiles((first_diag - 2, None), (first_diag - 1, -kblocks))
        else:
            @pl.when(n_plain % 2 == 1)
            def _():
                tiles((n_plain - 1, None))
            tiles((first_diag - 1, -kblocks))

    tiles(*[(first_diag + r, r * kblocks) for r in range(ratio)])

    lam = lam_ref[0]
    o_all = jnp.concatenate([acc_refs[c][...] * (1.0 / l_refs[c][...]) for c in range(n_chain)],
                            axis=1)
    o = o_all[:, 0:tq] - lam * o_all[:, tq:2 * tq]
    ms = jnp.mean(o * o, axis=0, keepdims=True)
    on = (o * lax.rsqrt(ms + EPS)) * g_ref[...] * (1.0 - LAM_INIT)
    o_ref[...] = on.T.astype(BF16)


def _attn_prompt(lam, q, kb, vt, bias_t, g_col, *, batch, seq, tq, tk):
    nq = seq // tq
    nk = seq // tk
    n = batch * seq
    n_chain = 2 * tq // ATT_COLS
    assert vt.shape == (batch * nk, QK_W, tk) and tq % tk == 0 and tk % BIAS_BLOCK == 0
    return pl.pallas_call(
        functools.partial(_attn_body, tq=tq, tk=tk),
        grid=(batch, N_HEADS, nq),
        in_specs=[pl.BlockSpec(memory_space=pltpu.SMEM),
                  pl.BlockSpec((tq, HEAD_W), lambda b, h, i: (b * nq + i, h)),
                  pl.BlockSpec((seq, HEAD_W), lambda b, h, i: (b, h)),
                  pl.BlockSpec((nk, HEAD_W, tk), lambda b, h, i: (b, h, 0)),
                  pl.BlockSpec((1, 2, BIAS_BLOCK, BIAS_BLOCK), lambda b, h, i: (h, 0, 0, 0)),
                  pl.BlockSpec((HEAD_W, 1), lambda b, h, i: (0, 0))],
        out_specs=pl.BlockSpec((tq, HEAD_W), lambda b, h, i: (b * nq + i, h)),
        out_shape=jax.ShapeDtypeStruct((n, QK_W), BF16),
        scratch_shapes=([pltpu.VMEM((ATT_COLS, HEAD_W), BF16)] * n_chain
                        + [pltpu.VMEM((1, ATT_COLS), F32)] * (2 * n_chain)
                        + [pltpu.VMEM((HEAD_W, ATT_COLS), F32)] * n_chain),
        compiler_params=_cparams(("arbitrary", "arbitrary", "arbitrary")),
        name="attn_prompt",
    )(lam, q, kb, vt, bias_t, g_col)


PAGE_ROWS = PAGE * N_HEADS
TAIL_TOKENS = 2 * PAGE


def _sattn_body(pt_ref, lam_ref, q_ref, knew_ref, vnew_ref, bl_ref, bn_ref, g_ref, ck_ref, cv_ref,
                o_ref, kbuf, vbuf, sem, mask_ref, m_ref, l_ref, acc_ref, *, pages, n_steps, total):
    b = pl.program_id(0)
    s = pl.program_id(1)
    step = b * n_steps + s
    slot = step % 2

    def page_copies(step_idx, sl):
        base = step_idx * pages
        out = []
        for i in range(pages):
            src = pl.ds(pl.multiple_of(pt_ref[base + i] * PAGE_ROWS, PAGE_ROWS), PAGE_ROWS)
            dst = pl.ds(i * PAGE_ROWS, PAGE_ROWS)
            out.append(pltpu.make_async_copy(ck_ref.at[src], kbuf.at[sl, dst], sem.at[sl, 0]))
            out.append(pltpu.make_async_copy(cv_ref.at[src], vbuf.at[sl, dst], sem.at[sl, 1]))
        return out

    def start_all(copies):
        for n, c in enumerate(copies):
            c.start(priority=n % 2)

    @pl.when(step == 0)
    def _():
        start_all(page_copies(0, 0))

    @pl.when(step + 1 < total)
    def _():
        start_all(page_copies(step + 1, 1 - slot))

    for c in page_copies(step, slot):
        c.wait()

    @pl.when(s == 0)
    def _():
        m_ref[...] = jnp.full(m_ref.shape, NEG, F32)
        l_ref[...] = jnp.zeros(l_ref.shape, F32)
        acc_ref[...] = jnp.zeros(acc_ref.shape, F32)

    @pl.when(step == 0)
    def _():
        row = lax.broadcasted_iota(jnp.int32, mask_ref.shape, 0)
        col = lax.broadcasted_iota(jnp.int32, mask_ref.shape, 1)
        same_head = (col % N_HEADS) == (row // (2 * SUBLANES))
        mask_ref[...] = jnp.where(same_head, 0.0, NEG)

    q = q_ref[...]
    lane = lax.broadcasted_iota(jnp.int32, (SUBLANES, HEAD_W), 1)
    pieces = []
    for h in range(N_HEADS):
        qh = q[:, h * HEAD_W:(h + 1) * HEAD_W]
        pieces += [jnp.where(lane < DK, qh, 0.0), jnp.where(lane >= DK, qh, 0.0)]
    qm = jnp.concatenate(pieces, axis=0).astype(BF16)

    def update(kb, vb, bias):
        sc = lax.dot_general(qm, kb, (((1,), (1,)), ((), ())),
                             preferred_element_type=F32) + bias
        m_old = m_ref[...]
        m_new = jnp.maximum(m_old, jnp.max(sc, axis=1, keepdims=True))
        alpha = jnp.exp(m_old - m_new)
        p = jnp.exp(sc - m_new)
        l_ref[...] = alpha * l_ref[...] + jnp.sum(p, axis=1, keepdims=True)
        acc_ref[...] = alpha * acc_ref[...] + jnp.dot(p.astype(BF16), vb,
                                                      preferred_element_type=F32)
        m_ref[...] = m_new

    is_last = s == n_steps - 1
    head_cols = mask_ref.shape[1] - bl_ref.shape[1]
    update(kbuf[slot].astype(BF16), vbuf[slot].astype(BF16),
           jnp.concatenate([mask_ref[:, :head_cols],
                            mask_ref[:, head_cols:] + bl_ref[...] * is_last.astype(F32)], axis=1))

    @pl.when(is_last)
    def _():
        update(knew_ref[0], vnew_ref[0], bn_ref[...])
        lam = lam_ref[0]
        o_all = acc_ref[...] * (1.0 / l_ref[...])
        for h in range(N_HEADS):
            r = h * 2 * SUBLANES
            o = o_all[r:r + SUBLANES] - lam * o_all[r + SUBLANES:r + 2 * SUBLANES]
            ms = jnp.mean(o * o, axis=-1, keepdims=True)
            o_ref[:, h * HEAD_W:(h + 1) * HEAD_W] = ((o * lax.rsqrt(ms + EPS)) * g_ref[...]
                                                     * (1.0 - LAM_INIT))


def _attn_sample(page_table_flat, lam, q_s, knew, vnew, bias_last, bias_new, g_row, cache_k, cache_v,
                 *, dec_b, n_pages):
    pages = PAGES_PER_STEP
    n_steps = n_pages // pages
    total = dec_b * n_steps
    nq = q_s.shape[0] // dec_b
    n_rows = 2 * N_HEADS * nq
    step_rows = pages * PAGE_ROWS
    grid_spec = pltpu.PrefetchScalarGridSpec(
        num_scalar_prefetch=1,
        grid=(dec_b, n_steps),
        in_specs=[pl.BlockSpec(memory_space=pltpu.SMEM),
                  pl.BlockSpec((nq, QK_W), lambda b, s, pt: (b, 0)),
                  pl.BlockSpec((1, PAGE, HEAD_W), lambda b, s, pt: (b, 0, 0)),
                  pl.BlockSpec((1, PAGE, HEAD_W), lambda b, s, pt: (b, 0, 0)),
                  pl.BlockSpec(bias_last.shape, lambda b, s, pt: (0, 0)),
                  pl.BlockSpec(bias_new.shape, lambda b, s, pt: (0, 0)),
                  pl.BlockSpec((1, HEAD_W), lambda b, s, pt: (0, 0)),
                  pl.BlockSpec(memory_space=pl.ANY),
                  pl.BlockSpec(memory_space=pl.ANY)],
        out_specs=pl.BlockSpec((nq, QK_W), lambda b, s, pt: (b, 0)),
        scratch_shapes=[pltpu.VMEM((2, step_rows, HEAD_W), F32),
                        pltpu.VMEM((2, step_rows, HEAD_W), F32),
                        pltpu.SemaphoreType.DMA((2, 2)),
                        pltpu.VMEM((n_rows, step_rows), F32),
                        pltpu.VMEM((n_rows, 1), F32), pltpu.VMEM((n_rows, 1), F32),
                        pltpu.VMEM((n_rows, HEAD_W), F32)])
    return pl.pallas_call(
        functools.partial(_sattn_body, pages=pages, n_steps=n_steps, total=total),
        grid_spec=grid_spec,
        out_shape=jax.ShapeDtypeStruct(q_s.shape, F32),
        compiler_params=_cparams(("arbitrary", "arbitrary")),
        name="attn_sample",
    )(page_table_flat, lam, q_s, knew, vnew, bias_last, bias_new, g_row, cache_k, cache_v)


INFO_W_LO, INFO_W_HI, INFO_CLS = 0, 1, 2


def _stage_c_body(o_ref, sg_ref, x_ref, mod_ref, wo_ref, wr_ref, br_ref, x1_ref, h2_ref, cls_ref, info_ref):
    x = x_ref[...]
    tm = x.shape[0]
    mod = mod_ref[0]
    g1 = mod[:, 2 * D_MODEL:3 * D_MODEL]
    sh2 = mod[:, 3 * D_MODEL:4 * D_MODEL]
    sc2 = mod[:, 4 * D_MODEL:5 * D_MODEL]
    mix = (jnp.dot(o_ref[...].astype(BF16), wo_ref[0:QK_W, :], preferred_element_type=F32)
           + jnp.dot(sg_ref[...], wo_ref[QK_W:2 * QK_W, :], preferred_element_type=F32))
    x1 = x + g1 * mix
    x1_ref[...] = x1
    ms = jnp.mean(x1 * x1, axis=-1, keepdims=True)
    h2 = ((x1 * lax.rsqrt(ms + EPS)) * (1.0 + sc2) + sh2).astype(BF16)
    h2_ref[...] = h2
    lg = lax.dot_general(wr_ref[...], h2, (((1,), (1,)), ((), ())),
                         preferred_element_type=F32) + br_ref[...]
    gl = [lg[i:i + 1, :] for i in range(N_EG)]
    el = [lg[N_EG + i:N_EG + i + 1, :] for i in range(N_EG * EPG)]
    gmax = jnp.maximum(jnp.maximum(gl[0], gl[1]), jnp.maximum(gl[2], gl[3]))
    gi = jnp.where(gl[0] == gmax, 0, jnp.where(gl[1] == gmax, 1, jnp.where(gl[2] == gmax, 2, 3)))
    gsum = (jnp.exp(gl[0] - gmax) + jnp.exp(gl[1] - gmax)
            + jnp.exp(gl[2] - gmax) + jnp.exp(gl[3] - gmax))
    gp = 1.0 / gsum
    sel = [jnp.where(gi == 0, el[j], jnp.where(gi == 1, el[EPG + j],
                                               jnp.where(gi == 2, el[2 * EPG + j], el[3 * EPG + j])))
           for j in range(EPG)]
    v0 = jnp.maximum(jnp.maximum(sel[0], sel[1]), jnp.maximum(sel[2], sel[3]))
    i0 = jnp.where(sel[0] == v0, 0, jnp.where(sel[1] == v0, 1, jnp.where(sel[2] == v0, 2, 3)))
    rest = [jnp.where(i0 == j, -3e38, sel[j]) for j in range(EPG)]
    v1 = jnp.maximum(jnp.maximum(rest[0], rest[1]), jnp.maximum(rest[2], rest[3]))
    i1 = jnp.where(rest[0] == v1, 0, jnp.where(rest[1] == v1, 1, jnp.where(rest[2] == v1, 2, 3)))
    e1 = jnp.exp(v1 - v0)
    den = 1.0 / (1.0 + e1)
    tw0 = den * gp
    tw1 = e1 * den * gp
    first_low = i0 < i1
    lo = jnp.where(first_low, i0, i1)
    hi = jnp.where(first_low, i1, i0)
    w_lo = jnp.where(first_low, tw0, tw1)
    w_hi = jnp.where(first_low, tw1, tw0)
    pair = jnp.where(lo == 0, 0, jnp.where(lo == 1, 3, 5)) + hi - lo - 1
    cls = gi * N_PAIRS + pair
    cls_ref[...] = jnp.broadcast_to(cls, cls_ref.shape).astype(jnp.int32)
    row = lax.broadcasted_iota(jnp.int32, (LANES, tm), 0)
    rec = jnp.where(row == INFO_W_LO, w_lo, jnp.where(row == INFO_W_HI, w_hi,
                                                      jnp.where(row == INFO_CLS, cls.astype(F32), 0.0)))
    info_ref[...] = rec.T


def _stage_c(o, sg, x, mod, wo_b, wr_t, br, *, tm, sample):
    n = x.shape[0]
    nt = n // tm
    mrows = mod.shape[1]
    blk = lambda w: pl.BlockSpec((tm, w), lambda i: (i, 0))
    full = lambda a: pl.BlockSpec(a.shape, lambda i: (0,) * a.ndim)
    if sample:
        mod_map = lambda i: (0, 0, 0)
    else:
        tiles_per_batch = 4096 // tm
        mod_map = lambda i: (i // tiles_per_batch, 0, 0)
    return pl.pallas_call(
        _stage_c_body,
        grid=(nt,),
        in_specs=[blk(QK_W), blk(QK_W), blk(D_MODEL),
                  pl.BlockSpec((1, mrows, 6 * D_MODEL), mod_map),
                  full(wo_b), full(wr_t), full(br)],
        out_specs=[blk(D_MODEL), blk(D_MODEL),
                   pl.BlockSpec((SUBLANES, tm), lambda i: (0, i)),
                   blk(LANES)],
        out_shape=[jax.ShapeDtypeStruct((n, D_MODEL), F32),
                   jax.ShapeDtypeStruct((n, D_MODEL), BF16),
                   jax.ShapeDtypeStruct((SUBLANES, n), jnp.int32),
                   jax.ShapeDtypeStruct((n, LANES), F32)],
        compiler_params=_cparams(("arbitrary",)),
        name="stage_c_sample" if sample else "stage_c_prompt",
    )(o, sg, x, mod, wo_b, wr_t, br)


def rows8(ref, start, count):
    scale = lambda v: v * SUBLANES if isinstance(v, int) else pl.multiple_of(v * SUBLANES, SUBLANES)
    return ref.at[pl.ds(scale(start), scale(count))]


def _perm_t(cls_col):
    n = cls_col.shape[0]
    lane = lax.broadcasted_iota(jnp.int32, (n, LANES), 1).astype(F32)
    onehot = (lane == cls_col).astype(BF16)
    r = lax.broadcasted_iota(jnp.int32, (n, n), 0)
    c = lax.broadcasted_iota(jnp.int32, (n, n), 1)
    before = (c < r).astype(BF16)
    rank = jnp.dot(before, onehot, preferred_element_type=F32)
    cnt = jnp.sum(onehot.astype(F32), axis=0, keepdims=True)
    cr = lax.broadcasted_iota(jnp.int32, (LANES, LANES), 0)
    cc = lax.broadcasted_iota(jnp.int32, (LANES, LANES), 1)
    lower_cls = (cr < cc).astype(BF16)
    base = jnp.dot(jnp.broadcast_to(cnt, (SUBLANES, LANES)).astype(BF16), lower_cls,
                   preferred_element_type=F32)[0:1, :]
    pos = jnp.sum(onehot.astype(F32) * (base + rank), axis=1, keepdims=True)
    dest = lax.broadcasted_iota(jnp.int32, (n, n), 1).astype(F32)
    return (dest == pos).astype(F32)


def _dispatch_body(soff_ref, slen_ref, poff_ref, plen_ref, ptot_ref, nact_ref, hp_ref, hs_ref, ip_ref, is_ref,
                   xs_ref, buf, zbuf, sem, zsem, *, tm, n_sub, n_tiles, n_slab_tiles):
    i = pl.program_id(0)
    slot = i % 2
    is_sample = i == n_tiles - 1

    def wait_tile(sl, tokens):
        pltpu.make_async_copy(rows8(buf.at[sl], 0, tokens), rows8(buf.at[sl], 0, tokens), sem.at[sl]).wait()

    @pl.when(i == 0)
    def _():
        zbuf[...] = jnp.zeros(zbuf.shape, F32)
        for c in range(N_CLASSES):
            @pl.when(plen_ref[c] > 0)
            def _():
                pltpu.make_async_copy(rows8(zbuf, 0, plen_ref[c]), rows8(xs_ref, poff_ref[c], plen_ref[c]),
                                      zsem).start()
        for j in range(n_slab_tiles - N_CLASSES, n_slab_tiles):
            @pl.when(j >= nact_ref[0])
            def _():
                pltpu.make_async_copy(zbuf, rows8(xs_ref, j * TM_E, TM_E), zsem).start()

    xp = []
    for u in range(n_sub):
        rows = slice(u * tm, (u + 1) * tm)
        x = hp_ref[rows, :]
        info = ip_ref[rows, :]
        if u == 0:
            x = jnp.where(is_sample, hs_ref[...], x)
            info = jnp.where(is_sample, is_ref[...], info)
        perm = _perm_t(info[:, INFO_CLS:INFO_CLS + 1]).T.astype(BF16)
        xp.append(jnp.dot(perm, x, preferred_element_type=F32))

    @pl.when(i >= 2)
    def _():
        wait_tile(slot, n_sub * tm)

    bs = buf.at[slot]
    for u in range(n_sub):
        for c in range(D_MODEL // LANES):
            bs[pl.ds(u * tm * SUBLANES + c, tm, stride=SUBLANES), :] = xp[u][:, c * LANES:(c + 1) * LANES]
    for u in range(n_sub):
        local = u * tm
        for c in range(N_CLASSES):
            k = (i * n_sub + u) * N_CLASSES + c
            n_rows = slen_ref[k]

            @pl.when(n_rows > 0)
            def _():
                pltpu.make_async_copy(rows8(bs, local, n_rows), rows8(xs_ref, soff_ref[k], n_rows),
                                      sem.at[slot]).start()
            local = local + n_rows

    @pl.when(i == n_tiles - 1)
    def _():
        wait_tile(slot, tm)
        if n_tiles >= 2:
            wait_tile(1 - slot, n_sub * tm)

        @pl.when(ptot_ref[0] > 0)
        def _():
            n = pl.multiple_of(ptot_ref[0] * SUBLANES, SUBLANES)
            pltpu.make_async_copy(xs_ref.at[pl.ds(0, n)], xs_ref.at[pl.ds(0, n)], zsem).wait()


def _dispatch(seg_off, seg_len, pad_off, pad_len, pad_tot, nact, h2_p, h2_s, info_p, info_s, *, tm, n_sub, n_slots):
    n_prompt_steps = h2_p.shape[0] // (n_sub * tm)
    n_tiles = n_prompt_steps + 1
    assert h2_s.shape[0] == tm and h2_p.shape[0] % (n_sub * tm) == 0
    assert seg_len.shape[0] == n_tiles * n_sub * N_CLASSES
    last_p = n_prompt_steps - 1
    grid_spec = pltpu.PrefetchScalarGridSpec(
        num_scalar_prefetch=6,
        grid=(n_tiles,),
        in_specs=[pl.BlockSpec((n_sub * tm, D_MODEL), lambda i, *_: (jnp.minimum(i, last_p), 0)),
                  pl.BlockSpec((tm, D_MODEL), lambda i, *_: (0, 0)),
                  pl.BlockSpec((n_sub * tm, LANES), lambda i, *_: (jnp.minimum(i, last_p), 0)),
                  pl.BlockSpec((tm, LANES), lambda i, *_: (0, 0))],
        out_specs=pl.BlockSpec(memory_space=pl.ANY),
        scratch_shapes=[pltpu.VMEM((2, n_sub * tm * SUBLANES, LANES), F32),
                        pltpu.VMEM((TM_E * SUBLANES, LANES), F32),
                        pltpu.SemaphoreType.DMA((2,)),
                        pltpu.SemaphoreType.DMA(())])
    return pl.pallas_call(
        functools.partial(_dispatch_body, tm=tm, n_sub=n_sub, n_tiles=n_tiles, n_slab_tiles=n_slots // TM_E),
        grid_spec=grid_spec,
        out_shape=jax.ShapeDtypeStruct((n_slots * SUBLANES, LANES), F32),
        compiler_params=_cparams(("arbitrary",)),
        name="moe_dispatch",
    )(seg_off, seg_len, pad_off, pad_len, pad_tot, nact, h2_p, h2_s, info_p, info_s)


def _moe_body(ea_ref, eb_ref, nact_ref, x_ref, wga_ref, wgb_ref, wua_ref, wub_ref, wda_ref, wdb_ref,
              ya_ref, yb_ref, *, tm):
    i = pl.program_id(0)

    @pl.when(i < nact_ref[0])
    def _():
        x = jnp.concatenate([x_ref[pl.ds(c, tm, stride=SUBLANES), :] for c in range(D_MODEL // LANES)],
                            axis=1).astype(BF16)

        def expert(wg_ref, wu_ref, wd_ref):
            gate = jnp.dot(x, wg_ref[0], preferred_element_type=F32)
            up = jnp.dot(x, wu_ref[0], preferred_element_type=F32)
            he = (gate * jax.nn.sigmoid(gate)) * up
            return jnp.dot(he.astype(BF16), wd_ref[0], preferred_element_type=F32)

        ya = expert(wga_ref, wua_ref, wda_ref)
        yb = expert(wgb_ref, wub_ref, wdb_ref)
        for c in range(D_MODEL // LANES):
            ya_ref[pl.ds(c, tm, stride=SUBLANES), :] = ya[:, c * LANES:(c + 1) * LANES]
            yb_ref[pl.ds(c, tm, stride=SUBLANES), :] = yb[:, c * LANES:(c + 1) * LANES]

    @pl.when(i >= nact_ref[0])
    def _():
        ya_ref[...] = jnp.zeros(ya_ref.shape, ya_ref.dtype)
        yb_ref[...] = jnp.zeros(yb_ref.shape, yb_ref.dtype)


def _moe(tile_ea, tile_eb, nact, x_sorted, wg_b, wu_b, wd_b, *, tm, n_max):
    wspec_in = lambda sel: pl.BlockSpec((1, D_MODEL, D_EXPERT), sel)
    wspec_out = lambda sel: pl.BlockSpec((1, D_EXPERT, D_MODEL), sel)
    sel_a = lambda i, ea, eb, na: (ea[i], 0, 0)
    sel_b = lambda i, ea, eb, na: (eb[i], 0, 0)
    rows_in = lambda i, ea, eb, na: (jnp.minimum(i, na[0] - 1), 0)
    grid_spec = pltpu.PrefetchScalarGridSpec(
        num_scalar_prefetch=3,
        grid=(n_max,),
        in_specs=[pl.BlockSpec((tm * SUBLANES, LANES), rows_in),
                  wspec_in(sel_a), wspec_in(sel_b), wspec_in(sel_a), wspec_in(sel_b),
                  wspec_out(sel_a), wspec_out(sel_b)],
        out_specs=[pl.BlockSpec((tm * SUBLANES, LANES), lambda i, ea, eb, na: (i, 0))] * 2)
    return pl.pallas_call(
        functools.partial(_moe_body, tm=tm),
        grid_spec=grid_spec,
        out_shape=[jax.ShapeDtypeStruct(x_sorted.shape, F32)] * 2,
        compiler_params=_cparams(("arbitrary",)),
        name="moe",
    )(tile_ea, tile_eb, nact, x_sorted, wg_b, wg_b, wu_b, wu_b, wd_b, wd_b)


def _final_body(soff_ref, slen_ref, x1_ref, info_ref, mod_ref, gf_ref, ya_ref, yb_ref, y_ref, buf, sem,
                *, tm, n_sub, n_tiles, tile_base):
    i = pl.program_id(0)
    slot = i % 2
    slabs = (ya_ref, yb_ref)

    def fetch(step, sl):
        for u in range(n_sub):
            local = u * tm
            for c in range(N_CLASSES):
                k = (step * n_sub + u + tile_base) * N_CLASSES + c
                n_rows = slen_ref[k]

                @pl.when(n_rows > 0)
                def _():
                    for e in range(2):
                        pltpu.make_async_copy(rows8(slabs[e], soff_ref[k], n_rows),
                                              rows8(buf.at[sl, e], local, n_rows), sem.at[sl, e]).start()
                local = local + n_rows

    @pl.when(i == 0)
    def _():
        fetch(0, 0)

    @pl.when(i + 1 < n_tiles)
    def _():
        fetch(i + 1, 1 - slot)

    info = info_ref[...]
    perm_t = [_perm_t(info[u * tm:(u + 1) * tm, INFO_CLS:INFO_CLS + 1]).astype(BF16) for u in range(n_sub)]
    moe = None
    for e, lane_w in enumerate((INFO_W_LO, INFO_W_HI)):
        pltpu.make_async_copy(buf.at[slot, e], buf.at[slot, e], sem.at[slot, e]).wait()
        bs = buf.at[slot, e]
        parts = []
        for u in range(n_sub):
            ye = jnp.concatenate([bs[pl.ds(u * tm * SUBLANES + c, tm, stride=SUBLANES), :]
                                  for c in range(D_MODEL // LANES)], axis=1)
            parts.append(jnp.dot(perm_t[u], ye.astype(BF16), preferred_element_type=F32))
        term = info[:, lane_w:lane_w + 1] * jnp.concatenate(parts, axis=0)
        moe = term if moe is None else moe + term
    x1 = x1_ref[...]
    g2 = mod_ref[0][:, 5 * D_MODEL:6 * D_MODEL]
    x2 = x1 + g2 * moe
    ms = jnp.mean(x2 * x2, axis=-1, keepdims=True)
    y_ref[...] = (x2 * lax.rsqrt(ms + EPS)) * gf_ref[...]


def _final(seg_off, seg_len, x1, info, mod, g_final, ya_sorted, yb_sorted, *, tm, n_sub, sample, tile_base):
    n = x1.shape[0]
    rows = n_sub * tm
    nt = n // rows
    mrows = mod.shape[1]
    if sample:
        mod_map = lambda i, *_: (0, 0, 0)
    else:
        tiles_per_batch = 4096 // rows
        mod_map = lambda i, *_: (i // tiles_per_batch, 0, 0)
    grid_spec = pltpu.PrefetchScalarGridSpec(
        num_scalar_prefetch=2,
        grid=(nt,),
        in_specs=[pl.BlockSpec((rows, D_MODEL), lambda i, *_: (i, 0)),
                  pl.BlockSpec((rows, LANES), lambda i, *_: (i, 0)),
                  pl.BlockSpec((1, mrows, 6 * D_MODEL), mod_map),
                  pl.BlockSpec((1, D_MODEL), lambda i, *_: (0, 0)),
                  pl.BlockSpec(memory_space=pl.ANY),
                  pl.BlockSpec(memory_space=pl.ANY)],
        out_specs=pl.BlockSpec((rows, D_MODEL), lambda i, *_: (i, 0)),
        scratch_shapes=[pltpu.VMEM((2, 2, rows * SUBLANES, LANES), F32),
                        pltpu.SemaphoreType.DMA((2, 2))])
    return pl.pallas_call(
        functools.partial(_final_body, tm=tm, n_sub=n_sub, n_tiles=nt, tile_base=tile_base),
        grid_spec=grid_spec,
        out_shape=jax.ShapeDtypeStruct((n, D_MODEL), F32),
        compiler_params=_cparams(("arbitrary",)),
        name="final_sample" if sample else "final_prompt",
    )(seg_off, seg_len, x1, info, mod, g_final.reshape(1, D_MODEL), ya_sorted, yb_sorted)


def _bucket_table(n):
    d = np.arange(n)
    max_exact = N_BUCKETS // 2
    nf = np.maximum(d, 1).astype(np.float64)
    large = max_exact + (np.log(nf / max_exact) / math.log(MAX_DISTANCE / max_exact)
                         * (N_BUCKETS - max_exact)).astype(np.int64)
    large = np.minimum(large, N_BUCKETS - 1)
    return np.where(d < max_exact, d, large).astype(np.int32)


def _toeplitz(v, n_rows, n_cols):
    length = n_rows + n_cols - 1
    lead = v.shape[:-1]
    vp = jnp.concatenate([v, jnp.zeros(lead + (1,), v.dtype)], axis=-1)
    skew = jnp.tile(vp, (1,) * len(lead) + (n_rows,))[..., :n_rows * length].reshape(lead + (n_rows, length))
    return skew[..., n_rows - 1:n_rows - 1 + n_cols]


def kernel(x_prompt, x_sample, c_prompt, c_sample, cache_k, cache_v, page_table, w_ada, b_ada, w_in, w_o,
           lam_q1, lam_k1, lam_q2, lam_k2, g_subln, rel_bias, g_sg_ln, b_sg_ln, w_s, b_s, w_rg, b_rg,
           w_re, b_re, w_gate, w_up, w_down, g_final):
    batch, seq, _ = x_prompt.shape
    dec_b, dec_t, _ = x_sample.shape
    n_pages = page_table.shape[1]
    n_p = batch * seq
    n_s = dec_b * dec_t
    n_tot = n_p + n_s
    assert w_in.shape[0] == 1 and cache_k.shape[1] == 1 and seq % TQ_ATT == 0 and n_pages % PAGES_PER_STEP == 0
    assert n_p % TM_TOK == 0 and n_p % n_s == 0 and n_tot % TM_E == 0 and dec_t == SUBLANES
    assert TAIL_TOKENS >= MAX_DISTANCE + dec_t and TAIL_TOKENS <= PAGES_PER_STEP * PAGE
    assert n_s == TM_D and n_p % TM_D == 0 and N_CLASSES <= LANES

    w_in_b = w_in[0].astype(BF16)
    w_o_b = w_o[0].astype(BF16)
    wr_t = jnp.zeros((32, D_MODEL), F32).at[0:N_EG].set(w_rg[0].T).at[N_EG:N_EG + N_EG * EPG].set(w_re[0].T)
    wr_t = wr_t.astype(BF16)
    br = jnp.zeros((32, 1), F32).at[0:N_EG, 0].set(b_rg[0]).at[N_EG:N_EG + N_EG * EPG, 0].set(b_re[0])
    wg_b = w_gate[0].astype(BF16)
    wu_b = w_up[0].astype(BF16)
    wd_b = w_down[0].astype(BF16)
    ws_tril = jnp.tril(w_s[0])
    ws_p = ws_tril.astype(BF16)
    bs_p = b_s[0][:, :, None]
    same_seq = np.kron(np.eye(dec_b, dtype=np.float32), np.ones((dec_t, dec_t), np.float32))
    ws_s = (jnp.tile(ws_tril[:, :dec_t, :dec_t], (1, dec_b, dec_b)) * same_seq).astype(BF16)
    bs_s = jnp.tile(b_s[0][:, :dec_t], (1, dec_b))[:, :, None]
    gln = g_sg_ln[0]
    bln = b_sg_ln[0]
    lam = (jnp.exp(jnp.sum(lam_q1[0] * lam_k1[0])) - jnp.exp(jnp.sum(lam_q2[0] * lam_k2[0]))
           + LAM_INIT).reshape(1).astype(F32)

    blk = BIAS_BLOCK
    n_dist = max(2 * blk, TAIL_TOKENS + dec_t)
    onehot = np.eye(N_BUCKETS, dtype=np.float32)[_bucket_table(n_dist)]
    ft = jnp.dot(onehot, rel_bias - rel_bias[N_BUCKETS - 1], precision=lax.Precision.HIGHEST).T
    neg = lambda n: jnp.full((N_HEADS, n), NEG, F32)
    bias_near = _toeplitz(jnp.concatenate([neg(blk - 1), ft[:, 0:blk]], axis=1), blk, blk)
    bias_next = _toeplitz(ft[:, 1:2 * blk], blk, blk)
    bias_t = jnp.stack([bias_next, bias_near], axis=1) * LOG2E
    bl = _toeplitz(jnp.flip(ft[:, 1:TAIL_TOKENS + dec_t], axis=1), dec_t, TAIL_TOKENS)
    head_eq = jnp.eye(N_HEADS, dtype=F32)
    bias_last = (bl[:, None, :, :, None] * head_eq[:, None, None, None, :])
    bias_last = jnp.broadcast_to(bias_last, (N_HEADS, 2, dec_t, TAIL_TOKENS, N_HEADS)).reshape(
        2 * N_HEADS * dec_t, TAIL_TOKENS * N_HEADS)
    bn = _toeplitz(jnp.concatenate([jnp.flip(ft[:, 0:dec_t], axis=1), neg(dec_t - 1)], axis=1),
                   dec_t, dec_t)
    bn = jnp.where(head_eq[:, None, None, :] > 0, bn[:, :, :, None], NEG)
    bn = jnp.broadcast_to(bn[:, None], (N_HEADS, 2, dec_t, dec_t, N_HEADS)).reshape(
        2 * N_HEADS * dec_t, dec_t * N_HEADS)
    bias_new = jnp.concatenate([bn, jnp.full((bn.shape[0], PAGE - bn.shape[1]), NEG, F32)], axis=1)

    c_all = jnp.concatenate([c_prompt, c_sample, jnp.zeros((4, D_MODEL), F32)], axis=0)
    mod_all = _ada(c_all, w_ada[0], b_ada[0])
    mod_p = mod_all[:batch].reshape(batch, 1, 6 * D_MODEL)
    mod_s = jnp.repeat(mod_all[batch:batch + dec_b], dec_t, axis=0).reshape(1, n_s, 6 * D_MODEL)

    xp = x_prompt.reshape(n_p, D_MODEL)
    xs = x_sample.reshape(n_s, D_MODEL)

    q_p, kf_p, kb_p, vf_p, vt_p, sg_p = _stage_a(xp, mod_p, w_in_b, ws_p, bs_p, gln, bln,
                                                 tm=TM_TOK, chunk=CHUNK, sample=False)
    q_s, kf_s, vf_s, sg_s, vsn_s = _stage_a(xs, mod_s, w_in_b, ws_s, bs_s, gln, bln,
                                            tm=n_s, chunk=n_s, sample=True)

    g_col = g_subln[0].reshape(HEAD_W, 1)
    g_row = g_subln[0].reshape(1, HEAD_W)
    o_p = _attn_prompt(lam, q_p, kb_p, vt_p, bias_t, g_col, batch=batch, seq=seq, tq=TQ_ATT, tk=TM_TOK)
    pad = ((0, 0), (0, PAGE - dec_t * N_HEADS), (0, 0))
    knew = jnp.pad(kf_s.reshape(dec_b, dec_t * N_HEADS, HEAD_W), pad).astype(BF16)
    vnew = jnp.pad(vf_s.reshape(dec_b, dec_t * N_HEADS, HEAD_W), pad).astype(BF16)
    ck = cache_k.reshape(-1, HEAD_W)
    cv = cache_v.reshape(-1, HEAD_W)
    o_s = _attn_sample(page_table.reshape(-1), lam, q_s, knew, vnew, bias_last, bias_new, g_row, ck, cv,
                       dec_b=dec_b, n_pages=n_pages)

    x1_p, h2_p, cls_p, info_p = _stage_c(o_p, sg_p, xp, mod_p, w_o_b, wr_t, br, tm=TM_TOK, sample=False)
    x1_s, h2_s, cls_s, info_s = _stage_c(o_s, sg_s, xs, mod_s, w_o_b, wr_t, br, tm=n_s, sample=True)

    tm_e = TM_E
    tm_d = TM_D
    n_max = n_tot // tm_e + N_CLASSES
    n_dt = n_tot // tm_d
    cls = jnp.concatenate([cls_p[0], cls_s[0]]).reshape(n_dt, tm_d)
    classes = jnp.arange(N_CLASSES, dtype=jnp.int32)
    seg_len = jnp.sum((cls[:, :, None] == classes).astype(jnp.int32), axis=1)
    counts = jnp.sum(seg_len, axis=0)
    ntile_c = (counts + tm_e - 1) // tm_e
    tile_end = jnp.cumsum(ntile_c)
    class_base = (tile_end - ntile_c) * tm_e
    nact = tile_end[-1]
    seg_off = class_base[None, :] + jnp.cumsum(seg_len, axis=0) - seg_len
    pad_off = class_base + counts
    pad_len = ntile_c * tm_e - counts
    tile_ids = jnp.arange(n_max, dtype=jnp.int32)
    tile_cls = jnp.sum((tile_ids[:, None] >= tile_end[None, :]).astype(jnp.int32), axis=1)
    last_cls = jnp.sum((nact - 1 >= tile_end).astype(jnp.int32))
    tile_cls = jnp.where(tile_ids < nact, tile_cls, last_cls)
    grp = tile_cls // N_PAIRS
    pidx = tile_cls % N_PAIRS
    pair_lo = jnp.where(pidx < 3, 0, jnp.where(pidx < 5, 1, 2))
    pair_hi = jnp.where(pidx < 3, pidx + 1, jnp.where(pidx < 5, pidx - 1, 3))
    tile_ea = (grp * EPG + pair_lo).astype(jnp.int32)
    tile_eb = (grp * EPG + pair_hi).astype(jnp.int32)
    empty = jnp.zeros((N_SUB - 1, N_CLASSES), jnp.int32)
    seg_off = jnp.concatenate([seg_off.astype(jnp.int32), empty]).reshape(-1)
    seg_len = jnp.concatenate([seg_len, empty]).reshape(-1)

    nact1 = nact.reshape(1).astype(jnp.int32)
    zero_rows = (jnp.sum(pad_len) + (n_max - nact) * tm_e).reshape(1).astype(jnp.int32)
    x_sorted = _dispatch(seg_off, seg_len, pad_off.astype(jnp.int32), pad_len.astype(jnp.int32), zero_rows,
                         nact1, h2_p, h2_s, info_p, info_s, tm=tm_d, n_sub=N_SUB, n_slots=n_max * tm_e)
    ya_sorted, yb_sorted = _moe(tile_ea, tile_eb, nact1, x_sorted, wg_b, wu_b, wd_b, tm=tm_e, n_max=n_max)

    y_p = _final(seg_off, seg_len, x1_p, info_p, mod_p, g_final, ya_sorted, yb_sorted,
                 tm=tm_d, n_sub=N_SUB, sample=False, tile_base=0)
    y_s = _final(seg_off, seg_len, x1_s, info_s, mod_s, g_final, ya_sorted, yb_sorted,
                 tm=tm_d, n_sub=1, sample=True, tile_base=n_p // tm_d)

    return (y_p.reshape(batch, seq, D_MODEL),
            y_s.reshape(dec_b, dec_t, D_MODEL),
            kf_p.reshape(batch, 1, seq, N_HEADS, HEAD_W),
            vf_p.reshape(batch, 1, seq, N_HEADS, HEAD_W),
            kf_s.reshape(dec_b, 1, dec_t, N_HEADS, HEAD_W),
            vf_s.reshape(dec_b, 1, dec_t, N_HEADS, HEAD_W),
            vsn_s.reshape(dec_b, 1, dec_t, N_GROUPS_SG, SG_CH))
```

```python
import functools
import math

import numpy as np
import jax
import jax.numpy as jnp
from jax import lax
from jax.experimental import pallas as pl
from jax.experimental.pallas import tpu as pltpu

F32 = jnp.float32
BF16 = jnp.bfloat16

D_MODEL = 1024
N_HEADS = 4
DK = 64
HEAD_W = 128
QK_W = N_HEADS * HEAD_W
N_GROUPS_SG = 4
SG_CH = 128
IN_W = 2560
CHUNK = 128
PAGE = 128
N_BUCKETS = 32
MAX_DISTANCE = 128
N_EG = 4
EPG = 4
N_PAIRS = 6
N_CLASSES = N_EG * N_PAIRS
D_EXPERT = 512
EPS = 1e-6
LAM_INIT = 0.8 - 0.6 * math.exp(-0.3 * 0)
NEG = -1e30
LOG2E = math.log2(math.e)
LANES = 128
SUBLANES = 8

TM_TOK = 512
STAGE_A_PARTS = 4
TQ_ATT = 2048
PAGES_PER_STEP = 16
TM_E = 256
TM_D = 256
N_SUB = 2
VMEM_LIMIT = 56 * 1024 * 1024


def _cparams(sem):
    return pltpu.CompilerParams(dimension_semantics=sem, vmem_limit_bytes=VMEM_LIMIT)


def _ada_body(c_ref, w_ref, b_ref, o_ref):
    c = c_ref[...]
    a = (c * jax.nn.sigmoid(c)).astype(BF16)
    o_ref[...] = jnp.dot(a, w_ref[...].astype(BF16), preferred_element_type=F32) + b_ref[...]


def _ada(c_all, w_ada, b_ada):
    m = c_all.shape[0]
    n = w_ada.shape[1]
    tn = 1536
    return pl.pallas_call(
        _ada_body,
        grid=(n // tn,),
        in_specs=[pl.BlockSpec((m, D_MODEL), lambda j: (0, 0)),
                  pl.BlockSpec((D_MODEL, tn), lambda j: (0, j)),
                  pl.BlockSpec((1, tn), lambda j: (0, j))],
        out_specs=pl.BlockSpec((m, tn), lambda j: (0, j)),
        out_shape=jax.ShapeDtypeStruct((m, n), F32),
        compiler_params=_cparams(("arbitrary",)),
        name="adaln",
    )(c_all, w_ada, b_ada.reshape(1, n))


def _stage_a_body(x_ref, mod_ref, w_in_ref, ws_ref, bs_ref, gln_ref, bln_ref,
                  *out_refs, chunk, sample):
    if sample:
        q_ref, kf_ref, vf_ref, sg_ref, vsn_ref = out_refs
    else:
        q_ref, kf_ref, kb_ref, vf_ref, vt_ref, sg_ref = out_refs
    x = x_ref[...]
    tm = x.shape[0]
    mod = mod_ref[0]
    sh1 = mod[:, 0:D_MODEL]
    sc1 = mod[:, D_MODEL:2 * D_MODEL]
    ms = jnp.mean(x * x, axis=-1, keepdims=True)
    h = (x * lax.rsqrt(ms + EPS)) * (1.0 + sc1) + sh1
    hb = h.astype(BF16)
    n_parts = 1 if sample else STAGE_A_PARTS
    rows_p = tm // n_parts
    zs = [jnp.dot(hb[p * rows_p:(p + 1) * rows_p], w_in_ref[...], preferred_element_type=F32)
          for p in range(n_parts)]
    for p, z in enumerate(zs):
        a = p * rows_p
        q = z[:, 0:QK_W] * (DK ** -0.5)
        k = z[:, QK_W:2 * QK_W]
        v = z[:, 2 * QK_W:3 * QK_W]
        for hd in range(N_HEADS):
            kf_ref[pl.ds(a * N_HEADS + hd, rows_p, stride=N_HEADS), :] = k[:, hd * HEAD_W:(hd + 1) * HEAD_W]
            vf_ref[pl.ds(a * N_HEADS + hd, rows_p, stride=N_HEADS), :] = v[:, hd * HEAD_W:(hd + 1) * HEAD_W]
        if sample:
            q_ref[...] = q
        else:
            q_ref[a:a + rows_p, :] = (q * LOG2E).astype(BF16)
            kb_ref[a:a + rows_p, :] = k.astype(BF16)
            vt_ref[0, :, a:a + rows_p] = v.T.astype(BF16)
        u = z[:, 3 * QK_W:4 * QK_W]
        vs = z[:, 4 * QK_W:5 * QK_W]
        for g in range(N_GROUPS_SG):
            lo, hi = g * SG_CH, (g + 1) * SG_CH
            vg = vs[:, lo:hi]
            mu = jnp.mean(vg, axis=-1, keepdims=True)
            dv = vg - mu
            var = jnp.mean(dv * dv, axis=-1, keepdims=True)
            vn = (dv * lax.rsqrt(var + EPS)) * gln_ref[g:g + 1, :] + bln_ref[g:g + 1, :]
            if sample:
                vsn_ref[:, lo:hi] = vn
            vnb = vn.astype(BF16)
            for c in range(rows_p // chunk):
                r0, r1 = c * chunk, (c + 1) * chunk
                s = jnp.dot(ws_ref[g], vnb[r0:r1], preferred_element_type=F32) + bs_ref[g]
                sg_ref[a + r0:a + r1, lo:hi] = (u[r0:r1, lo:hi] * s).astype(BF16)


def _stage_a(x, mod, w_in_b, ws, bs, gln, bln, *, tm, chunk, sample):
    n = x.shape[0]
    nt = n // tm
    mrows = mod.shape[1]
    row = lambda w, dt: jax.ShapeDtypeStruct((n, w), dt)
    blk = lambda w: pl.BlockSpec((tm, w), lambda i: (i, 0))
    cache_shape = jax.ShapeDtypeStruct((n * N_HEADS, HEAD_W), F32)
    cache_blk = pl.BlockSpec((tm * N_HEADS, HEAD_W), lambda i: (i, 0))
    if sample:
        out_shape = [row(QK_W, F32), cache_shape, cache_shape, row(QK_W, BF16), row(QK_W, F32)]
        out_specs = [blk(QK_W), cache_blk, cache_blk, blk(QK_W), blk(QK_W)]
        mod_map = lambda i: (0, 0, 0)
    else:
        out_shape = [row(QK_W, BF16), cache_shape, row(QK_W, BF16), cache_shape,
                     jax.ShapeDtypeStruct((nt, QK_W, tm), BF16), row(QK_W, BF16)]
        out_specs = [blk(QK_W), cache_blk, blk(QK_W), cache_blk,
                     pl.BlockSpec((1, QK_W, tm), lambda i: (i, 0, 0)), blk(QK_W)]
        tiles_per_batch = 4096 // tm
        mod_map = lambda i: (i // tiles_per_batch, 0, 0)
    full = lambda a: pl.BlockSpec(a.shape, lambda i: (0,) * a.ndim)
    return pl.pallas_call(
        functools.partial(_stage_a_body, chunk=chunk, sample=sample),
        grid=(nt,),
        in_specs=[blk(D_MODEL),
                  pl.BlockSpec((1, mrows, 6 * D_MODEL), mod_map),
                  full(w_in_b), full(ws), full(bs), full(gln), full(bln)],
        out_specs=out_specs,
        out_shape=out_shape,
        compiler_params=_cparams(("arbitrary",)),
        name="stage_a_sample" if sample else "stage_a_prompt",
    )(x, mod, w_in_b, ws, bs, gln, bln)


ATT_COLS = 256
BIAS_BLOCK = MAX_DISTANCE


def _attn_body(lam_ref, q_ref, k_ref, vt_ref, bias_ref, g_ref, o_ref, *scratch, tq, tk):
    n_chain = 2 * tq // ATT_COLS
    q2_refs, m_refs, l_refs, acc_refs = (scratch[i * n_chain:(i + 1) * n_chain] for i in range(4))
    qi = pl.program_id(2)
    for c in range(n_chain):
        q0 = (c * ATT_COLS) % tq
        q = q_ref[q0:q0 + ATT_COLS, :]
        lane = lax.broadcasted_iota(jnp.int32, q.shape, 1)
        keep = (lane < DK) if c < n_chain // 2 else (lane >= DK)
        q2_refs[c][...] = jnp.where(keep, q, jnp.zeros_like(q))
        m_refs[c][...] = jnp.full(m_refs[c].shape, NEG, F32)
        l_refs[c][...] = jnp.zeros(l_refs[c].shape, F32)
        acc_refs[c][...] = jnp.zeros(acc_refs[c].shape, F32)

    blk = BIAS_BLOCK

    kblocks = tk // blk

    def block_kinds(rel, q0):
        return [[(q0 // blk + b) - (rel + a) for b in range(ATT_COLS // blk)] for a in range(kblocks)]

    def keys_needed(rel, q0):
        if rel is None:
            return kblocks
        return sum(1 for row in block_kinds(rel, q0) if max(row) >= 0)

    def with_bias(s, rel, q0):
        if rel is None:
            return s
        kinds = block_kinds(rel, q0)[:s.shape[0] // blk]
        if all(d >= 2 for row in kinds for d in row):
            return s
        nxt, near = bias_ref[0, 0], bias_ref[0, 1]
        pick = lambda d: (jnp.full((blk, blk), NEG, F32) if d < 0 else near if d == 0 else nxt if d == 1
                          else jnp.zeros((blk, blk), F32))
        rows = []
        for a, row in enumerate(kinds):
            s_row = s[a * blk:(a + 1) * blk]
            if any(d < 2 for d in row):
                s_row = s_row + jnp.concatenate([pick(d) for d in row], axis=1)
            rows.append(s_row)
        return jnp.concatenate(rows, axis=0)

    def tiles(*work):
        scores = []
        for j, rel in work:
            for c in range(n_chain):
                nk = keys_needed(rel, (c * ATT_COLS) % tq) * blk
                if nk == 0:
                    scores.append(None)
                    continue
                k = k_ref[pl.ds(pl.multiple_of(j * tk, tk), nk), :]
                s = lax.dot_general(k, q2_refs[c][...], (((1,), (1,)), ((), ())),
                                    preferred_element_type=F32)
                scores.append(with_bias(s, rel, (c * ATT_COLS) % tq))
        for w, (j, _) in enumerate(work):
            for c in range(n_chain):
                s = scores[w * n_chain + c]
                if s is None:
                    continue
                vt = vt_ref[j, :, 0:s.shape[0]]
                m_old = m_refs[c][...]
                m_new = jnp.maximum(m_old, jnp.max(s, axis=0, keepdims=True))
                alpha = jnp.exp2(m_old - m_new)
                p = jnp.exp2(s - m_new)
                l_refs[c][...] = alpha * l_refs[c][...] + jnp.sum(p, axis=0, keepdims=True)
                acc_refs[c][...] = alpha * acc_refs[c][...] + jnp.dot(vt, p.astype(BF16),
                                                                      preferred_element_type=F32)
                m_refs[c][...] = m_new

    ratio = tq // tk
    first_diag = qi * ratio
    n_plain = jnp.maximum(first_diag - 1, 0)

    def plain_pair(jj, carry):
        tiles((2 * jj, None), (2 * jj + 1, None))
        return carry

    lax.fori_loop(0, n_plain // 2, plain_pair, 0)

    @pl.when(qi >= 1)
    def _():
        if ratio % 2 == 0:
            tiles((first_diag - 2, None), (first_diag - 1, -kblocks))
        else:
            @pl.when(n_plain % 2 == 1)
            def _():
                tiles((n_plain - 1, None))
            tiles((first_diag - 1, -kblocks))

    tiles(*[(first_diag + r, r * kblocks) for r in range(ratio)])

    lam = lam_ref[0]
    o_all = jnp.concatenate([acc_refs[c][...] * (1.0 / l_refs[c][...]) for c in range(n_chain)],
                            axis=1)
    o = o_all[:, 0:tq] - lam * o_all[:, tq:2 * tq]
    ms = jnp.mean(o * o, axis=0, keepdims=True)
    on = (o * lax.rsqrt(ms + EPS)) * g_ref[...] * (1.0 - LAM_INIT)
    o_ref[...] = on.T.astype(BF16)


def _attn_prompt(lam, q, kb, vt, bias_t, g_col, *, batch, seq, tq, tk):
    nq = seq // tq
    nk = seq // tk
    n = batch * seq
    n_chain = 2 * tq // ATT_COLS
    assert vt.shape == (batch * nk, QK_W, tk) and tq % tk == 0 and tk % BIAS_BLOCK == 0
    return pl.pallas_call(
        functools.partial(_attn_body, tq=tq, tk=tk),
        grid=(batch, N_HEADS, nq),
        in_specs=[pl.BlockSpec(memory_space=pltpu.SMEM),
                  pl.BlockSpec((tq, HEAD_W), lambda b, h, i: (b * nq + i, h)),
                  pl.BlockSpec((seq, HEAD_W), lambda b, h, i: (b, h)),
                  pl.BlockSpec((nk, HEAD_W, tk), lambda b, h, i: (b, h, 0)),
                  pl.BlockSpec((1, 2, BIAS_BLOCK, BIAS_BLOCK), lambda b, h, i: (h, 0, 0, 0)),
                  pl.BlockSpec((HEAD_W, 1), lambda b, h, i: (0, 0))],
        out_specs=pl.BlockSpec((tq, HEAD_W), lambda b, h, i: (b * nq + i, h)),
        out_shape=jax.ShapeDtypeStruct((n, QK_W), BF16),
        scratch_shapes=([pltpu.VMEM((ATT_COLS, HEAD_W), BF16)] * n_chain
                        + [pltpu.VMEM((1, ATT_COLS), F32)] * (2 * n_chain)
                        + [pltpu.VMEM((HEAD_W, ATT_COLS), F32)] * n_chain),
        compiler_params=_cparams(("arbitrary", "arbitrary", "arbitrary")),
        name="attn_prompt",
    )(lam, q, kb, vt, bias_t, g_col)


PAGE_ROWS = PAGE * N_HEADS
TAIL_TOKENS = 2 * PAGE


def _sattn_body(pt_ref, lam_ref, q_ref, knew_ref, vnew_ref, bl_ref, bn_ref, g_ref, ck_ref, cv_ref,
                o_ref, kbuf, vbuf, sem, mask_ref, m_ref, l_ref, acc_ref, *, pages, n_steps, total):
    b = pl.program_id(0)
    s = pl.program_id(1)
    step = b * n_steps + s
    slot = step % 2

    def page_copies(step_idx, sl):
        base = step_idx * pages
        out = []
        for i in range(pages):
            src = pl.ds(pl.multiple_of(pt_ref[base + i] * PAGE_ROWS, PAGE_ROWS), PAGE_ROWS)
            dst = pl.ds(i * PAGE_ROWS, PAGE_ROWS)
            out.append(pltpu.make_async_copy(ck_ref.at[src], kbuf.at[sl, dst], sem.at[sl, 0]))
            out.append(pltpu.make_async_copy(cv_ref.at[src], vbuf.at[sl, dst], sem.at[sl, 1]))
        return out

    def start_all(copies):
        for n, c in enumerate(copies):
            c.start(priority=n % 2)

    @pl.when(step == 0)
    def _():
        start_all(page_copies(0, 0))

    @pl.when(step + 1 < total)
    def _():
        start_all(page_copies(step + 1, 1 - slot))

    for c in page_copies(step, slot):
        c.wait()

    @pl.when(s == 0)
    def _():
        m_ref[...] = jnp.full(m_ref.shape, NEG, F32)
        l_ref[...] = jnp.zeros(l_ref.shape, F32)
        acc_ref[...] = jnp.zeros(acc_ref.shape, F32)

    @pl.when(step == 0)
    def _():
        row = lax.broadcasted_iota(jnp.int32, mask_ref.shape, 0)
        col = lax.broadcasted_iota(jnp.int32, mask_ref.shape, 1)
        same_head = (col % N_HEADS) == (row // (2 * SUBLANES))
        mask_ref[...] = jnp.where(same_head, 0.0, NEG)

    q = q_ref[...]
    lane = lax.broadcasted_iota(jnp.int32, (SUBLANES, HEAD_W), 1)
    pieces = []
    for h in range(N_HEADS):
        qh = q[:, h * HEAD_W:(h + 1) * HEAD_W]
        pieces += [jnp.where(lane < DK, qh, 0.0), jnp.where(lane >= DK, qh, 0.0)]
    qm = jnp.concatenate(pieces, axis=0).astype(BF16)

    def update(kb, vb, bias):
        sc = lax.dot_general(qm, kb, (((1,), (1,)), ((), ())),
                             preferred_element_type=F32) + bias
        m_old = m_ref[...]
        m_new = jnp.maximum(m_old, jnp.max(sc, axis=1, keepdims=True))
        alpha = jnp.exp(m_old - m_new)
        p = jnp.exp(sc - m_new)
        l_ref[...] = alpha * l_ref[...] + jnp.sum(p, axis=1, keepdims=True)
        acc_ref[...] = alpha * acc_ref[...] + jnp.dot(p.astype(BF16), vb,
                                                      preferred_element_type=F32)
        m_ref[...] = m_new

    is_last = s == n_steps - 1
    head_cols = mask_ref.shape[1] - bl_ref.shape[1]
    update(kbuf[slot].astype(BF16), vbuf[slot].astype(BF16),
           jnp.concatenate([mask_ref[:, :head_cols],
                            mask_ref[:, head_cols:] + bl_ref[...] * is_last.astype(F32)], axis=1))

    @pl.when(is_last)
    def _():
        update(knew_ref[0], vnew_ref[0], bn_ref[...])
        lam = lam_ref[0]
        o_all = acc_ref[...] * (1.0 / l_ref[...])
        for h in range(N_HEADS):
            r = h * 2 * SUBLANES
            o = o_all[r:r + SUBLANES] - lam * o_all[r + SUBLANES:r + 2 * SUBLANES]
            ms = jnp.mean(o * o, axis=-1, keepdims=True)
            o_ref[:, h * HEAD_W:(h + 1) * HEAD_W] = ((o * lax.rsqrt(ms + EPS)) * g_ref[...]
                                                     * (1.0 - LAM_INIT))


def _attn_sample(page_table_flat, lam, q_s, knew, vnew, bias_last, bias_new, g_row, cache_k, cache_v,
                 *, dec_b, n_pages):
    pages = PAGES_PER_STEP
    n_steps = n_pages // pages
    total = dec_b * n_steps
    nq = q_s.shape[0] // dec_b
    n_rows = 2 * N_HEADS * nq
    step_rows = pages * PAGE_ROWS
    grid_spec = pltpu.PrefetchScalarGridSpec(
        num_scalar_prefetch=1,
        grid=(dec_b, n_steps),
        in_specs=[pl.BlockSpec(memory_space=pltpu.SMEM),
                  pl.BlockSpec((nq, QK_W), lambda b, s, pt: (b, 0)),
                  pl.BlockSpec((1, PAGE, HEAD_W), lambda b, s, pt: (b, 0, 0)),
                  pl.BlockSpec((1, PAGE, HEAD_W), lambda b, s, pt: (b, 0, 0)),
                  pl.BlockSpec(bias_last.shape, lambda b, s, pt: (0, 0)),
                  pl.BlockSpec(bias_new.shape, lambda b, s, pt: (0, 0)),
                  pl.BlockSpec((1, HEAD_W), lambda b, s, pt: (0, 0)),
                  pl.BlockSpec(memory_space=pl.ANY),
                  pl.BlockSpec(memory_space=pl.ANY)],
        out_specs=pl.BlockSpec((nq, QK_W), lambda b, s, pt: (b, 0)),
        scratch_shapes=[pltpu.VMEM((2, step_rows, HEAD_W), F32),
                        pltpu.VMEM((2, step_rows, HEAD_W), F32),
                        pltpu.SemaphoreType.DMA((2, 2)),
                        pltpu.VMEM((n_rows, step_rows), F32),
                        pltpu.VMEM((n_rows, 1), F32), pltpu.VMEM((n_rows, 1), F32),
                        pltpu.VMEM((n_rows, HEAD_W), F32)])
    return pl.pallas_call(
        functools.partial(_sattn_body, pages=pages, n_steps=n_steps, total=total),
        grid_spec=grid_spec,
        out_shape=jax.ShapeDtypeStruct(q_s.shape, F32),
        compiler_params=_cparams(("arbitrary", "arbitrary")),
        name="attn_sample",
    )(page_table_flat, lam, q_s, knew, vnew, bias_last, bias_new, g_row, cache_k, cache_v)


INFO_W_A, INFO_W_B, INFO_CLS = 0, 1, 2
PAIR_A = (0, 0, 0, 1, 1, 3)
PAIR_B = (1, 2, 3, 3, 2, 2)


def _stage_c_body(o_ref, sg_ref, x_ref, mod_ref, wo_ref, wr_ref, br_ref, x1_ref, h2_ref, cls_ref, info_ref):
    x = x_ref[...]
    tm = x.shape[0]
    mod = mod_ref[0]
    g1 = mod[:, 2 * D_MODEL:3 * D_MODEL]
    sh2 = mod[:, 3 * D_MODEL:4 * D_MODEL]
    sc2 = mod[:, 4 * D_MODEL:5 * D_MODEL]
    mix = (jnp.dot(o_ref[...].astype(BF16), wo_ref[0:QK_W, :], preferred_element_type=F32)
           + jnp.dot(sg_ref[...], wo_ref[QK_W:2 * QK_W, :], preferred_element_type=F32))
    x1 = x + g1 * mix
    x1_ref[...] = x1
    ms = jnp.mean(x1 * x1, axis=-1, keepdims=True)
    h2 = ((x1 * lax.rsqrt(ms + EPS)) * (1.0 + sc2) + sh2).astype(BF16)
    h2_ref[...] = h2
    lg = lax.dot_general(wr_ref[...], h2, (((1,), (1,)), ((), ())),
                         preferred_element_type=F32) + br_ref[...]
    gl = [lg[i:i + 1, :] for i in range(N_EG)]
    el = [lg[N_EG + i:N_EG + i + 1, :] for i in range(N_EG * EPG)]
    gmax = jnp.maximum(jnp.maximum(gl[0], gl[1]), jnp.maximum(gl[2], gl[3]))
    gi = jnp.where(gl[0] == gmax, 0, jnp.where(gl[1] == gmax, 1, jnp.where(gl[2] == gmax, 2, 3)))
    gsum = (jnp.exp(gl[0] - gmax) + jnp.exp(gl[1] - gmax)
            + jnp.exp(gl[2] - gmax) + jnp.exp(gl[3] - gmax))
    gp = 1.0 / gsum
    sel = [jnp.where(gi == 0, el[j], jnp.where(gi == 1, el[EPG + j],
                                               jnp.where(gi == 2, el[2 * EPG + j], el[3 * EPG + j])))
           for j in range(EPG)]
    v0 = jnp.maximum(jnp.maximum(sel[0], sel[1]), jnp.maximum(sel[2], sel[3]))
    i0 = jnp.where(sel[0] == v0, 0, jnp.where(sel[1] == v0, 1, jnp.where(sel[2] == v0, 2, 3)))
    rest = [jnp.where(i0 == j, -3e38, sel[j]) for j in range(EPG)]
    v1 = jnp.maximum(jnp.maximum(rest[0], rest[1]), jnp.maximum(rest[2], rest[3]))
    i1 = jnp.where(rest[0] == v1, 0, jnp.where(rest[1] == v1, 1, jnp.where(rest[2] == v1, 2, 3)))
    e1 = jnp.exp(v1 - v0)
    den = 1.0 / (1.0 + e1)
    tw0 = den * gp
    tw1 = e1 * den * gp
    first_low = i0 < i1
    lo = jnp.where(first_low, i0, i1)
    hi = jnp.where(first_low, i1, i0)
    w_lo = jnp.where(first_low, tw0, tw1)
    w_hi = jnp.where(first_low, tw1, tw0)
    pair = jnp.where(lo == 0, hi - 1, jnp.where(lo == 2, 5, jnp.where(hi == 3, 3, 4)))
    swapped = pair == 5
    w_a = jnp.where(swapped, w_hi, w_lo)
    w_b = jnp.where(swapped, w_lo, w_hi)
    cls = gi * N_PAIRS + pair
    cls_ref[...] = jnp.broadcast_to(cls, cls_ref.shape).astype(jnp.int32)
    row = lax.broadcasted_iota(jnp.int32, (LANES, tm), 0)
    rec = jnp.where(row == INFO_W_A, w_a, jnp.where(row == INFO_W_B, w_b,
                                                    jnp.where(row == INFO_CLS, cls.astype(F32), 0.0)))
    info_ref[...] = rec.T


def _stage_c(o, sg, x, mod, wo_b, wr_t, br, *, tm, sample):
    n = x.shape[0]
    nt = n // tm
    mrows = mod.shape[1]
    blk = lambda w: pl.BlockSpec((tm, w), lambda i: (i, 0))
    full = lambda a: pl.BlockSpec(a.shape, lambda i: (0,) * a.ndim)
    if sample:
        mod_map = lambda i: (0, 0, 0)
    else:
        tiles_per_batch = 4096 // tm
        mod_map = lambda i: (i // tiles_per_batch, 0, 0)
    return pl.pallas_call(
        _stage_c_body,
        grid=(nt,),
        in_specs=[blk(QK_W), blk(QK_W), blk(D_MODEL),
                  pl.BlockSpec((1, mrows, 6 * D_MODEL), mod_map),
                  full(wo_b), full(wr_t), full(br)],
        out_specs=[blk(D_MODEL), blk(D_MODEL),
                   pl.BlockSpec((SUBLANES, tm), lambda i: (0, i)),
                   blk(LANES)],
        out_shape=[jax.ShapeDtypeStruct((n, D_MODEL), F32),
                   jax.ShapeDtypeStruct((n, D_MODEL), BF16),
                   jax.ShapeDtypeStruct((SUBLANES, n), jnp.int32),
                   jax.ShapeDtypeStruct((n, LANES), F32)],
        compiler_params=_cparams(("arbitrary",)),
        name="stage_c_sample" if sample else "stage_c_prompt",
    )(o, sg, x, mod, wo_b, wr_t, br)


def rows8(ref, start, count):
    scale = lambda v: v * SUBLANES if isinstance(v, int) else pl.multiple_of(v * SUBLANES, SUBLANES)
    return ref.at[pl.ds(scale(start), scale(count))]


def _perm_t(cls_col):
    n = cls_col.shape[0]
    lane = lax.broadcasted_iota(jnp.int32, (n, LANES), 1).astype(F32)
    onehot = (lane == cls_col).astype(BF16)
    r = lax.broadcasted_iota(jnp.int32, (n, n), 0)
    c = lax.broadcasted_iota(jnp.int32, (n, n), 1)
    before = (c < r).astype(BF16)
    rank = jnp.dot(before, onehot, preferred_element_type=F32)
    cnt = jnp.sum(onehot.astype(F32), axis=0, keepdims=True)
    cr = lax.broadcasted_iota(jnp.int32, (LANES, LANES), 0)
    cc = lax.broadcasted_iota(jnp.int32, (LANES, LANES), 1)
    lower_cls = (cr < cc).astype(BF16)
    base = jnp.dot(jnp.broadcast_to(cnt, (SUBLANES, LANES)).astype(BF16), lower_cls,
                   preferred_element_type=F32)[0:1, :]
    pos = jnp.sum(onehot.astype(F32) * (base + rank), axis=1, keepdims=True)
    dest = lax.broadcasted_iota(jnp.int32, (n, n), 1).astype(F32)
    return (dest == pos).astype(F32)


def _dispatch_body(soff_ref, slen_ref, poff_ref, plen_ref, ptot_ref, nact_ref, hp_ref, hs_ref, ip_ref, is_ref,
                   xs_ref, buf, zbuf, sem, zsem, *, tm, n_sub, n_tiles, n_slab_tiles):
    i = pl.program_id(0)
    slot = i % 2
    is_sample = i == n_tiles - 1

    def wait_tile(sl, tokens):
        pltpu.make_async_copy(rows8(buf.at[sl], 0, tokens), rows8(buf.at[sl], 0, tokens), sem.at[sl]).wait()

    @pl.when(i == 0)
    def _():
        zbuf[...] = jnp.zeros(zbuf.shape, F32)
        for c in range(N_CLASSES):
            @pl.when(plen_ref[c] > 0)
            def _():
                pltpu.make_async_copy(rows8(zbuf, 0, plen_ref[c]), rows8(xs_ref, poff_ref[c], plen_ref[c]),
                                      zsem).start()
        for j in range(n_slab_tiles - N_CLASSES, n_slab_tiles):
            @pl.when(j >= nact_ref[0])
            def _():
                pltpu.make_async_copy(zbuf, rows8(xs_ref, j * TM_E, TM_E), zsem).start()

    xp = []
    for u in range(n_sub):
        rows = slice(u * tm, (u + 1) * tm)
        x = hp_ref[rows, :]
        info = ip_ref[rows, :]
        if u == 0:
            x = jnp.where(is_sample, hs_ref[...], x)
            info = jnp.where(is_sample, is_ref[...], info)
        perm = _perm_t(info[:, INFO_CLS:INFO_CLS + 1]).T.astype(BF16)
        xp.append(jnp.dot(perm, x, preferred_element_type=F32))

    @pl.when(i >= 2)
    def _():
        wait_tile(slot, n_sub * tm)

    bs = buf.at[slot]
    for u in range(n_sub):
        for c in range(D_MODEL // LANES):
            bs[pl.ds(u * tm * SUBLANES + c, tm, stride=SUBLANES), :] = xp[u][:, c * LANES:(c + 1) * LANES]
    for u in range(n_sub):
        local = u * tm
        for c in range(N_CLASSES):
            k = (i * n_sub + u) * N_CLASSES + c
            n_rows = slen_ref[k]

            @pl.when(n_rows > 0)
            def _():
                pltpu.make_async_copy(rows8(bs, local, n_rows), rows8(xs_ref, soff_ref[k], n_rows),
                                      sem.at[slot]).start()
            local = local + n_rows

    @pl.when(i == n_tiles - 1)
    def _():
        wait_tile(slot, tm)
        if n_tiles >= 2:
            wait_tile(1 - slot, n_sub * tm)

        @pl.when(ptot_ref[0] > 0)
        def _():
            n = pl.multiple_of(ptot_ref[0] * SUBLANES, SUBLANES)
            pltpu.make_async_copy(xs_ref.at[pl.ds(0, n)], xs_ref.at[pl.ds(0, n)], zsem).wait()


def _dispatch(seg_off, seg_len, pad_off, pad_len, pad_tot, nact, h2_p, h2_s, info_p, info_s, *, tm, n_sub, n_slots):
    n_prompt_steps = h2_p.shape[0] // (n_sub * tm)
    n_tiles = n_prompt_steps + 1
    assert h2_s.shape[0] == tm and h2_p.shape[0] % (n_sub * tm) == 0
    assert seg_len.shape[0] == n_tiles * n_sub * N_CLASSES
    last_p = n_prompt_steps - 1
    grid_spec = pltpu.PrefetchScalarGridSpec(
        num_scalar_prefetch=6,
        grid=(n_tiles,),
        in_specs=[pl.BlockSpec((n_sub * tm, D_MODEL), lambda i, *_: (jnp.minimum(i, last_p), 0)),
                  pl.BlockSpec((tm, D_MODEL), lambda i, *_: (0, 0)),
                  pl.BlockSpec((n_sub * tm, LANES), lambda i, *_: (jnp.minimum(i, last_p), 0)),
                  pl.BlockSpec((tm, LANES), lambda i, *_: (0, 0))],
        out_specs=pl.BlockSpec(memory_space=pl.ANY),
        scratch_shapes=[pltpu.VMEM((2, n_sub * tm * SUBLANES, LANES), F32),
                        pltpu.VMEM((TM_E * SUBLANES, LANES), F32),
                        pltpu.SemaphoreType.DMA((2,)),
                        pltpu.SemaphoreType.DMA(())])
    return pl.pallas_call(
        functools.partial(_dispatch_body, tm=tm, n_sub=n_sub, n_tiles=n_tiles, n_slab_tiles=n_slots // TM_E),
        grid_spec=grid_spec,
        out_shape=jax.ShapeDtypeStruct((n_slots * SUBLANES, LANES), F32),
        compiler_params=_cparams(("arbitrary",)),
        name="moe_dispatch",
    )(seg_off, seg_len, pad_off, pad_len, pad_tot, nact, h2_p, h2_s, info_p, info_s)


def _moe_body(ea_ref, eb_ref, nact_ref, x_ref, wga_ref, wgb_ref, wua_ref, wub_ref, wda_ref, wdb_ref,
              ya_ref, yb_ref, *, tm):
    i = pl.program_id(0)

    @pl.when(i < nact_ref[0])
    def _():
        x = jnp.concatenate([x_ref[pl.ds(c, tm, stride=SUBLANES), :] for c in range(D_MODEL // LANES)],
                            axis=1).astype(BF16)

        def expert(wg_ref, wu_ref, wd_ref):
            gate = jnp.dot(x, wg_ref[0].astype(BF16), preferred_element_type=F32)
            up = jnp.dot(x, wu_ref[0].astype(BF16), preferred_element_type=F32)
            he = (gate * jax.nn.sigmoid(gate)) * up
            return jnp.dot(he.astype(BF16), wd_ref[0].astype(BF16), preferred_element_type=F32)

        ya = expert(wga_ref, wua_ref, wda_ref)
        yb = expert(wgb_ref, wub_ref, wdb_ref)
        for c in range(D_MODEL // LANES):
            ya_ref[pl.ds(c, tm, stride=SUBLANES), :] = ya[:, c * LANES:(c + 1) * LANES]
            yb_ref[pl.ds(c, tm, stride=SUBLANES), :] = yb[:, c * LANES:(c + 1) * LANES]

    @pl.when(i >= nact_ref[0])
    def _():
        ya_ref[...] = jnp.zeros(ya_ref.shape, ya_ref.dtype)
        yb_ref[...] = jnp.zeros(yb_ref.shape, yb_ref.dtype)


def _moe(tile_ea, tile_eb, nact, x_sorted, wg_b, wu_b, wd_b, *, tm, n_max):
    wspec_in = lambda sel: pl.BlockSpec((1, D_MODEL, D_EXPERT), sel)
    wspec_out = lambda sel: pl.BlockSpec((1, D_EXPERT, D_MODEL), sel)
    sel_a = lambda i, ea, eb, na: (ea[i], 0, 0)
    sel_b = lambda i, ea, eb, na: (eb[i], 0, 0)
    rows_in = lambda i, ea, eb, na: (jnp.minimum(i, na[0] - 1), 0)
    grid_spec = pltpu.PrefetchScalarGridSpec(
        num_scalar_prefetch=3,
        grid=(n_max,),
        in_specs=[pl.BlockSpec((tm * SUBLANES, LANES), rows_in),
                  wspec_in(sel_a), wspec_in(sel_b), wspec_in(sel_a), wspec_in(sel_b),
                  wspec_out(sel_a), wspec_out(sel_b)],
        out_specs=[pl.BlockSpec((tm * SUBLANES, LANES), lambda i, ea, eb, na: (i, 0))] * 2)
    return pl.pallas_call(
        functools.partial(_moe_body, tm=tm),
        grid_spec=grid_spec,
        out_shape=[jax.ShapeDtypeStruct(x_sorted.shape, F32)] * 2,
        compiler_params=_cparams(("arbitrary",)),
        name="moe",
    )(tile_ea, tile_eb, nact, x_sorted, wg_b, wg_b, wu_b, wu_b, wd_b, wd_b)


def _final_body(soff_ref, slen_ref, x1_ref, info_ref, mod_ref, gf_ref, ya_ref, yb_ref, y_ref, buf, sem,
                *, tm, n_sub, n_tiles, tile_base):
    i = pl.program_id(0)
    slot = i % 2
    slabs = (ya_ref, yb_ref)

    def fetch(step, sl):
        for u in range(n_sub):
            local = u * tm
            for c in range(N_CLASSES):
                k = (step * n_sub + u + tile_base) * N_CLASSES + c
                n_rows = slen_ref[k]

                @pl.when(n_rows > 0)
                def _():
                    for e in range(2):
                        pltpu.make_async_copy(rows8(slabs[e], soff_ref[k], n_rows),
                                              rows8(buf.at[sl, e], local, n_rows), sem.at[sl, e]).start()
                local = local + n_rows

    @pl.when(i == 0)
    def _():
        fetch(0, 0)

    @pl.when(i + 1 < n_tiles)
    def _():
        fetch(i + 1, 1 - slot)

    info = info_ref[...]
    perm_t = [_perm_t(info[u * tm:(u + 1) * tm, INFO_CLS:INFO_CLS + 1]).astype(BF16) for u in range(n_sub)]
    moe = None
    for e, lane_w in enumerate((INFO_W_A, INFO_W_B)):
        pltpu.make_async_copy(buf.at[slot, e], buf.at[slot, e], sem.at[slot, e]).wait()
        bs = buf.at[slot, e]
        parts = []
        for u in range(n_sub):
            ye = jnp.concatenate([bs[pl.ds(u * tm * SUBLANES + c, tm, stride=SUBLANES), :]
                                  for c in range(D_MODEL // LANES)], axis=1)
            parts.append(jnp.dot(perm_t[u], ye.astype(BF16), preferred_element_type=F32))
        term = info[:, lane_w:lane_w + 1] * jnp.concatenate(parts, axis=0)
        moe = term if moe is None else moe + term
    x1 = x1_ref[...]
    g2 = mod_ref[0][:, 5 * D_MODEL:6 * D_MODEL]
    x2 = x1 + g2 * moe
    ms = jnp.mean(x2 * x2, axis=-1, keepdims=True)
    y_ref[...] = (x2 * lax.rsqrt(ms + EPS)) * gf_ref[...]


def _final(seg_off, seg_len, x1, info, mod, g_final, ya_sorted, yb_sorted, *, tm, n_sub, sample, tile_base):
    n = x1.shape[0]
    rows = n_sub * tm
    nt = n // rows
    mrows = mod.shape[1]
    if sample:
        mod_map = lambda i, *_: (0, 0, 0)
    else:
        tiles_per_batch = 4096 // rows
        mod_map = lambda i, *_: (i // tiles_per_batch, 0, 0)
    grid_spec = pltpu.PrefetchScalarGridSpec(
        num_scalar_prefetch=2,
        grid=(nt,),
        in_specs=[pl.BlockSpec((rows, D_MODEL), lambda i, *_: (i, 0)),
                  pl.BlockSpec((rows, LANES), lambda i, *_: (i, 0)),
                  pl.BlockSpec((1, mrows, 6 * D_MODEL), mod_map),
                  pl.BlockSpec((1, D_MODEL), lambda i, *_: (0, 0)),
                  pl.BlockSpec(memory_space=pl.ANY),
                  pl.BlockSpec(memory_space=pl.ANY)],
        out_specs=pl.BlockSpec((rows, D_MODEL), lambda i, *_: (i, 0)),
        scratch_shapes=[pltpu.VMEM((2, 2, rows * SUBLANES, LANES), F32),
                        pltpu.SemaphoreType.DMA((2, 2))])
    return pl.pallas_call(
        functools.partial(_final_body, tm=tm, n_sub=n_sub, n_tiles=nt, tile_base=tile_base),
        grid_spec=grid_spec,
        out_shape=jax.ShapeDtypeStruct((n, D_MODEL), F32),
        compiler_params=_cparams(("arbitrary",)),
        name="final_sample" if sample else "final_prompt",
    )(seg_off, seg_len, x1, info, mod, g_final.reshape(1, D_MODEL), ya_sorted, yb_sorted)


def _bucket_table(n):
    d = np.arange(n)
    max_exact = N_BUCKETS // 2
    nf = np.maximum(d, 1).astype(np.float64)
    large = max_exact + (np.log(nf / max_exact) / math.log(MAX_DISTANCE / max_exact)
                         * (N_BUCKETS - max_exact)).astype(np.int64)
    large = np.minimum(large, N_BUCKETS - 1)
    return np.where(d < max_exact, d, large).astype(np.int32)


def _toeplitz(v, n_rows, n_cols):
    length = n_rows + n_cols - 1
    lead = v.shape[:-1]
    vp = jnp.concatenate([v, jnp.zeros(lead + (1,), v.dtype)], axis=-1)
    skew = jnp.tile(vp, (1,) * len(lead) + (n_rows,))[..., :n_rows * length].reshape(lead + (n_rows, length))
    return skew[..., n_rows - 1:n_rows - 1 + n_cols]


def kernel(x_prompt, x_sample, c_prompt, c_sample, cache_k, cache_v, page_table, w_ada, b_ada, w_in, w_o,
           lam_q1, lam_k1, lam_q2, lam_k2, g_subln, rel_bias, g_sg_ln, b_sg_ln, w_s, b_s, w_rg, b_rg,
           w_re, b_re, w_gate, w_up, w_down, g_final):
    batch, seq, _ = x_prompt.shape
    dec_b, dec_t, _ = x_sample.shape
    n_pages = page_table.shape[1]
    n_p = batch * seq
    n_s = dec_b * dec_t
    n_tot = n_p + n_s
    assert w_in.shape[0] == 1 and cache_k.shape[1] == 1 and seq % TQ_ATT == 0 and n_pages % PAGES_PER_STEP == 0
    assert n_p % TM_TOK == 0 and n_p % n_s == 0 and n_tot % TM_E == 0 and dec_t == SUBLANES
    assert TAIL_TOKENS >= MAX_DISTANCE + dec_t and TAIL_TOKENS <= PAGES_PER_STEP * PAGE
    assert n_s == TM_D and n_p % TM_D == 0 and N_CLASSES <= LANES

    w_in_b = w_in[0].astype(BF16)
    w_o_b = w_o[0].astype(BF16)
    wr_t = jnp.zeros((32, D_MODEL), F32).at[0:N_EG].set(w_rg[0].T).at[N_EG:N_EG + N_EG * EPG].set(w_re[0].T)
    wr_t = wr_t.astype(BF16)
    br = jnp.zeros((32, 1), F32).at[0:N_EG, 0].set(b_rg[0]).at[N_EG:N_EG + N_EG * EPG, 0].set(b_re[0])
    wg_b, wu_b, wd_b = w_gate[0], w_up[0], w_down[0]
    ws_tril = jnp.tril(w_s[0])
    ws_p = ws_tril.astype(BF16)
    bs_p = b_s[0][:, :, None]
    same_seq = np.kron(np.eye(dec_b, dtype=np.float32), np.ones((dec_t, dec_t), np.float32))
    rep = np.tile(np.eye(dec_t, dtype=np.float32), (dec_b, 1))
    ws_rep = jnp.einsum('ri,gij,cj->grc', rep, ws_tril[:, :dec_t, :dec_t], rep,
                        precision=lax.Precision.HIGHEST)
    ws_s = (ws_rep * same_seq).astype(BF16)
    bs_s = jnp.tile(b_s[0][:, :dec_t], (1, dec_b))[:, :, None]
    gln = g_sg_ln[0]
    bln = b_sg_ln[0]
    lam = (jnp.exp(jnp.sum(lam_q1[0] * lam_k1[0])) - jnp.exp(jnp.sum(lam_q2[0] * lam_k2[0]))
           + LAM_INIT).reshape(1).astype(F32)

    blk = BIAS_BLOCK
    n_dist = max(2 * blk, TAIL_TOKENS + dec_t)
    onehot = np.eye(N_BUCKETS, dtype=np.float32)[_bucket_table(n_dist)]
    ft = jnp.dot(onehot, rel_bias - rel_bias[N_BUCKETS - 1], precision=lax.Precision.HIGHEST).T
    neg = lambda n: jnp.full((N_HEADS, n), NEG, F32)
    bias_near = _toeplitz(jnp.concatenate([neg(blk - 1), ft[:, 0:blk]], axis=1), blk, blk)
    bias_next = _toeplitz(ft[:, 1:2 * blk], blk, blk)
    bias_t = jnp.stack([bias_next, bias_near], axis=1) * LOG2E
    bl = _toeplitz(jnp.flip(ft[:, 1:TAIL_TOKENS + dec_t], axis=1), dec_t, TAIL_TOKENS)
    head_eq = jnp.eye(N_HEADS, dtype=F32)
    bias_last = (bl[:, None, :, :, None] * head_eq[:, None, None, None, :])
    bias_last = jnp.broadcast_to(bias_last, (N_HEADS, 2, dec_t, TAIL_TOKENS, N_HEADS)).reshape(
        2 * N_HEADS * dec_t, TAIL_TOKENS * N_HEADS)
    bn = _toeplitz(jnp.concatenate([jnp.flip(ft[:, 0:dec_t], axis=1), neg(dec_t - 1)], axis=1),
                   dec_t, dec_t)
    bn = jnp.where(head_eq[:, None, None, :] > 0, bn[:, :, :, None], NEG)
    bn = jnp.broadcast_to(bn[:, None], (N_HEADS, 2, dec_t, dec_t, N_HEADS)).reshape(
        2 * N_HEADS * dec_t, dec_t * N_HEADS)
    bias_new = jnp.concatenate([bn, jnp.full((bn.shape[0], PAGE - bn.shape[1]), NEG, F32)], axis=1)

    c_all = jnp.concatenate([c_prompt, c_sample, jnp.zeros((4, D_MODEL), F32)], axis=0)
    mod_all = _ada(c_all, w_ada[0], b_ada[0])
    mod_p = mod_all[:batch].reshape(batch, 1, 6 * D_MODEL)
    mod_s = jnp.repeat(mod_all[batch:batch + dec_b], dec_t, axis=0).reshape(1, n_s, 6 * D_MODEL)

    xp = x_prompt.reshape(n_p, D_MODEL)
    xs = x_sample.reshape(n_s, D_MODEL)

    q_p, kf_p, kb_p, vf_p, vt_p, sg_p = _stage_a(xp, mod_p, w_in_b, ws_p, bs_p, gln, bln,
                                                 tm=TM_TOK, chunk=CHUNK, sample=False)
    q_s, kf_s, vf_s, sg_s, vsn_s = _stage_a(xs, mod_s, w_in_b, ws_s, bs_s, gln, bln,
                                            tm=n_s, chunk=n_s, sample=True)

    g_col = g_subln[0].reshape(HEAD_W, 1)
    g_row = g_subln[0].reshape(1, HEAD_W)
    o_p = _attn_prompt(lam, q_p, kb_p, vt_p, bias_t, g_col, batch=batch, seq=seq, tq=TQ_ATT, tk=TM_TOK)
    pad = ((0, 0), (0, PAGE - dec_t * N_HEADS), (0, 0))
    knew = jnp.pad(kf_s.reshape(dec_b, dec_t * N_HEADS, HEAD_W), pad).astype(BF16)
    vnew = jnp.pad(vf_s.reshape(dec_b, dec_t * N_HEADS, HEAD_W), pad).astype(BF16)
    ck = cache_k.reshape(-1, HEAD_W)
    cv = cache_v.reshape(-1, HEAD_W)
    o_s = _attn_sample(page_table.reshape(-1), lam, q_s, knew, vnew, bias_last, bias_new, g_row, ck, cv,
                       dec_b=dec_b, n_pages=n_pages)

    x1_p, h2_p, cls_p, info_p = _stage_c(o_p, sg_p, xp, mod_p, w_o_b, wr_t, br, tm=TM_TOK, sample=False)
    x1_s, h2_s, cls_s, info_s = _stage_c(o_s, sg_s, xs, mod_s, w_o_b, wr_t, br, tm=n_s, sample=True)

    tm_e = TM_E
    tm_d = TM_D
    n_max = n_tot // tm_e + N_CLASSES
    n_dt = n_tot // tm_d
    cls = jnp.concatenate([cls_p[0], cls_s[0]]).reshape(n_dt, tm_d)
    classes = jnp.arange(N_CLASSES, dtype=jnp.int32)
    seg_len = jnp.sum((cls[:, :, None] == classes).astype(jnp.int32), axis=1)
    counts = jnp.sum(seg_len, axis=0)
    ntile_c = (counts + tm_e - 1) // tm_e
    tile_end = jnp.cumsum(ntile_c)
    class_base = (tile_end - ntile_c) * tm_e
    nact = tile_end[-1]
    seg_off = class_base[None, :] + jnp.cumsum(seg_len, axis=0) - seg_len
    pad_off = class_base + counts
    pad_len = ntile_c * tm_e - counts
    tile_ids = jnp.arange(n_max, dtype=jnp.int32)
    tile_cls = jnp.sum((tile_ids[:, None] >= tile_end[None, :]).astype(jnp.int32), axis=1)
    last_cls = jnp.sum((nact - 1 >= tile_end).astype(jnp.int32))
    tile_cls = jnp.where(tile_ids < nact, tile_cls, last_cls)
    grp = tile_cls // N_PAIRS
    pidx = tile_cls % N_PAIRS
    pick = lambda table: sum(jnp.where(pidx == p, e, 0) for p, e in enumerate(table))
    tile_ea = (grp * EPG + pick(PAIR_A)).astype(jnp.int32)
    tile_eb = (grp * EPG + pick(PAIR_B)).astype(jnp.int32)
    empty = jnp.zeros((N_SUB - 1, N_CLASSES), jnp.int32)
    seg_off = jnp.concatenate([seg_off.astype(jnp.int32), empty]).reshape(-1)
    seg_len = jnp.concatenate([seg_len, empty]).reshape(-1)

    nact1 = nact.reshape(1).astype(jnp.int32)
    zero_rows = (jnp.sum(pad_len) + (n_max - nact) * tm_e).reshape(1).astype(jnp.int32)
    x_sorted = _dispatch(seg_off, seg_len, pad_off.astype(jnp.int32), pad_len.astype(jnp.int32), zero_rows,
                         nact1, h2_p, h2_s, info_p, info_s, tm=tm_d, n_sub=N_SUB, n_slots=n_max * tm_e)
    ya_sorted, yb_sorted = _moe(tile_ea, tile_eb, nact1, x_sorted, wg_b, wu_b, wd_b, tm=tm_e, n_max=n_max)

    y_p = _final(seg_off, seg_len, x1_p, info_p, mod_p, g_final, ya_sorted, yb_sorted,
                 tm=tm_d, n_sub=N_SUB, sample=False, tile_base=0)
    y_s = _final(seg_off, seg_len, x1_s, info_s, mod_s, g_final, ya_sorted, yb_sorted,
                 tm=tm_d, n_sub=1, sample=True, tile_base=n_p // tm_d)

    return (y_p.reshape(batch, seq, D_MODEL),
            y_s.reshape(dec_b, dec_t, D_MODEL),
            kf_p.reshape(batch, 1, seq, N_HEADS, HEAD_W),
            vf_p.reshape(batch, 1, seq, N_HEADS, HEAD_W),
            kf_s.reshape(dec_b, 1, dec_t, N_HEADS, HEAD_W),
            vf_s.reshape(dec_b, 1, dec_t, N_HEADS, HEAD_W),
            vsn_s.reshape(dec_b, 1, dec_t, N_GROUPS_SG, SG_CH))
```

```python
import functools
import math

import numpy as np
import jax
import jax.numpy as jnp
from jax import lax
from jax.experimental import pallas as pl
from jax.experimental.pallas import tpu as pltpu

F32 = jnp.float32
BF16 = jnp.bfloat16

D_MODEL = 1024
N_HEADS = 4
DK = 64
HEAD_W = 128
QK_W = N_HEADS * HEAD_W
N_GROUPS_SG = 4
SG_CH = 128
IN_W = 2560
CHUNK = 128
PAGE = 128
N_BUCKETS = 32
MAX_DISTANCE = 128
N_EG = 4
EPG = 4
N_PAIRS = 6
N_CLASSES = N_EG * N_PAIRS
D_EXPERT = 512
EPS = 1e-6
LAM_INIT = 0.8 - 0.6 * math.exp(-0.3 * 0)
NEG = -1e30
LOG2E = math.log2(math.e)
LANES = 128
SUBLANES = 8

TM_TOK = 512
STAGE_A_PARTS = 4
TQ_ATT = 2048
PAGES_PER_STEP = 16
TM_E = 256
TM_D = 256
N_SUB = 4
VMEM_LIMIT = 56 * 1024 * 1024


def _cparams(sem):
    return pltpu.CompilerParams(dimension_semantics=sem, vmem_limit_bytes=VMEM_LIMIT)


def _ada_body(c_ref, w_ref, b_ref, o_ref):
    c = c_ref[...]
    a = (c * jax.nn.sigmoid(c)).astype(BF16)
    o_ref[...] = jnp.dot(a, w_ref[...].astype(BF16), preferred_element_type=F32) + b_ref[...]


def _ada(c_all, w_ada, b_ada):
    m = c_all.shape[0]
    n = w_ada.shape[1]
    tn = 1536
    return pl.pallas_call(
        _ada_body,
        grid=(n // tn,),
        in_specs=[pl.BlockSpec((m, D_MODEL), lambda j: (0, 0)),
                  pl.BlockSpec((D_MODEL, tn), lambda j: (0, j)),
                  pl.BlockSpec((1, tn), lambda j: (0, j))],
        out_specs=pl.BlockSpec((m, tn), lambda j: (0, j)),
        out_shape=jax.ShapeDtypeStruct((m, n), F32),
        compiler_params=_cparams(("arbitrary",)),
        name="adaln",
    )(c_all, w_ada, b_ada.reshape(1, n))


def _stage_a_body(x_ref, mod_ref, w_in_ref, ws_ref, bs_ref, gln_ref, bln_ref,
                  *out_refs, chunk, sample):
    if sample:
        q_ref, kf_ref, vf_ref, sg_ref, vsn_ref = out_refs
    else:
        q_ref, kf_ref, kb_ref, vf_ref, vt_ref, sg_ref = out_refs
    x = x_ref[...]
    tm = x.shape[0]
    mod = mod_ref[0]
    sh1 = mod[:, 0:D_MODEL]
    sc1 = mod[:, D_MODEL:2 * D_MODEL]
    ms = jnp.mean(x * x, axis=-1, keepdims=True)
    h = (x * lax.rsqrt(ms + EPS)) * (1.0 + sc1) + sh1
    hb = h.astype(BF16)
    n_parts = 1 if sample else STAGE_A_PARTS
    rows_p = tm // n_parts
    zs = [jnp.dot(hb[p * rows_p:(p + 1) * rows_p], w_in_ref[...], preferred_element_type=F32)
          for p in range(n_parts)]
    for p, z in enumerate(zs):
        a = p * rows_p
        q = z[:, 0:QK_W] * (DK ** -0.5)
        k = z[:, QK_W:2 * QK_W]
        v = z[:, 2 * QK_W:3 * QK_W]
        for hd in range(N_HEADS):
            kf_ref[pl.ds(a * N_HEADS + hd, rows_p, stride=N_HEADS), :] = k[:, hd * HEAD_W:(hd + 1) * HEAD_W]
            vf_ref[pl.ds(a * N_HEADS + hd, rows_p, stride=N_HEADS), :] = v[:, hd * HEAD_W:(hd + 1) * HEAD_W]
        if sample:
            q_ref[...] = q
        else:
            q_ref[a:a + rows_p, :] = (q * LOG2E).astype(BF16)
            kb_ref[a:a + rows_p, :] = k.astype(BF16)
            vt_ref[0, :, a:a + rows_p] = v.T.astype(BF16)
        u = z[:, 3 * QK_W:4 * QK_W]
        vs = z[:, 4 * QK_W:5 * QK_W]
        for g in range(N_GROUPS_SG):
            lo, hi = g * SG_CH, (g + 1) * SG_CH
            vg = vs[:, lo:hi]
            mu = jnp.mean(vg, axis=-1, keepdims=True)
            dv = vg - mu
            var = jnp.mean(dv * dv, axis=-1, keepdims=True)
            vn = (dv * lax.rsqrt(var + EPS)) * gln_ref[g:g + 1, :] + bln_ref[g:g + 1, :]
            if sample:
                vsn_ref[:, lo:hi] = vn
            vnb = vn.astype(BF16)
            for c in range(rows_p // chunk):
                r0, r1 = c * chunk, (c + 1) * chunk
                s = jnp.dot(ws_ref[g], vnb[r0:r1], preferred_element_type=F32) + bs_ref[g]
                sg_ref[a + r0:a + r1, lo:hi] = (u[r0:r1, lo:hi] * s).astype(BF16)


def _stage_a(x, mod, w_in_b, ws, bs, gln, bln, *, tm, chunk, sample):
    n = x.shape[0]
    nt = n // tm
    mrows = mod.shape[1]
    row = lambda w, dt: jax.ShapeDtypeStruct((n, w), dt)
    blk = lambda w: pl.BlockSpec((tm, w), lambda i: (i, 0))
    cache_shape = jax.ShapeDtypeStruct((n * N_HEADS, HEAD_W), F32)
    cache_blk = pl.BlockSpec((tm * N_HEADS, HEAD_W), lambda i: (i, 0))
    if sample:
        out_shape = [row(QK_W, F32), cache_shape, cache_shape, row(QK_W, BF16), row(QK_W, F32)]
        out_specs = [blk(QK_W), cache_blk, cache_blk, blk(QK_W), blk(QK_W)]
        mod_map = lambda i: (0, 0, 0)
    else:
        out_shape = [row(QK_W, BF16), cache_shape, row(QK_W, BF16), cache_shape,
                     jax.ShapeDtypeStruct((nt, QK_W, tm), BF16), row(QK_W, BF16)]
        out_specs = [blk(QK_W), cache_blk, blk(QK_W), cache_blk,
                     pl.BlockSpec((1, QK_W, tm), lambda i: (i, 0, 0)), blk(QK_W)]
        tiles_per_batch = 4096 // tm
        mod_map = lambda i: (i // tiles_per_batch, 0, 0)
    full = lambda a: pl.BlockSpec(a.shape, lambda i: (0,) * a.ndim)
    return pl.pallas_call(
        functools.partial(_stage_a_body, chunk=chunk, sample=sample),
        grid=(nt,),
        in_specs=[blk(D_MODEL),
                  pl.BlockSpec((1, mrows, 6 * D_MODEL), mod_map),
                  full(w_in_b), full(ws), full(bs), full(gln), full(bln)],
        out_specs=out_specs,
        out_shape=out_shape,
        compiler_params=_cparams(("arbitrary",)),
        name="stage_a_sample" if sample else "stage_a_prompt",
    )(x, mod, w_in_b, ws, bs, gln, bln)


ATT_COLS = 256
BIAS_BLOCK = MAX_DISTANCE


def _attn_body(lam_ref, q_ref, k_ref, vt_ref, bias_ref, g_ref, o_ref, *scratch, tq, tk):
    n_chain = 2 * tq // ATT_COLS
    q2_refs, m_refs, l_refs, acc_refs = (scratch[i * n_chain:(i + 1) * n_chain] for i in range(4))
    qi = pl.program_id(2)
    for c in range(n_chain):
        q0 = (c * ATT_COLS) % tq
        q = q_ref[q0:q0 + ATT_COLS, :]
        lane = lax.broadcasted_iota(jnp.int32, q.shape, 1)
        keep = (lane < DK) if c < n_chain // 2 else (lane >= DK)
        q2_refs[c][...] = jnp.where(keep, q, jnp.zeros_like(q))
        m_refs[c][...] = jnp.full(m_refs[c].shape, NEG, F32)
        l_refs[c][...] = jnp.zeros(l_refs[c].shape, F32)
        acc_refs[c][...] = jnp.zeros(acc_refs[c].shape, F32)

    blk = BIAS_BLOCK

    kblocks = tk // blk

    def block_kinds(rel, q0):
        return [[(q0 // blk + b) - (rel + a) for b in range(ATT_COLS // blk)] for a in range(kblocks)]

    def keys_needed(rel, q0):
        if rel is None:
            return kblocks
        return sum(1 for row in block_kinds(rel, q0) if max(row) >= 0)

    def with_bias(s, rel, q0):
        if rel is None:
            return s
        kinds = block_kinds(rel, q0)[:s.shape[0] // blk]
        if all(d >= 2 for row in kinds for d in row):
            return s
        nxt, near = bias_ref[0, 0], bias_ref[0, 1]
        pick = lambda d: (jnp.full((blk, blk), NEG, F32) if d < 0 else near if d == 0 else nxt if d == 1
                          else jnp.zeros((blk, blk), F32))
        rows = []
        for a, row in enumerate(kinds):
            s_row = s[a * blk:(a + 1) * blk]
            if any(d < 2 for d in row):
                s_row = s_row + jnp.concatenate([pick(d) for d in row], axis=1)
            rows.append(s_row)
        return jnp.concatenate(rows, axis=0)

    def score(j, rel, c):
        nk = keys_needed(rel, (c * ATT_COLS) % tq) * blk
        if nk == 0:
            return None
        k = k_ref[pl.ds(pl.multiple_of(j * tk, tk), nk), :]
        s = lax.dot_general(k, q2_refs[c][...], (((1,), (1,)), ((), ())),
                            preferred_element_type=F32)
        return with_bias(s, rel, (c * ATT_COLS) % tq)

    def accumulate(j, c, s):
        vt = vt_ref[j, :, 0:s.shape[0]]
        m_old = m_refs[c][...]
        m_new = jnp.maximum(m_old, jnp.max(s, axis=0, keepdims=True))
        alpha = jnp.exp2(m_old - m_new)
        p = jnp.exp2(s - m_new)
        l_refs[c][...] = alpha * l_refs[c][...] + jnp.sum(p, axis=0, keepdims=True)
        acc_refs[c][...] = alpha * acc_refs[c][...] + jnp.dot(vt, p.astype(BF16), preferred_element_type=F32)
        m_refs[c][...] = m_new

    def tiles(*work):
        scores = [[score(j, rel, c) for c in range(n_chain)] for j, rel in work]
        for (j, _), tile_scores in zip(work, scores):
            for c, s in enumerate(tile_scores):
                if s is not None:
                    accumulate(j, c, s)

    ratio = tq // tk
    first_diag = qi * ratio
    n_plain = jnp.maximum(first_diag - 1, 0)

    def plain_pair(jj, carry):
        tiles((2 * jj, None), (2 * jj + 1, None))
        return carry

    lax.fori_loop(0, n_plain // 2, plain_pair, 0)

    @pl.when(qi >= 1)
    def _():
        if ratio % 2 == 0:
            tiles((first_diag - 2, None), (first_diag - 1, -kblocks))
        else:
            @pl.when(n_plain % 2 == 1)
            def _():
                tiles((n_plain - 1, None))
            tiles((first_diag - 1, -kblocks))

    tiles(*[(first_diag + r, r * kblocks) for r in range(ratio)])

    lam = lam_ref[0]
    o_all = jnp.concatenate([acc_refs[c][...] * (1.0 / l_refs[c][...]) for c in range(n_chain)],
                            axis=1)
    o = o_all[:, 0:tq] - lam * o_all[:, tq:2 * tq]
    ms = jnp.mean(o * o, axis=0, keepdims=True)
    on = (o * lax.rsqrt(ms + EPS)) * g_ref[...] * (1.0 - LAM_INIT)
    o_ref[...] = on.T.astype(BF16)


def _attn_prompt(lam, q, kb, vt, bias_t, g_col, *, batch, seq, tq, tk):
    nq = seq // tq
    nk = seq // tk
    n = batch * seq
    n_chain = 2 * tq // ATT_COLS
    assert vt.shape == (batch * nk, QK_W, tk) and tq % tk == 0 and tk % BIAS_BLOCK == 0
    return pl.pallas_call(
        functools.partial(_attn_body, tq=tq, tk=tk),
        grid=(batch, N_HEADS, nq),
        in_specs=[pl.BlockSpec(memory_space=pltpu.SMEM),
                  pl.BlockSpec((tq, HEAD_W), lambda b, h, i: (b * nq + i, h)),
                  pl.BlockSpec((seq, HEAD_W), lambda b, h, i: (b, h)),
                  pl.BlockSpec((nk, HEAD_W, tk), lambda b, h, i: (b, h, 0)),
                  pl.BlockSpec((1, 2, BIAS_BLOCK, BIAS_BLOCK), lambda b, h, i: (h, 0, 0, 0)),
                  pl.BlockSpec((HEAD_W, 1), lambda b, h, i: (0, 0))],
        out_specs=pl.BlockSpec((tq, HEAD_W), lambda b, h, i: (b * nq + i, h)),
        out_shape=jax.ShapeDtypeStruct((n, QK_W), BF16),
        scratch_shapes=([pltpu.VMEM((ATT_COLS, HEAD_W), BF16)] * n_chain
                        + [pltpu.VMEM((1, ATT_COLS), F32)] * (2 * n_chain)
                        + [pltpu.VMEM((HEAD_W, ATT_COLS), F32)] * n_chain),
        compiler_params=_cparams(("arbitrary", "arbitrary", "arbitrary")),
        name="attn_prompt",
    )(lam, q, kb, vt, bias_t, g_col)


PAGE_ROWS = PAGE * N_HEADS
TAIL_TOKENS = 2 * PAGE


def _sattn_body(pt_ref, lam_ref, q_ref, knew_ref, vnew_ref, bl_ref, bn_ref, g_ref, ck_ref, cv_ref,
                o_ref, kbuf, vbuf, sem, mask_ref, m_ref, l_ref, acc_ref, *, pages, n_steps, total):
    b = pl.program_id(0)
    s = pl.program_id(1)
    step = b * n_steps + s
    slot = step % 2

    def page_copies(step_idx, sl):
        base = step_idx * pages
        out = []
        for i in range(pages):
            src = pl.ds(pl.multiple_of(pt_ref[base + i] * PAGE_ROWS, PAGE_ROWS), PAGE_ROWS)
            dst = pl.ds(i * PAGE_ROWS, PAGE_ROWS)
            out.append(pltpu.make_async_copy(ck_ref.at[src], kbuf.at[sl, dst], sem.at[sl, 0]))
            out.append(pltpu.make_async_copy(cv_ref.at[src], vbuf.at[sl, dst], sem.at[sl, 1]))
        return out

    def start_all(copies):
        for n, c in enumerate(copies):
            c.start(priority=n % 2)

    @pl.when(step == 0)
    def _():
        start_all(page_copies(0, 0))

    @pl.when(step + 1 < total)
    def _():
        start_all(page_copies(step + 1, 1 - slot))

    for c in page_copies(step, slot):
        c.wait()

    @pl.when(s == 0)
    def _():
        m_ref[...] = jnp.full(m_ref.shape, NEG, F32)
        l_ref[...] = jnp.zeros(l_ref.shape, F32)
        acc_ref[...] = jnp.zeros(acc_ref.shape, F32)

    @pl.when(step == 0)
    def _():
        row = lax.broadcasted_iota(jnp.int32, mask_ref.shape, 0)
        col = lax.broadcasted_iota(jnp.int32, mask_ref.shape, 1)
        same_head = (col % N_HEADS) == (row // (2 * SUBLANES))
        mask_ref[...] = jnp.where(same_head, 0.0, NEG)

    q = q_ref[...]
    lane = lax.broadcasted_iota(jnp.int32, (SUBLANES, HEAD_W), 1)
    pieces = []
    for h in range(N_HEADS):
        qh = q[:, h * HEAD_W:(h + 1) * HEAD_W]
        pieces += [jnp.where(lane < DK, qh, 0.0), jnp.where(lane >= DK, qh, 0.0)]
    qm = jnp.concatenate(pieces, axis=0).astype(BF16)

    def update(kb, vb, bias):
        sc = lax.dot_general(qm, kb, (((1,), (1,)), ((), ())),
                             preferred_element_type=F32) + bias
        m_old = m_ref[...]
        m_new = jnp.maximum(m_old, jnp.max(sc, axis=1, keepdims=True))
        alpha = jnp.exp(m_old - m_new)
        p = jnp.exp(sc - m_new)
        l_ref[...] = alpha * l_ref[...] + jnp.sum(p, axis=1, keepdims=True)
        acc_ref[...] = alpha * acc_ref[...] + jnp.dot(p.astype(BF16), vb,
                                                      preferred_element_type=F32)
        m_ref[...] = m_new

    is_last = s == n_steps - 1
    head_cols = mask_ref.shape[1] - bl_ref.shape[1]
    update(kbuf[slot].astype(BF16), vbuf[slot].astype(BF16),
           jnp.concatenate([mask_ref[:, :head_cols],
                            mask_ref[:, head_cols:] + bl_ref[...] * is_last.astype(F32)], axis=1))

    @pl.when(is_last)
    def _():
        update(knew_ref[0], vnew_ref[0], bn_ref[...])
        lam = lam_ref[0]
        o_all = acc_ref[...] * (1.0 / l_ref[...])
        for h in range(N_HEADS):
            r = h * 2 * SUBLANES
            o = o_all[r:r + SUBLANES] - lam * o_all[r + SUBLANES:r + 2 * SUBLANES]
            ms = jnp.mean(o * o, axis=-1, keepdims=True)
            o_ref[:, h * HEAD_W:(h + 1) * HEAD_W] = ((o * lax.rsqrt(ms + EPS)) * g_ref[...]
                                                     * (1.0 - LAM_INIT))


def _attn_sample(page_table_flat, lam, q_s, knew, vnew, bias_last, bias_new, g_row, cache_k, cache_v,
                 *, dec_b, n_pages):
    pages = PAGES_PER_STEP
    n_steps = n_pages // pages
    total = dec_b * n_steps
    nq = q_s.shape[0] // dec_b
    n_rows = 2 * N_HEADS * nq
    step_rows = pages * PAGE_ROWS
    grid_spec = pltpu.PrefetchScalarGridSpec(
        num_scalar_prefetch=1,
        grid=(dec_b, n_steps),
        in_specs=[pl.BlockSpec(memory_space=pltpu.SMEM),
                  pl.BlockSpec((nq, QK_W), lambda b, s, pt: (b, 0)),
                  pl.BlockSpec((1, PAGE, HEAD_W), lambda b, s, pt: (b, 0, 0)),
                  pl.BlockSpec((1, PAGE, HEAD_W), lambda b, s, pt: (b, 0, 0)),
                  pl.BlockSpec(bias_last.shape, lambda b, s, pt: (0, 0)),
                  pl.BlockSpec(bias_new.shape, lambda b, s, pt: (0, 0)),
                  pl.BlockSpec((1, HEAD_W), lambda b, s, pt: (0, 0)),
                  pl.BlockSpec(memory_space=pl.ANY),
                  pl.BlockSpec(memory_space=pl.ANY)],
        out_specs=pl.BlockSpec((nq, QK_W), lambda b, s, pt: (b, 0)),
        scratch_shapes=[pltpu.VMEM((2, step_rows, HEAD_W), F32),
                        pltpu.VMEM((2, step_rows, HEAD_W), F32),
                        pltpu.SemaphoreType.DMA((2, 2)),
                        pltpu.VMEM((n_rows, step_rows), F32),
                        pltpu.VMEM((n_rows, 1), F32), pltpu.VMEM((n_rows, 1), F32),
                        pltpu.VMEM((n_rows, HEAD_W), F32)])
    return pl.pallas_call(
        functools.partial(_sattn_body, pages=pages, n_steps=n_steps, total=total),
        grid_spec=grid_spec,
        out_shape=jax.ShapeDtypeStruct(q_s.shape, F32),
        compiler_params=_cparams(("arbitrary", "arbitrary")),
        name="attn_sample",
    )(page_table_flat, lam, q_s, knew, vnew, bias_last, bias_new, g_row, cache_k, cache_v)


INFO_W_A, INFO_W_B, INFO_CLS = 0, 1, 2
PAIR_A = (0, 0, 0, 1, 1, 3)
PAIR_B = (1, 2, 3, 3, 2, 2)


def _stage_c_body(o_ref, sg_ref, x_ref, mod_ref, wo_ref, wr_ref, br_ref, x1_ref, h2_ref, cls_ref, info_ref):
    x = x_ref[...]
    tm = x.shape[0]
    mod = mod_ref[0]
    g1 = mod[:, 2 * D_MODEL:3 * D_MODEL]
    sh2 = mod[:, 3 * D_MODEL:4 * D_MODEL]
    sc2 = mod[:, 4 * D_MODEL:5 * D_MODEL]
    mix = (jnp.dot(o_ref[...].astype(BF16), wo_ref[0:QK_W, :], preferred_element_type=F32)
           + jnp.dot(sg_ref[...], wo_ref[QK_W:2 * QK_W, :], preferred_element_type=F32))
    x1 = x + g1 * mix
    x1_ref[...] = x1
    ms = jnp.mean(x1 * x1, axis=-1, keepdims=True)
    h2 = ((x1 * lax.rsqrt(ms + EPS)) * (1.0 + sc2) + sh2).astype(BF16)
    h2_ref[...] = h2
    lg = lax.dot_general(wr_ref[...], h2, (((1,), (1,)), ((), ())),
                         preferred_element_type=F32) + br_ref[...]
    gl = [lg[i:i + 1, :] for i in range(N_EG)]
    el = [lg[N_EG + i:N_EG + i + 1, :] for i in range(N_EG * EPG)]
    gmax = jnp.maximum(jnp.maximum(gl[0], gl[1]), jnp.maximum(gl[2], gl[3]))
    gi = jnp.where(gl[0] == gmax, 0, jnp.where(gl[1] == gmax, 1, jnp.where(gl[2] == gmax, 2, 3)))
    gsum = (jnp.exp(gl[0] - gmax) + jnp.exp(gl[1] - gmax)
            + jnp.exp(gl[2] - gmax) + jnp.exp(gl[3] - gmax))
    gp = 1.0 / gsum
    sel = [jnp.where(gi == 0, el[j], jnp.where(gi == 1, el[EPG + j],
                                               jnp.where(gi == 2, el[2 * EPG + j], el[3 * EPG + j])))
           for j in range(EPG)]
    v0 = jnp.maximum(jnp.maximum(sel[0], sel[1]), jnp.maximum(sel[2], sel[3]))
    i0 = jnp.where(sel[0] == v0, 0, jnp.where(sel[1] == v0, 1, jnp.where(sel[2] == v0, 2, 3)))
    rest = [jnp.where(i0 == j, -3e38, sel[j]) for j in range(EPG)]
    v1 = jnp.maximum(jnp.maximum(rest[0], rest[1]), jnp.maximum(rest[2], rest[3]))
    i1 = jnp.where(rest[0] == v1, 0, jnp.where(rest[1] == v1, 1, jnp.where(rest[2] == v1, 2, 3)))
    e1 = jnp.exp(v1 - v0)
    den = 1.0 / (1.0 + e1)
    tw0 = den * gp
    tw1 = e1 * den * gp
    first_low = i0 < i1
    lo = jnp.where(first_low, i0, i1)
    hi = jnp.where(first_low, i1, i0)
    w_lo = jnp.where(first_low, tw0, tw1)
    w_hi = jnp.where(first_low, tw1, tw0)
    pair = jnp.where(lo == 0, hi - 1, jnp.where(lo == 2, 5, jnp.where(hi == 3, 3, 4)))
    swapped = pair == 5
    w_a = jnp.where(swapped, w_hi, w_lo)
    w_b = jnp.where(swapped, w_lo, w_hi)
    cls = gi * N_PAIRS + pair
    cls_ref[...] = jnp.broadcast_to(cls, cls_ref.shape).astype(jnp.int32)
    row = lax.broadcasted_iota(jnp.int32, (LANES, tm), 0)
    rec = jnp.where(row == INFO_W_A, w_a, jnp.where(row == INFO_W_B, w_b,
                                                    jnp.where(row == INFO_CLS, cls.astype(F32), 0.0)))
    info_ref[...] = rec.T


def _stage_c(o, sg, x, mod, wo_b, wr_t, br, *, tm, sample):
    n = x.shape[0]
    nt = n // tm
    mrows = mod.shape[1]
    blk = lambda w: pl.BlockSpec((tm, w), lambda i: (i, 0))
    full = lambda a: pl.BlockSpec(a.shape, lambda i: (0,) * a.ndim)
    if sample:
        mod_map = lambda i: (0, 0, 0)
    else:
        tiles_per_batch = 4096 // tm
        mod_map = lambda i: (i // tiles_per_batch, 0, 0)
    return pl.pallas_call(
        _stage_c_body,
        grid=(nt,),
        in_specs=[blk(QK_W), blk(QK_W), blk(D_MODEL),
                  pl.BlockSpec((1, mrows, 6 * D_MODEL), mod_map),
                  full(wo_b), full(wr_t), full(br)],
        out_specs=[blk(D_MODEL), blk(D_MODEL),
                   pl.BlockSpec((SUBLANES, tm), lambda i: (0, i)),
                   blk(LANES)],
        out_shape=[jax.ShapeDtypeStruct((n, D_MODEL), F32),
                   jax.ShapeDtypeStruct((n, D_MODEL), BF16),
                   jax.ShapeDtypeStruct((SUBLANES, n), jnp.int32),
                   jax.ShapeDtypeStruct((n, LANES), F32)],
        compiler_params=_cparams(("arbitrary",)),
        name="stage_c_sample" if sample else "stage_c_prompt",
    )(o, sg, x, mod, wo_b, wr_t, br)


def rows8(ref, start, count):
    scale = lambda v: v * SUBLANES if isinstance(v, int) else pl.multiple_of(v * SUBLANES, SUBLANES)
    return ref.at[pl.ds(scale(start), scale(count))]


def _perm_t(cls_col):
    n = cls_col.shape[0]
    lane = lax.broadcasted_iota(jnp.int32, (n, LANES), 1).astype(F32)
    onehot = (lane == cls_col).astype(BF16)
    r = lax.broadcasted_iota(jnp.int32, (n, n), 0)
    c = lax.broadcasted_iota(jnp.int32, (n, n), 1)
    before = (c < r).astype(BF16)
    rank = jnp.dot(before, onehot, preferred_element_type=F32)
    cnt = jnp.sum(onehot.astype(F32), axis=0, keepdims=True)
    cr = lax.broadcasted_iota(jnp.int32, (LANES, LANES), 0)
    cc = lax.broadcasted_iota(jnp.int32, (LANES, LANES), 1)
    lower_cls = (cr < cc).astype(BF16)
    base = jnp.dot(jnp.broadcast_to(cnt, (SUBLANES, LANES)).astype(BF16), lower_cls,
                   preferred_element_type=F32)[0:1, :]
    pos = jnp.sum(onehot.astype(F32) * (base + rank), axis=1, keepdims=True)
    dest = lax.broadcasted_iota(jnp.int32, (n, n), 1).astype(F32)
    return (dest == pos).astype(F32)


def _dispatch_body(soff_ref, slen_ref, poff_ref, plen_ref, ptot_ref, nact_ref, hp_ref, hs_ref, ip_ref, is_ref,
                   xs_ref, buf, zbuf, sem, zsem, *, tm, n_sub, n_tiles, n_slab_tiles):
    i = pl.program_id(0)
    slot = i % 2
    is_sample = i == n_tiles - 1

    def wait_tile(sl, tokens):
        pltpu.make_async_copy(rows8(buf.at[sl], 0, tokens), rows8(buf.at[sl], 0, tokens), sem.at[sl]).wait()

    @pl.when(i == 0)
    def _():
        zbuf[...] = jnp.zeros(zbuf.shape, F32)
        for c in range(N_CLASSES):
            @pl.when(plen_ref[c] > 0)
            def _():
                pltpu.make_async_copy(rows8(zbuf, 0, plen_ref[c]), rows8(xs_ref, poff_ref[c], plen_ref[c]),
                                      zsem).start()
        for j in range(n_slab_tiles - N_CLASSES, n_slab_tiles):
            @pl.when(j >= nact_ref[0])
            def _():
                pltpu.make_async_copy(zbuf, rows8(xs_ref, j * TM_E, TM_E), zsem).start()

    xp = []
    for u in range(n_sub):
        rows = slice(u * tm, (u + 1) * tm)
        x = hp_ref[rows, :]
        info = ip_ref[rows, :]
        if u == 0:
            x = jnp.where(is_sample, hs_ref[...], x)
            info = jnp.where(is_sample, is_ref[...], info)
        perm = _perm_t(info[:, INFO_CLS:INFO_CLS + 1]).T.astype(BF16)
        xp.append(jnp.dot(perm, x, preferred_element_type=F32))

    @pl.when(i >= 2)
    def _():
        wait_tile(slot, n_sub * tm)

    bs = buf.at[slot]
    for u in range(n_sub):
        for c in range(D_MODEL // LANES):
            bs[pl.ds(u * tm * SUBLANES + c, tm, stride=SUBLANES), :] = xp[u][:, c * LANES:(c + 1) * LANES]
    for u in range(n_sub):
        local = u * tm
        for c in range(N_CLASSES):
            k = (i * n_sub + u) * N_CLASSES + c
            n_rows = slen_ref[k]

            @pl.when(n_rows > 0)
            def _():
                pltpu.make_async_copy(rows8(bs, local, n_rows), rows8(xs_ref, soff_ref[k], n_rows),
                                      sem.at[slot]).start()
            local = local + n_rows

    @pl.when(i == n_tiles - 1)
    def _():
        wait_tile(slot, tm)
        if n_tiles >= 2:
            wait_tile(1 - slot, n_sub * tm)

        @pl.when(ptot_ref[0] > 0)
        def _():
            n = pl.multiple_of(ptot_ref[0] * SUBLANES, SUBLANES)
            pltpu.make_async_copy(xs_ref.at[pl.ds(0, n)], xs_ref.at[pl.ds(0, n)], zsem).wait()


def _dispatch(seg_off, seg_len, pad_off, pad_len, pad_tot, nact, h2_p, h2_s, info_p, info_s, *, tm, n_sub, n_slots):
    n_prompt_steps = h2_p.shape[0] // (n_sub * tm)
    n_tiles = n_prompt_steps + 1
    assert h2_s.shape[0] == tm and h2_p.shape[0] % (n_sub * tm) == 0
    assert seg_len.shape[0] == n_tiles * n_sub * N_CLASSES
    last_p = n_prompt_steps - 1
    grid_spec = pltpu.PrefetchScalarGridSpec(
        num_scalar_prefetch=6,
        grid=(n_tiles,),
        in_specs=[pl.BlockSpec((n_sub * tm, D_MODEL), lambda i, *_: (jnp.minimum(i, last_p), 0)),
                  pl.BlockSpec((tm, D_MODEL), lambda i, *_: (0, 0)),
                  pl.BlockSpec((n_sub * tm, LANES), lambda i, *_: (jnp.minimum(i, last_p), 0)),
                  pl.BlockSpec((tm, LANES), lambda i, *_: (0, 0))],
        out_specs=pl.BlockSpec(memory_space=pl.ANY),
        scratch_shapes=[pltpu.VMEM((2, n_sub * tm * SUBLANES, LANES), F32),
                        pltpu.VMEM((TM_E * SUBLANES, LANES), F32),
                        pltpu.SemaphoreType.DMA((2,)),
                        pltpu.SemaphoreType.DMA(())])
    return pl.pallas_call(
        functools.partial(_dispatch_body, tm=tm, n_sub=n_sub, n_tiles=n_tiles, n_slab_tiles=n_slots // TM_E),
        grid_spec=grid_spec,
        out_shape=jax.ShapeDtypeStruct((n_slots * SUBLANES, LANES), F32),
        compiler_params=_cparams(("arbitrary",)),
        name="moe_dispatch",
    )(seg_off, seg_len, pad_off, pad_len, pad_tot, nact, h2_p, h2_s, info_p, info_s)


def _moe_body(ea_ref, eb_ref, nact_ref, x_ref, wga_ref, wgb_ref, wua_ref, wub_ref, wda_ref, wdb_ref,
              ya_ref, yb_ref, *, tm):
    i = pl.program_id(0)

    @pl.when(i < nact_ref[0])
    def _():
        x = jnp.concatenate([x_ref[pl.ds(c, tm, stride=SUBLANES), :] for c in range(D_MODEL // LANES)],
                            axis=1).astype(BF16)

        def expert(wg_ref, wu_ref, wd_ref):
            gate = jnp.dot(x, wg_ref[0].astype(BF16), preferred_element_type=F32)
            up = jnp.dot(x, wu_ref[0].astype(BF16), preferred_element_type=F32)
            he = (gate * jax.nn.sigmoid(gate)) * up
            return jnp.dot(he.astype(BF16), wd_ref[0].astype(BF16), preferred_element_type=F32)

        ya = expert(wga_ref, wua_ref, wda_ref)
        yb = expert(wgb_ref, wub_ref, wdb_ref)
        for c in range(D_MODEL // LANES):
            ya_ref[pl.ds(c, tm, stride=SUBLANES), :] = ya[:, c * LANES:(c + 1) * LANES]
            yb_ref[pl.ds(c, tm, stride=SUBLANES), :] = yb[:, c * LANES:(c + 1) * LANES]

    @pl.when(i >= nact_ref[0])
    def _():
        ya_ref[...] = jnp.zeros(ya_ref.shape, ya_ref.dtype)
        yb_ref[...] = jnp.zeros(yb_ref.shape, yb_ref.dtype)


def _moe(tile_ea, tile_eb, nact, x_sorted, wg_b, wu_b, wd_b, *, tm, n_max):
    wspec_in = lambda sel: pl.BlockSpec((1, D_MODEL, D_EXPERT), sel)
    wspec_out = lambda sel: pl.BlockSpec((1, D_EXPERT, D_MODEL), sel)
    sel_a = lambda i, ea, eb, na: (ea[i], 0, 0)
    sel_b = lambda i, ea, eb, na: (eb[i], 0, 0)
    rows_in = lambda i, ea, eb, na: (jnp.minimum(i, na[0] - 1), 0)
    grid_spec = pltpu.PrefetchScalarGridSpec(
        num_scalar_prefetch=3,
        grid=(n_max,),
        in_specs=[pl.BlockSpec((tm * SUBLANES, LANES), rows_in),
                  wspec_in(sel_a), wspec_in(sel_b), wspec_in(sel_a), wspec_in(sel_b),
                  wspec_out(sel_a), wspec_out(sel_b)],
        out_specs=[pl.BlockSpec((tm * SUBLANES, LANES), lambda i, ea, eb, na: (i, 0))] * 2)
    return pl.pallas_call(
        functools.partial(_moe_body, tm=tm),
        grid_spec=grid_spec,
        out_shape=[jax.ShapeDtypeStruct(x_sorted.shape, F32)] * 2,
        compiler_params=_cparams(("arbitrary",)),
        name="moe",
    )(tile_ea, tile_eb, nact, x_sorted, wg_b, wg_b, wu_b, wu_b, wd_b, wd_b)


def _final_body(soff_ref, slen_ref, x1_ref, info_ref, mod_ref, gf_ref, ya_ref, yb_ref, y_ref, buf, sem,
                *, tm, n_sub, n_tiles, tile_base):
    i = pl.program_id(0)
    slot = i % 2
    slabs = (ya_ref, yb_ref)

    def fetch(step, sl):
        for u in range(n_sub):
            local = u * tm
            for c in range(N_CLASSES):
                k = (step * n_sub + u + tile_base) * N_CLASSES + c
                n_rows = slen_ref[k]

                @pl.when(n_rows > 0)
                def _():
                    for e in range(2):
                        pltpu.make_async_copy(rows8(slabs[e], soff_ref[k], n_rows),
                                              rows8(buf.at[sl, e], local, n_rows), sem.at[sl, e]).start()
                local = local + n_rows

    @pl.when(i == 0)
    def _():
        fetch(0, 0)

    @pl.when(i + 1 < n_tiles)
    def _():
        fetch(i + 1, 1 - slot)

    info = info_ref[...]
    perm_t = [_perm_t(info[u * tm:(u + 1) * tm, INFO_CLS:INFO_CLS + 1]).astype(BF16) for u in range(n_sub)]
    moe = None
    for e, lane_w in enumerate((INFO_W_A, INFO_W_B)):
        pltpu.make_async_copy(buf.at[slot, e], buf.at[slot, e], sem.at[slot, e]).wait()
        bs = buf.at[slot, e]
        parts = []
        for u in range(n_sub):
            ye = jnp.concatenate([bs[pl.ds(u * tm * SUBLANES + c, tm, stride=SUBLANES), :]
                                  for c in range(D_MODEL // LANES)], axis=1)
            parts.append(jnp.dot(perm_t[u], ye.astype(BF16), preferred_element_type=F32))
        term = info[:, lane_w:lane_w + 1] * jnp.concatenate(parts, axis=0)
        moe = term if moe is None else moe + term
    x1 = x1_ref[...]
    g2 = mod_ref[0][:, 5 * D_MODEL:6 * D_MODEL]
    x2 = x1 + g2 * moe
    ms = jnp.mean(x2 * x2, axis=-1, keepdims=True)
    y_ref[...] = (x2 * lax.rsqrt(ms + EPS)) * gf_ref[...]


def _final(seg_off, seg_len, x1, info, mod, g_final, ya_sorted, yb_sorted, *, tm, n_sub, sample, tile_base):
    n = x1.shape[0]
    rows = n_sub * tm
    nt = n // rows
    mrows = mod.shape[1]
    if sample:
        mod_map = lambda i, *_: (0, 0, 0)
    else:
        tiles_per_batch = 4096 // rows
        mod_map = lambda i, *_: (i // tiles_per_batch, 0, 0)
    grid_spec = pltpu.PrefetchScalarGridSpec(
        num_scalar_prefetch=2,
        grid=(nt,),
        in_specs=[pl.BlockSpec((rows, D_MODEL), lambda i, *_: (i, 0)),
                  pl.BlockSpec((rows, LANES), lambda i, *_: (i, 0)),
                  pl.BlockSpec((1, mrows, 6 * D_MODEL), mod_map),
                  pl.BlockSpec((1, D_MODEL), lambda i, *_: (0, 0)),
                  pl.BlockSpec(memory_space=pl.ANY),
                  pl.BlockSpec(memory_space=pl.ANY)],
        out_specs=pl.BlockSpec((rows, D_MODEL), lambda i, *_: (i, 0)),
        scratch_shapes=[pltpu.VMEM((2, 2, rows * SUBLANES, LANES), F32),
                        pltpu.SemaphoreType.DMA((2, 2))])
    return pl.pallas_call(
        functools.partial(_final_body, tm=tm, n_sub=n_sub, n_tiles=nt, tile_base=tile_base),
        grid_spec=grid_spec,
        out_shape=jax.ShapeDtypeStruct((n, D_MODEL), F32),
        compiler_params=_cparams(("arbitrary",)),
        name="final_sample" if sample else "final_prompt",
    )(seg_off, seg_len, x1, info, mod, g_final.reshape(1, D_MODEL), ya_sorted, yb_sorted)


def _bucket_table(n):
    d = np.arange(n)
    max_exact = N_BUCKETS // 2
    nf = np.maximum(d, 1).astype(np.float64)
    large = max_exact + (np.log(nf / max_exact) / math.log(MAX_DISTANCE / max_exact)
                         * (N_BUCKETS - max_exact)).astype(np.int64)
    large = np.minimum(large, N_BUCKETS - 1)
    return np.where(d < max_exact, d, large).astype(np.int32)


def _toeplitz(v, n_rows, n_cols):
    length = n_rows + n_cols - 1
    lead = v.shape[:-1]
    vp = jnp.concatenate([v, jnp.zeros(lead + (1,), v.dtype)], axis=-1)
    skew = jnp.tile(vp, (1,) * len(lead) + (n_rows,))[..., :n_rows * length].reshape(lead + (n_rows, length))
    return skew[..., n_rows - 1:n_rows - 1 + n_cols]


def kernel(x_prompt, x_sample, c_prompt, c_sample, cache_k, cache_v, page_table, w_ada, b_ada, w_in, w_o,
           lam_q1, lam_k1, lam_q2, lam_k2, g_subln, rel_bias, g_sg_ln, b_sg_ln, w_s, b_s, w_rg, b_rg,
           w_re, b_re, w_gate, w_up, w_down, g_final):
    batch, seq, _ = x_prompt.shape
    dec_b, dec_t, _ = x_sample.shape
    n_pages = page_table.shape[1]
    n_p = batch * seq
    n_s = dec_b * dec_t
    n_tot = n_p + n_s
    assert w_in.shape[0] == 1 and cache_k.shape[1] == 1 and seq % TQ_ATT == 0 and n_pages % PAGES_PER_STEP == 0
    assert n_p % TM_TOK == 0 and n_p % n_s == 0 and n_tot % TM_E == 0 and dec_t == SUBLANES
    assert TAIL_TOKENS >= MAX_DISTANCE + dec_t and TAIL_TOKENS <= PAGES_PER_STEP * PAGE
    assert n_s == TM_D and n_p % TM_D == 0 and N_CLASSES <= LANES

    w_in_b = w_in[0].astype(BF16)
    w_o_b = w_o[0].astype(BF16)
    wr_t = jnp.zeros((32, D_MODEL), F32).at[0:N_EG].set(w_rg[0].T).at[N_EG:N_EG + N_EG * EPG].set(w_re[0].T)
    wr_t = wr_t.astype(BF16)
    br = jnp.zeros((32, 1), F32).at[0:N_EG, 0].set(b_rg[0]).at[N_EG:N_EG + N_EG * EPG, 0].set(b_re[0])
    wg_b, wu_b, wd_b = w_gate[0], w_up[0], w_down[0]
    ws_tril = jnp.tril(w_s[0])
    ws_p = ws_tril.astype(BF16)
    bs_p = b_s[0][:, :, None]
    same_seq = np.kron(np.eye(dec_b, dtype=np.float32), np.ones((dec_t, dec_t), np.float32))
    rep = np.tile(np.eye(dec_t, dtype=np.float32), (dec_b, 1))
    ws_rep = jnp.einsum('ri,gij,cj->grc', rep, ws_tril[:, :dec_t, :dec_t], rep,
                        precision=lax.Precision.HIGHEST)
    ws_s = (ws_rep * same_seq).astype(BF16)
    bs_s = jnp.tile(b_s[0][:, :dec_t], (1, dec_b))[:, :, None]
    gln = g_sg_ln[0]
    bln = b_sg_ln[0]
    lam = (jnp.exp(jnp.sum(lam_q1[0] * lam_k1[0])) - jnp.exp(jnp.sum(lam_q2[0] * lam_k2[0]))
           + LAM_INIT).reshape(1).astype(F32)

    blk = BIAS_BLOCK
    n_dist = max(2 * blk, TAIL_TOKENS + dec_t)
    onehot = np.eye(N_BUCKETS, dtype=np.float32)[_bucket_table(n_dist)]
    ft = jnp.dot(onehot, rel_bias - rel_bias[N_BUCKETS - 1], precision=lax.Precision.HIGHEST).T
    neg = lambda n: jnp.full((N_HEADS, n), NEG, F32)
    bias_near = _toeplitz(jnp.concatenate([neg(blk - 1), ft[:, 0:blk]], axis=1), blk, blk)
    bias_next = _toeplitz(ft[:, 1:2 * blk], blk, blk)
    bias_t = jnp.stack([bias_next, bias_near], axis=1) * LOG2E
    bl = _toeplitz(jnp.flip(ft[:, 1:TAIL_TOKENS + dec_t], axis=1), dec_t, TAIL_TOKENS)
    head_eq = jnp.eye(N_HEADS, dtype=F32)
    bias_last = (bl[:, None, :, :, None] * head_eq[:, None, None, None, :])
    bias_last = jnp.broadcast_to(bias_last, (N_HEADS, 2, dec_t, TAIL_TOKENS, N_HEADS)).reshape(
        2 * N_HEADS * dec_t, TAIL_TOKENS * N_HEADS)
    bn = _toeplitz(jnp.concatenate([jnp.flip(ft[:, 0:dec_t], axis=1), neg(dec_t - 1)], axis=1),
                   dec_t, dec_t)
    bn = jnp.where(head_eq[:, None, None, :] > 0, bn[:, :, :, None], NEG)
    bn = jnp.broadcast_to(bn[:, None], (N_HEADS, 2, dec_t, dec_t, N_HEADS)).reshape(
        2 * N_HEADS * dec_t, dec_t * N_HEADS)
    bias_new = jnp.concatenate([bn, jnp.full((bn.shape[0], PAGE - bn.shape[1]), NEG, F32)], axis=1)

    c_all = jnp.concatenate([c_prompt, c_sample, jnp.zeros((4, D_MODEL), F32)], axis=0)
    mod_all = _ada(c_all, w_ada[0], b_ada[0])
    mod_p = mod_all[:batch].reshape(batch, 1, 6 * D_MODEL)
    mod_s = jnp.repeat(mod_all[batch:batch + dec_b], dec_t, axis=0).reshape(1, n_s, 6 * D_MODEL)

    xp = x_prompt.reshape(n_p, D_MODEL)
    xs = x_sample.reshape(n_s, D_MODEL)

    q_p, kf_p, kb_p, vf_p, vt_p, sg_p = _stage_a(xp, mod_p, w_in_b, ws_p, bs_p, gln, bln,
                                                 tm=TM_TOK, chunk=CHUNK, sample=False)
    q_s, kf_s, vf_s, sg_s, vsn_s = _stage_a(xs, mod_s, w_in_b, ws_s, bs_s, gln, bln,
                                            tm=n_s, chunk=n_s, sample=True)

    g_col = g_subln[0].reshape(HEAD_W, 1)
    g_row = g_subln[0].reshape(1, HEAD_W)
    o_p = _attn_prompt(lam, q_p, kb_p, vt_p, bias_t, g_col, batch=batch, seq=seq, tq=TQ_ATT, tk=TM_TOK)
    pad = ((0, 0), (0, PAGE - dec_t * N_HEADS), (0, 0))
    knew = jnp.pad(kf_s.reshape(dec_b, dec_t * N_HEADS, HEAD_W), pad).astype(BF16)
    vnew = jnp.pad(vf_s.reshape(dec_b, dec_t * N_HEADS, HEAD_W), pad).astype(BF16)
    ck = cache_k.reshape(-1, HEAD_W)
    cv = cache_v.reshape(-1, HEAD_W)
    o_s = _attn_sample(page_table.reshape(-1), lam, q_s, knew, vnew, bias_last, bias_new, g_row, ck, cv,
                       dec_b=dec_b, n_pages=n_pages)

    x1_p, h2_p, cls_p, info_p = _stage_c(o_p, sg_p, xp, mod_p, w_o_b, wr_t, br, tm=TM_TOK, sample=False)
    x1_s, h2_s, cls_s, info_s = _stage_c(o_s, sg_s, xs, mod_s, w_o_b, wr_t, br, tm=n_s, sample=True)

    tm_e = TM_E
    tm_d = TM_D
    n_max = n_tot // tm_e + N_CLASSES
    n_dt = n_tot // tm_d
    cls = jnp.concatenate([cls_p[0], cls_s[0]]).reshape(n_dt, tm_d)
    classes = jnp.arange(N_CLASSES, dtype=jnp.int32)
    seg_len = jnp.sum((cls[:, :, None] == classes).astype(jnp.int32), axis=1)
    counts = jnp.sum(seg_len, axis=0)
    ntile_c = (counts + tm_e - 1) // tm_e
    tile_end = jnp.cumsum(ntile_c)
    class_base = (tile_end - ntile_c) * tm_e
    nact = tile_end[-1]
    seg_off = class_base[None, :] + jnp.cumsum(seg_len, axis=0) - seg_len
    pad_off = class_base + counts
    pad_len = ntile_c * tm_e - counts
    tile_ids = jnp.arange(n_max, dtype=jnp.int32)
    tile_cls = jnp.sum((tile_ids[:, None] >= tile_end[None, :]).astype(jnp.int32), axis=1)
    last_cls = jnp.sum((nact - 1 >= tile_end).astype(jnp.int32))
    tile_cls = jnp.where(tile_ids < nact, tile_cls, last_cls)
    grp = tile_cls // N_PAIRS
    pidx = tile_cls % N_PAIRS
    pick = lambda table: sum(jnp.where(pidx == p, e, 0) for p, e in enumerate(table))
    tile_ea = (grp * EPG + pick(PAIR_A)).astype(jnp.int32)
    tile_eb = (grp * EPG + pick(PAIR_B)).astype(jnp.int32)
    empty = jnp.zeros((N_SUB - 1, N_CLASSES), jnp.int32)
    seg_off = jnp.concatenate([seg_off.astype(jnp.int32), empty]).reshape(-1)
    seg_len = jnp.concatenate([seg_len, empty]).reshape(-1)

    nact1 = nact.reshape(1).astype(jnp.int32)
    zero_rows = (jnp.sum(pad_len) + (n_max - nact) * tm_e).reshape(1).astype(jnp.int32)
    x_sorted = _dispatch(seg_off, seg_len, pad_off.astype(jnp.int32), pad_len.astype(jnp.int32), zero_rows,
                         nact1, h2_p, h2_s, info_p, info_s, tm=tm_d, n_sub=N_SUB, n_slots=n_max * tm_e)
    ya_sorted, yb_sorted = _moe(tile_ea, tile_eb, nact1, x_sorted, wg_b, wu_b, wd_b, tm=tm_e, n_max=n_max)

    y_p = _final(seg_off, seg_len, x1_p, info_p, mod_p, g_final, ya_sorted, yb_sorted,
                 tm=tm_d, n_sub=N_SUB, sample=False, tile_base=0)
    y_s = _final(seg_off, seg_len, x1_s, info_s, mod_s, g_final, ya_sorted, yb_sorted,
                 tm=tm_d, n_sub=1, sample=True, tile_base=n_p // tm_d)

    return (y_p.reshape(batch, seq, D_MODEL),
            y_s.reshape(dec_b, dec_t, D_MODEL),
            kf_p.reshape(batch, 1, seq, N_HEADS, HEAD_W),
            vf_p.reshape(batch, 1, seq, N_HEADS, HEAD_W),
            kf_s.reshape(dec_b, 1, dec_t, N_HEADS, HEAD_W),
            vf_s.reshape(dec_b, 1, dec_t, N_HEADS, HEAD_W),
            vsn_s.reshape(dec_b, 1, dec_t, N_GROUPS_SG, SG_CH))
```

```python
import functools
import math

import numpy as np
import jax
import jax.numpy as jnp
from jax import lax
from jax.experimental import pallas as pl
from jax.experimental.pallas import tpu as pltpu

F32 = jnp.float32
BF16 = jnp.bfloat16

D_MODEL = 1024
N_HEADS = 4
DK = 64
HEAD_W = 128
QK_W = N_HEADS * HEAD_W
N_GROUPS_SG = 4
SG_CH = 128
CHUNK = 128
PAGE = 128
N_BUCKETS = 32
MAX_DISTANCE = 128
N_EG = 4
EPG = 4
N_PAIRS = 6
N_CLASSES = N_EG * N_PAIRS
D_EXPERT = 512
EPS = 1e-6
LAM_INIT = 0.8 - 0.6 * math.exp(-0.3 * 0)
NEG = -1e30
LOG2E = math.log2(math.e)
LANES = 128
SUBLANES = 8

TM_TOK = 512
STAGE_A_PARTS = 4
TQ_ATT = 2048
PAGES_PER_STEP = 16
TM_E = 256
MOE_PARTS = 1
TM_D = 256
N_SUB = 4
VMEM_LIMIT = 56 * 1024 * 1024


def _cparams(sem):
    return pltpu.CompilerParams(dimension_semantics=sem, vmem_limit_bytes=VMEM_LIMIT)


def _ada_body(c_ref, w_ref, b_ref, o_ref):
    c = c_ref[...]
    a = (c * jax.nn.sigmoid(c)).astype(BF16)
    o_ref[...] = jnp.dot(a, w_ref[...].astype(BF16), preferred_element_type=F32) + b_ref[...]


def _ada(c_all, w_ada, b_ada):
    m = c_all.shape[0]
    n = w_ada.shape[1]
    tn = 1536
    return pl.pallas_call(
        _ada_body,
        grid=(n // tn,),
        in_specs=[pl.BlockSpec((m, D_MODEL), lambda j: (0, 0)),
                  pl.BlockSpec((D_MODEL, tn), lambda j: (0, j)),
                  pl.BlockSpec((1, tn), lambda j: (0, j))],
        out_specs=pl.BlockSpec((m, tn), lambda j: (0, j)),
        out_shape=jax.ShapeDtypeStruct((m, n), F32),
        compiler_params=_cparams(("arbitrary",)),
        name="adaln",
    )(c_all, w_ada, b_ada.reshape(1, n))


def _stage_a_body(x_ref, mod_ref, w_in_ref, ws_ref, bs_ref, gln_ref, bln_ref,
                  *out_refs, chunk, sample):
    if sample:
        q_ref, kf_ref, vf_ref, sg_ref, vsn_ref = out_refs
    else:
        q_ref, kf_ref, kb_ref, vf_ref, vt_ref, sg_ref = out_refs
    x = x_ref[...]
    tm = x.shape[0]
    mod = mod_ref[0]
    sh1 = mod[:, 0:D_MODEL]
    sc1 = mod[:, D_MODEL:2 * D_MODEL]
    ms = jnp.mean(x * x, axis=-1, keepdims=True)
    h = (x * lax.rsqrt(ms + EPS)) * (1.0 + sc1) + sh1
    hb = h.astype(BF16)
    n_parts = 1 if sample else STAGE_A_PARTS
    rows_p = tm // n_parts
    zs = [jnp.dot(hb[p * rows_p:(p + 1) * rows_p], w_in_ref[...], preferred_element_type=F32)
          for p in range(n_parts)]
    for p, z in enumerate(zs):
        a = p * rows_p
        q = z[:, 0:QK_W] * (DK ** -0.5)
        k = z[:, QK_W:2 * QK_W]
        v = z[:, 2 * QK_W:3 * QK_W]
        for hd in range(N_HEADS):
            kf_ref[pl.ds(a * N_HEADS + hd, rows_p, stride=N_HEADS), :] = k[:, hd * HEAD_W:(hd + 1) * HEAD_W]
            vf_ref[pl.ds(a * N_HEADS + hd, rows_p, stride=N_HEADS), :] = v[:, hd * HEAD_W:(hd + 1) * HEAD_W]
        if sample:
            q_ref[...] = q
        else:
            q_ref[a:a + rows_p, :] = (q * LOG2E).astype(BF16)
            kb_ref[a:a + rows_p, :] = k.astype(BF16)
            vt_ref[0, :, a:a + rows_p] = v.T.astype(BF16)
        u = z[:, 3 * QK_W:4 * QK_W]
        vs = z[:, 4 * QK_W:5 * QK_W]
        for g in range(N_GROUPS_SG):
            lo, hi = g * SG_CH, (g + 1) * SG_CH
            vg = vs[:, lo:hi]
            mu = jnp.mean(vg, axis=-1, keepdims=True)
            dv = vg - mu
            var = jnp.mean(dv * dv, axis=-1, keepdims=True)
            vn = (dv * lax.rsqrt(var + EPS)) * gln_ref[g:g + 1, :] + bln_ref[g:g + 1, :]
            if sample:
                vsn_ref[:, lo:hi] = vn
            vnb = vn.astype(BF16)
            for c in range(rows_p // chunk):
                r0, r1 = c * chunk, (c + 1) * chunk
                s = jnp.dot(ws_ref[g], vnb[r0:r1], preferred_element_type=F32) + bs_ref[g]
                sg_ref[a + r0:a + r1, lo:hi] = (u[r0:r1, lo:hi] * s).astype(BF16)


def _stage_a(x, mod, w_in_b, ws, bs, gln, bln, *, tm, chunk, sample):
    n = x.shape[0]
    nt = n // tm
    mrows = mod.shape[1]
    row = lambda w, dt: jax.ShapeDtypeStruct((n, w), dt)
    blk = lambda w: pl.BlockSpec((tm, w), lambda i: (i, 0))
    cache_shape = jax.ShapeDtypeStruct((n * N_HEADS, HEAD_W), F32)
    cache_blk = pl.BlockSpec((tm * N_HEADS, HEAD_W), lambda i: (i, 0))
    if sample:
        out_shape = [row(QK_W, F32), cache_shape, cache_shape, row(QK_W, BF16), row(QK_W, F32)]
        out_specs = [blk(QK_W), cache_blk, cache_blk, blk(QK_W), blk(QK_W)]
        mod_map = lambda i: (0, 0, 0)
    else:
        out_shape = [row(QK_W, BF16), cache_shape, row(QK_W, BF16), cache_shape,
                     jax.ShapeDtypeStruct((nt, QK_W, tm), BF16), row(QK_W, BF16)]
        out_specs = [blk(QK_W), cache_blk, blk(QK_W), cache_blk,
                     pl.BlockSpec((1, QK_W, tm), lambda i: (i, 0, 0)), blk(QK_W)]
        tiles_per_batch = 4096 // tm
        mod_map = lambda i: (i // tiles_per_batch, 0, 0)
    full = lambda a: pl.BlockSpec(a.shape, lambda i: (0,) * a.ndim)
    return pl.pallas_call(
        functools.partial(_stage_a_body, chunk=chunk, sample=sample),
        grid=(nt,),
        in_specs=[blk(D_MODEL),
                  pl.BlockSpec((1, mrows, 6 * D_MODEL), mod_map),
                  full(w_in_b), full(ws), full(bs), full(gln), full(bln)],
        out_specs=out_specs,
        out_shape=out_shape,
        compiler_params=_cparams(("arbitrary",)),
        name="stage_a_sample" if sample else "stage_a_prompt",
    )(x, mod, w_in_b, ws, bs, gln, bln)


ATT_COLS = 256
BIAS_BLOCK = MAX_DISTANCE


def _attn_body(lam_ref, q_ref, k_ref, vt_ref, bias_ref, g_ref, o_ref, *scratch, tq, tk):
    n_chain = 2 * tq // ATT_COLS
    q2_refs, m_refs, l_refs, acc_refs = (scratch[i * n_chain:(i + 1) * n_chain] for i in range(4))
    qi = pl.program_id(2)
    for c in range(n_chain):
        q0 = (c * ATT_COLS) % tq
        q = q_ref[q0:q0 + ATT_COLS, :]
        lane = lax.broadcasted_iota(jnp.int32, q.shape, 1)
        keep = (lane < DK) if c < n_chain // 2 else (lane >= DK)
        q2_refs[c][...] = jnp.where(keep, q, jnp.zeros_like(q))
        m_refs[c][...] = jnp.full(m_refs[c].shape, NEG, F32)
        l_refs[c][...] = jnp.zeros(l_refs[c].shape, F32)
        acc_refs[c][...] = jnp.zeros(acc_refs[c].shape, F32)

    blk = BIAS_BLOCK

    kblocks = tk // blk

    def block_kinds(rel, q0):
        return [[(q0 // blk + b) - (rel + a) for b in range(ATT_COLS // blk)] for a in range(kblocks)]

    def keys_needed(rel, q0):
        if rel is None:
            return kblocks
        return sum(1 for row in block_kinds(rel, q0) if max(row) >= 0)

    def with_bias(s, rel, q0):
        if rel is None:
            return s
        kinds = block_kinds(rel, q0)[:s.shape[0] // blk]
        if all(d >= 2 for row in kinds for d in row):
            return s
        nxt, near = bias_ref[0, 0], bias_ref[0, 1]
        pick = lambda d: (jnp.full((blk, blk), NEG, F32) if d < 0 else near if d == 0 else nxt if d == 1
                          else jnp.zeros((blk, blk), F32))
        rows = []
        for a, row in enumerate(kinds):
            s_row = s[a * blk:(a + 1) * blk]
            if any(d < 2 for d in row):
                s_row = s_row + jnp.concatenate([pick(d) for d in row], axis=1)
            rows.append(s_row)
        return jnp.concatenate(rows, axis=0)

    def score(j, rel, c):
        nk = keys_needed(rel, (c * ATT_COLS) % tq) * blk
        if nk == 0:
            return None
        k = k_ref[pl.ds(pl.multiple_of(j * tk, tk), nk), :]
        s = lax.dot_general(k, q2_refs[c][...], (((1,), (1,)), ((), ())),
                            preferred_element_type=F32)
        return with_bias(s, rel, (c * ATT_COLS) % tq)

    def accumulate(j, c, s):
        vt = vt_ref[j, :, 0:s.shape[0]]
        m_old = m_refs[c][...]
        m_new = jnp.maximum(m_old, jnp.max(s, axis=0, keepdims=True))
        alpha = jnp.exp2(m_old - m_new)
        p = jnp.exp2(s - m_new)
        l_refs[c][...] = alpha * l_refs[c][...] + jnp.sum(p, axis=0, keepdims=True)
        acc_refs[c][...] = alpha * acc_refs[c][...] + jnp.dot(vt, p.astype(BF16), preferred_element_type=F32)
        m_refs[c][...] = m_new

    def tiles(*work):
        scores = [[score(j, rel, c) for c in range(n_chain)] for j, rel in work]
        for (j, _), tile_scores in zip(work, scores):
            for c, s in enumerate(tile_scores):
                if s is not None:
                    accumulate(j, c, s)

    ratio = tq // tk
    first_diag = qi * ratio
    n_plain = jnp.maximum(first_diag - 1, 0)

    def plain_pair(jj, carry):
        tiles((2 * jj, None), (2 * jj + 1, None))
        return carry

    lax.fori_loop(0, n_plain // 2, plain_pair, 0)

    @pl.when(qi >= 1)
    def _():
        if ratio % 2 == 0:
            tiles((first_diag - 2, None), (first_diag - 1, -kblocks))
        else:
            @pl.when(n_plain % 2 == 1)
            def _():
                tiles((n_plain - 1, None))
            tiles((first_diag - 1, -kblocks))

    tiles(*[(first_diag + r, r * kblocks) for r in range(ratio)])

    lam = lam_ref[0]
    o_all = jnp.concatenate([acc_refs[c][...] * (1.0 / l_refs[c][...]) for c in range(n_chain)],
                            axis=1)
    o = o_all[:, 0:tq] - lam * o_all[:, tq:2 * tq]
    ms = jnp.mean(o * o, axis=0, keepdims=True)
    on = (o * lax.rsqrt(ms + EPS)) * g_ref[...] * (1.0 - LAM_INIT)
    o_ref[...] = on.T.astype(BF16)


def _attn_prompt(lam, q, kb, vt, bias_t, g_col, *, batch, seq, tq, tk):
    nq = seq // tq
    nk = seq // tk
    n = batch * seq
    n_chain = 2 * tq // ATT_COLS
    assert vt.shape == (batch * nk, QK_W, tk) and tq % tk == 0 and tk % BIAS_BLOCK == 0
    return pl.pallas_call(
        functools.partial(_attn_body, tq=tq, tk=tk),
        grid=(batch, N_HEADS, nq),
        in_specs=[pl.BlockSpec(memory_space=pltpu.SMEM),
                  pl.BlockSpec((tq, HEAD_W), lambda b, h, i: (b * nq + i, h)),
                  pl.BlockSpec((seq, HEAD_W), lambda b, h, i: (b, h)),
                  pl.BlockSpec((nk, HEAD_W, tk), lambda b, h, i: (b, h, 0)),
                  pl.BlockSpec((1, 2, BIAS_BLOCK, BIAS_BLOCK), lambda b, h, i: (h, 0, 0, 0)),
                  pl.BlockSpec((HEAD_W, 1), lambda b, h, i: (0, 0))],
        out_specs=pl.BlockSpec((tq, HEAD_W), lambda b, h, i: (b * nq + i, h)),
        out_shape=jax.ShapeDtypeStruct((n, QK_W), BF16),
        scratch_shapes=([pltpu.VMEM((ATT_COLS, HEAD_W), BF16)] * n_chain
                        + [pltpu.VMEM((1, ATT_COLS), F32)] * (2 * n_chain)
                        + [pltpu.VMEM((HEAD_W, ATT_COLS), F32)] * n_chain),
        compiler_params=_cparams(("arbitrary", "arbitrary", "arbitrary")),
        name="attn_prompt",
    )(lam, q, kb, vt, bias_t, g_col)


PAGE_ROWS = PAGE * N_HEADS
TAIL_TOKENS = 2 * PAGE


def _sattn_body(pt_ref, lam_ref, q_ref, knew_ref, vnew_ref, bl_ref, bn_ref, g_ref, ck_ref, cv_ref,
                o_ref, kbuf, vbuf, sem, mask_ref, m_ref, l_ref, acc_ref, *, pages, n_steps, total):
    b = pl.program_id(0)
    s = pl.program_id(1)
    step = b * n_steps + s
    slot = step % 2

    def page_copies(step_idx, sl):
        base = step_idx * pages
        out = []
        for i in range(pages):
            src = pl.ds(pl.multiple_of(pt_ref[base + i] * PAGE_ROWS, PAGE_ROWS), PAGE_ROWS)
            dst = pl.ds(i * PAGE_ROWS, PAGE_ROWS)
            out.append(pltpu.make_async_copy(ck_ref.at[src], kbuf.at[sl, dst], sem.at[sl, 0]))
            out.append(pltpu.make_async_copy(cv_ref.at[src], vbuf.at[sl, dst], sem.at[sl, 1]))
        return out

    def start_all(copies):
        for n, c in enumerate(copies):
            c.start(priority=n % 2)

    @pl.when(step == 0)
    def _():
        start_all(page_copies(0, 0))

    @pl.when(step + 1 < total)
    def _():
        start_all(page_copies(step + 1, 1 - slot))

    for c in page_copies(step, slot):
        c.wait()

    @pl.when(s == 0)
    def _():
        m_ref[...] = jnp.full(m_ref.shape, NEG, F32)
        l_ref[...] = jnp.zeros(l_ref.shape, F32)
        acc_ref[...] = jnp.zeros(acc_ref.shape, F32)

    @pl.when(step == 0)
    def _():
        row = lax.broadcasted_iota(jnp.int32, mask_ref.shape, 0)
        col = lax.broadcasted_iota(jnp.int32, mask_ref.shape, 1)
        same_head = (col % N_HEADS) == (row // (2 * SUBLANES))
        mask_ref[...] = jnp.where(same_head, 0.0, NEG)

    q = q_ref[...]
    lane = lax.broadcasted_iota(jnp.int32, (SUBLANES, HEAD_W), 1)
    pieces = []
    for h in range(N_HEADS):
        qh = q[:, h * HEAD_W:(h + 1) * HEAD_W]
        pieces += [jnp.where(lane < DK, qh, 0.0), jnp.where(lane >= DK, qh, 0.0)]
    qm = jnp.concatenate(pieces, axis=0).astype(BF16)

    def update(kb, vb, bias):
        sc = lax.dot_general(qm, kb, (((1,), (1,)), ((), ())),
                             preferred_element_type=F32) + bias
        m_old = m_ref[...]
        m_new = jnp.maximum(m_old, jnp.max(sc, axis=1, keepdims=True))
        alpha = jnp.exp(m_old - m_new)
        p = jnp.exp(sc - m_new)
        l_ref[...] = alpha * l_ref[...] + jnp.sum(p, axis=1, keepdims=True)
        acc_ref[...] = alpha * acc_ref[...] + jnp.dot(p.astype(BF16), vb,
                                                      preferred_element_type=F32)
        m_ref[...] = m_new

    is_last = s == n_steps - 1
    head_cols = mask_ref.shape[1] - bl_ref.shape[1]
    update(kbuf[slot].astype(BF16), vbuf[slot].astype(BF16),
           jnp.concatenate([mask_ref[:, :head_cols],
                            mask_ref[:, head_cols:] + bl_ref[...] * is_last.astype(F32)], axis=1))

    @pl.when(is_last)
    def _():
        update(knew_ref[0], vnew_ref[0], bn_ref[...])
        lam = lam_ref[0]
        o_all = acc_ref[...] * (1.0 / l_ref[...])
        for h in range(N_HEADS):
            r = h * 2 * SUBLANES
            o = o_all[r:r + SUBLANES] - lam * o_all[r + SUBLANES:r + 2 * SUBLANES]
            ms = jnp.mean(o * o, axis=-1, keepdims=True)
            o_ref[:, h * HEAD_W:(h + 1) * HEAD_W] = ((o * lax.rsqrt(ms + EPS)) * g_ref[...]
                                                     * (1.0 - LAM_INIT))


def _attn_sample(page_table_flat, lam, q_s, knew, vnew, bias_last, bias_new, g_row, cache_k, cache_v,
                 *, dec_b, n_pages):
    pages = PAGES_PER_STEP
    n_steps = n_pages // pages
    total = dec_b * n_steps
    nq = q_s.shape[0] // dec_b
    n_rows = 2 * N_HEADS * nq
    step_rows = pages * PAGE_ROWS
    grid_spec = pltpu.PrefetchScalarGridSpec(
        num_scalar_prefetch=1,
        grid=(dec_b, n_steps),
        in_specs=[pl.BlockSpec(memory_space=pltpu.SMEM),
                  pl.BlockSpec((nq, QK_W), lambda b, s, pt: (b, 0)),
                  pl.BlockSpec((1, PAGE, HEAD_W), lambda b, s, pt: (b, 0, 0)),
                  pl.BlockSpec((1, PAGE, HEAD_W), lambda b, s, pt: (b, 0, 0)),
                  pl.BlockSpec(bias_last.shape, lambda b, s, pt: (0, 0)),
                  pl.BlockSpec(bias_new.shape, lambda b, s, pt: (0, 0)),
                  pl.BlockSpec((1, HEAD_W), lambda b, s, pt: (0, 0)),
                  pl.BlockSpec(memory_space=pl.ANY),
                  pl.BlockSpec(memory_space=pl.ANY)],
        out_specs=pl.BlockSpec((nq, QK_W), lambda b, s, pt: (b, 0)),
        scratch_shapes=[pltpu.VMEM((2, step_rows, HEAD_W), F32),
                        pltpu.VMEM((2, step_rows, HEAD_W), F32),
                        pltpu.SemaphoreType.DMA((2, 2)),
                        pltpu.VMEM((n_rows, step_rows), F32),
                        pltpu.VMEM((n_rows, 1), F32), pltpu.VMEM((n_rows, 1), F32),
                        pltpu.VMEM((n_rows, HEAD_W), F32)])
    return pl.pallas_call(
        functools.partial(_sattn_body, pages=pages, n_steps=n_steps, total=total),
        grid_spec=grid_spec,
        out_shape=jax.ShapeDtypeStruct(q_s.shape, F32),
        compiler_params=_cparams(("arbitrary", "arbitrary")),
        name="attn_sample",
    )(page_table_flat, lam, q_s, knew, vnew, bias_last, bias_new, g_row, cache_k, cache_v)


INFO_W_A, INFO_W_B, INFO_CLS = 0, 1, 2
PAIR_A = (0, 0, 0, 1, 1, 3)
PAIR_B = (1, 2, 3, 3, 2, 2)


def _stage_c_body(o_ref, sg_ref, x_ref, mod_ref, wo_ref, wr_ref, br_ref, x1_ref, h2_ref, cls_ref, info_ref):
    x = x_ref[...]
    tm = x.shape[0]
    mod = mod_ref[0]
    g1 = mod[:, 2 * D_MODEL:3 * D_MODEL]
    sh2 = mod[:, 3 * D_MODEL:4 * D_MODEL]
    sc2 = mod[:, 4 * D_MODEL:5 * D_MODEL]
    mix = (jnp.dot(o_ref[...].astype(BF16), wo_ref[0:QK_W, :], preferred_element_type=F32)
           + jnp.dot(sg_ref[...], wo_ref[QK_W:2 * QK_W, :], preferred_element_type=F32))
    x1 = x + g1 * mix
    x1_ref[...] = x1
    ms = jnp.mean(x1 * x1, axis=-1, keepdims=True)
    h2 = ((x1 * lax.rsqrt(ms + EPS)) * (1.0 + sc2) + sh2).astype(BF16)
    h2_ref[...] = h2
    lg = lax.dot_general(wr_ref[...], h2, (((1,), (1,)), ((), ())),
                         preferred_element_type=F32) + br_ref[...]
    gl = [lg[i:i + 1, :] for i in range(N_EG)]
    el = [lg[N_EG + i:N_EG + i + 1, :] for i in range(N_EG * EPG)]
    gmax = jnp.maximum(jnp.maximum(gl[0], gl[1]), jnp.maximum(gl[2], gl[3]))
    gi = jnp.where(gl[0] == gmax, 0, jnp.where(gl[1] == gmax, 1, jnp.where(gl[2] == gmax, 2, 3)))
    gsum = (jnp.exp(gl[0] - gmax) + jnp.exp(gl[1] - gmax)
            + jnp.exp(gl[2] - gmax) + jnp.exp(gl[3] - gmax))
    gp = 1.0 / gsum
    sel = [jnp.where(gi == 0, el[j], jnp.where(gi == 1, el[EPG + j],
                                               jnp.where(gi == 2, el[2 * EPG + j], el[3 * EPG + j])))
           for j in range(EPG)]
    v0 = jnp.maximum(jnp.maximum(sel[0], sel[1]), jnp.maximum(sel[2], sel[3]))
    i0 = jnp.where(sel[0] == v0, 0, jnp.where(sel[1] == v0, 1, jnp.where(sel[2] == v0, 2, 3)))
    rest = [jnp.where(i0 == j, -3e38, sel[j]) for j in range(EPG)]
    v1 = jnp.maximum(jnp.maximum(rest[0], rest[1]), jnp.maximum(rest[2], rest[3]))
    i1 = jnp.where(rest[0] == v1, 0, jnp.where(rest[1] == v1, 1, jnp.where(rest[2] == v1, 2, 3)))
    e1 = jnp.exp(v1 - v0)
    den = 1.0 / (1.0 + e1)
    tw0 = den * gp
    tw1 = e1 * den * gp
    first_low = i0 < i1
    lo = jnp.where(first_low, i0, i1)
    hi = jnp.where(first_low, i1, i0)
    w_lo = jnp.where(first_low, tw0, tw1)
    w_hi = jnp.where(first_low, tw1, tw0)
    pair = jnp.where(lo == 0, hi - 1, jnp.where(lo == 2, 5, jnp.where(hi == 3, 3, 4)))
    swapped = pair == 5
    w_a = jnp.where(swapped, w_hi, w_lo)
    w_b = jnp.where(swapped, w_lo, w_hi)
    cls = gi * N_PAIRS + pair
    cls_ref[...] = jnp.broadcast_to(cls, cls_ref.shape).astype(jnp.int32)
    row = lax.broadcasted_iota(jnp.int32, (LANES, tm), 0)
    rec = jnp.where(row == INFO_W_A, w_a, jnp.where(row == INFO_W_B, w_b,
                                                    jnp.where(row == INFO_CLS, cls.astype(F32), 0.0)))
    info_ref[...] = rec.T


def _stage_c(o, sg, x, mod, wo_b, wr_t, br, *, tm, sample):
    n = x.shape[0]
    nt = n // tm
    mrows = mod.shape[1]
    blk = lambda w: pl.BlockSpec((tm, w), lambda i: (i, 0))
    full = lambda a: pl.BlockSpec(a.shape, lambda i: (0,) * a.ndim)
    if sample:
        mod_map = lambda i: (0, 0, 0)
    else:
        tiles_per_batch = 4096 // tm
        mod_map = lambda i: (i // tiles_per_batch, 0, 0)
    return pl.pallas_call(
        _stage_c_body,
        grid=(nt,),
        in_specs=[blk(QK_W), blk(QK_W), blk(D_MODEL),
                  pl.BlockSpec((1, mrows, 6 * D_MODEL), mod_map),
                  full(wo_b), full(wr_t), full(br)],
        out_specs=[blk(D_MODEL), blk(D_MODEL),
                   pl.BlockSpec((SUBLANES, tm), lambda i: (0, i)),
                   blk(LANES)],
        out_shape=[jax.ShapeDtypeStruct((n, D_MODEL), F32),
                   jax.ShapeDtypeStruct((n, D_MODEL), BF16),
                   jax.ShapeDtypeStruct((SUBLANES, n), jnp.int32),
                   jax.ShapeDtypeStruct((n, LANES), F32)],
        compiler_params=_cparams(("arbitrary",)),
        name="stage_c_sample" if sample else "stage_c_prompt",
    )(o, sg, x, mod, wo_b, wr_t, br)


def rows8(ref, start, count):
    scale = lambda v: v * SUBLANES if isinstance(v, int) else pl.multiple_of(v * SUBLANES, SUBLANES)
    return ref.at[pl.ds(scale(start), scale(count))]


def _perm_t(cls_col):
    n = cls_col.shape[0]
    lane = lax.broadcasted_iota(jnp.int32, (n, LANES), 1).astype(F32)
    onehot = (lane == cls_col).astype(BF16)
    r = lax.broadcasted_iota(jnp.int32, (n, n), 0)
    c = lax.broadcasted_iota(jnp.int32, (n, n), 1)
    before = (c < r).astype(BF16)
    rank = jnp.dot(before, onehot, preferred_element_type=F32)
    cnt = jnp.sum(onehot.astype(F32), axis=0, keepdims=True)
    cr = lax.broadcasted_iota(jnp.int32, (LANES, LANES), 0)
    cc = lax.broadcasted_iota(jnp.int32, (LANES, LANES), 1)
    lower_cls = (cr < cc).astype(BF16)
    base = jnp.dot(jnp.broadcast_to(cnt, (SUBLANES, LANES)).astype(BF16), lower_cls,
                   preferred_element_type=F32)[0:1, :]
    pos = jnp.sum(onehot.astype(F32) * (base + rank), axis=1, keepdims=True)
    dest = lax.broadcasted_iota(jnp.int32, (n, n), 1).astype(F32)
    return (dest == pos).astype(F32)


def _dispatch_body(soff_ref, slen_ref, poff_ref, plen_ref, ptot_ref, nact_ref, hp_ref, hs_ref, ip_ref, is_ref,
                   xs_ref, buf, zbuf, sem, zsem, *, tm, n_sub, n_tiles, n_slab_tiles):
    i = pl.program_id(0)
    slot = i % 2
    is_sample = i == n_tiles - 1

    def wait_tile(sl, tokens):
        pltpu.make_async_copy(rows8(buf.at[sl], 0, tokens), rows8(buf.at[sl], 0, tokens), sem.at[sl]).wait()

    @pl.when(i == 0)
    def _():
        zbuf[...] = jnp.zeros(zbuf.shape, F32)
        for c in range(N_CLASSES):
            @pl.when(plen_ref[c] > 0)
            def _():
                pltpu.make_async_copy(rows8(zbuf, 0, plen_ref[c]), rows8(xs_ref, poff_ref[c], plen_ref[c]),
                                      zsem).start()
        for j in range(n_slab_tiles - N_CLASSES, n_slab_tiles):
            @pl.when(j >= nact_ref[0])
            def _():
                pltpu.make_async_copy(zbuf, rows8(xs_ref, j * TM_E, TM_E), zsem).start()

    xp = []
    for u in range(n_sub):
        rows = slice(u * tm, (u + 1) * tm)
        x = hp_ref[rows, :]
        info = ip_ref[rows, :]
        if u == 0:
            x = jnp.where(is_sample, hs_ref[...], x)
            info = jnp.where(is_sample, is_ref[...], info)
        perm = _perm_t(info[:, INFO_CLS:INFO_CLS + 1]).T.astype(BF16)
        xp.append(jnp.dot(perm, x, preferred_element_type=F32))

    @pl.when(i >= 2)
    def _():
        wait_tile(slot, n_sub * tm)

    bs = buf.at[slot]
    for u in range(n_sub):
        for c in range(D_MODEL // LANES):
            bs[pl.ds(u * tm * SUBLANES + c, tm, stride=SUBLANES), :] = xp[u][:, c * LANES:(c + 1) * LANES]
    for u in range(n_sub):
        local = u * tm
        for c in range(N_CLASSES):
            k = (i * n_sub + u) * N_CLASSES + c
            n_rows = slen_ref[k]

            @pl.when(n_rows > 0)
            def _():
                pltpu.make_async_copy(rows8(bs, local, n_rows), rows8(xs_ref, soff_ref[k], n_rows),
                                      sem.at[slot]).start()
            local = local + n_rows

    @pl.when(i == n_tiles - 1)
    def _():
        wait_tile(slot, tm)
        if n_tiles >= 2:
            wait_tile(1 - slot, n_sub * tm)

        @pl.when(ptot_ref[0] > 0)
        def _():
            n = pl.multiple_of(ptot_ref[0] * SUBLANES, SUBLANES)
            pltpu.make_async_copy(xs_ref.at[pl.ds(0, n)], xs_ref.at[pl.ds(0, n)], zsem).wait()


def _dispatch(seg_off, seg_len, pad_off, pad_len, pad_tot, nact, h2_p, h2_s, info_p, info_s, *, tm, n_sub, n_slots):
    n_prompt_steps = h2_p.shape[0] // (n_sub * tm)
    n_tiles = n_prompt_steps + 1
    assert h2_s.shape[0] == tm and h2_p.shape[0] % (n_sub * tm) == 0
    assert seg_len.shape[0] == n_tiles * n_sub * N_CLASSES
    last_p = n_prompt_steps - 1
    grid_spec = pltpu.PrefetchScalarGridSpec(
        num_scalar_prefetch=6,
        grid=(n_tiles,),
        in_specs=[pl.BlockSpec((n_sub * tm, D_MODEL), lambda i, *_: (jnp.minimum(i, last_p), 0)),
                  pl.BlockSpec((tm, D_MODEL), lambda i, *_: (0, 0)),
                  pl.BlockSpec((n_sub * tm, LANES), lambda i, *_: (jnp.minimum(i, last_p), 0)),
                  pl.BlockSpec((tm, LANES), lambda i, *_: (0, 0))],
        out_specs=pl.BlockSpec(memory_space=pl.ANY),
        scratch_shapes=[pltpu.VMEM((2, n_sub * tm * SUBLANES, LANES), F32),
                        pltpu.VMEM((TM_E * SUBLANES, LANES), F32),
                        pltpu.SemaphoreType.DMA((2,)),
                        pltpu.SemaphoreType.DMA(())])
    return pl.pallas_call(
        functools.partial(_dispatch_body, tm=tm, n_sub=n_sub, n_tiles=n_tiles, n_slab_tiles=n_slots // TM_E),
        grid_spec=grid_spec,
        out_shape=jax.ShapeDtypeStruct((n_slots * SUBLANES, LANES), F32),
        compiler_params=_cparams(("arbitrary",)),
        name="moe_dispatch",
    )(seg_off, seg_len, pad_off, pad_len, pad_tot, nact, h2_p, h2_s, info_p, info_s)


def _moe_body(ea_ref, eb_ref, nact_ref, x_ref, wga_ref, wgb_ref, wua_ref, wub_ref, wda_ref, wdb_ref,
              ya_ref, yb_ref, *, tm):
    i = pl.program_id(0)

    @pl.when(i < nact_ref[0])
    def _():
        x = jnp.concatenate([x_ref[pl.ds(c, tm, stride=SUBLANES), :] for c in range(D_MODEL // LANES)],
                            axis=1).astype(BF16)

        wts = [r[0].astype(BF16) for r in (wga_ref, wua_ref, wda_ref, wgb_ref, wub_ref, wdb_ref)]
        rows_p = tm // MOE_PARTS

        def hidden(xp, wg, wu):
            return (jnp.dot(xp, wg, preferred_element_type=F32), jnp.dot(xp, wu, preferred_element_type=F32))

        def down(gate_up, wd):
            gate, up = gate_up
            he = (gate * jax.nn.sigmoid(gate)) * up
            return jnp.dot(he.astype(BF16), wd, preferred_element_type=F32)

        hs = []
        for p in range(MOE_PARTS):
            xp = x[p * rows_p:(p + 1) * rows_p]
            hs.append((hidden(xp, wts[0], wts[1]), hidden(xp, wts[3], wts[4])))
        for p, (ha, hb) in enumerate(hs):
            for y_ref, h, wd in ((ya_ref, ha, wts[2]), (yb_ref, hb, wts[5])):
                y = down(h, wd)
                for c in range(D_MODEL // LANES):
                    y_ref[pl.ds(p * rows_p * SUBLANES + c, rows_p, stride=SUBLANES), :] = (
                        y[:, c * LANES:(c + 1) * LANES])

    @pl.when(i >= nact_ref[0])
    def _():
        ya_ref[...] = jnp.zeros(ya_ref.shape, ya_ref.dtype)
        yb_ref[...] = jnp.zeros(yb_ref.shape, yb_ref.dtype)


def _moe(tile_ea, tile_eb, nact, x_sorted, wg_b, wu_b, wd_b, *, tm, n_max):
    wspec_in = lambda sel: pl.BlockSpec((1, D_MODEL, D_EXPERT), sel)
    wspec_out = lambda sel: pl.BlockSpec((1, D_EXPERT, D_MODEL), sel)
    sel_a = lambda i, ea, eb, na: (ea[i], 0, 0)
    sel_b = lambda i, ea, eb, na: (eb[i], 0, 0)
    rows_in = lambda i, ea, eb, na: (jnp.minimum(i, na[0] - 1), 0)
    grid_spec = pltpu.PrefetchScalarGridSpec(
        num_scalar_prefetch=3,
        grid=(n_max,),
        in_specs=[pl.BlockSpec((tm * SUBLANES, LANES), rows_in),
                  wspec_in(sel_a), wspec_in(sel_b), wspec_in(sel_a), wspec_in(sel_b),
                  wspec_out(sel_a), wspec_out(sel_b)],
        out_specs=[pl.BlockSpec((tm * SUBLANES, LANES), lambda i, ea, eb, na: (i, 0))] * 2)
    return pl.pallas_call(
        functools.partial(_moe_body, tm=tm),
        grid_spec=grid_spec,
        out_shape=[jax.ShapeDtypeStruct(x_sorted.shape, F32)] * 2,
        compiler_params=_cparams(("arbitrary",)),
        name="moe",
    )(tile_ea, tile_eb, nact, x_sorted, wg_b, wg_b, wu_b, wu_b, wd_b, wd_b)


def _final_body(soff_ref, slen_ref, x1_ref, info_ref, mod_ref, gf_ref, ya_ref, yb_ref, y_ref, buf, sem,
                *, tm, n_sub, n_tiles, tile_base):
    i = pl.program_id(0)
    slot = i % 2
    slabs = (ya_ref, yb_ref)

    def fetch(step, sl):
        for u in range(n_sub):
            local = u * tm
            for c in range(N_CLASSES):
                k = (step * n_sub + u + tile_base) * N_CLASSES + c
                n_rows = slen_ref[k]

                @pl.when(n_rows > 0)
                def _():
                    for e in range(2):
                        pltpu.make_async_copy(rows8(slabs[e], soff_ref[k], n_rows),
                                              rows8(buf.at[sl, e], local, n_rows), sem.at[sl, e]).start()
                local = local + n_rows

    @pl.when(i == 0)
    def _():
        fetch(0, 0)

    @pl.when(i + 1 < n_tiles)
    def _():
        fetch(i + 1, 1 - slot)

    info = info_ref[...]
    perm_t = [_perm_t(info[u * tm:(u + 1) * tm, INFO_CLS:INFO_CLS + 1]).astype(BF16) for u in range(n_sub)]
    moe = None
    for e, lane_w in enumerate((INFO_W_A, INFO_W_B)):
        pltpu.make_async_copy(buf.at[slot, e], buf.at[slot, e], sem.at[slot, e]).wait()
        bs = buf.at[slot, e]
        parts = []
        for u in range(n_sub):
            ye = jnp.concatenate([bs[pl.ds(u * tm * SUBLANES + c, tm, stride=SUBLANES), :]
                                  for c in range(D_MODEL // LANES)], axis=1)
            parts.append(jnp.dot(perm_t[u], ye.astype(BF16), preferred_element_type=F32))
        term = info[:, lane_w:lane_w + 1] * jnp.concatenate(parts, axis=0)
        moe = term if moe is None else moe + term
    x1 = x1_ref[...]
    g2 = mod_ref[0][:, 5 * D_MODEL:6 * D_MODEL]
    x2 = x1 + g2 * moe
    ms = jnp.mean(x2 * x2, axis=-1, keepdims=True)
    y_ref[...] = (x2 * lax.rsqrt(ms + EPS)) * gf_ref[...]


def _final(seg_off, seg_len, x1, info, mod, g_final, ya_sorted, yb_sorted, *, tm, n_sub, sample, tile_base):
    n = x1.shape[0]
    rows = n_sub * tm
    nt = n // rows
    mrows = mod.shape[1]
    if sample:
        mod_map = lambda i, *_: (0, 0, 0)
    else:
        tiles_per_batch = 4096 // rows
        mod_map = lambda i, *_: (i // tiles_per_batch, 0, 0)
    grid_spec = pltpu.PrefetchScalarGridSpec(
        num_scalar_prefetch=2,
        grid=(nt,),
        in_specs=[pl.BlockSpec((rows, D_MODEL), lambda i, *_: (i, 0)),
                  pl.BlockSpec((rows, LANES), lambda i, *_: (i, 0)),
                  pl.BlockSpec((1, mrows, 6 * D_MODEL), mod_map),
                  pl.BlockSpec((1, D_MODEL), lambda i, *_: (0, 0)),
                  pl.BlockSpec(memory_space=pl.ANY),
                  pl.BlockSpec(memory_space=pl.ANY)],
        out_specs=pl.BlockSpec((rows, D_MODEL), lambda i, *_: (i, 0)),
        scratch_shapes=[pltpu.VMEM((2, 2, rows * SUBLANES, LANES), F32),
                        pltpu.SemaphoreType.DMA((2, 2))])
    return pl.pallas_call(
        functools.partial(_final_body, tm=tm, n_sub=n_sub, n_tiles=nt, tile_base=tile_base),
        grid_spec=grid_spec,
        out_shape=jax.ShapeDtypeStruct((n, D_MODEL), F32),
        compiler_params=_cparams(("arbitrary",)),
        name="final_sample" if sample else "final_prompt",
    )(seg_off, seg_len, x1, info, mod, g_final.reshape(1, D_MODEL), ya_sorted, yb_sorted)


def _bucket_table(n):
    d = np.arange(n)
    max_exact = N_BUCKETS // 2
    nf = np.maximum(d, 1).astype(np.float64)
    large = max_exact + (np.log(nf / max_exact) / math.log(MAX_DISTANCE / max_exact)
                         * (N_BUCKETS - max_exact)).astype(np.int64)
    large = np.minimum(large, N_BUCKETS - 1)
    return np.where(d < max_exact, d, large).astype(np.int32)


def _toeplitz(v, n_rows, n_cols):
    length = n_rows + n_cols - 1
    lead = v.shape[:-1]
    vp = jnp.concatenate([v, jnp.zeros(lead + (1,), v.dtype)], axis=-1)
    skew = jnp.tile(vp, (1,) * len(lead) + (n_rows,))[..., :n_rows * length].reshape(lead + (n_rows, length))
    return skew[..., n_rows - 1:n_rows - 1 + n_cols]


def kernel(x_prompt, x_sample, c_prompt, c_sample, cache_k, cache_v, page_table, w_ada, b_ada, w_in, w_o,
           lam_q1, lam_k1, lam_q2, lam_k2, g_subln, rel_bias, g_sg_ln, b_sg_ln, w_s, b_s, w_rg, b_rg,
           w_re, b_re, w_gate, w_up, w_down, g_final):
    batch, seq, _ = x_prompt.shape
    dec_b, dec_t, _ = x_sample.shape
    n_pages = page_table.shape[1]
    n_p = batch * seq
    n_s = dec_b * dec_t
    n_tot = n_p + n_s
    assert w_in.shape[0] == 1 and cache_k.shape[1] == 1 and seq % TQ_ATT == 0 and n_pages % PAGES_PER_STEP == 0
    assert n_p % TM_TOK == 0 and n_p % n_s == 0 and n_tot % TM_E == 0 and dec_t == SUBLANES
    assert TAIL_TOKENS >= MAX_DISTANCE + dec_t and TAIL_TOKENS <= PAGES_PER_STEP * PAGE
    assert n_s == TM_D and n_p % TM_D == 0 and N_CLASSES <= LANES

    w_in_b = w_in[0].astype(BF16)
    w_o_b = w_o[0].astype(BF16)
    wr_t = jnp.zeros((32, D_MODEL), F32).at[0:N_EG].set(w_rg[0].T).at[N_EG:N_EG + N_EG * EPG].set(w_re[0].T)
    wr_t = wr_t.astype(BF16)
    br = jnp.zeros((32, 1), F32).at[0:N_EG, 0].set(b_rg[0]).at[N_EG:N_EG + N_EG * EPG, 0].set(b_re[0])
    wg_b, wu_b, wd_b = w_gate[0], w_up[0], w_down[0]
    ws_tril = jnp.tril(w_s[0])
    ws_p = ws_tril.astype(BF16)
    bs_p = b_s[0][:, :, None]
    same_seq = np.kron(np.eye(dec_b, dtype=np.float32), np.ones((dec_t, dec_t), np.float32))
    rep = np.tile(np.eye(dec_t, dtype=np.float32), (dec_b, 1))
    ws_rep = jnp.einsum('ri,gij,cj->grc', rep, ws_tril[:, :dec_t, :dec_t], rep,
                        precision=lax.Precision.HIGHEST)
    ws_s = (ws_rep * same_seq).astype(BF16)
    bs_s = jnp.tile(b_s[0][:, :dec_t], (1, dec_b))[:, :, None]
    gln = g_sg_ln[0]
    bln = b_sg_ln[0]
    lam = (jnp.exp(jnp.sum(lam_q1[0] * lam_k1[0])) - jnp.exp(jnp.sum(lam_q2[0] * lam_k2[0]))
           + LAM_INIT).reshape(1).astype(F32)

    blk = BIAS_BLOCK
    n_dist = max(2 * blk, TAIL_TOKENS + dec_t)
    onehot = np.eye(N_BUCKETS, dtype=np.float32)[_bucket_table(n_dist)]
    ft = jnp.dot(onehot, rel_bias - rel_bias[N_BUCKETS - 1], precision=lax.Precision.HIGHEST).T
    neg = lambda n: jnp.full((N_HEADS, n), NEG, F32)
    bias_near = _toeplitz(jnp.concatenate([neg(blk - 1), ft[:, 0:blk]], axis=1), blk, blk)
    bias_next = _toeplitz(ft[:, 1:2 * blk], blk, blk)
    bias_t = jnp.stack([bias_next, bias_near], axis=1) * LOG2E
    bl = _toeplitz(jnp.flip(ft[:, 1:TAIL_TOKENS + dec_t], axis=1), dec_t, TAIL_TOKENS)
    head_eq = jnp.eye(N_HEADS, dtype=F32)
    bias_last = (bl[:, None, :, :, None] * head_eq[:, None, None, None, :])
    bias_last = jnp.broadcast_to(bias_last, (N_HEADS, 2, dec_t, TAIL_TOKENS, N_HEADS)).reshape(
        2 * N_HEADS * dec_t, TAIL_TOKENS * N_HEADS)
    bn = _toeplitz(jnp.concatenate([jnp.flip(ft[:, 0:dec_t], axis=1), neg(dec_t - 1)], axis=1),
                   dec_t, dec_t)
    bn = jnp.where(head_eq[:, None, None, :] > 0, bn[:, :, :, None], NEG)
    bn = jnp.broadcast_to(bn[:, None], (N_HEADS, 2, dec_t, dec_t, N_HEADS)).reshape(
        2 * N_HEADS * dec_t, dec_t * N_HEADS)
    bias_new = jnp.concatenate([bn, jnp.full((bn.shape[0], PAGE - bn.shape[1]), NEG, F32)], axis=1)

    c_all = jnp.concatenate([c_prompt, c_sample, jnp.zeros((4, D_MODEL), F32)], axis=0)
    mod_all = _ada(c_all, w_ada[0], b_ada[0])
    mod_p = mod_all[:batch].reshape(batch, 1, 6 * D_MODEL)
    mod_s = jnp.repeat(mod_all[batch:batch + dec_b], dec_t, axis=0).reshape(1, n_s, 6 * D_MODEL)

    xp = x_prompt.reshape(n_p, D_MODEL)
    xs = x_sample.reshape(n_s, D_MODEL)

    q_p, kf_p, kb_p, vf_p, vt_p, sg_p = _stage_a(xp, mod_p, w_in_b, ws_p, bs_p, gln, bln,
                                                 tm=TM_TOK, chunk=CHUNK, sample=False)
    q_s, kf_s, vf_s, sg_s, vsn_s = _stage_a(xs, mod_s, w_in_b, ws_s, bs_s, gln, bln,
                                            tm=n_s, chunk=n_s, sample=True)

    g_col = g_subln[0].reshape(HEAD_W, 1)
    g_row = g_subln[0].reshape(1, HEAD_W)
    o_p = _attn_prompt(lam, q_p, kb_p, vt_p, bias_t, g_col, batch=batch, seq=seq, tq=TQ_ATT, tk=TM_TOK)
    pad = ((0, 0), (0, PAGE - dec_t * N_HEADS), (0, 0))
    knew = jnp.pad(kf_s.reshape(dec_b, dec_t * N_HEADS, HEAD_W), pad).astype(BF16)
    vnew = jnp.pad(vf_s.reshape(dec_b, dec_t * N_HEADS, HEAD_W), pad).astype(BF16)
    ck = cache_k.reshape(-1, HEAD_W)
    cv = cache_v.reshape(-1, HEAD_W)
    o_s = _attn_sample(page_table.reshape(-1), lam, q_s, knew, vnew, bias_last, bias_new, g_row, ck, cv,
                       dec_b=dec_b, n_pages=n_pages)

    x1_p, h2_p, cls_p, info_p = _stage_c(o_p, sg_p, xp, mod_p, w_o_b, wr_t, br, tm=TM_TOK, sample=False)
    x1_s, h2_s, cls_s, info_s = _stage_c(o_s, sg_s, xs, mod_s, w_o_b, wr_t, br, tm=n_s, sample=True)

    tm_e = TM_E
    tm_d = TM_D
    n_max = n_tot // tm_e + N_CLASSES
    n_dt = n_tot // tm_d
    cls = jnp.concatenate([cls_p[0], cls_s[0]]).reshape(n_dt, tm_d)
    classes = jnp.arange(N_CLASSES, dtype=jnp.int32)
    seg_len = jnp.sum((cls[:, :, None] == classes).astype(jnp.int32), axis=1)
    counts = jnp.sum(seg_len, axis=0)
    ntile_c = (counts + tm_e - 1) // tm_e
    tile_end = jnp.cumsum(ntile_c)
    class_base = (tile_end - ntile_c) * tm_e
    nact = tile_end[-1]
    seg_off = class_base[None, :] + jnp.cumsum(seg_len, axis=0) - seg_len
    pad_off = class_base + counts
    pad_len = ntile_c * tm_e - counts
    tile_ids = jnp.arange(n_max, dtype=jnp.int32)
    tile_cls = jnp.sum((tile_ids[:, None] >= tile_end[None, :]).astype(jnp.int32), axis=1)
    last_cls = jnp.sum((nact - 1 >= tile_end).astype(jnp.int32))
    tile_cls = jnp.where(tile_ids < nact, tile_cls, last_cls)
    grp = tile_cls // N_PAIRS
    pidx = tile_cls % N_PAIRS
    pick = lambda table: sum(jnp.where(pidx == p, e, 0) for p, e in enumerate(table))
    tile_ea = (grp * EPG + pick(PAIR_A)).astype(jnp.int32)
    tile_eb = (grp * EPG + pick(PAIR_B)).astype(jnp.int32)
    empty = jnp.zeros((N_SUB - 1, N_CLASSES), jnp.int32)
    seg_off = jnp.concatenate([seg_off.astype(jnp.int32), empty]).reshape(-1)
    seg_len = jnp.concatenate([seg_len, empty]).reshape(-1)

    nact1 = nact.reshape(1).astype(jnp.int32)
    zero_rows = (jnp.sum(pad_len) + (n_max - nact) * tm_e).reshape(1).astype(jnp.int32)
    x_sorted = _dispatch(seg_off, seg_len, pad_off.astype(jnp.int32), pad_len.astype(jnp.int32), zero_rows,
                         nact1, h2_p, h2_s, info_p, info_s, tm=tm_d, n_sub=N_SUB, n_slots=n_max * tm_e)
    ya_sorted, yb_sorted = _moe(tile_ea, tile_eb, nact1, x_sorted, wg_b, wu_b, wd_b, tm=tm_e, n_max=n_max)

    y_p = _final(seg_off, seg_len, x1_p, info_p, mod_p, g_final, ya_sorted, yb_sorted,
                 tm=tm_d, n_sub=N_SUB, sample=False, tile_base=0)
    y_s = _final(seg_off, seg_len, x1_s, info_s, mod_s, g_final, ya_sorted, yb_sorted,
                 tm=tm_d, n_sub=1, sample=True, tile_base=n_p // tm_d)

    return (y_p.reshape(batch, seq, D_MODEL),
            y_s.reshape(dec_b, dec_t, D_MODEL),
            kf_p.reshape(batch, 1, seq, N_HEADS, HEAD_W),
            vf_p.reshape(batch, 1, seq, N_HEADS, HEAD_W),
            kf_s.reshape(dec_b, 1, dec_t, N_HEADS, HEAD_W),
            vf_s.reshape(dec_b, 1, dec_t, N_HEADS, HEAD_W),
            vsn_s.reshape(dec_b, 1, dec_t, N_GROUPS_SG, SG_CH))
```

```python
import functools
import math

import numpy as np
import jax
import jax.numpy as jnp
from jax import lax
from jax.experimental import pallas as pl
from jax.experimental.pallas import tpu as pltpu

F32 = jnp.float32
BF16 = jnp.bfloat16

D_MODEL = 1024
N_HEADS = 4
DK = 64
HEAD_W = 128
QK_W = N_HEADS * HEAD_W
N_GROUPS_SG = 4
SG_CH = 128
CHUNK = 128
PAGE = 128
N_BUCKETS = 32
MAX_DISTANCE = 128
N_EG = 4
EPG = 4
N_PAIRS = 6
N_CLASSES = N_EG * N_PAIRS
D_EXPERT = 512
EPS = 1e-6
LAM_INIT = 0.8 - 0.6 * math.exp(-0.3 * 0)
NEG = -1e30
LOG2E = math.log2(math.e)
LANES = 128
SUBLANES = 8

TM_TOK = 512
STAGE_A_PARTS = 4
TQ_ATT = 2048
PAGES_PER_STEP = 16
TM_E = 256
MOE_PARTS = 1
TM_D = 256
N_SUB = 4
VMEM_LIMIT = 56 * 1024 * 1024


def _cparams(sem):
    return pltpu.CompilerParams(dimension_semantics=sem, vmem_limit_bytes=VMEM_LIMIT)


def _ada_body(c_ref, w_ref, b_ref, o_ref):
    c = c_ref[...]
    a = (c * jax.nn.sigmoid(c)).astype(BF16)
    o_ref[...] = jnp.dot(a, w_ref[...].astype(BF16), preferred_element_type=F32) + b_ref[...]


def _ada(c_all, w_ada, b_ada):
    m = c_all.shape[0]
    n = w_ada.shape[1]
    tn = 1536
    return pl.pallas_call(
        _ada_body,
        grid=(n // tn,),
        in_specs=[pl.BlockSpec((m, D_MODEL), lambda j: (0, 0)),
                  pl.BlockSpec((D_MODEL, tn), lambda j: (0, j)),
                  pl.BlockSpec((1, tn), lambda j: (0, j))],
        out_specs=pl.BlockSpec((m, tn), lambda j: (0, j)),
        out_shape=jax.ShapeDtypeStruct((m, n), F32),
        compiler_params=_cparams(("arbitrary",)),
        name="adaln",
    )(c_all, w_ada, b_ada.reshape(1, n))


def _stage_a_body(x_ref, mod_ref, w_in_ref, ws_ref, bs_ref, gln_ref, bln_ref,
                  *out_refs, chunk, sample):
    if sample:
        q_ref, kf_ref, vf_ref, sg_ref, vsn_ref = out_refs
    else:
        q_ref, kf_ref, kb_ref, vf_ref, vt_ref, sg_ref = out_refs
    x = x_ref[...]
    tm = x.shape[0]
    mod = mod_ref[0]
    sh1 = mod[:, 0:D_MODEL]
    sc1 = mod[:, D_MODEL:2 * D_MODEL]
    ms = jnp.mean(x * x, axis=-1, keepdims=True)
    h = (x * lax.rsqrt(ms + EPS)) * (1.0 + sc1) + sh1
    hb = h.astype(BF16)
    n_parts = 1 if sample else STAGE_A_PARTS
    rows_p = tm // n_parts
    zs = [jnp.dot(hb[p * rows_p:(p + 1) * rows_p], w_in_ref[...], preferred_element_type=F32)
          for p in range(n_parts)]
    for p, z in enumerate(zs):
        a = p * rows_p
        q = z[:, 0:QK_W] * (DK ** -0.5)
        k = z[:, QK_W:2 * QK_W]
        v = z[:, 2 * QK_W:3 * QK_W]
        for hd in range(N_HEADS):
            kf_ref[pl.ds(a * N_HEADS + hd, rows_p, stride=N_HEADS), :] = k[:, hd * HEAD_W:(hd + 1) * HEAD_W]
            vf_ref[pl.ds(a * N_HEADS + hd, rows_p, stride=N_HEADS), :] = v[:, hd * HEAD_W:(hd + 1) * HEAD_W]
        if sample:
            q_ref[...] = q
        else:
            q_ref[a:a + rows_p, :] = (q * LOG2E).astype(BF16)
            kb_ref[a:a + rows_p, :] = k.astype(BF16)
            vt_ref[0, :, a:a + rows_p] = v.T.astype(BF16)
        u = z[:, 3 * QK_W:4 * QK_W]
        vs = z[:, 4 * QK_W:5 * QK_W]
        for g in range(N_GROUPS_SG):
            lo, hi = g * SG_CH, (g + 1) * SG_CH
            vg = vs[:, lo:hi]
            mu = jnp.mean(vg, axis=-1, keepdims=True)
            dv = vg - mu
            var = jnp.mean(dv * dv, axis=-1, keepdims=True)
            vn = (dv * lax.rsqrt(var + EPS)) * gln_ref[g:g + 1, :] + bln_ref[g:g + 1, :]
            if sample:
                vsn_ref[:, lo:hi] = vn
            vnb = vn.astype(BF16)
            for c in range(rows_p // chunk):
                r0, r1 = c * chunk, (c + 1) * chunk
                s = jnp.dot(ws_ref[g], vnb[r0:r1], preferred_element_type=F32) + bs_ref[g]
                sg_ref[a + r0:a + r1, lo:hi] = (u[r0:r1, lo:hi] * s).astype(BF16)


def _stage_a(x, mod, w_in_b, ws, bs, gln, bln, *, tm, chunk, sample):
    n = x.shape[0]
    nt = n // tm
    mrows = mod.shape[1]
    row = lambda w, dt: jax.ShapeDtypeStruct((n, w), dt)
    blk = lambda w: pl.BlockSpec((tm, w), lambda i: (i, 0))
    cache_shape = jax.ShapeDtypeStruct((n * N_HEADS, HEAD_W), F32)
    cache_blk = pl.BlockSpec((tm * N_HEADS, HEAD_W), lambda i: (i, 0))
    if sample:
        out_shape = [row(QK_W, F32), cache_shape, cache_shape, row(QK_W, BF16), row(QK_W, F32)]
        out_specs = [blk(QK_W), cache_blk, cache_blk, blk(QK_W), blk(QK_W)]
        mod_map = lambda i: (0, 0, 0)
    else:
        out_shape = [row(QK_W, BF16), cache_shape, row(QK_W, BF16), cache_shape,
                     jax.ShapeDtypeStruct((nt, QK_W, tm), BF16), row(QK_W, BF16)]
        out_specs = [blk(QK_W), cache_blk, blk(QK_W), cache_blk,
                     pl.BlockSpec((1, QK_W, tm), lambda i: (i, 0, 0)), blk(QK_W)]
        tiles_per_batch = 4096 // tm
        mod_map = lambda i: (i // tiles_per_batch, 0, 0)
    full = lambda a: pl.BlockSpec(a.shape, lambda i: (0,) * a.ndim)
    return pl.pallas_call(
        functools.partial(_stage_a_body, chunk=chunk, sample=sample),
        grid=(nt,),
        in_specs=[blk(D_MODEL),
                  pl.BlockSpec((1, mrows, 6 * D_MODEL), mod_map),
                  full(w_in_b), full(ws), full(bs), full(gln), full(bln)],
        out_specs=out_specs,
        out_shape=out_shape,
        compiler_params=_cparams(("arbitrary",)),
        name="stage_a_sample" if sample else "stage_a_prompt",
    )(x, mod, w_in_b, ws, bs, gln, bln)


ATT_COLS = 256
BIAS_BLOCK = MAX_DISTANCE


def _attn_body(lam_ref, q_ref, k_ref, vt_ref, bias_ref, g_ref, o_ref, *scratch, tq, tk, between=None):
    n_chain = 2 * tq // ATT_COLS
    q2_refs, m_refs, l_refs, acc_refs = (scratch[i * n_chain:(i + 1) * n_chain] for i in range(4))
    qi = pl.program_id(2)
    if between is not None:
        between(0)
    for c in range(n_chain):
        q0 = (c * ATT_COLS) % tq
        q = q_ref[q0:q0 + ATT_COLS, :]
        lane = lax.broadcasted_iota(jnp.int32, q.shape, 1)
        keep = (lane < DK) if c < n_chain // 2 else (lane >= DK)
        q2_refs[c][...] = jnp.where(keep, q, jnp.zeros_like(q))
        m_refs[c][...] = jnp.full(m_refs[c].shape, NEG, F32)
        l_refs[c][...] = jnp.zeros(l_refs[c].shape, F32)
        acc_refs[c][...] = jnp.zeros(acc_refs[c].shape, F32)

    blk = BIAS_BLOCK

    kblocks = tk // blk

    def block_kinds(rel, q0):
        return [[(q0 // blk + b) - (rel + a) for b in range(ATT_COLS // blk)] for a in range(kblocks)]

    def keys_needed(rel, q0):
        if rel is None:
            return kblocks
        return sum(1 for row in block_kinds(rel, q0) if max(row) >= 0)

    def with_bias(s, rel, q0):
        if rel is None:
            return s
        kinds = block_kinds(rel, q0)[:s.shape[0] // blk]
        if all(d >= 2 for row in kinds for d in row):
            return s
        nxt, near = bias_ref[0, 0], bias_ref[0, 1]
        pick = lambda d: (jnp.full((blk, blk), NEG, F32) if d < 0 else near if d == 0 else nxt if d == 1
                          else jnp.zeros((blk, blk), F32))
        rows = []
        for a, row in enumerate(kinds):
            s_row = s[a * blk:(a + 1) * blk]
            if any(d < 2 for d in row):
                s_row = s_row + jnp.concatenate([pick(d) for d in row], axis=1)
            rows.append(s_row)
        return jnp.concatenate(rows, axis=0)

    def score(j, rel, c):
        nk = keys_needed(rel, (c * ATT_COLS) % tq) * blk
        if nk == 0:
            return None
        k = k_ref[pl.ds(pl.multiple_of(j * tk, tk), nk), :]
        s = lax.dot_general(k, q2_refs[c][...], (((1,), (1,)), ((), ())),
                            preferred_element_type=F32)
        return with_bias(s, rel, (c * ATT_COLS) % tq)

    def accumulate(j, c, s):
        vt = vt_ref[j, :, 0:s.shape[0]]
        m_old = m_refs[c][...]
        m_new = jnp.maximum(m_old, jnp.max(s, axis=0, keepdims=True))
        alpha = jnp.exp2(m_old - m_new)
        p = jnp.exp2(s - m_new)
        l_refs[c][...] = alpha * l_refs[c][...] + jnp.sum(p, axis=0, keepdims=True)
        acc_refs[c][...] = alpha * acc_refs[c][...] + jnp.dot(vt, p.astype(BF16), preferred_element_type=F32)
        m_refs[c][...] = m_new

    def tiles(*work):
        scores = [[score(j, rel, c) for c in range(n_chain)] for j, rel in work]
        for (j, _), tile_scores in zip(work, scores):
            for c, s in enumerate(tile_scores):
                if s is not None:
                    accumulate(j, c, s)

    ratio = tq // tk
    first_diag = qi * ratio
    n_plain = jnp.maximum(first_diag - 1, 0)

    def plain_pair(jj, carry):
        tiles((2 * jj, None), (2 * jj + 1, None))
        return carry

    diag = [(first_diag + r, r * kblocks) for r in range(ratio)]
    if between is None:
        lax.fori_loop(0, n_plain // 2, plain_pair, 0)

        @pl.when(qi >= 1)
        def _():
            if ratio % 2 == 0:
                tiles((first_diag - 2, None), (first_diag - 1, -kblocks))
            else:
                @pl.when(n_plain % 2 == 1)
                def _():
                    tiles((n_plain - 1, None))
                tiles((first_diag - 1, -kblocks))

        tiles(*diag)
    else:
        assert ratio % 2 == 0 and ratio >= 4
        half = ratio // 2

        @pl.when(qi == 0)
        def _():
            tiles(*diag[:half])

        @pl.when(qi >= 1)
        def _():
            lax.fori_loop(0, n_plain // 2, plain_pair, 0)

        between(1)

        @pl.when(qi == 0)
        def _():
            tiles(*diag[half:])

        @pl.when(qi >= 1)
        def _():
            tiles((first_diag - 2, None), (first_diag - 1, -kblocks))

        between(2)

        @pl.when(qi >= 1)
        def _():
            tiles(*diag)

        between(3)

    lam = lam_ref[0]
    o_all = jnp.concatenate([acc_refs[c][...] * (1.0 / l_refs[c][...]) for c in range(n_chain)],
                            axis=1)
    o = o_all[:, 0:tq] - lam * o_all[:, tq:2 * tq]
    ms = jnp.mean(o * o, axis=0, keepdims=True)
    on = (o * lax.rsqrt(ms + EPS)) * g_ref[...] * (1.0 - LAM_INIT)
    o_ref[...] = on.T.astype(BF16)


def _attn_prompt(lam, q, kb, vt, bias_t, g_col, *, batch, seq, tq, tk):
    nq = seq // tq
    nk = seq // tk
    n = batch * seq
    n_chain = 2 * tq // ATT_COLS
    assert vt.shape == (batch * nk, QK_W, tk) and tq % tk == 0 and tk % BIAS_BLOCK == 0
    return pl.pallas_call(
        functools.partial(_attn_body, tq=tq, tk=tk),
        grid=(batch, N_HEADS, nq),
        in_specs=[pl.BlockSpec(memory_space=pltpu.SMEM),
                  pl.BlockSpec((tq, HEAD_W), lambda b, h, i: (b * nq + i, h)),
                  pl.BlockSpec((seq, HEAD_W), lambda b, h, i: (b, h)),
                  pl.BlockSpec((nk, HEAD_W, tk), lambda b, h, i: (b, h, 0)),
                  pl.BlockSpec((1, 2, BIAS_BLOCK, BIAS_BLOCK), lambda b, h, i: (h, 0, 0, 0)),
                  pl.BlockSpec((HEAD_W, 1), lambda b, h, i: (0, 0))],
        out_specs=pl.BlockSpec((tq, HEAD_W), lambda b, h, i: (b * nq + i, h)),
        out_shape=jax.ShapeDtypeStruct((n, QK_W), BF16),
        scratch_shapes=([pltpu.VMEM((ATT_COLS, HEAD_W), BF16)] * n_chain
                        + [pltpu.VMEM((1, ATT_COLS), F32)] * (2 * n_chain)
                        + [pltpu.VMEM((HEAD_W, ATT_COLS), F32)] * n_chain),
        compiler_params=_cparams(("arbitrary", "arbitrary", "arbitrary")),
        name="attn_prompt",
    )(lam, q, kb, vt, bias_t, g_col)


PAGE_ROWS = PAGE * N_HEADS
TAIL_TOKENS = 2 * PAGE


def _sattn_body(pt_ref, lam_ref, q_ref, knew_ref, vnew_ref, bl_ref, bn_ref, g_ref, ck_ref, cv_ref,
                o_ref, kbuf, vbuf, sem, mask_ref, m_ref, l_ref, acc_ref, *, pages, n_steps, total):
    b = pl.program_id(0)
    s = pl.program_id(1)
    step = b * n_steps + s
    slot = step % 2

    def page_copies(step_idx, sl):
        base = step_idx * pages
        out = []
        for i in range(pages):
            src = pl.ds(pl.multiple_of(pt_ref[base + i] * PAGE_ROWS, PAGE_ROWS), PAGE_ROWS)
            dst = pl.ds(i * PAGE_ROWS, PAGE_ROWS)
            out.append(pltpu.make_async_copy(ck_ref.at[src], kbuf.at[sl, dst], sem.at[sl, 0]))
            out.append(pltpu.make_async_copy(cv_ref.at[src], vbuf.at[sl, dst], sem.at[sl, 1]))
        return out

    def start_all(copies):
        for n, c in enumerate(copies):
            c.start(priority=n % 2)

    @pl.when(step == 0)
    def _():
        start_all(page_copies(0, 0))

    @pl.when(step + 1 < total)
    def _():
        start_all(page_copies(step + 1, 1 - slot))

    for c in page_copies(step, slot):
        c.wait()

    @pl.when(s == 0)
    def _():
        m_ref[...] = jnp.full(m_ref.shape, NEG, F32)
        l_ref[...] = jnp.zeros(l_ref.shape, F32)
        acc_ref[...] = jnp.zeros(acc_ref.shape, F32)

    @pl.when(step == 0)
    def _():
        row = lax.broadcasted_iota(jnp.int32, mask_ref.shape, 0)
        col = lax.broadcasted_iota(jnp.int32, mask_ref.shape, 1)
        same_head = (col % N_HEADS) == (row // (2 * SUBLANES))
        mask_ref[...] = jnp.where(same_head, 0.0, NEG)

    q = q_ref[...]
    lane = lax.broadcasted_iota(jnp.int32, (SUBLANES, HEAD_W), 1)
    pieces = []
    for h in range(N_HEADS):
        qh = q[:, h * HEAD_W:(h + 1) * HEAD_W]
        pieces += [jnp.where(lane < DK, qh, 0.0), jnp.where(lane >= DK, qh, 0.0)]
    qm = jnp.concatenate(pieces, axis=0).astype(BF16)

    def update(kb, vb, bias):
        sc = lax.dot_general(qm, kb, (((1,), (1,)), ((), ())),
                             preferred_element_type=F32) + bias
        m_old = m_ref[...]
        m_new = jnp.maximum(m_old, jnp.max(sc, axis=1, keepdims=True))
        alpha = jnp.exp(m_old - m_new)
        p = jnp.exp(sc - m_new)
        l_ref[...] = alpha * l_ref[...] + jnp.sum(p, axis=1, keepdims=True)
        acc_ref[...] = alpha * acc_ref[...] + jnp.dot(p.astype(BF16), vb,
                                                      preferred_element_type=F32)
        m_ref[...] = m_new

    is_last = s == n_steps - 1
    head_cols = mask_ref.shape[1] - bl_ref.shape[1]
    update(kbuf[slot].astype(BF16), vbuf[slot].astype(BF16),
           jnp.concatenate([mask_ref[:, :head_cols],
                            mask_ref[:, head_cols:] + bl_ref[...] * is_last.astype(F32)], axis=1))

    @pl.when(is_last)
    def _():
        update(knew_ref[0], vnew_ref[0], bn_ref[...])
        lam = lam_ref[0]
        o_all = acc_ref[...] * (1.0 / l_ref[...])
        for h in range(N_HEADS):
            r = h * 2 * SUBLANES
            o = o_all[r:r + SUBLANES] - lam * o_all[r + SUBLANES:r + 2 * SUBLANES]
            ms = jnp.mean(o * o, axis=-1, keepdims=True)
            o_ref[:, h * HEAD_W:(h + 1) * HEAD_W] = ((o * lax.rsqrt(ms + EPS)) * g_ref[...]
                                                     * (1.0 - LAM_INIT))


def _attn_sample(page_table_flat, lam, q_s, knew, vnew, bias_last, bias_new, g_row, cache_k, cache_v,
                 *, dec_b, n_pages):
    pages = PAGES_PER_STEP
    n_steps = n_pages // pages
    total = dec_b * n_steps
    nq = q_s.shape[0] // dec_b
    n_rows = 2 * N_HEADS * nq
    step_rows = pages * PAGE_ROWS
    grid_spec = pltpu.PrefetchScalarGridSpec(
        num_scalar_prefetch=1,
        grid=(dec_b, n_steps),
        in_specs=[pl.BlockSpec(memory_space=pltpu.SMEM),
                  pl.BlockSpec((nq, QK_W), lambda b, s, pt: (b, 0)),
                  pl.BlockSpec((1, PAGE, HEAD_W), lambda b, s, pt: (b, 0, 0)),
                  pl.BlockSpec((1, PAGE, HEAD_W), lambda b, s, pt: (b, 0, 0)),
                  pl.BlockSpec(bias_last.shape, lambda b, s, pt: (0, 0)),
                  pl.BlockSpec(bias_new.shape, lambda b, s, pt: (0, 0)),
                  pl.BlockSpec((1, HEAD_W), lambda b, s, pt: (0, 0)),
                  pl.BlockSpec(memory_space=pl.ANY),
                  pl.BlockSpec(memory_space=pl.ANY)],
        out_specs=pl.BlockSpec((nq, QK_W), lambda b, s, pt: (b, 0)),
        scratch_shapes=[pltpu.VMEM((2, step_rows, HEAD_W), F32),
                        pltpu.VMEM((2, step_rows, HEAD_W), F32),
                        pltpu.SemaphoreType.DMA((2, 2)),
                        pltpu.VMEM((n_rows, step_rows), F32),
                        pltpu.VMEM((n_rows, 1), F32), pltpu.VMEM((n_rows, 1), F32),
                        pltpu.VMEM((n_rows, HEAD_W), F32)])
    return pl.pallas_call(
        functools.partial(_sattn_body, pages=pages, n_steps=n_steps, total=total),
        grid_spec=grid_spec,
        out_shape=jax.ShapeDtypeStruct(q_s.shape, F32),
        compiler_params=_cparams(("arbitrary", "arbitrary")),
        name="attn_sample",
    )(page_table_flat, lam, q_s, knew, vnew, bias_last, bias_new, g_row, cache_k, cache_v)


def _attn_fused_body(pt_ref, lam_ref, q_ref, k_ref, vt_ref, bias_ref, gcol_ref,
                     qs_ref, knew_ref, vnew_ref, bl_ref, bn_ref, grow_ref, ck_ref, cv_ref,
                     o_ref, os_ref, *scratch, tq, tk, pages, n_sub, n_steps):
    n_prompt_scratch = 4 * (2 * tq // ATT_COLS)
    kbuf, vbuf, sem, mask_ref, qm_ref, m_ref, l_ref, acc_ref = scratch[n_prompt_scratch:]
    step = (pl.program_id(0) * pl.num_programs(1) + pl.program_id(1)) * pl.num_programs(2) + pl.program_id(2)

    def page_copies(group, sl):
        base = group * pages
        out = []
        for i in range(pages):
            src = pl.ds(pl.multiple_of(pt_ref[base + i] * PAGE_ROWS, PAGE_ROWS), PAGE_ROWS)
            dst = pl.ds(i * PAGE_ROWS, PAGE_ROWS)
            out.append(pltpu.make_async_copy(ck_ref.at[src], kbuf.at[sl, dst], sem.at[sl, 0]))
            out.append(pltpu.make_async_copy(cv_ref.at[src], vbuf.at[sl, dst], sem.at[sl, 1]))
        return out

    def start_all(copies):
        for c in copies:
            c.start()

    def update(kb, vb, bias):
        sc = lax.dot_general(qm_ref[...], kb, (((1,), (1,)), ((), ())),
                             preferred_element_type=F32) + bias
        m_old = m_ref[...]
        m_new = jnp.maximum(m_old, jnp.max(sc, axis=1, keepdims=True))
        alpha = jnp.exp(m_old - m_new)
        p = jnp.exp(sc - m_new)
        l_ref[...] = alpha * l_ref[...] + jnp.sum(p, axis=1, keepdims=True)
        acc_ref[...] = alpha * acc_ref[...] + jnp.dot(p.astype(BF16), vb, preferred_element_type=F32)
        m_ref[...] = m_new

    def sample_group(u):
        group = step * n_sub + u
        slot = u % 2
        if u == 0:
            @pl.when(step == 0)
            def _():
                start_all(page_copies(0, 0))
                row = lax.broadcasted_iota(jnp.int32, mask_ref.shape, 0)
                col = lax.broadcasted_iota(jnp.int32, mask_ref.shape, 1)
                same_head = (col % N_HEADS) == (row // (2 * SUBLANES))
                mask_ref[...] = jnp.where(same_head, 0.0, NEG)
        if u < n_sub - 1:
            start_all(page_copies(group + 1, 1 - slot))
        else:
            @pl.when(step + 1 < n_steps)
            def _():
                start_all(page_copies(group + 1, 1 - slot))
        for c in page_copies(group, slot):
            c.wait()
        if u == 0:
            m_ref[...] = jnp.full(m_ref.shape, NEG, F32)
            l_ref[...] = jnp.zeros(l_ref.shape, F32)
            acc_ref[...] = jnp.zeros(acc_ref.shape, F32)
            q = qs_ref[...]
            lane = lax.broadcasted_iota(jnp.int32, (SUBLANES, HEAD_W), 1)
            pieces = []
            for h in range(N_HEADS):
                qh = q[:, h * HEAD_W:(h + 1) * HEAD_W]
                pieces += [jnp.where(lane < DK, qh, 0.0), jnp.where(lane >= DK, qh, 0.0)]
            qm_ref[...] = jnp.concatenate(pieces, axis=0).astype(BF16)
        bias = mask_ref[...]
        if u == n_sub - 1:
            head_cols = mask_ref.shape[1] - bl_ref.shape[1]
            bias = jnp.concatenate([bias[:, :head_cols], bias[:, head_cols:] + bl_ref[...]], axis=1)
        update(kbuf[slot].astype(BF16), vbuf[slot].astype(BF16), bias)
        if u == n_sub - 1:
            update(knew_ref[0], vnew_ref[0], bn_ref[...])
            lam = lam_ref[0]
            o_all = acc_ref[...] * (1.0 / l_ref[...])
            for h in range(N_HEADS):
                r = h * 2 * SUBLANES
                o = o_all[r:r + SUBLANES] - lam * o_all[r + SUBLANES:r + 2 * SUBLANES]
                ms = jnp.mean(o * o, axis=-1, keepdims=True)
                os_ref[:, h * HEAD_W:(h + 1) * HEAD_W] = ((o * lax.rsqrt(ms + EPS)) * grow_ref[...]
                                                          * (1.0 - LAM_INIT))

    _attn_body(lam_ref, q_ref, k_ref, vt_ref, bias_ref, gcol_ref, o_ref, *scratch[:n_prompt_scratch],
               tq=tq, tk=tk, between=sample_group)


def _attn_fused(page_table_flat, lam, q, kb, vt, bias_t, g_col, q_s, knew, vnew, bias_last, bias_new, g_row,
                cache_k, cache_v, *, batch, seq, tq, tk, dec_b, n_pages):
    nq = seq // tq
    nk = seq // tk
    n = batch * seq
    n_chain = 2 * tq // ATT_COLS
    pages = PAGES_PER_STEP
    n_sub = n_pages // pages
    n_steps = batch * N_HEADS * nq
    dec_t = q_s.shape[0] // dec_b
    n_rows = 2 * N_HEADS * dec_t
    step_rows = pages * PAGE_ROWS
    assert n_steps == dec_b and n_sub == 4 and nq == 2
    assert vt.shape == (batch * nk, QK_W, tk) and tq % tk == 0 and tk % BIAS_BLOCK == 0
    lin = lambda b, h, i: (b * N_HEADS + h) * nq + i
    grid_spec = pltpu.PrefetchScalarGridSpec(
        num_scalar_prefetch=1,
        grid=(batch, N_HEADS, nq),
        in_specs=[pl.BlockSpec(memory_space=pltpu.SMEM),
                  pl.BlockSpec((tq, HEAD_W), lambda b, h, i, pt: (b * nq + i, h)),
                  pl.BlockSpec((seq, HEAD_W), lambda b, h, i, pt: (b, h)),
                  pl.BlockSpec((nk, HEAD_W, tk), lambda b, h, i, pt: (b, h, 0)),
                  pl.BlockSpec((1, 2, BIAS_BLOCK, BIAS_BLOCK), lambda b, h, i, pt: (h, 0, 0, 0)),
                  pl.BlockSpec((HEAD_W, 1), lambda b, h, i, pt: (0, 0)),
                  pl.BlockSpec((dec_t, QK_W), lambda b, h, i, pt: (lin(b, h, i), 0)),
                  pl.BlockSpec((1, PAGE, HEAD_W), lambda b, h, i, pt: (lin(b, h, i), 0, 0)),
                  pl.BlockSpec((1, PAGE, HEAD_W), lambda b, h, i, pt: (lin(b, h, i), 0, 0)),
                  pl.BlockSpec(bias_last.shape, lambda b, h, i, pt: (0, 0)),
                  pl.BlockSpec(bias_new.shape, lambda b, h, i, pt: (0, 0)),
                  pl.BlockSpec((1, HEAD_W), lambda b, h, i, pt: (0, 0)),
                  pl.BlockSpec(memory_space=pl.ANY),
                  pl.BlockSpec(memory_space=pl.ANY)],
        out_specs=[pl.BlockSpec((tq, HEAD_W), lambda b, h, i, pt: (b * nq + i, h)),
                   pl.BlockSpec((dec_t, QK_W), lambda b, h, i, pt: (lin(b, h, i), 0))],
        scratch_shapes=([pltpu.VMEM((ATT_COLS, HEAD_W), BF16)] * n_chain
                        + [pltpu.VMEM((1, ATT_COLS), F32)] * (2 * n_chain)
                        + [pltpu.VMEM((HEAD_W, ATT_COLS), F32)] * n_chain
                        + [pltpu.VMEM((2, step_rows, HEAD_W), F32),
                           pltpu.VMEM((2, step_rows, HEAD_W), F32),
                           pltpu.SemaphoreType.DMA((2, 2)),
                           pltpu.VMEM((n_rows, step_rows), F32),
                           pltpu.VMEM((n_rows, HEAD_W), BF16),
                           pltpu.VMEM((n_rows, 1), F32), pltpu.VMEM((n_rows, 1), F32),
                           pltpu.VMEM((n_rows, HEAD_W), F32)]))
    return pl.pallas_call(
        functools.partial(_attn_fused_body, tq=tq, tk=tk, pages=pages, n_sub=n_sub, n_steps=n_steps),
        grid_spec=grid_spec,
        out_shape=[jax.ShapeDtypeStruct((n, QK_W), BF16), jax.ShapeDtypeStruct(q_s.shape, F32)],
        compiler_params=_cparams(("arbitrary", "arbitrary", "arbitrary")),
        name="attn_fused",
    )(page_table_flat, lam, q, kb, vt, bias_t, g_col, q_s, knew, vnew, bias_last, bias_new, g_row,
      cache_k, cache_v)


INFO_W_A, INFO_W_B, INFO_CLS = 0, 1, 2
PAIR_A = (0, 0, 0, 1, 1, 3)
PAIR_B = (1, 2, 3, 3, 2, 2)


def _stage_c_body(o_ref, sg_ref, x_ref, mod_ref, wo_ref, wr_ref, br_ref, x1_ref, h2_ref, cls_ref, info_ref):
    x = x_ref[...]
    tm = x.shape[0]
    mod = mod_ref[0]
    g1 = mod[:, 2 * D_MODEL:3 * D_MODEL]
    sh2 = mod[:, 3 * D_MODEL:4 * D_MODEL]
    sc2 = mod[:, 4 * D_MODEL:5 * D_MODEL]
    mix = (jnp.dot(o_ref[...].astype(BF16), wo_ref[0:QK_W, :], preferred_element_type=F32)
           + jnp.dot(sg_ref[...], wo_ref[QK_W:2 * QK_W, :], preferred_element_type=F32))
    x1 = x + g1 * mix
    x1_ref[...] = x1
    ms = jnp.mean(x1 * x1, axis=-1, keepdims=True)
    h2 = ((x1 * lax.rsqrt(ms + EPS)) * (1.0 + sc2) + sh2).astype(BF16)
    h2_ref[...] = h2
    lg = lax.dot_general(wr_ref[...], h2, (((1,), (1,)), ((), ())),
                         preferred_element_type=F32) + br_ref[...]
    gl = [lg[i:i + 1, :] for i in range(N_EG)]
    el = [lg[N_EG + i:N_EG + i + 1, :] for i in range(N_EG * EPG)]
    gmax = jnp.maximum(jnp.maximum(gl[0], gl[1]), jnp.maximum(gl[2], gl[3]))
    gi = jnp.where(gl[0] == gmax, 0, jnp.where(gl[1] == gmax, 1, jnp.where(gl[2] == gmax, 2, 3)))
    gsum = (jnp.exp(gl[0] - gmax) + jnp.exp(gl[1] - gmax)
            + jnp.exp(gl[2] - gmax) + jnp.exp(gl[3] - gmax))
    gp = 1.0 / gsum
    sel = [jnp.where(gi == 0, el[j], jnp.where(gi == 1, el[EPG + j],
                                               jnp.where(gi == 2, el[2 * EPG + j], el[3 * EPG + j])))
           for j in range(EPG)]
    v0 = jnp.maximum(jnp.maximum(sel[0], sel[1]), jnp.maximum(sel[2], sel[3]))
    i0 = jnp.where(sel[0] == v0, 0, jnp.where(sel[1] == v0, 1, jnp.where(sel[2] == v0, 2, 3)))
    rest = [jnp.where(i0 == j, -3e38, sel[j]) for j in range(EPG)]
    v1 = jnp.maximum(jnp.maximum(rest[0], rest[1]), jnp.maximum(rest[2], rest[3]))
    i1 = jnp.where(rest[0] == v1, 0, jnp.where(rest[1] == v1, 1, jnp.where(rest[2] == v1, 2, 3)))
    e1 = jnp.exp(v1 - v0)
    den = 1.0 / (1.0 + e1)
    tw0 = den * gp
    tw1 = e1 * den * gp
    first_low = i0 < i1
    lo = jnp.where(first_low, i0, i1)
    hi = jnp.where(first_low, i1, i0)
    w_lo = jnp.where(first_low, tw0, tw1)
    w_hi = jnp.where(first_low, tw1, tw0)
    pair = jnp.where(lo == 0, hi - 1, jnp.where(lo == 2, 5, jnp.where(hi == 3, 3, 4)))
    swapped = pair == 5
    w_a = jnp.where(swapped, w_hi, w_lo)
    w_b = jnp.where(swapped, w_lo, w_hi)
    cls = gi * N_PAIRS + pair
    cls_ref[...] = jnp.broadcast_to(cls, cls_ref.shape).astype(jnp.int32)
    row = lax.broadcasted_iota(jnp.int32, (LANES, tm), 0)
    rec = jnp.where(row == INFO_W_A, w_a, jnp.where(row == INFO_W_B, w_b,
                                                    jnp.where(row == INFO_CLS, cls.astype(F32), 0.0)))
    info_ref[...] = rec.T


def _stage_c(o, sg, x, mod, wo_b, wr_t, br, *, tm, sample):
    n = x.shape[0]
    nt = n // tm
    mrows = mod.shape[1]
    blk = lambda w: pl.BlockSpec((tm, w), lambda i: (i, 0))
    full = lambda a: pl.BlockSpec(a.shape, lambda i: (0,) * a.ndim)
    if sample:
        mod_map = lambda i: (0, 0, 0)
    else:
        tiles_per_batch = 4096 // tm
        mod_map = lambda i: (i // tiles_per_batch, 0, 0)
    return pl.pallas_call(
        _stage_c_body,
        grid=(nt,),
        in_specs=[blk(QK_W), blk(QK_W), blk(D_MODEL),
                  pl.BlockSpec((1, mrows, 6 * D_MODEL), mod_map),
                  full(wo_b), full(wr_t), full(br)],
        out_specs=[blk(D_MODEL), blk(D_MODEL),
                   pl.BlockSpec((SUBLANES, tm), lambda i: (0, i)),
                   blk(LANES)],
        out_shape=[jax.ShapeDtypeStruct((n, D_MODEL), F32),
                   jax.ShapeDtypeStruct((n, D_MODEL), BF16),
                   jax.ShapeDtypeStruct((SUBLANES, n), jnp.int32),
                   jax.ShapeDtypeStruct((n, LANES), F32)],
        compiler_params=_cparams(("arbitrary",)),
        name="stage_c_sample" if sample else "stage_c_prompt",
    )(o, sg, x, mod, wo_b, wr_t, br)


def rows8(ref, start, count):
    scale = lambda v: v * SUBLANES if isinstance(v, int) else pl.multiple_of(v * SUBLANES, SUBLANES)
    return ref.at[pl.ds(scale(start), scale(count))]


def _perm_t(cls_col):
    n = cls_col.shape[0]
    lane = lax.broadcasted_iota(jnp.int32, (n, LANES), 1).astype(F32)
    onehot = (lane == cls_col).astype(BF16)
    r = lax.broadcasted_iota(jnp.int32, (n, n), 0)
    c = lax.broadcasted_iota(jnp.int32, (n, n), 1)
    before = (c < r).astype(BF16)
    rank = jnp.dot(before, onehot, preferred_element_type=F32)
    cnt = jnp.sum(onehot.astype(F32), axis=0, keepdims=True)
    cr = lax.broadcasted_iota(jnp.int32, (LANES, LANES), 0)
    cc = lax.broadcasted_iota(jnp.int32, (LANES, LANES), 1)
    lower_cls = (cr < cc).astype(BF16)
    base = jnp.dot(jnp.broadcast_to(cnt, (SUBLANES, LANES)).astype(BF16), lower_cls,
                   preferred_element_type=F32)[0:1, :]
    pos = jnp.sum(onehot.astype(F32) * (base + rank), axis=1, keepdims=True)
    dest = lax.broadcasted_iota(jnp.int32, (n, n), 1).astype(F32)
    return (dest == pos).astype(F32)


def _dispatch_body(soff_ref, slen_ref, poff_ref, plen_ref, ptot_ref, nact_ref, hp_ref, hs_ref, ip_ref, is_ref,
                   xs_ref, buf, zbuf, sem, zsem, *, tm, n_sub, n_tiles, n_slab_tiles):
    i = pl.program_id(0)
    slot = i % 2
    is_sample = i == n_tiles - 1

    def wait_tile(sl, tokens):
        pltpu.make_async_copy(rows8(buf.at[sl], 0, tokens), rows8(buf.at[sl], 0, tokens), sem.at[sl]).wait()

    @pl.when(i == 0)
    def _():
        zbuf[...] = jnp.zeros(zbuf.shape, F32)
        for c in range(N_CLASSES):
            @pl.when(plen_ref[c] > 0)
            def _():
                pltpu.make_async_copy(rows8(zbuf, 0, plen_ref[c]), rows8(xs_ref, poff_ref[c], plen_ref[c]),
                                      zsem).start()
        for j in range(n_slab_tiles - N_CLASSES, n_slab_tiles):
            @pl.when(j >= nact_ref[0])
            def _():
                pltpu.make_async_copy(zbuf, rows8(xs_ref, j * TM_E, TM_E), zsem).start()

    xp = []
    for u in range(n_sub):
        rows = slice(u * tm, (u + 1) * tm)
        x = hp_ref[rows, :]
        info = ip_ref[rows, :]
        if u == 0:
            x = jnp.where(is_sample, hs_ref[...], x)
            info = jnp.where(is_sample, is_ref[...], info)
        perm = _perm_t(info[:, INFO_CLS:INFO_CLS + 1]).T.astype(BF16)
        xp.append(jnp.dot(perm, x, preferred_element_type=F32))

    @pl.when(i >= 2)
    def _():
        wait_tile(slot, n_sub * tm)

    bs = buf.at[slot]
    for u in range(n_sub):
        for c in range(D_MODEL // LANES):
            bs[pl.ds(u * tm * SUBLANES + c, tm, stride=SUBLANES), :] = xp[u][:, c * LANES:(c + 1) * LANES]
    for u in range(n_sub):
        local = u * tm
        for c in range(N_CLASSES):
            k = (i * n_sub + u) * N_CLASSES + c
            n_rows = slen_ref[k]

            @pl.when(n_rows > 0)
            def _():
                pltpu.make_async_copy(rows8(bs, local, n_rows), rows8(xs_ref, soff_ref[k], n_rows),
                                      sem.at[slot]).start()
            local = local + n_rows

    @pl.when(i == n_tiles - 1)
    def _():
        wait_tile(slot, tm)
        if n_tiles >= 2:
            wait_tile(1 - slot, n_sub * tm)

        @pl.when(ptot_ref[0] > 0)
        def _():
            n = pl.multiple_of(ptot_ref[0] * SUBLANES, SUBLANES)
            pltpu.make_async_copy(xs_ref.at[pl.ds(0, n)], xs_ref.at[pl.ds(0, n)], zsem).wait()


def _dispatch(seg_off, seg_len, pad_off, pad_len, pad_tot, nact, h2_p, h2_s, info_p, info_s, *, tm, n_sub, n_slots):
    n_prompt_steps = h2_p.shape[0] // (n_sub * tm)
    n_tiles = n_prompt_steps + 1
    assert h2_s.shape[0] == tm and h2_p.shape[0] % (n_sub * tm) == 0
    assert seg_len.shape[0] == n_tiles * n_sub * N_CLASSES
    last_p = n_prompt_steps - 1
    grid_spec = pltpu.PrefetchScalarGridSpec(
        num_scalar_prefetch=6,
        grid=(n_tiles,),
        in_specs=[pl.BlockSpec((n_sub * tm, D_MODEL), lambda i, *_: (jnp.minimum(i, last_p), 0)),
                  pl.BlockSpec((tm, D_MODEL), lambda i, *_: (0, 0)),
                  pl.BlockSpec((n_sub * tm, LANES), lambda i, *_: (jnp.minimum(i, last_p), 0)),
                  pl.BlockSpec((tm, LANES), lambda i, *_: (0, 0))],
        out_specs=pl.BlockSpec(memory_space=pl.ANY),
        scratch_shapes=[pltpu.VMEM((2, n_sub * tm * SUBLANES, LANES), F32),
                        pltpu.VMEM((TM_E * SUBLANES, LANES), F32),
                        pltpu.SemaphoreType.DMA((2,)),
                        pltpu.SemaphoreType.DMA(())])
    return pl.pallas_call(
        functools.partial(_dispatch_body, tm=tm, n_sub=n_sub, n_tiles=n_tiles, n_slab_tiles=n_slots // TM_E),
        grid_spec=grid_spec,
        out_shape=jax.ShapeDtypeStruct((n_slots * SUBLANES, LANES), F32),
        compiler_params=_cparams(("arbitrary",)),
        name="moe_dispatch",
    )(seg_off, seg_len, pad_off, pad_len, pad_tot, nact, h2_p, h2_s, info_p, info_s)


def _moe_body(ea_ref, eb_ref, nact_ref, x_ref, wga_ref, wgb_ref, wua_ref, wub_ref, wda_ref, wdb_ref,
              ya_ref, yb_ref, *, tm):
    i = pl.program_id(0)

    @pl.when(i < nact_ref[0])
    def _():
        x = jnp.concatenate([x_ref[pl.ds(c, tm, stride=SUBLANES), :] for c in range(D_MODEL // LANES)],
                            axis=1).astype(BF16)

        wts = [r[0].astype(BF16) for r in (wga_ref, wua_ref, wda_ref, wgb_ref, wub_ref, wdb_ref)]
        rows_p = tm // MOE_PARTS

        def hidden(xp, wg, wu):
            return (jnp.dot(xp, wg, preferred_element_type=F32), jnp.dot(xp, wu, preferred_element_type=F32))

        def down(gate_up, wd):
            gate, up = gate_up
            he = (gate * jax.nn.sigmoid(gate)) * up
            return jnp.dot(he.astype(BF16), wd, preferred_element_type=F32)

        hs = []
        for p in range(MOE_PARTS):
            xp = x[p * rows_p:(p + 1) * rows_p]
            hs.append((hidden(xp, wts[0], wts[1]), hidden(xp, wts[3], wts[4])))
        for p, (ha, hb) in enumerate(hs):
            for y_ref, h, wd in ((ya_ref, ha, wts[2]), (yb_ref, hb, wts[5])):
                y = down(h, wd)
                for c in range(D_MODEL // LANES):
                    y_ref[pl.ds(p * rows_p * SUBLANES + c, rows_p, stride=SUBLANES), :] = (
                        y[:, c * LANES:(c + 1) * LANES])

    @pl.when(i >= nact_ref[0])
    def _():
        ya_ref[...] = jnp.zeros(ya_ref.shape, ya_ref.dtype)
        yb_ref[...] = jnp.zeros(yb_ref.shape, yb_ref.dtype)


def _moe(tile_ea, tile_eb, nact, x_sorted, wg_b, wu_b, wd_b, *, tm, n_max):
    wspec_in = lambda sel: pl.BlockSpec((1, D_MODEL, D_EXPERT), sel)
    wspec_out = lambda sel: pl.BlockSpec((1, D_EXPERT, D_MODEL), sel)
    sel_a = lambda i, ea, eb, na: (ea[i], 0, 0)
    sel_b = lambda i, ea, eb, na: (eb[i], 0, 0)
    rows_in = lambda i, ea, eb, na: (jnp.minimum(i, na[0] - 1), 0)
    grid_spec = pltpu.PrefetchScalarGridSpec(
        num_scalar_prefetch=3,
        grid=(n_max,),
        in_specs=[pl.BlockSpec((tm * SUBLANES, LANES), rows_in),
                  wspec_in(sel_a), wspec_in(sel_b), wspec_in(sel_a), wspec_in(sel_b),
                  wspec_out(sel_a), wspec_out(sel_b)],
        out_specs=[pl.BlockSpec((tm * SUBLANES, LANES), lambda i, ea, eb, na: (i, 0))] * 2)
    return pl.pallas_call(
        functools.partial(_moe_body, tm=tm),
        grid_spec=grid_spec,
        out_shape=[jax.ShapeDtypeStruct(x_sorted.shape, F32)] * 2,
        compiler_params=_cparams(("arbitrary",)),
        name="moe",
    )(tile_ea, tile_eb, nact, x_sorted, wg_b, wg_b, wu_b, wu_b, wd_b, wd_b)


def _final_body(soff_ref, slen_ref, x1_ref, info_ref, mod_ref, gf_ref, ya_ref, yb_ref, y_ref, buf, sem,
                *, tm, n_sub, n_tiles, tile_base):
    i = pl.program_id(0)
    slot = i % 2
    slabs = (ya_ref, yb_ref)

    def fetch(step, sl):
        for u in range(n_sub):
            local = u * tm
            for c in range(N_CLASSES):
                k = (step * n_sub + u + tile_base) * N_CLASSES + c
                n_rows = slen_ref[k]

                @pl.when(n_rows > 0)
                def _():
                    for e in range(2):
                        pltpu.make_async_copy(rows8(slabs[e], soff_ref[k], n_rows),
                                              rows8(buf.at[sl, e], local, n_rows), sem.at[sl, e]).start()
                local = local + n_rows

    @pl.when(i == 0)
    def _():
        fetch(0, 0)

    @pl.when(i + 1 < n_tiles)
    def _():
        fetch(i + 1, 1 - slot)

    info = info_ref[...]
    perm_t = [_perm_t(info[u * tm:(u + 1) * tm, INFO_CLS:INFO_CLS + 1]).astype(BF16) for u in range(n_sub)]
    moe = None
    for e, lane_w in enumerate((INFO_W_A, INFO_W_B)):
        pltpu.make_async_copy(buf.at[slot, e], buf.at[slot, e], sem.at[slot, e]).wait()
        bs = buf.at[slot, e]
        parts = []
        for u in range(n_sub):
            ye = jnp.concatenate([bs[pl.ds(u * tm * SUBLANES + c, tm, stride=SUBLANES), :]
                                  for c in range(D_MODEL // LANES)], axis=1)
            parts.append(jnp.dot(perm_t[u], ye.astype(BF16), preferred_element_type=F32))
        term = info[:, lane_w:lane_w + 1] * jnp.concatenate(parts, axis=0)
        moe = term if moe is None else moe + term
    x1 = x1_ref[...]
    g2 = mod_ref[0][:, 5 * D_MODEL:6 * D_MODEL]
    x2 = x1 + g2 * moe
    ms = jnp.mean(x2 * x2, axis=-1, keepdims=True)
    y_ref[...] = (x2 * lax.rsqrt(ms + EPS)) * gf_ref[...]


def _final(seg_off, seg_len, x1, info, mod, g_final, ya_sorted, yb_sorted, *, tm, n_sub, sample, tile_base):
    n = x1.shape[0]
    rows = n_sub * tm
    nt = n // rows
    mrows = mod.shape[1]
    if sample:
        mod_map = lambda i, *_: (0, 0, 0)
    else:
        tiles_per_batch = 4096 // rows
        mod_map = lambda i, *_: (i // tiles_per_batch, 0, 0)
    grid_spec = pltpu.PrefetchScalarGridSpec(
        num_scalar_prefetch=2,
        grid=(nt,),
        in_specs=[pl.BlockSpec((rows, D_MODEL), lambda i, *_: (i, 0)),
                  pl.BlockSpec((rows, LANES), lambda i, *_: (i, 0)),
                  pl.BlockSpec((1, mrows, 6 * D_MODEL), mod_map),
                  pl.BlockSpec((1, D_MODEL), lambda i, *_: (0, 0)),
                  pl.BlockSpec(memory_space=pl.ANY),
                  pl.BlockSpec(memory_space=pl.ANY)],
        out_specs=pl.BlockSpec((rows, D_MODEL), lambda i, *_: (i, 0)),
        scratch_shapes=[pltpu.VMEM((2, 2, rows * SUBLANES, LANES), F32),
                        pltpu.SemaphoreType.DMA((2, 2))])
    return pl.pallas_call(
        functools.partial(_final_body, tm=tm, n_sub=n_sub, n_tiles=nt, tile_base=tile_base),
        grid_spec=grid_spec,
        out_shape=jax.ShapeDtypeStruct((n, D_MODEL), F32),
        compiler_params=_cparams(("arbitrary",)),
        name="final_sample" if sample else "final_prompt",
    )(seg_off, seg_len, x1, info, mod, g_final.reshape(1, D_MODEL), ya_sorted, yb_sorted)


def _bucket_table(n):
    d = np.arange(n)
    max_exact = N_BUCKETS // 2
    nf = np.maximum(d, 1).astype(np.float64)
    large = max_exact + (np.log(nf / max_exact) / math.log(MAX_DISTANCE / max_exact)
                         * (N_BUCKETS - max_exact)).astype(np.int64)
    large = np.minimum(large, N_BUCKETS - 1)
    return np.where(d < max_exact, d, large).astype(np.int32)


def _toeplitz(v, n_rows, n_cols):
    length = n_rows + n_cols - 1
    lead = v.shape[:-1]
    vp = jnp.concatenate([v, jnp.zeros(lead + (1,), v.dtype)], axis=-1)
    skew = jnp.tile(vp, (1,) * len(lead) + (n_rows,))[..., :n_rows * length].reshape(lead + (n_rows, length))
    return skew[..., n_rows - 1:n_rows - 1 + n_cols]


def kernel(x_prompt, x_sample, c_prompt, c_sample, cache_k, cache_v, page_table, w_ada, b_ada, w_in, w_o,
           lam_q1, lam_k1, lam_q2, lam_k2, g_subln, rel_bias, g_sg_ln, b_sg_ln, w_s, b_s, w_rg, b_rg,
           w_re, b_re, w_gate, w_up, w_down, g_final):
    batch, seq, _ = x_prompt.shape
    dec_b, dec_t, _ = x_sample.shape
    n_pages = page_table.shape[1]
    n_p = batch * seq
    n_s = dec_b * dec_t
    n_tot = n_p + n_s
    assert w_in.shape[0] == 1 and cache_k.shape[1] == 1 and seq % TQ_ATT == 0 and n_pages % PAGES_PER_STEP == 0
    assert n_p % TM_TOK == 0 and n_p % n_s == 0 and n_tot % TM_E == 0 and dec_t == SUBLANES
    assert TAIL_TOKENS >= MAX_DISTANCE + dec_t and TAIL_TOKENS <= PAGES_PER_STEP * PAGE
    assert n_s == TM_D and n_p % TM_D == 0 and N_CLASSES <= LANES

    w_in_b = w_in[0].astype(BF16)
    w_o_b = w_o[0].astype(BF16)
    wr_t = jnp.zeros((32, D_MODEL), F32).at[0:N_EG].set(w_rg[0].T).at[N_EG:N_EG + N_EG * EPG].set(w_re[0].T)
    wr_t = wr_t.astype(BF16)
    br = jnp.zeros((32, 1), F32).at[0:N_EG, 0].set(b_rg[0]).at[N_EG:N_EG + N_EG * EPG, 0].set(b_re[0])
    wg_b, wu_b, wd_b = w_gate[0], w_up[0], w_down[0]
    ws_tril = jnp.tril(w_s[0])
    ws_p = ws_tril.astype(BF16)
    bs_p = b_s[0][:, :, None]
    same_seq = np.kron(np.eye(dec_b, dtype=np.float32), np.ones((dec_t, dec_t), np.float32))
    rep = np.tile(np.eye(dec_t, dtype=np.float32), (dec_b, 1))
    ws_rep = jnp.einsum('ri,gij,cj->grc', rep, ws_tril[:, :dec_t, :dec_t], rep,
                        precision=lax.Precision.HIGHEST)
    ws_s = (ws_rep * same_seq).astype(BF16)
    bs_s = jnp.tile(b_s[0][:, :dec_t], (1, dec_b))[:, :, None]
    gln = g_sg_ln[0]
    bln = b_sg_ln[0]
    lam = (jnp.exp(jnp.sum(lam_q1[0] * lam_k1[0])) - jnp.exp(jnp.sum(lam_q2[0] * lam_k2[0]))
           + LAM_INIT).reshape(1).astype(F32)

    blk = BIAS_BLOCK
    n_dist = max(2 * blk, TAIL_TOKENS + dec_t)
    onehot = np.eye(N_BUCKETS, dtype=np.float32)[_bucket_table(n_dist)]
    ft = jnp.dot(onehot, rel_bias - rel_bias[N_BUCKETS - 1], precision=lax.Precision.HIGHEST).T
    neg = lambda n: jnp.full((N_HEADS, n), NEG, F32)
    bias_near = _toeplitz(jnp.concatenate([neg(blk - 1), ft[:, 0:blk]], axis=1), blk, blk)
    bias_next = _toeplitz(ft[:, 1:2 * blk], blk, blk)
    bias_t = jnp.stack([bias_next, bias_near], axis=1) * LOG2E
    bl = _toeplitz(jnp.flip(ft[:, 1:TAIL_TOKENS + dec_t], axis=1), dec_t, TAIL_TOKENS)
    head_eq = jnp.eye(N_HEADS, dtype=F32)
    bias_last = (bl[:, None, :, :, None] * head_eq[:, None, None, None, :])
    bias_last = jnp.broadcast_to(bias_last, (N_HEADS, 2, dec_t, TAIL_TOKENS, N_HEADS)).reshape(
        2 * N_HEADS * dec_t, TAIL_TOKENS * N_HEADS)
    bn = _toeplitz(jnp.concatenate([jnp.flip(ft[:, 0:dec_t], axis=1), neg(dec_t - 1)], axis=1),
                   dec_t, dec_t)
    bn = jnp.where(head_eq[:, None, None, :] > 0, bn[:, :, :, None], NEG)
    bn = jnp.broadcast_to(bn[:, None], (N_HEADS, 2, dec_t, dec_t, N_HEADS)).reshape(
        2 * N_HEADS * dec_t, dec_t * N_HEADS)
    bias_new = jnp.concatenate([bn, jnp.full((bn.shape[0], PAGE - bn.shape[1]), NEG, F32)], axis=1)

    c_all = jnp.concatenate([c_prompt, c_sample, jnp.zeros((4, D_MODEL), F32)], axis=0)
    mod_all = _ada(c_all, w_ada[0], b_ada[0])
    mod_p = mod_all[:batch].reshape(batch, 1, 6 * D_MODEL)
    mod_s = jnp.repeat(mod_all[batch:batch + dec_b], dec_t, axis=0).reshape(1, n_s, 6 * D_MODEL)

    xp = x_prompt.reshape(n_p, D_MODEL)
    xs = x_sample.reshape(n_s, D_MODEL)

    q_p, kf_p, kb_p, vf_p, vt_p, sg_p = _stage_a(xp, mod_p, w_in_b, ws_p, bs_p, gln, bln,
                                                 tm=TM_TOK, chunk=CHUNK, sample=False)
    q_s, kf_s, vf_s, sg_s, vsn_s = _stage_a(xs, mod_s, w_in_b, ws_s, bs_s, gln, bln,
                                            tm=n_s, chunk=n_s, sample=True)

    g_col = g_subln[0].reshape(HEAD_W, 1)
    g_row = g_subln[0].reshape(1, HEAD_W)
    pad = ((0, 0), (0, PAGE - dec_t * N_HEADS), (0, 0))
    knew = jnp.pad(kf_s.reshape(dec_b, dec_t * N_HEADS, HEAD_W), pad).astype(BF16)
    vnew = jnp.pad(vf_s.reshape(dec_b, dec_t * N_HEADS, HEAD_W), pad).astype(BF16)
    ck = cache_k.reshape(-1, HEAD_W)
    cv = cache_v.reshape(-1, HEAD_W)
    o_p, o_s = _attn_fused(page_table.reshape(-1), lam, q_p, kb_p, vt_p, bias_t, g_col,
                           q_s, knew, vnew, bias_last, bias_new, g_row, ck, cv,
                           batch=batch, seq=seq, tq=TQ_ATT, tk=TM_TOK, dec_b=dec_b, n_pages=n_pages)

    x1_p, h2_p, cls_p, info_p = _stage_c(o_p, sg_p, xp, mod_p, w_o_b, wr_t, br, tm=TM_TOK, sample=False)
    x1_s, h2_s, cls_s, info_s = _stage_c(o_s, sg_s, xs, mod_s, w_o_b, wr_t, br, tm=n_s, sample=True)

    tm_e = TM_E
    tm_d = TM_D
    n_max = n_tot // tm_e + N_CLASSES
    n_dt = n_tot // tm_d
    cls = jnp.concatenate([cls_p[0], cls_s[0]]).reshape(n_dt, tm_d)
    classes = jnp.arange(N_CLASSES, dtype=jnp.int32)
    seg_len = jnp.sum((cls[:, :, None] == classes).astype(jnp.int32), axis=1)
    counts = jnp.sum(seg_len, axis=0)
    ntile_c = (counts + tm_e - 1) // tm_e
    tile_end = jnp.cumsum(ntile_c)
    class_base = (tile_end - ntile_c) * tm_e
    nact = tile_end[-1]
    seg_off = class_base[None, :] + jnp.cumsum(seg_len, axis=0) - seg_len
    pad_off = class_base + counts
    pad_len = ntile_c * tm_e - counts
    tile_ids = jnp.arange(n_max, dtype=jnp.int32)
    tile_cls = jnp.sum((tile_ids[:, None] >= tile_end[None, :]).astype(jnp.int32), axis=1)
    last_cls = jnp.sum((nact - 1 >= tile_end).astype(jnp.int32))
    tile_cls = jnp.where(tile_ids < nact, tile_cls, last_cls)
    grp = tile_cls // N_PAIRS
    pidx = tile_cls % N_PAIRS
    pick = lambda table: sum(jnp.where(pidx == p, e, 0) for p, e in enumerate(table))
    tile_ea = (grp * EPG + pick(PAIR_A)).astype(jnp.int32)
    tile_eb = (grp * EPG + pick(PAIR_B)).astype(jnp.int32)
    empty = jnp.zeros((N_SUB - 1, N_CLASSES), jnp.int32)
    seg_off = jnp.concatenate([seg_off.astype(jnp.int32), empty]).reshape(-1)
    seg_len = jnp.concatenate([seg_len, empty]).reshape(-1)

    nact1 = nact.reshape(1).astype(jnp.int32)
    zero_rows = (jnp.sum(pad_len) + (n_max - nact) * tm_e).reshape(1).astype(jnp.int32)
    x_sorted = _dispatch(seg_off, seg_len, pad_off.astype(jnp.int32), pad_len.astype(jnp.int32), zero_rows,
                         nact1, h2_p, h2_s, info_p, info_s, tm=tm_d, n_sub=N_SUB, n_slots=n_max * tm_e)
    ya_sorted, yb_sorted = _moe(tile_ea, tile_eb, nact1, x_sorted, wg_b, wu_b, wd_b, tm=tm_e, n_max=n_max)

    y_p = _final(seg_off, seg_len, x1_p, info_p, mod_p, g_final, ya_sorted, yb_sorted,
                 tm=tm_d, n_sub=N_SUB, sample=False, tile_base=0)
    y_s = _final(seg_off, seg_len, x1_s, info_s, mod_s, g_final, ya_sorted, yb_sorted,
                 tm=tm_d, n_sub=1, sample=True, tile_base=n_p // tm_d)

    return (y_p.reshape(batch, seq, D_MODEL),
            y_s.reshape(dec_b, dec_t, D_MODEL),
            kf_p.reshape(batch, 1, seq, N_HEADS, HEAD_W),
            vf_p.reshape(batch, 1, seq, N_HEADS, HEAD_W),
            kf_s.reshape(dec_b, 1, dec_t, N_HEADS, HEAD_W),
            vf_s.reshape(dec_b, 1, dec_t, N_HEADS, HEAD_W),
            vsn_s.reshape(dec_b, 1, dec_t, N_GROUPS_SG, SG_CH))
```

```python
import functools
import math

import numpy as np
import jax
import jax.numpy as jnp
from jax import lax
from jax.experimental import pallas as pl
from jax.experimental.pallas import tpu as pltpu

F32 = jnp.float32
BF16 = jnp.bfloat16

D_MODEL = 1024
N_HEADS = 4
DK = 64
HEAD_W = 128
QK_W = N_HEADS * HEAD_W
N_GROUPS_SG = 4
SG_CH = 128
CHUNK = 128
PAGE = 128
N_BUCKETS = 32
MAX_DISTANCE = 128
N_EG = 4
EPG = 4
N_PAIRS = 6
N_CLASSES = N_EG * N_PAIRS
D_EXPERT = 512
EPS = 1e-6
LAM_INIT = 0.8 - 0.6 * math.exp(-0.3 * 0)
NEG = -1e30
LOG2E = math.log2(math.e)
LANES = 128
SUBLANES = 8

TM_TOK = 512
STAGE_A_PARTS = 4
TQ_ATT = 2048
PAGES_PER_STEP = 16
PAGES_PER_GROUP = 8
RING = 4
TM_E = 256
MOE_PARTS = 1
TM_D = 256
N_SUB = 4
VMEM_LIMIT = 56 * 1024 * 1024


def _cparams(sem):
    return pltpu.CompilerParams(dimension_semantics=sem, vmem_limit_bytes=VMEM_LIMIT)


def _ada_body(c_ref, w_ref, b_ref, o_ref):
    c = c_ref[...]
    a = (c * jax.nn.sigmoid(c)).astype(BF16)
    o_ref[...] = jnp.dot(a, w_ref[...].astype(BF16), preferred_element_type=F32) + b_ref[...]


def _ada(c_all, w_ada, b_ada):
    m = c_all.shape[0]
    n = w_ada.shape[1]
    tn = 1536
    return pl.pallas_call(
        _ada_body,
        grid=(n // tn,),
        in_specs=[pl.BlockSpec((m, D_MODEL), lambda j: (0, 0)),
                  pl.BlockSpec((D_MODEL, tn), lambda j: (0, j)),
                  pl.BlockSpec((1, tn), lambda j: (0, j))],
        out_specs=pl.BlockSpec((m, tn), lambda j: (0, j)),
        out_shape=jax.ShapeDtypeStruct((m, n), F32),
        compiler_params=_cparams(("arbitrary",)),
        name="adaln",
    )(c_all, w_ada, b_ada.reshape(1, n))


def _stage_a_body(x_ref, mod_ref, w_in_ref, ws_ref, bs_ref, gln_ref, bln_ref,
                  *out_refs, chunk, sample):
    if sample:
        q_ref, kf_ref, vf_ref, sg_ref, vsn_ref = out_refs
    else:
        q_ref, kf_ref, kb_ref, vf_ref, vt_ref, sg_ref = out_refs
    x = x_ref[...]
    tm = x.shape[0]
    mod = mod_ref[0]
    sh1 = mod[:, 0:D_MODEL]
    sc1 = mod[:, D_MODEL:2 * D_MODEL]
    ms = jnp.mean(x * x, axis=-1, keepdims=True)
    h = (x * lax.rsqrt(ms + EPS)) * (1.0 + sc1) + sh1
    hb = h.astype(BF16)
    n_parts = 1 if sample else STAGE_A_PARTS
    rows_p = tm // n_parts
    zs = [jnp.dot(hb[p * rows_p:(p + 1) * rows_p], w_in_ref[...], preferred_element_type=F32)
          for p in range(n_parts)]
    for p, z in enumerate(zs):
        a = p * rows_p
        q = z[:, 0:QK_W] * (DK ** -0.5)
        k = z[:, QK_W:2 * QK_W]
        v = z[:, 2 * QK_W:3 * QK_W]
        for hd in range(N_HEADS):
            kf_ref[pl.ds(a * N_HEADS + hd, rows_p, stride=N_HEADS), :] = k[:, hd * HEAD_W:(hd + 1) * HEAD_W]
            vf_ref[pl.ds(a * N_HEADS + hd, rows_p, stride=N_HEADS), :] = v[:, hd * HEAD_W:(hd + 1) * HEAD_W]
        if sample:
            q_ref[...] = q
        else:
            q_ref[a:a + rows_p, :] = (q * LOG2E).astype(BF16)
            kb_ref[a:a + rows_p, :] = k.astype(BF16)
            vt_ref[0, :, a:a + rows_p] = v.T.astype(BF16)
        u = z[:, 3 * QK_W:4 * QK_W]
        vs = z[:, 4 * QK_W:5 * QK_W]
        for g in range(N_GROUPS_SG):
            lo, hi = g * SG_CH, (g + 1) * SG_CH
            vg = vs[:, lo:hi]
            mu = jnp.mean(vg, axis=-1, keepdims=True)
            dv = vg - mu
            var = jnp.mean(dv * dv, axis=-1, keepdims=True)
            vn = (dv * lax.rsqrt(var + EPS)) * gln_ref[g:g + 1, :] + bln_ref[g:g + 1, :]
            if sample:
                vsn_ref[:, lo:hi] = vn
            vnb = vn.astype(BF16)
            for c in range(rows_p // chunk):
                r0, r1 = c * chunk, (c + 1) * chunk
                s = jnp.dot(ws_ref[g], vnb[r0:r1], preferred_element_type=F32) + bs_ref[g]
                sg_ref[a + r0:a + r1, lo:hi] = (u[r0:r1, lo:hi] * s).astype(BF16)


def _stage_a(x, mod, w_in_b, ws, bs, gln, bln, *, tm, chunk, sample):
    n = x.shape[0]
    nt = n // tm
    mrows = mod.shape[1]
    row = lambda w, dt: jax.ShapeDtypeStruct((n, w), dt)
    blk = lambda w: pl.BlockSpec((tm, w), lambda i: (i, 0))
    cache_shape = jax.ShapeDtypeStruct((n * N_HEADS, HEAD_W), F32)
    cache_blk = pl.BlockSpec((tm * N_HEADS, HEAD_W), lambda i: (i, 0))
    if sample:
        out_shape = [row(QK_W, F32), cache_shape, cache_shape, row(QK_W, BF16), row(QK_W, F32)]
        out_specs = [blk(QK_W), cache_blk, cache_blk, blk(QK_W), blk(QK_W)]
        mod_map = lambda i: (0, 0, 0)
    else:
        out_shape = [row(QK_W, BF16), cache_shape, row(QK_W, BF16), cache_shape,
                     jax.ShapeDtypeStruct((nt, QK_W, tm), BF16), row(QK_W, BF16)]
        out_specs = [blk(QK_W), cache_blk, blk(QK_W), cache_blk,
                     pl.BlockSpec((1, QK_W, tm), lambda i: (i, 0, 0)), blk(QK_W)]
        tiles_per_batch = 4096 // tm
        mod_map = lambda i: (i // tiles_per_batch, 0, 0)
    full = lambda a: pl.BlockSpec(a.shape, lambda i: (0,) * a.ndim)
    return pl.pallas_call(
        functools.partial(_stage_a_body, chunk=chunk, sample=sample),
        grid=(nt,),
        in_specs=[blk(D_MODEL),
                  pl.BlockSpec((1, mrows, 6 * D_MODEL), mod_map),
                  full(w_in_b), full(ws), full(bs), full(gln), full(bln)],
        out_specs=out_specs,
        out_shape=out_shape,
        compiler_params=_cparams(("arbitrary",)),
        name="stage_a_sample" if sample else "stage_a_prompt",
    )(x, mod, w_in_b, ws, bs, gln, bln)


ATT_COLS = 256
BIAS_BLOCK = MAX_DISTANCE


def _attn_body(lam_ref, q_ref, k_ref, vt_ref, bias_ref, g_ref, o_ref, *scratch, tq, tk, between=None):
    n_chain = 2 * tq // ATT_COLS
    q2_refs, m_refs, l_refs, acc_refs = (scratch[i * n_chain:(i + 1) * n_chain] for i in range(4))
    qi = pl.program_id(2)
    if between is not None:
        between(0)
    for c in range(n_chain):
        q0 = (c * ATT_COLS) % tq
        q = q_ref[q0:q0 + ATT_COLS, :]
        lane = lax.broadcasted_iota(jnp.int32, q.shape, 1)
        keep = (lane < DK) if c < n_chain // 2 else (lane >= DK)
        q2_refs[c][...] = jnp.where(keep, q, jnp.zeros_like(q))
        m_refs[c][...] = jnp.full(m_refs[c].shape, NEG, F32)
        l_refs[c][...] = jnp.zeros(l_refs[c].shape, F32)
        acc_refs[c][...] = jnp.zeros(acc_refs[c].shape, F32)

    blk = BIAS_BLOCK

    kblocks = tk // blk

    def block_kinds(rel, q0):
        return [[(q0 // blk + b) - (rel + a) for b in range(ATT_COLS // blk)] for a in range(kblocks)]

    def keys_needed(rel, q0):
        if rel is None:
            return kblocks
        return sum(1 for row in block_kinds(rel, q0) if max(row) >= 0)

    def with_bias(s, rel, q0):
        if rel is None:
            return s
        kinds = block_kinds(rel, q0)[:s.shape[0] // blk]
        if all(d >= 2 for row in kinds for d in row):
            return s
        nxt, near = bias_ref[0, 0], bias_ref[0, 1]
        pick = lambda d: (jnp.full((blk, blk), NEG, F32) if d < 0 else near if d == 0 else nxt if d == 1
                          else jnp.zeros((blk, blk), F32))
        rows = []
        for a, row in enumerate(kinds):
            s_row = s[a * blk:(a + 1) * blk]
            if any(d < 2 for d in row):
                s_row = s_row + jnp.concatenate([pick(d) for d in row], axis=1)
            rows.append(s_row)
        return jnp.concatenate(rows, axis=0)

    def score(j, rel, c):
        nk = keys_needed(rel, (c * ATT_COLS) % tq) * blk
        if nk == 0:
            return None
        k = k_ref[pl.ds(pl.multiple_of(j * tk, tk), nk), :]
        s = lax.dot_general(k, q2_refs[c][...], (((1,), (1,)), ((), ())),
                            preferred_element_type=F32)
        return with_bias(s, rel, (c * ATT_COLS) % tq)

    def accumulate(j, c, s):
        vt = vt_ref[j, :, 0:s.shape[0]]
        m_old = m_refs[c][...]
        m_new = jnp.maximum(m_old, jnp.max(s, axis=0, keepdims=True))
        alpha = jnp.exp2(m_old - m_new)
        p = jnp.exp2(s - m_new)
        l_refs[c][...] = alpha * l_refs[c][...] + jnp.sum(p, axis=0, keepdims=True)
        acc_refs[c][...] = alpha * acc_refs[c][...] + jnp.dot(vt, p.astype(BF16), preferred_element_type=F32)
        m_refs[c][...] = m_new

    def tiles(*work):
        scores = [[score(j, rel, c) for c in range(n_chain)] for j, rel in work]
        for (j, _), tile_scores in zip(work, scores):
            for c, s in enumerate(tile_scores):
                if s is not None:
                    accumulate(j, c, s)

    ratio = tq // tk
    first_diag = qi * ratio
    n_plain = jnp.maximum(first_diag - 1, 0)

    def plain_pair(jj, carry):
        tiles((2 * jj, None), (2 * jj + 1, None))
        return carry

    diag = [(first_diag + r, r * kblocks) for r in range(ratio)]
    if between is None:
        lax.fori_loop(0, n_plain // 2, plain_pair, 0)

        @pl.when(qi >= 1)
        def _():
            if ratio % 2 == 0:
                tiles((first_diag - 2, None), (first_diag - 1, -kblocks))
            else:
                @pl.when(n_plain % 2 == 1)
                def _():
                    tiles((n_plain - 1, None))
                tiles((first_diag - 1, -kblocks))

        tiles(*diag)
    else:
        assert ratio % 2 == 0 and ratio >= 4
        half = ratio // 2

        @pl.when(qi == 0)
        def _():
            tiles(*diag[:half])

        @pl.when(qi >= 1)
        def _():
            lax.fori_loop(0, n_plain // 2, plain_pair, 0)

        between(1)

        @pl.when(qi == 0)
        def _():
            tiles(*diag[half:])

        @pl.when(qi >= 1)
        def _():
            tiles((first_diag - 2, None), (first_diag - 1, -kblocks))

        between(2)

        @pl.when(qi >= 1)
        def _():
            tiles(*diag)

        between(3)

    lam = lam_ref[0]
    o_all = jnp.concatenate([acc_refs[c][...] * (1.0 / l_refs[c][...]) for c in range(n_chain)],
                            axis=1)
    o = o_all[:, 0:tq] - lam * o_all[:, tq:2 * tq]
    ms = jnp.mean(o * o, axis=0, keepdims=True)
    on = (o * lax.rsqrt(ms + EPS)) * g_ref[...] * (1.0 - LAM_INIT)
    o_ref[...] = on.T.astype(BF16)


def _attn_prompt(lam, q, kb, vt, bias_t, g_col, *, batch, seq, tq, tk):
    nq = seq // tq
    nk = seq // tk
    n = batch * seq
    n_chain = 2 * tq // ATT_COLS
    assert vt.shape == (batch * nk, QK_W, tk) and tq % tk == 0 and tk % BIAS_BLOCK == 0
    return pl.pallas_call(
        functools.partial(_attn_body, tq=tq, tk=tk),
        grid=(batch, N_HEADS, nq),
        in_specs=[pl.BlockSpec(memory_space=pltpu.SMEM),
                  pl.BlockSpec((tq, HEAD_W), lambda b, h, i: (b * nq + i, h)),
                  pl.BlockSpec((seq, HEAD_W), lambda b, h, i: (b, h)),
                  pl.BlockSpec((nk, HEAD_W, tk), lambda b, h, i: (b, h, 0)),
                  pl.BlockSpec((1, 2, BIAS_BLOCK, BIAS_BLOCK), lambda b, h, i: (h, 0, 0, 0)),
                  pl.BlockSpec((HEAD_W, 1), lambda b, h, i: (0, 0))],
        out_specs=pl.BlockSpec((tq, HEAD_W), lambda b, h, i: (b * nq + i, h)),
        out_shape=jax.ShapeDtypeStruct((n, QK_W), BF16),
        scratch_shapes=([pltpu.VMEM((ATT_COLS, HEAD_W), BF16)] * n_chain
                        + [pltpu.VMEM((1, ATT_COLS), F32)] * (2 * n_chain)
                        + [pltpu.VMEM((HEAD_W, ATT_COLS), F32)] * n_chain),
        compiler_params=_cparams(("arbitrary", "arbitrary", "arbitrary")),
        name="attn_prompt",
    )(lam, q, kb, vt, bias_t, g_col)


PAGE_ROWS = PAGE * N_HEADS
TAIL_TOKENS = 2 * PAGE


def _sattn_body(pt_ref, lam_ref, q_ref, knew_ref, vnew_ref, bl_ref, bn_ref, g_ref, ck_ref, cv_ref,
                o_ref, kbuf, vbuf, sem, mask_ref, m_ref, l_ref, acc_ref, *, pages, n_steps, total):
    b = pl.program_id(0)
    s = pl.program_id(1)
    step = b * n_steps + s
    slot = step % 2

    def page_copies(step_idx, sl):
        base = step_idx * pages
        out = []
        for i in range(pages):
            src = pl.ds(pl.multiple_of(pt_ref[base + i] * PAGE_ROWS, PAGE_ROWS), PAGE_ROWS)
            dst = pl.ds(i * PAGE_ROWS, PAGE_ROWS)
            out.append(pltpu.make_async_copy(ck_ref.at[src], kbuf.at[sl, dst], sem.at[sl, 0]))
            out.append(pltpu.make_async_copy(cv_ref.at[src], vbuf.at[sl, dst], sem.at[sl, 1]))
        return out

    def start_all(copies):
        for n, c in enumerate(copies):
            c.start(priority=n % 2)

    @pl.when(step == 0)
    def _():
        start_all(page_copies(0, 0))

    @pl.when(step + 1 < total)
    def _():
        start_all(page_copies(step + 1, 1 - slot))

    for c in page_copies(step, slot):
        c.wait()

    @pl.when(s == 0)
    def _():
        m_ref[...] = jnp.full(m_ref.shape, NEG, F32)
        l_ref[...] = jnp.zeros(l_ref.shape, F32)
        acc_ref[...] = jnp.zeros(acc_ref.shape, F32)

    @pl.when(step == 0)
    def _():
        row = lax.broadcasted_iota(jnp.int32, mask_ref.shape, 0)
        col = lax.broadcasted_iota(jnp.int32, mask_ref.shape, 1)
        same_head = (col % N_HEADS) == (row // (2 * SUBLANES))
        mask_ref[...] = jnp.where(same_head, 0.0, NEG)

    q = q_ref[...]
    lane = lax.broadcasted_iota(jnp.int32, (SUBLANES, HEAD_W), 1)
    pieces = []
    for h in range(N_HEADS):
        qh = q[:, h * HEAD_W:(h + 1) * HEAD_W]
        pieces += [jnp.where(lane < DK, qh, 0.0), jnp.where(lane >= DK, qh, 0.0)]
    qm = jnp.concatenate(pieces, axis=0).astype(BF16)

    def update(kb, vb, bias):
        sc = lax.dot_general(qm, kb, (((1,), (1,)), ((), ())),
                             preferred_element_type=F32) + bias
        m_old = m_ref[...]
        m_new = jnp.maximum(m_old, jnp.max(sc, axis=1, keepdims=True))
        alpha = jnp.exp(m_old - m_new)
        p = jnp.exp(sc - m_new)
        l_ref[...] = alpha * l_ref[...] + jnp.sum(p, axis=1, keepdims=True)
        acc_ref[...] = alpha * acc_ref[...] + jnp.dot(p.astype(BF16), vb,
                                                      preferred_element_type=F32)
        m_ref[...] = m_new

    is_last = s == n_steps - 1
    head_cols = mask_ref.shape[1] - bl_ref.shape[1]
    update(kbuf[slot].astype(BF16), vbuf[slot].astype(BF16),
           jnp.concatenate([mask_ref[:, :head_cols],
                            mask_ref[:, head_cols:] + bl_ref[...] * is_last.astype(F32)], axis=1))

    @pl.when(is_last)
    def _():
        update(knew_ref[0], vnew_ref[0], bn_ref[...])
        lam = lam_ref[0]
        o_all = acc_ref[...] * (1.0 / l_ref[...])
        for h in range(N_HEADS):
            r = h * 2 * SUBLANES
            o = o_all[r:r + SUBLANES] - lam * o_all[r + SUBLANES:r + 2 * SUBLANES]
            ms = jnp.mean(o * o, axis=-1, keepdims=True)
            o_ref[:, h * HEAD_W:(h + 1) * HEAD_W] = ((o * lax.rsqrt(ms + EPS)) * g_ref[...]
                                                     * (1.0 - LAM_INIT))


def _attn_sample(page_table_flat, lam, q_s, knew, vnew, bias_last, bias_new, g_row, cache_k, cache_v,
                 *, dec_b, n_pages):
    pages = PAGES_PER_STEP
    n_steps = n_pages // pages
    total = dec_b * n_steps
    nq = q_s.shape[0] // dec_b
    n_rows = 2 * N_HEADS * nq
    step_rows = pages * PAGE_ROWS
    grid_spec = pltpu.PrefetchScalarGridSpec(
        num_scalar_prefetch=1,
        grid=(dec_b, n_steps),
        in_specs=[pl.BlockSpec(memory_space=pltpu.SMEM),
                  pl.BlockSpec((nq, QK_W), lambda b, s, pt: (b, 0)),
                  pl.BlockSpec((1, PAGE, HEAD_W), lambda b, s, pt: (b, 0, 0)),
                  pl.BlockSpec((1, PAGE, HEAD_W), lambda b, s, pt: (b, 0, 0)),
                  pl.BlockSpec(bias_last.shape, lambda b, s, pt: (0, 0)),
                  pl.BlockSpec(bias_new.shape, lambda b, s, pt: (0, 0)),
                  pl.BlockSpec((1, HEAD_W), lambda b, s, pt: (0, 0)),
                  pl.BlockSpec(memory_space=pl.ANY),
                  pl.BlockSpec(memory_space=pl.ANY)],
        out_specs=pl.BlockSpec((nq, QK_W), lambda b, s, pt: (b, 0)),
        scratch_shapes=[pltpu.VMEM((2, step_rows, HEAD_W), F32),
                        pltpu.VMEM((2, step_rows, HEAD_W), F32),
                        pltpu.SemaphoreType.DMA((2, 2)),
                        pltpu.VMEM((n_rows, step_rows), F32),
                        pltpu.VMEM((n_rows, 1), F32), pltpu.VMEM((n_rows, 1), F32),
                        pltpu.VMEM((n_rows, HEAD_W), F32)])
    return pl.pallas_call(
        functools.partial(_sattn_body, pages=pages, n_steps=n_steps, total=total),
        grid_spec=grid_spec,
        out_shape=jax.ShapeDtypeStruct(q_s.shape, F32),
        compiler_params=_cparams(("arbitrary", "arbitrary")),
        name="attn_sample",
    )(page_table_flat, lam, q_s, knew, vnew, bias_last, bias_new, g_row, cache_k, cache_v)


def _attn_fused_body(pt_ref, lam_ref, q_ref, k_ref, vt_ref, bias_ref, gcol_ref,
                     qs_ref, knew_ref, vnew_ref, bl_ref, bn_ref, grow_ref, ck_ref, cv_ref,
                     o_ref, os_ref, *scratch, tq, tk, pages, n_sub, n_steps):
    n_prompt_scratch = 4 * (2 * tq // ATT_COLS)
    kbuf, vbuf, sem, mask_ref, qm_ref, m_ref, l_ref, acc_ref = scratch[n_prompt_scratch:]
    step = (pl.program_id(0) * pl.num_programs(1) + pl.program_id(1)) * pl.num_programs(2) + pl.program_id(2)

    def page_copies(group, sl):
        base = group * pages
        out = []
        for i in range(pages):
            src = pl.ds(pl.multiple_of(pt_ref[base + i] * PAGE_ROWS, PAGE_ROWS), PAGE_ROWS)
            dst = pl.ds(i * PAGE_ROWS, PAGE_ROWS)
            out.append(pltpu.make_async_copy(ck_ref.at[src], kbuf.at[sl, dst], sem.at[sl, 0]))
            out.append(pltpu.make_async_copy(cv_ref.at[src], vbuf.at[sl, dst], sem.at[sl, 1]))
        return out

    def start_all(copies):
        for c in copies:
            c.start()

    def update(kb, vb, bias):
        sc = lax.dot_general(qm_ref[...], kb, (((1,), (1,)), ((), ())),
                             preferred_element_type=F32) + bias
        m_old = m_ref[...]
        m_new = jnp.maximum(m_old, jnp.max(sc, axis=1, keepdims=True))
        alpha = jnp.exp(m_old - m_new)
        p = jnp.exp(sc - m_new)
        l_ref[...] = alpha * l_ref[...] + jnp.sum(p, axis=1, keepdims=True)
        acc_ref[...] = alpha * acc_ref[...] + jnp.dot(p.astype(BF16), vb, preferred_element_type=F32)
        m_ref[...] = m_new

    def sample_group(u):
        group = step * n_sub + u
        slot = u % RING
        ahead = RING - 1
        if u == 0:
            @pl.when(step == 0)
            def _():
                for g0 in range(ahead):
                    start_all(page_copies(g0, g0))
                row = lax.broadcasted_iota(jnp.int32, mask_ref.shape, 0)
                col = lax.broadcasted_iota(jnp.int32, mask_ref.shape, 1)
                same_head = (col % N_HEADS) == (row // (2 * SUBLANES))
                mask_ref[...] = jnp.where(same_head, 0.0, NEG)
        if u + ahead < n_sub:
            start_all(page_copies(group + ahead, (u + ahead) % RING))
        else:
            @pl.when(step + 1 < n_steps)
            def _():
                start_all(page_copies(group + ahead, (u + ahead) % RING))
        for c in page_copies(group, slot):
            c.wait()
        if u == 0:
            m_ref[...] = jnp.full(m_ref.shape, NEG, F32)
            l_ref[...] = jnp.zeros(l_ref.shape, F32)
            acc_ref[...] = jnp.zeros(acc_ref.shape, F32)
            q = qs_ref[...]
            lane = lax.broadcasted_iota(jnp.int32, (SUBLANES, HEAD_W), 1)
            pieces = []
            for h in range(N_HEADS):
                qh = q[:, h * HEAD_W:(h + 1) * HEAD_W]
                pieces += [jnp.where(lane < DK, qh, 0.0), jnp.where(lane >= DK, qh, 0.0)]
            qm_ref[...] = jnp.concatenate(pieces, axis=0).astype(BF16)
        bias = mask_ref[...]
        if u == n_sub - 1:
            head_cols = mask_ref.shape[1] - bl_ref.shape[1]
            bias = jnp.concatenate([bias[:, :head_cols], bias[:, head_cols:] + bl_ref[...]], axis=1)
        update(kbuf[slot].astype(BF16), vbuf[slot].astype(BF16), bias)
        if u == n_sub - 1:
            update(knew_ref[0], vnew_ref[0], bn_ref[...])
            lam = lam_ref[0]
            o_all = acc_ref[...] * (1.0 / l_ref[...])
            for h in range(N_HEADS):
                r = h * 2 * SUBLANES
                o = o_all[r:r + SUBLANES] - lam * o_all[r + SUBLANES:r + 2 * SUBLANES]
                ms = jnp.mean(o * o, axis=-1, keepdims=True)
                os_ref[:, h * HEAD_W:(h + 1) * HEAD_W] = ((o * lax.rsqrt(ms + EPS)) * grow_ref[...]
                                                          * (1.0 - LAM_INIT))

    per_call = n_sub // 4

    def between(k):
        for u in range(per_call * k, per_call * (k + 1)):
            sample_group(u)

    _attn_body(lam_ref, q_ref, k_ref, vt_ref, bias_ref, gcol_ref, o_ref, *scratch[:n_prompt_scratch],
               tq=tq, tk=tk, between=between)


def _attn_fused(page_table_flat, lam, q, kb, vt, bias_t, g_col, q_s, knew, vnew, bias_last, bias_new, g_row,
                cache_k, cache_v, *, batch, seq, tq, tk, dec_b, n_pages):
    nq = seq // tq
    nk = seq // tk
    n = batch * seq
    n_chain = 2 * tq // ATT_COLS
    pages = PAGES_PER_GROUP
    n_sub = n_pages // pages
    n_steps = batch * N_HEADS * nq
    dec_t = q_s.shape[0] // dec_b
    n_rows = 2 * N_HEADS * dec_t
    step_rows = pages * PAGE_ROWS
    assert n_steps == dec_b and n_sub % 4 == 0 and n_sub % RING == 0 and nq == 2
    assert bias_last.shape[1] <= step_rows
    assert vt.shape == (batch * nk, QK_W, tk) and tq % tk == 0 and tk % BIAS_BLOCK == 0
    lin = lambda b, h, i: (b * N_HEADS + h) * nq + i
    grid_spec = pltpu.PrefetchScalarGridSpec(
        num_scalar_prefetch=1,
        grid=(batch, N_HEADS, nq),
        in_specs=[pl.BlockSpec(memory_space=pltpu.SMEM),
                  pl.BlockSpec((tq, HEAD_W), lambda b, h, i, pt: (b * nq + i, h)),
                  pl.BlockSpec((seq, HEAD_W), lambda b, h, i, pt: (b, h)),
                  pl.BlockSpec((nk, HEAD_W, tk), lambda b, h, i, pt: (b, h, 0)),
                  pl.BlockSpec((1, 2, BIAS_BLOCK, BIAS_BLOCK), lambda b, h, i, pt: (h, 0, 0, 0)),
                  pl.BlockSpec((HEAD_W, 1), lambda b, h, i, pt: (0, 0)),
                  pl.BlockSpec((dec_t, QK_W), lambda b, h, i, pt: (lin(b, h, i), 0)),
                  pl.BlockSpec((1, PAGE, HEAD_W), lambda b, h, i, pt: (lin(b, h, i), 0, 0)),
                  pl.BlockSpec((1, PAGE, HEAD_W), lambda b, h, i, pt: (lin(b, h, i), 0, 0)),
                  pl.BlockSpec(bias_last.shape, lambda b, h, i, pt: (0, 0)),
                  pl.BlockSpec(bias_new.shape, lambda b, h, i, pt: (0, 0)),
                  pl.BlockSpec((1, HEAD_W), lambda b, h, i, pt: (0, 0)),
                  pl.BlockSpec(memory_space=pl.ANY),
                  pl.BlockSpec(memory_space=pl.ANY)],
        out_specs=[pl.BlockSpec((tq, HEAD_W), lambda b, h, i, pt: (b * nq + i, h)),
                   pl.BlockSpec((dec_t, QK_W), lambda b, h, i, pt: (lin(b, h, i), 0))],
        scratch_shapes=([pltpu.VMEM((ATT_COLS, HEAD_W), BF16)] * n_chain
                        + [pltpu.VMEM((1, ATT_COLS), F32)] * (2 * n_chain)
                        + [pltpu.VMEM((HEAD_W, ATT_COLS), F32)] * n_chain
                        + [pltpu.VMEM((RING, step_rows, HEAD_W), F32),
                           pltpu.VMEM((RING, step_rows, HEAD_W), F32),
                           pltpu.SemaphoreType.DMA((RING, 2)),
                           pltpu.VMEM((n_rows, step_rows), F32),
                           pltpu.VMEM((n_rows, HEAD_W), BF16),
                           pltpu.VMEM((n_rows, 1), F32), pltpu.VMEM((n_rows, 1), F32),
                           pltpu.VMEM((n_rows, HEAD_W), F32)]))
    return pl.pallas_call(
        functools.partial(_attn_fused_body, tq=tq, tk=tk, pages=pages, n_sub=n_sub, n_steps=n_steps),
        grid_spec=grid_spec,
        out_shape=[jax.ShapeDtypeStruct((n, QK_W), BF16), jax.ShapeDtypeStruct(q_s.shape, F32)],
        compiler_params=_cparams(("arbitrary", "arbitrary", "arbitrary")),
        name="attn_fused",
    )(page_table_flat, lam, q, kb, vt, bias_t, g_col, q_s, knew, vnew, bias_last, bias_new, g_row,
      cache_k, cache_v)


INFO_W_A, INFO_W_B, INFO_CLS = 0, 1, 2
PAIR_A = (0, 0, 0, 1, 1, 3)
PAIR_B = (1, 2, 3, 3, 2, 2)


def _stage_c_body(o_ref, sg_ref, x_ref, mod_ref, wo_ref, wr_ref, br_ref, x1_ref, h2_ref, cls_ref, info_ref):
    x = x_ref[...]
    tm = x.shape[0]
    mod = mod_ref[0]
    g1 = mod[:, 2 * D_MODEL:3 * D_MODEL]
    sh2 = mod[:, 3 * D_MODEL:4 * D_MODEL]
    sc2 = mod[:, 4 * D_MODEL:5 * D_MODEL]
    mix = (jnp.dot(o_ref[...].astype(BF16), wo_ref[0:QK_W, :], preferred_element_type=F32)
           + jnp.dot(sg_ref[...], wo_ref[QK_W:2 * QK_W, :], preferred_element_type=F32))
    x1 = x + g1 * mix
    x1_ref[...] = x1
    ms = jnp.mean(x1 * x1, axis=-1, keepdims=True)
    h2 = ((x1 * lax.rsqrt(ms + EPS)) * (1.0 + sc2) + sh2).astype(BF16)
    h2_ref[...] = h2
    lg = lax.dot_general(wr_ref[...], h2, (((1,), (1,)), ((), ())),
                         preferred_element_type=F32) + br_ref[...]
    gl = [lg[i:i + 1, :] for i in range(N_EG)]
    el = [lg[N_EG + i:N_EG + i + 1, :] for i in range(N_EG * EPG)]
    gmax = jnp.maximum(jnp.maximum(gl[0], gl[1]), jnp.maximum(gl[2], gl[3]))
    gi = jnp.where(gl[0] == gmax, 0, jnp.where(gl[1] == gmax, 1, jnp.where(gl[2] == gmax, 2, 3)))
    gsum = (jnp.exp(gl[0] - gmax) + jnp.exp(gl[1] - gmax)
            + jnp.exp(gl[2] - gmax) + jnp.exp(gl[3] - gmax))
    gp = 1.0 / gsum
    sel = [jnp.where(gi == 0, el[j], jnp.where(gi == 1, el[EPG + j],
                                               jnp.where(gi == 2, el[2 * EPG + j], el[3 * EPG + j])))
           for j in range(EPG)]
    v0 = jnp.maximum(jnp.maximum(sel[0], sel[1]), jnp.maximum(sel[2], sel[3]))
    i0 = jnp.where(sel[0] == v0, 0, jnp.where(sel[1] == v0, 1, jnp.where(sel[2] == v0, 2, 3)))
    rest = [jnp.where(i0 == j, -3e38, sel[j]) for j in range(EPG)]
    v1 = jnp.maximum(jnp.maximum(rest[0], rest[1]), jnp.maximum(rest[2], rest[3]))
    i1 = jnp.where(rest[0] == v1, 0, jnp.where(rest[1] == v1, 1, jnp.where(rest[2] == v1, 2, 3)))
    e1 = jnp.exp(v1 - v0)
    den = 1.0 / (1.0 + e1)
    tw0 = den * gp
    tw1 = e1 * den * gp
    first_low = i0 < i1
    lo = jnp.where(first_low, i0, i1)
    hi = jnp.where(first_low, i1, i0)
    w_lo = jnp.where(first_low, tw0, tw1)
    w_hi = jnp.where(first_low, tw1, tw0)
    pair = jnp.where(lo == 0, hi - 1, jnp.where(lo == 2, 5, jnp.where(hi == 3, 3, 4)))
    swapped = pair == 5
    w_a = jnp.where(swapped, w_hi, w_lo)
    w_b = jnp.where(swapped, w_lo, w_hi)
    cls = gi * N_PAIRS + pair
    cls_ref[...] = jnp.broadcast_to(cls, cls_ref.shape).astype(jnp.int32)
    row = lax.broadcasted_iota(jnp.int32, (LANES, tm), 0)
    rec = jnp.where(row == INFO_W_A, w_a, jnp.where(row == INFO_W_B, w_b,
                                                    jnp.where(row == INFO_CLS, cls.astype(F32), 0.0)))
    info_ref[...] = rec.T


def _stage_c(o, sg, x, mod, wo_b, wr_t, br, *, tm, sample):
    n = x.shape[0]
    nt = n // tm
    mrows = mod.shape[1]
    blk = lambda w: pl.BlockSpec((tm, w), lambda i: (i, 0))
    full = lambda a: pl.BlockSpec(a.shape, lambda i: (0,) * a.ndim)
    if sample:
        mod_map = lambda i: (0, 0, 0)
    else:
        tiles_per_batch = 4096 // tm
        mod_map = lambda i: (i // tiles_per_batch, 0, 0)
    return pl.pallas_call(
        _stage_c_body,
        grid=(nt,),
        in_specs=[blk(QK_W), blk(QK_W), blk(D_MODEL),
                  pl.BlockSpec((1, mrows, 6 * D_MODEL), mod_map),
                  full(wo_b), full(wr_t), full(br)],
        out_specs=[blk(D_MODEL), blk(D_MODEL),
                   pl.BlockSpec((SUBLANES, tm), lambda i: (0, i)),
                   blk(LANES)],
        out_shape=[jax.ShapeDtypeStruct((n, D_MODEL), F32),
                   jax.ShapeDtypeStruct((n, D_MODEL), BF16),
                   jax.ShapeDtypeStruct((SUBLANES, n), jnp.int32),
                   jax.ShapeDtypeStruct((n, LANES), F32)],
        compiler_params=_cparams(("arbitrary",)),
        name="stage_c_sample" if sample else "stage_c_prompt",
    )(o, sg, x, mod, wo_b, wr_t, br)


def rows8(ref, start, count):
    scale = lambda v: v * SUBLANES if isinstance(v, int) else pl.multiple_of(v * SUBLANES, SUBLANES)
    return ref.at[pl.ds(scale(start), scale(count))]


def _perm_t(cls_col):
    n = cls_col.shape[0]
    lane = lax.broadcasted_iota(jnp.int32, (n, LANES), 1).astype(F32)
    onehot = (lane == cls_col).astype(BF16)
    r = lax.broadcasted_iota(jnp.int32, (n, n), 0)
    c = lax.broadcasted_iota(jnp.int32, (n, n), 1)
    before = (c < r).astype(BF16)
    rank = jnp.dot(before, onehot, preferred_element_type=F32)
    cnt = jnp.sum(onehot.astype(F32), axis=0, keepdims=True)
    cr = lax.broadcasted_iota(jnp.int32, (LANES, LANES), 0)
    cc = lax.broadcasted_iota(jnp.int32, (LANES, LANES), 1)
    lower_cls = (cr < cc).astype(BF16)
    base = jnp.dot(jnp.broadcast_to(cnt, (SUBLANES, LANES)).astype(BF16), lower_cls,
                   preferred_element_type=F32)[0:1, :]
    pos = jnp.sum(onehot.astype(F32) * (base + rank), axis=1, keepdims=True)
    dest = lax.broadcasted_iota(jnp.int32, (n, n), 1).astype(F32)
    return (dest == pos).astype(F32)


def _dispatch_body(soff_ref, slen_ref, poff_ref, plen_ref, ptot_ref, nact_ref, hp_ref, hs_ref, ip_ref, is_ref,
                   xs_ref, buf, zbuf, sem, zsem, *, tm, n_sub, n_tiles, n_slab_tiles):
    i = pl.program_id(0)
    slot = i % 2
    is_sample = i == n_tiles - 1

    def wait_tile(sl, tokens):
        pltpu.make_async_copy(rows8(buf.at[sl], 0, tokens), rows8(buf.at[sl], 0, tokens), sem.at[sl]).wait()

    @pl.when(i == 0)
    def _():
        zbuf[...] = jnp.zeros(zbuf.shape, F32)
        for c in range(N_CLASSES):
            @pl.when(plen_ref[c] > 0)
            def _():
                pltpu.make_async_copy(rows8(zbuf, 0, plen_ref[c]), rows8(xs_ref, poff_ref[c], plen_ref[c]),
                                      zsem).start()
        for j in range(n_slab_tiles - N_CLASSES, n_slab_tiles):
            @pl.when(j >= nact_ref[0])
            def _():
                pltpu.make_async_copy(zbuf, rows8(xs_ref, j * TM_E, TM_E), zsem).start()

    xp = []
    for u in range(n_sub):
        rows = slice(u * tm, (u + 1) * tm)
        x = hp_ref[rows, :]
        info = ip_ref[rows, :]
        if u == 0:
            x = jnp.where(is_sample, hs_ref[...], x)
            info = jnp.where(is_sample, is_ref[...], info)
        perm = _perm_t(info[:, INFO_CLS:INFO_CLS + 1]).T.astype(BF16)
        xp.append(jnp.dot(perm, x, preferred_element_type=F32))

    @pl.when(i >= 2)
    def _():
        wait_tile(slot, n_sub * tm)

    bs = buf.at[slot]
    for u in range(n_sub):
        for c in range(D_MODEL // LANES):
            bs[pl.ds(u * tm * SUBLANES + c, tm, stride=SUBLANES), :] = xp[u][:, c * LANES:(c + 1) * LANES]
    for u in range(n_sub):
        local = u * tm
        for c in range(N_CLASSES):
            k = (i * n_sub + u) * N_CLASSES + c
            n_rows = slen_ref[k]

            @pl.when(n_rows > 0)
            def _():
                pltpu.make_async_copy(rows8(bs, local, n_rows), rows8(xs_ref, soff_ref[k], n_rows),
                                      sem.at[slot]).start()
            local = local + n_rows

    @pl.when(i == n_tiles - 1)
    def _():
        wait_tile(slot, tm)
        if n_tiles >= 2:
            wait_tile(1 - slot, n_sub * tm)

        @pl.when(ptot_ref[0] > 0)
        def _():
            n = pl.multiple_of(ptot_ref[0] * SUBLANES, SUBLANES)
            pltpu.make_async_copy(xs_ref.at[pl.ds(0, n)], xs_ref.at[pl.ds(0, n)], zsem).wait()


def _dispatch(seg_off, seg_len, pad_off, pad_len, pad_tot, nact, h2_p, h2_s, info_p, info_s, *, tm, n_sub, n_slots):
    n_prompt_steps = h2_p.shape[0] // (n_sub * tm)
    n_tiles = n_prompt_steps + 1
    assert h2_s.shape[0] == tm and h2_p.shape[0] % (n_sub * tm) == 0
    assert seg_len.shape[0] == n_tiles * n_sub * N_CLASSES
    last_p = n_prompt_steps - 1
    grid_spec = pltpu.PrefetchScalarGridSpec(
        num_scalar_prefetch=6,
        grid=(n_tiles,),
        in_specs=[pl.BlockSpec((n_sub * tm, D_MODEL), lambda i, *_: (jnp.minimum(i, last_p), 0)),
                  pl.BlockSpec((tm, D_MODEL), lambda i, *_: (0, 0)),
                  pl.BlockSpec((n_sub * tm, LANES), lambda i, *_: (jnp.minimum(i, last_p), 0)),
                  pl.BlockSpec((tm, LANES), lambda i, *_: (0, 0))],
        out_specs=pl.BlockSpec(memory_space=pl.ANY),
        scratch_shapes=[pltpu.VMEM((2, n_sub * tm * SUBLANES, LANES), F32),
                        pltpu.VMEM((TM_E * SUBLANES, LANES), F32),
                        pltpu.SemaphoreType.DMA((2,)),
                        pltpu.SemaphoreType.DMA(())])
    return pl.pallas_call(
        functools.partial(_dispatch_body, tm=tm, n_sub=n_sub, n_tiles=n_tiles, n_slab_tiles=n_slots // TM_E),
        grid_spec=grid_spec,
        out_shape=jax.ShapeDtypeStruct((n_slots * SUBLANES, LANES), F32),
        compiler_params=_cparams(("arbitrary",)),
        name="moe_dispatch",
    )(seg_off, seg_len, pad_off, pad_len, pad_tot, nact, h2_p, h2_s, info_p, info_s)


def _moe_body(ea_ref, eb_ref, nact_ref, x_ref, wga_ref, wgb_ref, wua_ref, wub_ref, wda_ref, wdb_ref,
              ya_ref, yb_ref, *, tm):
    i = pl.program_id(0)

    @pl.when(i < nact_ref[0])
    def _():
        x = jnp.concatenate([x_ref[pl.ds(c, tm, stride=SUBLANES), :] for c in range(D_MODEL // LANES)],
                            axis=1).astype(BF16)

        wts = [r[0].astype(BF16) for r in (wga_ref, wua_ref, wda_ref, wgb_ref, wub_ref, wdb_ref)]
        rows_p = tm // MOE_PARTS

        def hidden(xp, wg, wu):
            return (jnp.dot(xp, wg, preferred_element_type=F32), jnp.dot(xp, wu, preferred_element_type=F32))

        def down(gate_up, wd):
            gate, up = gate_up
            he = (gate * jax.nn.sigmoid(gate)) * up
            return jnp.dot(he.astype(BF16), wd, preferred_element_type=F32)

        hs = []
        for p in range(MOE_PARTS):
            xp = x[p * rows_p:(p + 1) * rows_p]
            hs.append((hidden(xp, wts[0], wts[1]), hidden(xp, wts[3], wts[4])))
        for p, (ha, hb) in enumerate(hs):
            for y_ref, h, wd in ((ya_ref, ha, wts[2]), (yb_ref, hb, wts[5])):
                y = down(h, wd)
                for c in range(D_MODEL // LANES):
                    y_ref[pl.ds(p * rows_p * SUBLANES + c, rows_p, stride=SUBLANES), :] = (
                        y[:, c * LANES:(c + 1) * LANES])

    @pl.when(i >= nact_ref[0])
    def _():
        ya_ref[...] = jnp.zeros(ya_ref.shape, ya_ref.dtype)
        yb_ref[...] = jnp.zeros(yb_ref.shape, yb_ref.dtype)


def _moe(tile_ea, tile_eb, nact, x_sorted, wg_b, wu_b, wd_b, *, tm, n_max):
    wspec_in = lambda sel: pl.BlockSpec((1, D_MODEL, D_EXPERT), sel)
    wspec_out = lambda sel: pl.BlockSpec((1, D_EXPERT, D_MODEL), sel)
    sel_a = lambda i, ea, eb, na: (ea[i], 0, 0)
    sel_b = lambda i, ea, eb, na: (eb[i], 0, 0)
    rows_in = lambda i, ea, eb, na: (jnp.minimum(i, na[0] - 1), 0)
    grid_spec = pltpu.PrefetchScalarGridSpec(
        num_scalar_prefetch=3,
        grid=(n_max,),
        in_specs=[pl.BlockSpec((tm * SUBLANES, LANES), rows_in),
                  wspec_in(sel_a), wspec_in(sel_b), wspec_in(sel_a), wspec_in(sel_b),
                  wspec_out(sel_a), wspec_out(sel_b)],
        out_specs=[pl.BlockSpec((tm * SUBLANES, LANES), lambda i, ea, eb, na: (i, 0))] * 2)
    return pl.pallas_call(
        functools.partial(_moe_body, tm=tm),
        grid_spec=grid_spec,
        out_shape=[jax.ShapeDtypeStruct(x_sorted.shape, F32)] * 2,
        compiler_params=_cparams(("arbitrary",)),
        name="moe",
    )(tile_ea, tile_eb, nact, x_sorted, wg_b, wg_b, wu_b, wu_b, wd_b, wd_b)


def _final_body(soff_ref, slen_ref, x1_ref, info_ref, mod_ref, gf_ref, ya_ref, yb_ref, y_ref, buf, sem,
                *, tm, n_sub, n_tiles, tile_base):
    i = pl.program_id(0)
    slot = i % 2
    slabs = (ya_ref, yb_ref)

    def fetch(step, sl):
        for u in range(n_sub):
            local = u * tm
            for c in range(N_CLASSES):
                k = (step * n_sub + u + tile_base) * N_CLASSES + c
                n_rows = slen_ref[k]

                @pl.when(n_rows > 0)
                def _():
                    for e in range(2):
                        pltpu.make_async_copy(rows8(slabs[e], soff_ref[k], n_rows),
                                              rows8(buf.at[sl, e], local, n_rows), sem.at[sl, e]).start()
                local = local + n_rows

    @pl.when(i == 0)
    def _():
        fetch(0, 0)

    @pl.when(i + 1 < n_tiles)
    def _():
        fetch(i + 1, 1 - slot)

    info = info_ref[...]
    perm_t = [_perm_t(info[u * tm:(u + 1) * tm, INFO_CLS:INFO_CLS + 1]).astype(BF16) for u in range(n_sub)]
    moe = None
    for e, lane_w in enumerate((INFO_W_A, INFO_W_B)):
        pltpu.make_async_copy(buf.at[slot, e], buf.at[slot, e], sem.at[slot, e]).wait()
        bs = buf.at[slot, e]
        parts = []
        for u in range(n_sub):
            ye = jnp.concatenate([bs[pl.ds(u * tm * SUBLANES + c, tm, stride=SUBLANES), :]
                                  for c in range(D_MODEL // LANES)], axis=1)
            parts.append(jnp.dot(perm_t[u], ye.astype(BF16), preferred_element_type=F32))
        term = info[:, lane_w:lane_w + 1] * jnp.concatenate(parts, axis=0)
        moe = term if moe is None else moe + term
    x1 = x1_ref[...]
    g2 = mod_ref[0][:, 5 * D_MODEL:6 * D_MODEL]
    x2 = x1 + g2 * moe
    ms = jnp.mean(x2 * x2, axis=-1, keepdims=True)
    y_ref[...] = (x2 * lax.rsqrt(ms + EPS)) * gf_ref[...]


def _final(seg_off, seg_len, x1, info, mod, g_final, ya_sorted, yb_sorted, *, tm, n_sub, sample, tile_base):
    n = x1.shape[0]
    rows = n_sub * tm
    nt = n // rows
    mrows = mod.shape[1]
    if sample:
        mod_map = lambda i, *_: (0, 0, 0)
    else:
        tiles_per_batch = 4096 // rows
        mod_map = lambda i, *_: (i // tiles_per_batch, 0, 0)
    grid_spec = pltpu.PrefetchScalarGridSpec(
        num_scalar_prefetch=2,
        grid=(nt,),
        in_specs=[pl.BlockSpec((rows, D_MODEL), lambda i, *_: (i, 0)),
                  pl.BlockSpec((rows, LANES), lambda i, *_: (i, 0)),
                  pl.BlockSpec((1, mrows, 6 * D_MODEL), mod_map),
                  pl.BlockSpec((1, D_MODEL), lambda i, *_: (0, 0)),
                  pl.BlockSpec(memory_space=pl.ANY),
                  pl.BlockSpec(memory_space=pl.ANY)],
        out_specs=pl.BlockSpec((rows, D_MODEL), lambda i, *_: (i, 0)),
        scratch_shapes=[pltpu.VMEM((2, 2, rows * SUBLANES, LANES), F32),
                        pltpu.SemaphoreType.DMA((2, 2))])
    return pl.pallas_call(
        functools.partial(_final_body, tm=tm, n_sub=n_sub, n_tiles=nt, tile_base=tile_base),
        grid_spec=grid_spec,
        out_shape=jax.ShapeDtypeStruct((n, D_MODEL), F32),
        compiler_params=_cparams(("arbitrary",)),
        name="final_sample" if sample else "final_prompt",
    )(seg_off, seg_len, x1, info, mod, g_final.reshape(1, D_MODEL), ya_sorted, yb_sorted)


def _bucket_table(n):
    d = np.arange(n)
    max_exact = N_BUCKETS // 2
    nf = np.maximum(d, 1).astype(np.float64)
    large = max_exact + (np.log(nf / max_exact) / math.log(MAX_DISTANCE / max_exact)
                         * (N_BUCKETS - max_exact)).astype(np.int64)
    large = np.minimum(large, N_BUCKETS - 1)
    return np.where(d < max_exact, d, large).astype(np.int32)


def _toeplitz(v, n_rows, n_cols):
    length = n_rows + n_cols - 1
    lead = v.shape[:-1]
    vp = jnp.concatenate([v, jnp.zeros(lead + (1,), v.dtype)], axis=-1)
    skew = jnp.tile(vp, (1,) * len(lead) + (n_rows,))[..., :n_rows * length].reshape(lead + (n_rows, length))
    return skew[..., n_rows - 1:n_rows - 1 + n_cols]


def kernel(x_prompt, x_sample, c_prompt, c_sample, cache_k, cache_v, page_table, w_ada, b_ada, w_in, w_o,
           lam_q1, lam_k1, lam_q2, lam_k2, g_subln, rel_bias, g_sg_ln, b_sg_ln, w_s, b_s, w_rg, b_rg,
           w_re, b_re, w_gate, w_up, w_down, g_final):
    batch, seq, _ = x_prompt.shape
    dec_b, dec_t, _ = x_sample.shape
    n_pages = page_table.shape[1]
    n_p = batch * seq
    n_s = dec_b * dec_t
    n_tot = n_p + n_s
    assert w_in.shape[0] == 1 and cache_k.shape[1] == 1 and seq % TQ_ATT == 0 and n_pages % PAGES_PER_STEP == 0
    assert n_p % TM_TOK == 0 and n_p % n_s == 0 and n_tot % TM_E == 0 and dec_t == SUBLANES
    assert TAIL_TOKENS >= MAX_DISTANCE + dec_t and TAIL_TOKENS <= PAGES_PER_STEP * PAGE
    assert n_s == TM_D and n_p % TM_D == 0 and N_CLASSES <= LANES

    w_in_b = w_in[0].astype(BF16)
    w_o_b = w_o[0].astype(BF16)
    wr_t = jnp.zeros((32, D_MODEL), F32).at[0:N_EG].set(w_rg[0].T).at[N_EG:N_EG + N_EG * EPG].set(w_re[0].T)
    wr_t = wr_t.astype(BF16)
    br = jnp.zeros((32, 1), F32).at[0:N_EG, 0].set(b_rg[0]).at[N_EG:N_EG + N_EG * EPG, 0].set(b_re[0])
    wg_b, wu_b, wd_b = w_gate[0], w_up[0], w_down[0]
    ws_tril = jnp.tril(w_s[0])
    ws_p = ws_tril.astype(BF16)
    bs_p = b_s[0][:, :, None]
    same_seq = np.kron(np.eye(dec_b, dtype=np.float32), np.ones((dec_t, dec_t), np.float32))
    rep = np.tile(np.eye(dec_t, dtype=np.float32), (dec_b, 1))
    ws_rep = jnp.einsum('ri,gij,cj->grc', rep, ws_tril[:, :dec_t, :dec_t], rep,
                        precision=lax.Precision.HIGHEST)
    ws_s = (ws_rep * same_seq).astype(BF16)
    bs_s = jnp.tile(b_s[0][:, :dec_t], (1, dec_b))[:, :, None]
    gln = g_sg_ln[0]
    bln = b_sg_ln[0]
    lam = (jnp.exp(jnp.sum(lam_q1[0] * lam_k1[0])) - jnp.exp(jnp.sum(lam_q2[0] * lam_k2[0]))
           + LAM_INIT).reshape(1).astype(F32)

    blk = BIAS_BLOCK
    n_dist = max(2 * blk, TAIL_TOKENS + dec_t)
    onehot = np.eye(N_BUCKETS, dtype=np.float32)[_bucket_table(n_dist)]
    ft = jnp.dot(onehot, rel_bias - rel_bias[N_BUCKETS - 1], precision=lax.Precision.HIGHEST).T
    neg = lambda n: jnp.full((N_HEADS, n), NEG, F32)
    bias_near = _toeplitz(jnp.concatenate([neg(blk - 1), ft[:, 0:blk]], axis=1), blk, blk)
    bias_next = _toeplitz(ft[:, 1:2 * blk], blk, blk)
    bias_t = jnp.stack([bias_next, bias_near], axis=1) * LOG2E
    bl = _toeplitz(jnp.flip(ft[:, 1:TAIL_TOKENS + dec_t], axis=1), dec_t, TAIL_TOKENS)
    head_eq = jnp.eye(N_HEADS, dtype=F32)
    bias_last = (bl[:, None, :, :, None] * head_eq[:, None, None, None, :])
    bias_last = jnp.broadcast_to(bias_last, (N_HEADS, 2, dec_t, TAIL_TOKENS, N_HEADS)).reshape(
        2 * N_HEADS * dec_t, TAIL_TOKENS * N_HEADS)
    bn = _toeplitz(jnp.concatenate([jnp.flip(ft[:, 0:dec_t], axis=1), neg(dec_t - 1)], axis=1),
                   dec_t, dec_t)
    bn = jnp.where(head_eq[:, None, None, :] > 0, bn[:, :, :, None], NEG)
    bn = jnp.broadcast_to(bn[:, None], (N_HEADS, 2, dec_t, dec_t, N_HEADS)).reshape(
        2 * N_HEADS * dec_t, dec_t * N_HEADS)
    bias_new = jnp.concatenate([bn, jnp.full((bn.shape[0], PAGE - bn.shape[1]), NEG, F32)], axis=1)

    c_all = jnp.concatenate([c_prompt, c_sample, jnp.zeros((4, D_MODEL), F32)], axis=0)
    mod_all = _ada(c_all, w_ada[0], b_ada[0])
    mod_p = mod_all[:batch].reshape(batch, 1, 6 * D_MODEL)
    mod_s = jnp.repeat(mod_all[batch:batch + dec_b], dec_t, axis=0).reshape(1, n_s, 6 * D_MODEL)

    xp = x_prompt.reshape(n_p, D_MODEL)
    xs = x_sample.reshape(n_s, D_MODEL)

    q_p, kf_p, kb_p, vf_p, vt_p, sg_p = _stage_a(xp, mod_p, w_in_b, ws_p, bs_p, gln, bln,
                                                 tm=TM_TOK, chunk=CHUNK, sample=False)
    q_s, kf_s, vf_s, sg_s, vsn_s = _stage_a(xs, mod_s, w_in_b, ws_s, bs_s, gln, bln,
                                            tm=n_s, chunk=n_s, sample=True)

    g_col = g_subln[0].reshape(HEAD_W, 1)
    g_row = g_subln[0].reshape(1, HEAD_W)
    pad = ((0, 0), (0, PAGE - dec_t * N_HEADS), (0, 0))
    knew = jnp.pad(kf_s.reshape(dec_b, dec_t * N_HEADS, HEAD_W), pad).astype(BF16)
    vnew = jnp.pad(vf_s.reshape(dec_b, dec_t * N_HEADS, HEAD_W), pad).astype(BF16)
    ck = cache_k.reshape(-1, HEAD_W)
    cv = cache_v.reshape(-1, HEAD_W)
    o_p, o_s = _attn_fused(page_table.reshape(-1), lam, q_p, kb_p, vt_p, bias_t, g_col,
                           q_s, knew, vnew, bias_last, bias_new, g_row, ck, cv,
                           batch=batch, seq=seq, tq=TQ_ATT, tk=TM_TOK, dec_b=dec_b, n_pages=n_pages)

    x1_p, h2_p, cls_p, info_p = _stage_c(o_p, sg_p, xp, mod_p, w_o_b, wr_t, br, tm=TM_TOK, sample=False)
    x1_s, h2_s, cls_s, info_s = _stage_c(o_s, sg_s, xs, mod_s, w_o_b, wr_t, br, tm=n_s, sample=True)

    tm_e = TM_E
    tm_d = TM_D
    n_max = n_tot // tm_e + N_CLASSES
    n_dt = n_tot // tm_d
    cls = jnp.concatenate([cls_p[0], cls_s[0]]).reshape(n_dt, tm_d)
    classes = jnp.arange(N_CLASSES, dtype=jnp.int32)
    seg_len = jnp.sum((cls[:, :, None] == classes).astype(jnp.int32), axis=1)
    counts = jnp.sum(seg_len, axis=0)
    ntile_c = (counts + tm_e - 1) // tm_e
    tile_end = jnp.cumsum(ntile_c)
    class_base = (tile_end - ntile_c) * tm_e
    nact = tile_end[-1]
    seg_off = class_base[None, :] + jnp.cumsum(seg_len, axis=0) - seg_len
    pad_off = class_base + counts
    pad_len = ntile_c * tm_e - counts
    tile_ids = jnp.arange(n_max, dtype=jnp.int32)
    tile_cls = jnp.sum((tile_ids[:, None] >= tile_end[None, :]).astype(jnp.int32), axis=1)
    last_cls = jnp.sum((nact - 1 >= tile_end).astype(jnp.int32))
    tile_cls = jnp.where(tile_ids < nact, tile_cls, last_cls)
    grp = tile_cls // N_PAIRS
    pidx = tile_cls % N_PAIRS
    pick = lambda table: sum(jnp.where(pidx == p, e, 0) for p, e in enumerate(table))
    tile_ea = (grp * EPG + pick(PAIR_A)).astype(jnp.int32)
    tile_eb = (grp * EPG + pick(PAIR_B)).astype(jnp.int32)
    empty = jnp.zeros((N_SUB - 1, N_CLASSES), jnp.int32)
    seg_off = jnp.concatenate([seg_off.astype(jnp.int32), empty]).reshape(-1)
    seg_len = jnp.concatenate([seg_len, empty]).reshape(-1)

    nact1 = nact.reshape(1).astype(jnp.int32)
    zero_rows = (jnp.sum(pad_len) + (n_max - nact) * tm_e).reshape(1).astype(jnp.int32)
    x_sorted = _dispatch(seg_off, seg_len, pad_off.astype(jnp.int32), pad_len.astype(jnp.int32), zero_rows,
                         nact1, h2_p, h2_s, info_p, info_s, tm=tm_d, n_sub=N_SUB, n_slots=n_max * tm_e)
    ya_sorted, yb_sorted = _moe(tile_ea, tile_eb, nact1, x_sorted, wg_b, wu_b, wd_b, tm=tm_e, n_max=n_max)

    y_p = _final(seg_off, seg_len, x1_p, info_p, mod_p, g_final, ya_sorted, yb_sorted,
                 tm=tm_d, n_sub=N_SUB, sample=False, tile_base=0)
    y_s = _final(seg_off, seg_len, x1_s, info_s, mod_s, g_final, ya_sorted, yb_sorted,
                 tm=tm_d, n_sub=1, sample=True, tile_base=n_p // tm_d)

    return (y_p.reshape(batch, seq, D_MODEL),
            y_s.reshape(dec_b, dec_t, D_MODEL),
            kf_p.reshape(batch, 1, seq, N_HEADS, HEAD_W),
            vf_p.reshape(batch, 1, seq, N_HEADS, HEAD_W),
            kf_s.reshape(dec_b, 1, dec_t, N_HEADS, HEAD_W),
            vf_s.reshape(dec_b, 1, dec_t, N_HEADS, HEAD_W),
            vsn_s.reshape(dec_b, 1, dec_t, N_GROUPS_SG, SG_CH))
```

```python
import functools
import math

import numpy as np
import jax
import jax.numpy as jnp
from jax import lax
from jax.experimental import pallas as pl
from jax.experimental.pallas import tpu as pltpu

F32 = jnp.float32
BF16 = jnp.bfloat16

D_MODEL = 1024
N_HEADS = 4
DK = 64
HEAD_W = 128
QK_W = N_HEADS * HEAD_W
N_GROUPS_SG = 4
SG_CH = 128
CHUNK = 128
PAGE = 128
N_BUCKETS = 32
MAX_DISTANCE = 128
N_EG = 4
EPG = 4
N_PAIRS = 6
N_CLASSES = N_EG * N_PAIRS
D_EXPERT = 512
EPS = 1e-6
LAM_INIT = 0.8 - 0.6 * math.exp(-0.3 * 0)
NEG = -1e30
LOG2E = math.log2(math.e)
LANES = 128
SUBLANES = 8

TM_TOK = 512
STAGE_A_PARTS = 4
TQ_ATT = 2048
PAGES_PER_GROUP = 8
RING = 4
TM_E = 256
MOE_PARTS = 1
TM_D = 256
N_SUB = 4
VMEM_LIMIT = 56 * 1024 * 1024


def _cparams(sem):
    return pltpu.CompilerParams(dimension_semantics=sem, vmem_limit_bytes=VMEM_LIMIT)


def _ada_body(c_ref, w_ref, b_ref, o_ref):
    c = c_ref[...]
    a = (c * jax.nn.sigmoid(c)).astype(BF16)
    o_ref[...] = jnp.dot(a, w_ref[...].astype(BF16), preferred_element_type=F32) + b_ref[...]


def _ada(c_all, w_ada, b_ada):
    m = c_all.shape[0]
    n = w_ada.shape[1]
    tn = 1536
    return pl.pallas_call(
        _ada_body,
        grid=(n // tn,),
        in_specs=[pl.BlockSpec((m, D_MODEL), lambda j: (0, 0)),
                  pl.BlockSpec((D_MODEL, tn), lambda j: (0, j)),
                  pl.BlockSpec((1, tn), lambda j: (0, j))],
        out_specs=pl.BlockSpec((m, tn), lambda j: (0, j)),
        out_shape=jax.ShapeDtypeStruct((m, n), F32),
        compiler_params=_cparams(("arbitrary",)),
        name="adaln",
    )(c_all, w_ada, b_ada.reshape(1, n))


def _stage_a_body(x_ref, mod_ref, w_in_ref, ws_ref, bs_ref, gln_ref, bln_ref,
                  *out_refs, chunk, sample):
    if sample:
        q_ref, kf_ref, vf_ref, sg_ref, vsn_ref = out_refs
    else:
        q_ref, kf_ref, kb_ref, vf_ref, vt_ref, sg_ref = out_refs
    x = x_ref[...]
    tm = x.shape[0]
    mod = mod_ref[0]
    sh1 = mod[:, 0:D_MODEL]
    sc1 = mod[:, D_MODEL:2 * D_MODEL]
    ms = jnp.mean(x * x, axis=-1, keepdims=True)
    h = (x * lax.rsqrt(ms + EPS)) * (1.0 + sc1) + sh1
    hb = h.astype(BF16)
    n_parts = 1 if sample else STAGE_A_PARTS
    rows_p = tm // n_parts
    zs = [jnp.dot(hb[p * rows_p:(p + 1) * rows_p], w_in_ref[...], preferred_element_type=F32)
          for p in range(n_parts)]
    for p, z in enumerate(zs):
        a = p * rows_p
        q = z[:, 0:QK_W] * (DK ** -0.5)
        k = z[:, QK_W:2 * QK_W]
        v = z[:, 2 * QK_W:3 * QK_W]
        for hd in range(N_HEADS):
            kf_ref[pl.ds(a * N_HEADS + hd, rows_p, stride=N_HEADS), :] = k[:, hd * HEAD_W:(hd + 1) * HEAD_W]
            vf_ref[pl.ds(a * N_HEADS + hd, rows_p, stride=N_HEADS), :] = v[:, hd * HEAD_W:(hd + 1) * HEAD_W]
        if sample:
            q_ref[...] = q
        else:
            q_ref[a:a + rows_p, :] = (q * LOG2E).astype(BF16)
            kb_ref[a:a + rows_p, :] = k.astype(BF16)
            vt_ref[0, :, a:a + rows_p] = v.T.astype(BF16)
        u = z[:, 3 * QK_W:4 * QK_W]
        vs = z[:, 4 * QK_W:5 * QK_W]
        for g in range(N_GROUPS_SG):
            lo, hi = g * SG_CH, (g + 1) * SG_CH
            vg = vs[:, lo:hi]
            mu = jnp.mean(vg, axis=-1, keepdims=True)
            dv = vg - mu
            var = jnp.mean(dv * dv, axis=-1, keepdims=True)
            vn = (dv * lax.rsqrt(var + EPS)) * gln_ref[g:g + 1, :] + bln_ref[g:g + 1, :]
            if sample:
                vsn_ref[:, lo:hi] = vn
            vnb = vn.astype(BF16)
            for c in range(rows_p // chunk):
                r0, r1 = c * chunk, (c + 1) * chunk
                s = jnp.dot(ws_ref[g], vnb[r0:r1], preferred_element_type=F32) + bs_ref[g]
                sg_ref[a + r0:a + r1, lo:hi] = (u[r0:r1, lo:hi] * s).astype(BF16)


def _stage_a(x, mod, w_in_b, ws, bs, gln, bln, *, tm, chunk, sample):
    n = x.shape[0]
    nt = n // tm
    mrows = mod.shape[1]
    row = lambda w, dt: jax.ShapeDtypeStruct((n, w), dt)
    blk = lambda w: pl.BlockSpec((tm, w), lambda i: (i, 0))
    cache_shape = jax.ShapeDtypeStruct((n * N_HEADS, HEAD_W), F32)
    cache_blk = pl.BlockSpec((tm * N_HEADS, HEAD_W), lambda i: (i, 0))
    if sample:
        out_shape = [row(QK_W, F32), cache_shape, cache_shape, row(QK_W, BF16), row(QK_W, F32)]
        out_specs = [blk(QK_W), cache_blk, cache_blk, blk(QK_W), blk(QK_W)]
        mod_map = lambda i: (0, 0, 0)
    else:
        out_shape = [row(QK_W, BF16), cache_shape, row(QK_W, BF16), cache_shape,
                     jax.ShapeDtypeStruct((nt, QK_W, tm), BF16), row(QK_W, BF16)]
        out_specs = [blk(QK_W), cache_blk, blk(QK_W), cache_blk,
                     pl.BlockSpec((1, QK_W, tm), lambda i: (i, 0, 0)), blk(QK_W)]
        tiles_per_batch = 4096 // tm
        mod_map = lambda i: (i // tiles_per_batch, 0, 0)
    full = lambda a: pl.BlockSpec(a.shape, lambda i: (0,) * a.ndim)
    return pl.pallas_call(
        functools.partial(_stage_a_body, chunk=chunk, sample=sample),
        grid=(nt,),
        in_specs=[blk(D_MODEL),
                  pl.BlockSpec((1, mrows, 6 * D_MODEL), mod_map),
                  full(w_in_b), full(ws), full(bs), full(gln), full(bln)],
        out_specs=out_specs,
        out_shape=out_shape,
        compiler_params=_cparams(("arbitrary",)),
        name="stage_a_sample" if sample else "stage_a_prompt",
    )(x, mod, w_in_b, ws, bs, gln, bln)


ATT_COLS = 256
BIAS_BLOCK = MAX_DISTANCE


def _attn_body(lam_ref, q_ref, k_ref, vt_ref, bias_ref, g_ref, o_ref, *scratch, tq, tk, between):
    n_chain = 2 * tq // ATT_COLS
    q2_refs, m_refs, l_refs, acc_refs = (scratch[i * n_chain:(i + 1) * n_chain] for i in range(4))
    qi = pl.program_id(2)
    between(0)
    for c in range(n_chain):
        q0 = (c * ATT_COLS) % tq
        q = q_ref[q0:q0 + ATT_COLS, :]
        lane = lax.broadcasted_iota(jnp.int32, q.shape, 1)
        keep = (lane < DK) if c < n_chain // 2 else (lane >= DK)
        q2_refs[c][...] = jnp.where(keep, q, jnp.zeros_like(q))
        m_refs[c][...] = jnp.full(m_refs[c].shape, NEG, F32)
        l_refs[c][...] = jnp.zeros(l_refs[c].shape, F32)
        acc_refs[c][...] = jnp.zeros(acc_refs[c].shape, F32)

    blk = BIAS_BLOCK

    kblocks = tk // blk

    def block_kinds(rel, q0):
        return [[(q0 // blk + b) - (rel + a) for b in range(ATT_COLS // blk)] for a in range(kblocks)]

    def keys_needed(rel, q0):
        if rel is None:
            return kblocks
        return sum(1 for row in block_kinds(rel, q0) if max(row) >= 0)

    def with_bias(s, rel, q0):
        if rel is None:
            return s
        kinds = block_kinds(rel, q0)[:s.shape[0] // blk]
        if all(d >= 2 for row in kinds for d in row):
            return s
        nxt, near = bias_ref[0, 0], bias_ref[0, 1]
        pick = lambda d: (jnp.full((blk, blk), NEG, F32) if d < 0 else near if d == 0 else nxt if d == 1
                          else jnp.zeros((blk, blk), F32))
        rows = []
        for a, row in enumerate(kinds):
            s_row = s[a * blk:(a + 1) * blk]
            if any(d < 2 for d in row):
                s_row = s_row + jnp.concatenate([pick(d) for d in row], axis=1)
            rows.append(s_row)
        return jnp.concatenate(rows, axis=0)

    def score(j, rel, c):
        nk = keys_needed(rel, (c * ATT_COLS) % tq) * blk
        if nk == 0:
            return None
        k = k_ref[pl.ds(pl.multiple_of(j * tk, tk), nk), :]
        s = lax.dot_general(k, q2_refs[c][...], (((1,), (1,)), ((), ())),
                            preferred_element_type=F32)
        return with_bias(s, rel, (c * ATT_COLS) % tq)

    def accumulate(j, c, s):
        vt = vt_ref[j, :, 0:s.shape[0]]
        m_old = m_refs[c][...]
        m_new = jnp.maximum(m_old, jnp.max(s, axis=0, keepdims=True))
        alpha = jnp.exp2(m_old - m_new)
        p = jnp.exp2(s - m_new)
        l_refs[c][...] = alpha * l_refs[c][...] + jnp.sum(p, axis=0, keepdims=True)
        acc_refs[c][...] = alpha * acc_refs[c][...] + jnp.dot(vt, p.astype(BF16), preferred_element_type=F32)
        m_refs[c][...] = m_new

    def tiles(*work):
        scores = [[score(j, rel, c) for c in range(n_chain)] for j, rel in work]
        for (j, _), tile_scores in zip(work, scores):
            for c, s in enumerate(tile_scores):
                if s is not None:
                    accumulate(j, c, s)

    ratio = tq // tk
    first_diag = qi * ratio
    n_plain = jnp.maximum(first_diag - 1, 0)

    def plain_pair(jj, carry):
        tiles((2 * jj, None), (2 * jj + 1, None))
        return carry

    diag = [(first_diag + r, r * kblocks) for r in range(ratio)]
    assert ratio % 2 == 0 and ratio >= 4
    half = ratio // 2

    @pl.when(qi == 0)
    def _():
        tiles(*diag[:half])

    @pl.when(qi >= 1)
    def _():
        lax.fori_loop(0, n_plain // 2, plain_pair, 0)

    between(1)

    @pl.when(qi == 0)
    def _():
        tiles(*diag[half:])

    @pl.when(qi >= 1)
    def _():
        tiles((first_diag - 2, None), (first_diag - 1, -kblocks))

    between(2)

    @pl.when(qi >= 1)
    def _():
        tiles(*diag)

    between(3)

    lam = lam_ref[0]
    o_all = jnp.concatenate([acc_refs[c][...] * (1.0 / l_refs[c][...]) for c in range(n_chain)],
                            axis=1)
    o = o_all[:, 0:tq] - lam * o_all[:, tq:2 * tq]
    ms = jnp.mean(o * o, axis=0, keepdims=True)
    on = (o * lax.rsqrt(ms + EPS)) * g_ref[...] * (1.0 - LAM_INIT)
    o_ref[...] = on.T.astype(BF16)


PAGE_ROWS = PAGE * N_HEADS
TAIL_TOKENS = 2 * PAGE


def _attn_fused_body(pt_ref, lam_ref, q_ref, k_ref, vt_ref, bias_ref, gcol_ref,
                     qs_ref, knew_ref, vnew_ref, bl_ref, bn_ref, grow_ref, ck_ref, cv_ref,
                     o_ref, os_ref, *scratch, tq, tk, pages, n_sub, n_steps):
    n_prompt_scratch = 4 * (2 * tq // ATT_COLS)
    kbuf, vbuf, sem, mask_ref, qm_ref, m_ref, l_ref, acc_ref = scratch[n_prompt_scratch:]
    step = (pl.program_id(0) * pl.num_programs(1) + pl.program_id(1)) * pl.num_programs(2) + pl.program_id(2)

    def page_copies(group, sl):
        base = group * pages
        out = []
        for i in range(pages):
            src = pl.ds(pl.multiple_of(pt_ref[base + i] * PAGE_ROWS, PAGE_ROWS), PAGE_ROWS)
            dst = pl.ds(i * PAGE_ROWS, PAGE_ROWS)
            out.append(pltpu.make_async_copy(ck_ref.at[src], kbuf.at[sl, dst], sem.at[sl, 0]))
            out.append(pltpu.make_async_copy(cv_ref.at[src], vbuf.at[sl, dst], sem.at[sl, 1]))
        return out

    def start_all(copies):
        for c in copies:
            c.start()

    def update(kb, vb, bias):
        sc = lax.dot_general(qm_ref[...], kb, (((1,), (1,)), ((), ())),
                             preferred_element_type=F32) + bias
        m_old = m_ref[...]
        m_new = jnp.maximum(m_old, jnp.max(sc, axis=1, keepdims=True))
        alpha = jnp.exp(m_old - m_new)
        p = jnp.exp(sc - m_new)
        l_ref[...] = alpha * l_ref[...] + jnp.sum(p, axis=1, keepdims=True)
        acc_ref[...] = alpha * acc_ref[...] + jnp.dot(p.astype(BF16), vb, preferred_element_type=F32)
        m_ref[...] = m_new

    def sample_group(u):
        group = step * n_sub + u
        slot = u % RING
        ahead = RING - 1
        if u == 0:
            @pl.when(step == 0)
            def _():
                for g0 in range(ahead):
                    start_all(page_copies(g0, g0))
                row = lax.broadcasted_iota(jnp.int32, mask_ref.shape, 0)
                col = lax.broadcasted_iota(jnp.int32, mask_ref.shape, 1)
                same_head = (col % N_HEADS) == (row // (2 * SUBLANES))
                mask_ref[...] = jnp.where(same_head, 0.0, NEG)
        if u + ahead < n_sub:
            start_all(page_copies(group + ahead, (u + ahead) % RING))
        else:
            @pl.when(step + 1 < n_steps)
            def _():
                start_all(page_copies(group + ahead, (u + ahead) % RING))
        for c in page_copies(group, slot):
            c.wait()
        if u == 0:
            m_ref[...] = jnp.full(m_ref.shape, NEG, F32)
            l_ref[...] = jnp.zeros(l_ref.shape, F32)
            acc_ref[...] = jnp.zeros(acc_ref.shape, F32)
            q = qs_ref[...]
            lane = lax.broadcasted_iota(jnp.int32, (SUBLANES, HEAD_W), 1)
            pieces = []
            for h in range(N_HEADS):
                qh = q[:, h * HEAD_W:(h + 1) * HEAD_W]
                pieces += [jnp.where(lane < DK, qh, 0.0), jnp.where(lane >= DK, qh, 0.0)]
            qm_ref[...] = jnp.concatenate(pieces, axis=0).astype(BF16)
        bias = mask_ref[...]
        if u == n_sub - 1:
            head_cols = mask_ref.shape[1] - bl_ref.shape[1]
            bias = jnp.concatenate([bias[:, :head_cols], bias[:, head_cols:] + bl_ref[...]], axis=1)
        update(kbuf[slot].astype(BF16), vbuf[slot].astype(BF16), bias)
        if u == n_sub - 1:
            update(knew_ref[0], vnew_ref[0], bn_ref[...])
            lam = lam_ref[0]
            o_all = acc_ref[...] * (1.0 / l_ref[...])
            for h in range(N_HEADS):
                r = h * 2 * SUBLANES
                o = o_all[r:r + SUBLANES] - lam * o_all[r + SUBLANES:r + 2 * SUBLANES]
                ms = jnp.mean(o * o, axis=-1, keepdims=True)
                os_ref[:, h * HEAD_W:(h + 1) * HEAD_W] = ((o * lax.rsqrt(ms + EPS)) * grow_ref[...]
                                                          * (1.0 - LAM_INIT))

    per_call = n_sub // 4

    def between(k):
        for u in range(per_call * k, per_call * (k + 1)):
            sample_group(u)

    _attn_body(lam_ref, q_ref, k_ref, vt_ref, bias_ref, gcol_ref, o_ref, *scratch[:n_prompt_scratch],
               tq=tq, tk=tk, between=between)


def _attn_fused(page_table_flat, lam, q, kb, vt, bias_t, g_col, q_s, knew, vnew, bias_last, bias_new, g_row,
                cache_k, cache_v, *, batch, seq, tq, tk, dec_b, n_pages):
    nq = seq // tq
    nk = seq // tk
    n = batch * seq
    n_chain = 2 * tq // ATT_COLS
    pages = PAGES_PER_GROUP
    n_sub = n_pages // pages
    n_steps = batch * N_HEADS * nq
    dec_t = q_s.shape[0] // dec_b
    n_rows = 2 * N_HEADS * dec_t
    step_rows = pages * PAGE_ROWS
    assert n_steps == dec_b and n_sub % 4 == 0 and n_sub % RING == 0 and nq == 2
    assert bias_last.shape[1] <= step_rows
    assert vt.shape == (batch * nk, QK_W, tk) and tq % tk == 0 and tk % BIAS_BLOCK == 0
    lin = lambda b, h, i: (b * N_HEADS + h) * nq + i
    grid_spec = pltpu.PrefetchScalarGridSpec(
        num_scalar_prefetch=1,
        grid=(batch, N_HEADS, nq),
        in_specs=[pl.BlockSpec(memory_space=pltpu.SMEM),
                  pl.BlockSpec((tq, HEAD_W), lambda b, h, i, pt: (b * nq + i, h)),
                  pl.BlockSpec((seq, HEAD_W), lambda b, h, i, pt: (b, h)),
                  pl.BlockSpec((nk, HEAD_W, tk), lambda b, h, i, pt: (b, h, 0)),
                  pl.BlockSpec((1, 2, BIAS_BLOCK, BIAS_BLOCK), lambda b, h, i, pt: (h, 0, 0, 0)),
                  pl.BlockSpec((HEAD_W, 1), lambda b, h, i, pt: (0, 0)),
                  pl.BlockSpec((dec_t, QK_W), lambda b, h, i, pt: (lin(b, h, i), 0)),
                  pl.BlockSpec((1, PAGE, HEAD_W), lambda b, h, i, pt: (lin(b, h, i), 0, 0)),
                  pl.BlockSpec((1, PAGE, HEAD_W), lambda b, h, i, pt: (lin(b, h, i), 0, 0)),
                  pl.BlockSpec(bias_last.shape, lambda b, h, i, pt: (0, 0)),
                  pl.BlockSpec(bias_new.shape, lambda b, h, i, pt: (0, 0)),
                  pl.BlockSpec((1, HEAD_W), lambda b, h, i, pt: (0, 0)),
                  pl.BlockSpec(memory_space=pl.ANY),
                  pl.BlockSpec(memory_space=pl.ANY)],
        out_specs=[pl.BlockSpec((tq, HEAD_W), lambda b, h, i, pt: (b * nq + i, h)),
                   pl.BlockSpec((dec_t, QK_W), lambda b, h, i, pt: (lin(b, h, i), 0))],
        scratch_shapes=([pltpu.VMEM((ATT_COLS, HEAD_W), BF16)] * n_chain
                        + [pltpu.VMEM((1, ATT_COLS), F32)] * (2 * n_chain)
                        + [pltpu.VMEM((HEAD_W, ATT_COLS), F32)] * n_chain
                        + [pltpu.VMEM((RING, step_rows, HEAD_W), F32),
                           pltpu.VMEM((RING, step_rows, HEAD_W), F32),
                           pltpu.SemaphoreType.DMA((RING, 2)),
                           pltpu.VMEM((n_rows, step_rows), F32),
                           pltpu.VMEM((n_rows, HEAD_W), BF16),
                           pltpu.VMEM((n_rows, 1), F32), pltpu.VMEM((n_rows, 1), F32),
                           pltpu.VMEM((n_rows, HEAD_W), F32)]))
    return pl.pallas_call(
        functools.partial(_attn_fused_body, tq=tq, tk=tk, pages=pages, n_sub=n_sub, n_steps=n_steps),
        grid_spec=grid_spec,
        out_shape=[jax.ShapeDtypeStruct((n, QK_W), BF16), jax.ShapeDtypeStruct(q_s.shape, F32)],
        compiler_params=_cparams(("arbitrary", "arbitrary", "arbitrary")),
        name="attn_fused",
    )(page_table_flat, lam, q, kb, vt, bias_t, g_col, q_s, knew, vnew, bias_last, bias_new, g_row,
      cache_k, cache_v)


INFO_W_A, INFO_W_B, INFO_CLS = 0, 1, 2
PAIR_A = (0, 0, 0, 1, 1, 3)
PAIR_B = (1, 2, 3, 3, 2, 2)


def _stage_c_body(o_ref, sg_ref, x_ref, mod_ref, wo_ref, wr_ref, br_ref, x1_ref, h2_ref, cls_ref, info_ref):
    x = x_ref[...]
    tm = x.shape[0]
    mod = mod_ref[0]
    g1 = mod[:, 2 * D_MODEL:3 * D_MODEL]
    sh2 = mod[:, 3 * D_MODEL:4 * D_MODEL]
    sc2 = mod[:, 4 * D_MODEL:5 * D_MODEL]
    mix = (jnp.dot(o_ref[...].astype(BF16), wo_ref[0:QK_W, :], preferred_element_type=F32)
           + jnp.dot(sg_ref[...], wo_ref[QK_W:2 * QK_W, :], preferred_element_type=F32))
    x1 = x + g1 * mix
    x1_ref[...] = x1
    ms = jnp.mean(x1 * x1, axis=-1, keepdims=True)
    h2 = ((x1 * lax.rsqrt(ms + EPS)) * (1.0 + sc2) + sh2).astype(BF16)
    h2_ref[...] = h2
    lg = lax.dot_general(wr_ref[...], h2, (((1,), (1,)), ((), ())),
                         preferred_element_type=F32) + br_ref[...]
    gl = [lg[i:i + 1, :] for i in range(N_EG)]
    el = [lg[N_EG + i:N_EG + i + 1, :] for i in range(N_EG * EPG)]
    gmax = jnp.maximum(jnp.maximum(gl[0], gl[1]), jnp.maximum(gl[2], gl[3]))
    gi = jnp.where(gl[0] == gmax, 0, jnp.where(gl[1] == gmax, 1, jnp.where(gl[2] == gmax, 2, 3)))
    gsum = (jnp.exp(gl[0] - gmax) + jnp.exp(gl[1] - gmax)
            + jnp.exp(gl[2] - gmax) + jnp.exp(gl[3] - gmax))
    gp = 1.0 / gsum
    sel = [jnp.where(gi == 0, el[j], jnp.where(gi == 1, el[EPG + j],
                                               jnp.where(gi == 2, el[2 * EPG + j], el[3 * EPG + j])))
           for j in range(EPG)]
    v0 = jnp.maximum(jnp.maximum(sel[0], sel[1]), jnp.maximum(sel[2], sel[3]))
    i0 = jnp.where(sel[0] == v0, 0, jnp.where(sel[1] == v0, 1, jnp.where(sel[2] == v0, 2, 3)))
    rest = [jnp.where(i0 == j, -3e38, sel[j]) for j in range(EPG)]
    v1 = jnp.maximum(jnp.maximum(rest[0], rest[1]), jnp.maximum(rest[2], rest[3]))
    i1 = jnp.where(rest[0] == v1, 0, jnp.where(rest[1] == v1, 1, jnp.where(rest[2] == v1, 2, 3)))
    e1 = jnp.exp(v1 - v0)
    den = 1.0 / (1.0 + e1)
    tw0 = den * gp
    tw1 = e1 * den * gp
    first_low = i0 < i1
    lo = jnp.where(first_low, i0, i1)
    hi = jnp.where(first_low, i1, i0)
    w_lo = jnp.where(first_low, tw0, tw1)
    w_hi = jnp.where(first_low, tw1, tw0)
    pair = jnp.where(lo == 0, hi - 1, jnp.where(lo == 2, 5, jnp.where(hi == 3, 3, 4)))
    swapped = pair == 5
    w_a = jnp.where(swapped, w_hi, w_lo)
    w_b = jnp.where(swapped, w_lo, w_hi)
    cls = gi * N_PAIRS + pair
    cls_ref[...] = jnp.broadcast_to(cls, cls_ref.shape).astype(jnp.int32)
    row = lax.broadcasted_iota(jnp.int32, (LANES, tm), 0)
    rec = jnp.where(row == INFO_W_A, w_a, jnp.where(row == INFO_W_B, w_b,
                                                    jnp.where(row == INFO_CLS, cls.astype(F32), 0.0)))
    info_ref[...] = rec.T


def _stage_c(o, sg, x, mod, wo_b, wr_t, br, *, tm, sample):
    n = x.shape[0]
    nt = n // tm
    mrows = mod.shape[1]
    blk = lambda w: pl.BlockSpec((tm, w), lambda i: (i, 0))
    full = lambda a: pl.BlockSpec(a.shape, lambda i: (0,) * a.ndim)
    if sample:
        mod_map = lambda i: (0, 0, 0)
    else:
        tiles_per_batch = 4096 // tm
        mod_map = lambda i: (i // tiles_per_batch, 0, 0)
    return pl.pallas_call(
        _stage_c_body,
        grid=(nt,),
        in_specs=[blk(QK_W), blk(QK_W), blk(D_MODEL),
                  pl.BlockSpec((1, mrows, 6 * D_MODEL), mod_map),
                  full(wo_b), full(wr_t), full(br)],
        out_specs=[blk(D_MODEL), blk(D_MODEL),
                   pl.BlockSpec((SUBLANES, tm), lambda i: (0, i)),
                   blk(LANES)],
        out_shape=[jax.ShapeDtypeStruct((n, D_MODEL), F32),
                   jax.ShapeDtypeStruct((n, D_MODEL), BF16),
                   jax.ShapeDtypeStruct((SUBLANES, n), jnp.int32),
                   jax.ShapeDtypeStruct((n, LANES), F32)],
        compiler_params=_cparams(("arbitrary",)),
        name="stage_c_sample" if sample else "stage_c_prompt",
    )(o, sg, x, mod, wo_b, wr_t, br)


def rows8(ref, start, count):
    scale = lambda v: v * SUBLANES if isinstance(v, int) else pl.multiple_of(v * SUBLANES, SUBLANES)
    return ref.at[pl.ds(scale(start), scale(count))]


def _perm_t(cls_col):
    n = cls_col.shape[0]
    lane = lax.broadcasted_iota(jnp.int32, (n, LANES), 1).astype(F32)
    onehot = (lane == cls_col).astype(BF16)
    r = lax.broadcasted_iota(jnp.int32, (n, n), 0)
    c = lax.broadcasted_iota(jnp.int32, (n, n), 1)
    before = (c < r).astype(BF16)
    rank = jnp.dot(before, onehot, preferred_element_type=F32)
    cnt = jnp.sum(onehot.astype(F32), axis=0, keepdims=True)
    cr = lax.broadcasted_iota(jnp.int32, (LANES, LANES), 0)
    cc = lax.broadcasted_iota(jnp.int32, (LANES, LANES), 1)
    lower_cls = (cr < cc).astype(BF16)
    base = jnp.dot(jnp.broadcast_to(cnt, (SUBLANES, LANES)).astype(BF16), lower_cls,
                   preferred_element_type=F32)[0:1, :]
    pos = jnp.sum(onehot.astype(F32) * (base + rank), axis=1, keepdims=True)
    dest = lax.broadcasted_iota(jnp.int32, (n, n), 1).astype(F32)
    return (dest == pos).astype(F32)


def _dispatch_body(soff_ref, slen_ref, poff_ref, plen_ref, ptot_ref, nact_ref, hp_ref, hs_ref, ip_ref, is_ref,
                   xs_ref, buf, zbuf, sem, zsem, *, tm, n_sub, n_tiles, n_slab_tiles):
    i = pl.program_id(0)
    slot = i % 2
    is_sample = i == n_tiles - 1

    def wait_tile(sl, tokens):
        pltpu.make_async_copy(rows8(buf.at[sl], 0, tokens), rows8(buf.at[sl], 0, tokens), sem.at[sl]).wait()

    @pl.when(i == 0)
    def _():
        zbuf[...] = jnp.zeros(zbuf.shape, F32)
        for c in range(N_CLASSES):
            @pl.when(plen_ref[c] > 0)
            def _():
                pltpu.make_async_copy(rows8(zbuf, 0, plen_ref[c]), rows8(xs_ref, poff_ref[c], plen_ref[c]),
                                      zsem).start()
        for j in range(n_slab_tiles - N_CLASSES, n_slab_tiles):
            @pl.when(j >= nact_ref[0])
            def _():
                pltpu.make_async_copy(zbuf, rows8(xs_ref, j * TM_E, TM_E), zsem).start()

    xp = []
    for u in range(n_sub):
        rows = slice(u * tm, (u + 1) * tm)
        x = hp_ref[rows, :]
        info = ip_ref[rows, :]
        if u == 0:
            x = jnp.where(is_sample, hs_ref[...], x)
            info = jnp.where(is_sample, is_ref[...], info)
        perm = _perm_t(info[:, INFO_CLS:INFO_CLS + 1]).T.astype(BF16)
        xp.append(jnp.dot(perm, x, preferred_element_type=F32))

    @pl.when(i >= 2)
    def _():
        wait_tile(slot, n_sub * tm)

    bs = buf.at[slot]
    for u in range(n_sub):
        for c in range(D_MODEL // LANES):
            bs[pl.ds(u * tm * SUBLANES + c, tm, stride=SUBLANES), :] = xp[u][:, c * LANES:(c + 1) * LANES]
    for u in range(n_sub):
        local = u * tm
        for c in range(N_CLASSES):
            k = (i * n_sub + u) * N_CLASSES + c
            n_rows = slen_ref[k]

            @pl.when(n_rows > 0)
            def _():
                pltpu.make_async_copy(rows8(bs, local, n_rows), rows8(xs_ref, soff_ref[k], n_rows),
                                      sem.at[slot]).start()
            local = local + n_rows

    @pl.when(i == n_tiles - 1)
    def _():
        wait_tile(slot, tm)
        if n_tiles >= 2:
            wait_tile(1 - slot, n_sub * tm)

        @pl.when(ptot_ref[0] > 0)
        def _():
            n = pl.multiple_of(ptot_ref[0] * SUBLANES, SUBLANES)
            pltpu.make_async_copy(xs_ref.at[pl.ds(0, n)], xs_ref.at[pl.ds(0, n)], zsem).wait()


def _dispatch(seg_off, seg_len, pad_off, pad_len, pad_tot, nact, h2_p, h2_s, info_p, info_s, *, tm, n_sub, n_slots):
    n_prompt_steps = h2_p.shape[0] // (n_sub * tm)
    n_tiles = n_prompt_steps + 1
    assert h2_s.shape[0] == tm and h2_p.shape[0] % (n_sub * tm) == 0
    assert seg_len.shape[0] == n_tiles * n_sub * N_CLASSES
    last_p = n_prompt_steps - 1
    grid_spec = pltpu.PrefetchScalarGridSpec(
        num_scalar_prefetch=6,
        grid=(n_tiles,),
        in_specs=[pl.BlockSpec((n_sub * tm, D_MODEL), lambda i, *_: (jnp.minimum(i, last_p), 0)),
                  pl.BlockSpec((tm, D_MODEL), lambda i, *_: (0, 0)),
                  pl.BlockSpec((n_sub * tm, LANES), lambda i, *_: (jnp.minimum(i, last_p), 0)),
                  pl.BlockSpec((tm, LANES), lambda i, *_: (0, 0))],
        out_specs=pl.BlockSpec(memory_space=pl.ANY),
        scratch_shapes=[pltpu.VMEM((2, n_sub * tm * SUBLANES, LANES), F32),
                        pltpu.VMEM((TM_E * SUBLANES, LANES), F32),
                        pltpu.SemaphoreType.DMA((2,)),
                        pltpu.SemaphoreType.DMA(())])
    return pl.pallas_call(
        functools.partial(_dispatch_body, tm=tm, n_sub=n_sub, n_tiles=n_tiles, n_slab_tiles=n_slots // TM_E),
        grid_spec=grid_spec,
        out_shape=jax.ShapeDtypeStruct((n_slots * SUBLANES, LANES), F32),
        compiler_params=_cparams(("arbitrary",)),
        name="moe_dispatch",
    )(seg_off, seg_len, pad_off, pad_len, pad_tot, nact, h2_p, h2_s, info_p, info_s)


def _moe_body(ea_ref, eb_ref, nact_ref, x_ref, wga_ref, wgb_ref, wua_ref, wub_ref, wda_ref, wdb_ref,
              ya_ref, yb_ref, *, tm):
    i = pl.program_id(0)

    @pl.when(i < nact_ref[0])
    def _():
        x = jnp.concatenate([x_ref[pl.ds(c, tm, stride=SUBLANES), :] for c in range(D_MODEL // LANES)],
                            axis=1).astype(BF16)

        wts = [r[0].astype(BF16) for r in (wga_ref, wua_ref, wda_ref, wgb_ref, wub_ref, wdb_ref)]
        rows_p = tm // MOE_PARTS

        def hidden(xp, wg, wu):
            return (jnp.dot(xp, wg, preferred_element_type=F32), jnp.dot(xp, wu, preferred_element_type=F32))

        def down(gate_up, wd):
            gate, up = gate_up
            he = (gate * jax.nn.sigmoid(gate)) * up
            return jnp.dot(he.astype(BF16), wd, preferred_element_type=F32)

        hs = []
        for p in range(MOE_PARTS):
            xp = x[p * rows_p:(p + 1) * rows_p]
            hs.append((hidden(xp, wts[0], wts[1]), hidden(xp, wts[3], wts[4])))
        for p, (ha, hb) in enumerate(hs):
            for y_ref, h, wd in ((ya_ref, ha, wts[2]), (yb_ref, hb, wts[5])):
                y = down(h, wd)
                for c in range(D_MODEL // LANES):
                    y_ref[pl.ds(p * rows_p * SUBLANES + c, rows_p, stride=SUBLANES), :] = (
                        y[:, c * LANES:(c + 1) * LANES])

    @pl.when(i >= nact_ref[0])
    def _():
        ya_ref[...] = jnp.zeros(ya_ref.shape, ya_ref.dtype)
        yb_ref[...] = jnp.zeros(yb_ref.shape, yb_ref.dtype)


def _moe(tile_ea, tile_eb, nact, x_sorted, wg_b, wu_b, wd_b, *, tm, n_max):
    wspec_in = lambda sel: pl.BlockSpec((1, D_MODEL, D_EXPERT), sel)
    wspec_out = lambda sel: pl.BlockSpec((1, D_EXPERT, D_MODEL), sel)
    sel_a = lambda i, ea, eb, na: (ea[i], 0, 0)
    sel_b = lambda i, ea, eb, na: (eb[i], 0, 0)
    rows_in = lambda i, ea, eb, na: (jnp.minimum(i, na[0] - 1), 0)
    grid_spec = pltpu.PrefetchScalarGridSpec(
        num_scalar_prefetch=3,
        grid=(n_max,),
        in_specs=[pl.BlockSpec((tm * SUBLANES, LANES), rows_in),
                  wspec_in(sel_a), wspec_in(sel_b), wspec_in(sel_a), wspec_in(sel_b),
                  wspec_out(sel_a), wspec_out(sel_b)],
        out_specs=[pl.BlockSpec((tm * SUBLANES, LANES), lambda i, ea, eb, na: (i, 0))] * 2)
    return pl.pallas_call(
        functools.partial(_moe_body, tm=tm),
        grid_spec=grid_spec,
        out_shape=[jax.ShapeDtypeStruct(x_sorted.shape, F32)] * 2,
        compiler_params=_cparams(("arbitrary",)),
        name="moe",
    )(tile_ea, tile_eb, nact, x_sorted, wg_b, wg_b, wu_b, wu_b, wd_b, wd_b)


def _final_body(soff_ref, slen_ref, x1_ref, info_ref, mod_ref, gf_ref, ya_ref, yb_ref, y_ref, buf, sem,
                *, tm, n_sub, n_tiles, tile_base):
    i = pl.program_id(0)
    slot = i % 2
    slabs = (ya_ref, yb_ref)

    def fetch(step, sl):
        for u in range(n_sub):
            local = u * tm
            for c in range(N_CLASSES):
                k = (step * n_sub + u + tile_base) * N_CLASSES + c
                n_rows = slen_ref[k]

                @pl.when(n_rows > 0)
                def _():
                    for e in range(2):
                        pltpu.make_async_copy(rows8(slabs[e], soff_ref[k], n_rows),
                                              rows8(buf.at[sl, e], local, n_rows), sem.at[sl, e]).start()
                local = local + n_rows

    @pl.when(i == 0)
    def _():
        fetch(0, 0)

    @pl.when(i + 1 < n_tiles)
    def _():
        fetch(i + 1, 1 - slot)

    info = info_ref[...]
    perm_t = [_perm_t(info[u * tm:(u + 1) * tm, INFO_CLS:INFO_CLS + 1]).astype(BF16) for u in range(n_sub)]
    moe = None
    for e, lane_w in enumerate((INFO_W_A, INFO_W_B)):
        pltpu.make_async_copy(buf.at[slot, e], buf.at[slot, e], sem.at[slot, e]).wait()
        bs = buf.at[slot, e]
        parts = []
        for u in range(n_sub):
            ye = jnp.concatenate([bs[pl.ds(u * tm * SUBLANES + c, tm, stride=SUBLANES), :]
                                  for c in range(D_MODEL // LANES)], axis=1)
            parts.append(jnp.dot(perm_t[u], ye.astype(BF16), preferred_element_type=F32))
        term = info[:, lane_w:lane_w + 1] * jnp.concatenate(parts, axis=0)
        moe = term if moe is None else moe + term
    x1 = x1_ref[...]
    g2 = mod_ref[0][:, 5 * D_MODEL:6 * D_MODEL]
    x2 = x1 + g2 * moe
    ms = jnp.mean(x2 * x2, axis=-1, keepdims=True)
    y_ref[...] = (x2 * lax.rsqrt(ms + EPS)) * gf_ref[...]


def _final(seg_off, seg_len, x1, info, mod, g_final, ya_sorted, yb_sorted, *, tm, n_sub, sample, tile_base):
    n = x1.shape[0]
    rows = n_sub * tm
    nt = n // rows
    mrows = mod.shape[1]
    if sample:
        mod_map = lambda i, *_: (0, 0, 0)
    else:
        tiles_per_batch = 4096 // rows
        mod_map = lambda i, *_: (i // tiles_per_batch, 0, 0)
    grid_spec = pltpu.PrefetchScalarGridSpec(
        num_scalar_prefetch=2,
        grid=(nt,),
        in_specs=[pl.BlockSpec((rows, D_MODEL), lambda i, *_: (i, 0)),
                  pl.BlockSpec((rows, LANES), lambda i, *_: (i, 0)),
                  pl.BlockSpec((1, mrows, 6 * D_MODEL), mod_map),
                  pl.BlockSpec((1, D_MODEL), lambda i, *_: (0, 0)),
                  pl.BlockSpec(memory_space=pl.ANY),
                  pl.BlockSpec(memory_space=pl.ANY)],
        out_specs=pl.BlockSpec((rows, D_MODEL), lambda i, *_: (i, 0)),
        scratch_shapes=[pltpu.VMEM((2, 2, rows * SUBLANES, LANES), F32),
                        pltpu.SemaphoreType.DMA((2, 2))])
    return pl.pallas_call(
        functools.partial(_final_body, tm=tm, n_sub=n_sub, n_tiles=nt, tile_base=tile_base),
        grid_spec=grid_spec,
        out_shape=jax.ShapeDtypeStruct((n, D_MODEL), F32),
        compiler_params=_cparams(("arbitrary",)),
        name="final_sample" if sample else "final_prompt",
    )(seg_off, seg_len, x1, info, mod, g_final.reshape(1, D_MODEL), ya_sorted, yb_sorted)


def _bucket_table(n):
    d = np.arange(n)
    max_exact = N_BUCKETS // 2
    nf = np.maximum(d, 1).astype(np.float64)
    large = max_exact + (np.log(nf / max_exact) / math.log(MAX_DISTANCE / max_exact)
                         * (N_BUCKETS - max_exact)).astype(np.int64)
    large = np.minimum(large, N_BUCKETS - 1)
    return np.where(d < max_exact, d, large).astype(np.int32)


def _toeplitz(v, n_rows, n_cols):
    length = n_rows + n_cols - 1
    lead = v.shape[:-1]
    vp = jnp.concatenate([v, jnp.zeros(lead + (1,), v.dtype)], axis=-1)
    skew = jnp.tile(vp, (1,) * len(lead) + (n_rows,))[..., :n_rows * length].reshape(lead + (n_rows, length))
    return skew[..., n_rows - 1:n_rows - 1 + n_cols]


def kernel(x_prompt, x_sample, c_prompt, c_sample, cache_k, cache_v, page_table, w_ada, b_ada, w_in, w_o,
           lam_q1, lam_k1, lam_q2, lam_k2, g_subln, rel_bias, g_sg_ln, b_sg_ln, w_s, b_s, w_rg, b_rg,
           w_re, b_re, w_gate, w_up, w_down, g_final):
    batch, seq, _ = x_prompt.shape
    dec_b, dec_t, _ = x_sample.shape
    n_pages = page_table.shape[1]
    n_p = batch * seq
    n_s = dec_b * dec_t
    n_tot = n_p + n_s
    assert w_in.shape[0] == 1 and cache_k.shape[1] == 1 and seq % TQ_ATT == 0 and n_pages % PAGES_PER_GROUP == 0
    assert n_p % TM_TOK == 0 and n_p % n_s == 0 and n_tot % TM_E == 0 and dec_t == SUBLANES
    assert TAIL_TOKENS >= MAX_DISTANCE + dec_t and TAIL_TOKENS <= PAGES_PER_GROUP * PAGE
    assert n_s == TM_D and n_p % TM_D == 0 and N_CLASSES <= LANES

    w_in_b = w_in[0].astype(BF16)
    w_o_b = w_o[0].astype(BF16)
    wr_t = jnp.zeros((32, D_MODEL), F32).at[0:N_EG].set(w_rg[0].T).at[N_EG:N_EG + N_EG * EPG].set(w_re[0].T)
    wr_t = wr_t.astype(BF16)
    br = jnp.zeros((32, 1), F32).at[0:N_EG, 0].set(b_rg[0]).at[N_EG:N_EG + N_EG * EPG, 0].set(b_re[0])
    wg_b, wu_b, wd_b = w_gate[0], w_up[0], w_down[0]
    ws_tril = jnp.tril(w_s[0])
    ws_p = ws_tril.astype(BF16)
    bs_p = b_s[0][:, :, None]
    same_seq = np.kron(np.eye(dec_b, dtype=np.float32), np.ones((dec_t, dec_t), np.float32))
    rep = np.tile(np.eye(dec_t, dtype=np.float32), (dec_b, 1))
    ws_rep = jnp.einsum('ri,gij,cj->grc', rep, ws_tril[:, :dec_t, :dec_t], rep,
                        precision=lax.Precision.HIGHEST)
    ws_s = (ws_rep * same_seq).astype(BF16)
    bs_s = jnp.tile(b_s[0][:, :dec_t], (1, dec_b))[:, :, None]
    gln = g_sg_ln[0]
    bln = b_sg_ln[0]
    lam = (jnp.exp(jnp.sum(lam_q1[0] * lam_k1[0])) - jnp.exp(jnp.sum(lam_q2[0] * lam_k2[0]))
           + LAM_INIT).reshape(1).astype(F32)

    blk = BIAS_BLOCK
    n_dist = max(2 * blk, TAIL_TOKENS + dec_t)
    onehot = np.eye(N_BUCKETS, dtype=np.float32)[_bucket_table(n_dist)]
    ft = jnp.dot(onehot, rel_bias - rel_bias[N_BUCKETS - 1], precision=lax.Precision.HIGHEST).T
    neg = lambda n: jnp.full((N_HEADS, n), NEG, F32)
    bias_near = _toeplitz(jnp.concatenate([neg(blk - 1), ft[:, 0:blk]], axis=1), blk, blk)
    bias_next = _toeplitz(ft[:, 1:2 * blk], blk, blk)
    bias_t = jnp.stack([bias_next, bias_near], axis=1) * LOG2E
    bl = _toeplitz(jnp.flip(ft[:, 1:TAIL_TOKENS + dec_t], axis=1), dec_t, TAIL_TOKENS)
    head_eq = jnp.eye(N_HEADS, dtype=F32)
    bias_last = (bl[:, None, :, :, None] * head_eq[:, None, None, None, :])
    bias_last = jnp.broadcast_to(bias_last, (N_HEADS, 2, dec_t, TAIL_TOKENS, N_HEADS)).reshape(
        2 * N_HEADS * dec_t, TAIL_TOKENS * N_HEADS)
    bn = _toeplitz(jnp.concatenate([jnp.flip(ft[:, 0:dec_t], axis=1), neg(dec_t - 1)], axis=1),
                   dec_t, dec_t)
    bn = jnp.where(head_eq[:, None, None, :] > 0, bn[:, :, :, None], NEG)
    bn = jnp.broadcast_to(bn[:, None], (N_HEADS, 2, dec_t, dec_t, N_HEADS)).reshape(
        2 * N_HEADS * dec_t, dec_t * N_HEADS)
    bias_new = jnp.concatenate([bn, jnp.full((bn.shape[0], PAGE - bn.shape[1]), NEG, F32)], axis=1)

    c_all = jnp.concatenate([c_prompt, c_sample, jnp.zeros((4, D_MODEL), F32)], axis=0)
    mod_all = _ada(c_all, w_ada[0], b_ada[0])
    mod_p = mod_all[:batch].reshape(batch, 1, 6 * D_MODEL)
    mod_s = jnp.repeat(mod_all[batch:batch + dec_b], dec_t, axis=0).reshape(1, n_s, 6 * D_MODEL)

    xp = x_prompt.reshape(n_p, D_MODEL)
    xs = x_sample.reshape(n_s, D_MODEL)

    q_p, kf_p, kb_p, vf_p, vt_p, sg_p = _stage_a(xp, mod_p, w_in_b, ws_p, bs_p, gln, bln,
                                                 tm=TM_TOK, chunk=CHUNK, sample=False)
    q_s, kf_s, vf_s, sg_s, vsn_s = _stage_a(xs, mod_s, w_in_b, ws_s, bs_s, gln, bln,
                                            tm=n_s, chunk=n_s, sample=True)

    g_col = g_subln[0].reshape(HEAD_W, 1)
    g_row = g_subln[0].reshape(1, HEAD_W)
    pad = ((0, 0), (0, PAGE - dec_t * N_HEADS), (0, 0))
    knew = jnp.pad(kf_s.reshape(dec_b, dec_t * N_HEADS, HEAD_W), pad).astype(BF16)
    vnew = jnp.pad(vf_s.reshape(dec_b, dec_t * N_HEADS, HEAD_W), pad).astype(BF16)
    ck = cache_k.reshape(-1, HEAD_W)
    cv = cache_v.reshape(-1, HEAD_W)
    o_p, o_s = _attn_fused(page_table.reshape(-1), lam, q_p, kb_p, vt_p, bias_t, g_col,
                           q_s, knew, vnew, bias_last, bias_new, g_row, ck, cv,
                           batch=batch, seq=seq, tq=TQ_ATT, tk=TM_TOK, dec_b=dec_b, n_pages=n_pages)

    x1_p, h2_p, cls_p, info_p = _stage_c(o_p, sg_p, xp, mod_p, w_o_b, wr_t, br, tm=TM_TOK, sample=False)
    x1_s, h2_s, cls_s, info_s = _stage_c(o_s, sg_s, xs, mod_s, w_o_b, wr_t, br, tm=n_s, sample=True)

    tm_e = TM_E
    tm_d = TM_D
    n_max = n_tot // tm_e + N_CLASSES
    n_dt = n_tot // tm_d
    cls = jnp.concatenate([cls_p[0], cls_s[0]]).reshape(n_dt, tm_d)
    classes = jnp.arange(N_CLASSES, dtype=jnp.int32)
    seg_len = jnp.sum((cls[:, :, None] == classes).astype(jnp.int32), axis=1)
    counts = jnp.sum(seg_len, axis=0)
    ntile_c = (counts + tm_e - 1) // tm_e
    tile_end = jnp.cumsum(ntile_c)
    class_base = (tile_end - ntile_c) * tm_e
    nact = tile_end[-1]
    seg_off = class_base[None, :] + jnp.cumsum(seg_len, axis=0) - seg_len
    pad_off = class_base + counts
    pad_len = ntile_c * tm_e - counts
    tile_ids = jnp.arange(n_max, dtype=jnp.int32)
    tile_cls = jnp.sum((tile_ids[:, None] >= tile_end[None, :]).astype(jnp.int32), axis=1)
    last_cls = jnp.sum((nact - 1 >= tile_end).astype(jnp.int32))
    tile_cls = jnp.where(tile_ids < nact, tile_cls, last_cls)
    grp = tile_cls // N_PAIRS
    pidx = tile_cls % N_PAIRS
    pick = lambda table: sum(jnp.where(pidx == p, e, 0) for p, e in enumerate(table))
    tile_ea = (grp * EPG + pick(PAIR_A)).astype(jnp.int32)
    tile_eb = (grp * EPG + pick(PAIR_B)).astype(jnp.int32)
    empty = jnp.zeros((N_SUB - 1, N_CLASSES), jnp.int32)
    seg_off = jnp.concatenate([seg_off.astype(jnp.int32), empty]).reshape(-1)
    seg_len = jnp.concatenate([seg_len, empty]).reshape(-1)

    nact1 = nact.reshape(1).astype(jnp.int32)
    zero_rows = (jnp.sum(pad_len) + (n_max - nact) * tm_e).reshape(1).astype(jnp.int32)
    x_sorted = _dispatch(seg_off, seg_len, pad_off.astype(jnp.int32), pad_len.astype(jnp.int32), zero_rows,
                         nact1, h2_p, h2_s, info_p, info_s, tm=tm_d, n_sub=N_SUB, n_slots=n_max * tm_e)
    ya_sorted, yb_sorted = _moe(tile_ea, tile_eb, nact1, x_sorted, wg_b, wu_b, wd_b, tm=tm_e, n_max=n_max)

    y_p = _final(seg_off, seg_len, x1_p, info_p, mod_p, g_final, ya_sorted, yb_sorted,
                 tm=tm_d, n_sub=N_SUB, sample=False, tile_base=0)
    y_s = _final(seg_off, seg_len, x1_s, info_s, mod_s, g_final, ya_sorted, yb_sorted,
                 tm=tm_d, n_sub=1, sample=True, tile_base=n_p // tm_d)

    return (y_p.reshape(batch, seq, D_MODEL),
            y_s.reshape(dec_b, dec_t, D_MODEL),
            kf_p.reshape(batch, 1, seq, N_HEADS, HEAD_W),
            vf_p.reshape(batch, 1, seq, N_HEADS, HEAD_W),
            kf_s.reshape(dec_b, 1, dec_t, N_HEADS, HEAD_W),
            vf_s.reshape(dec_b, 1, dec_t, N_HEADS, HEAD_W),
            vsn_s.reshape(dec_b, 1, dec_t, N_GROUPS_SG, SG_CH))
```

```python
import functools
import math

import numpy as np
import jax
import jax.numpy as jnp
from jax import lax
from jax.experimental import pallas as pl
from jax.experimental.pallas import tpu as pltpu

F32 = jnp.float32
BF16 = jnp.bfloat16

D_MODEL = 1024
N_HEADS = 4
DK = 64
HEAD_W = 128
QK_W = N_HEADS * HEAD_W
N_GROUPS_SG = 4
SG_CH = 128
CHUNK = 128
PAGE = 128
N_BUCKETS = 32
MAX_DISTANCE = 128
N_EG = 4
EPG = 4
N_PAIRS = 6
N_CLASSES = N_EG * N_PAIRS
D_EXPERT = 512
EPS = 1e-6
LAM_INIT = 0.8 - 0.6 * math.exp(-0.3 * 0)
NEG = -1e30
LOG2E = math.log2(math.e)
LANES = 128
SUBLANES = 8

TM_TOK = 512
STAGE_A_PARTS = 4
TQ_ATT = 2048
PAGES_PER_GROUP = 8
RING = 4
TM_E = 256
MOE_PARTS = 1
TM_D = 256
N_SUB = 4
VMEM_LIMIT = 56 * 1024 * 1024


def _cparams(sem):
    return pltpu.CompilerParams(dimension_semantics=sem, vmem_limit_bytes=VMEM_LIMIT)


def _ada_body(c_ref, w_ref, b_ref, o_ref):
    c = c_ref[...]
    a = (c * jax.nn.sigmoid(c)).astype(BF16)
    o_ref[...] = jnp.dot(a, w_ref[...].astype(BF16), preferred_element_type=F32) + b_ref[...]


def _ada(c_all, w_ada, b_ada):
    m = c_all.shape[0]
    n = w_ada.shape[1]
    tn = 1536
    return pl.pallas_call(
        _ada_body,
        grid=(n // tn,),
        in_specs=[pl.BlockSpec((m, D_MODEL), lambda j: (0, 0)),
                  pl.BlockSpec((D_MODEL, tn), lambda j: (0, j)),
                  pl.BlockSpec((1, tn), lambda j: (0, j))],
        out_specs=pl.BlockSpec((m, tn), lambda j: (0, j)),
        out_shape=jax.ShapeDtypeStruct((m, n), F32),
        compiler_params=_cparams(("arbitrary",)),
        name="adaln",
    )(c_all, w_ada, b_ada.reshape(1, n))


def _stage_a_body(x_ref, mod_ref, w_in_ref, ws_ref, bs_ref, gln_ref, bln_ref,
                  *out_refs, chunk, sample):
    if sample:
        q_ref, kf_ref, vf_ref, sg_ref, vsn_ref = out_refs
    else:
        q_ref, kf_ref, kb_ref, vf_ref, vt_ref, sg_ref = out_refs
    x = x_ref[...]
    tm = x.shape[0]
    mod = mod_ref[0]
    sh1 = mod[:, 0:D_MODEL]
    sc1 = mod[:, D_MODEL:2 * D_MODEL]
    ms = jnp.mean(x * x, axis=-1, keepdims=True)
    h = (x * lax.rsqrt(ms + EPS)) * (1.0 + sc1) + sh1
    hb = h.astype(BF16)
    n_parts = 1 if sample else STAGE_A_PARTS
    rows_p = tm // n_parts
    zs = [jnp.dot(hb[p * rows_p:(p + 1) * rows_p], w_in_ref[...], preferred_element_type=F32)
          for p in range(n_parts)]
    for p, z in enumerate(zs):
        a = p * rows_p
        q = z[:, 0:QK_W] * (DK ** -0.5)
        k = z[:, QK_W:2 * QK_W]
        v = z[:, 2 * QK_W:3 * QK_W]
        for hd in range(N_HEADS):
            kf_ref[pl.ds(a * N_HEADS + hd, rows_p, stride=N_HEADS), :] = k[:, hd * HEAD_W:(hd + 1) * HEAD_W]
            vf_ref[pl.ds(a * N_HEADS + hd, rows_p, stride=N_HEADS), :] = v[:, hd * HEAD_W:(hd + 1) * HEAD_W]
        if sample:
            q_ref[...] = q
        else:
            q_ref[a:a + rows_p, :] = (q * LOG2E).astype(BF16)
            kb_ref[a:a + rows_p, :] = k.astype(BF16)
            vt_ref[0, :, a:a + rows_p] = v.T.astype(BF16)
        u = z[:, 3 * QK_W:4 * QK_W]
        vs = z[:, 4 * QK_W:5 * QK_W]
        for g in range(N_GROUPS_SG):
            lo, hi = g * SG_CH, (g + 1) * SG_CH
            vg = vs[:, lo:hi]
            mu = jnp.mean(vg, axis=-1, keepdims=True)
            dv = vg - mu
            var = jnp.mean(dv * dv, axis=-1, keepdims=True)
            vn = (dv * lax.rsqrt(var + EPS)) * gln_ref[g:g + 1, :] + bln_ref[g:g + 1, :]
            if sample:
                vsn_ref[:, lo:hi] = vn
            vnb = vn.astype(BF16)
            for c in range(rows_p // chunk):
                r0, r1 = c * chunk, (c + 1) * chunk
                s = jnp.dot(ws_ref[g], vnb[r0:r1], preferred_element_type=F32) + bs_ref[g]
                sg_ref[a + r0:a + r1, lo:hi] = (u[r0:r1, lo:hi] * s).astype(BF16)


def _stage_a(x, mod, w_in_b, ws, bs, gln, bln, *, tm, chunk, sample):
    n = x.shape[0]
    nt = n // tm
    mrows = mod.shape[1]
    row = lambda w, dt: jax.ShapeDtypeStruct((n, w), dt)
    blk = lambda w: pl.BlockSpec((tm, w), lambda i: (i, 0))
    cache_shape = jax.ShapeDtypeStruct((n * N_HEADS, HEAD_W), F32)
    cache_blk = pl.BlockSpec((tm * N_HEADS, HEAD_W), lambda i: (i, 0))
    if sample:
        out_shape = [row(QK_W, F32), cache_shape, cache_shape, row(QK_W, BF16), row(QK_W, F32)]
        out_specs = [blk(QK_W), cache_blk, cache_blk, blk(QK_W), blk(QK_W)]
        mod_map = lambda i: (0, 0, 0)
    else:
        out_shape = [row(QK_W, BF16), cache_shape, row(QK_W, BF16), cache_shape,
                     jax.ShapeDtypeStruct((nt, QK_W, tm), BF16), row(QK_W, BF16)]
        out_specs = [blk(QK_W), cache_blk, blk(QK_W), cache_blk,
                     pl.BlockSpec((1, QK_W, tm), lambda i: (i, 0, 0)), blk(QK_W)]
        tiles_per_batch = 4096 // tm
        mod_map = lambda i: (i // tiles_per_batch, 0, 0)
    full = lambda a: pl.BlockSpec(a.shape, lambda i: (0,) * a.ndim)
    return pl.pallas_call(
        functools.partial(_stage_a_body, chunk=chunk, sample=sample),
        grid=(nt,),
        in_specs=[blk(D_MODEL),
                  pl.BlockSpec((1, mrows, 6 * D_MODEL), mod_map),
                  full(w_in_b), full(ws), full(bs), full(gln), full(bln)],
        out_specs=out_specs,
        out_shape=out_shape,
        compiler_params=_cparams(("arbitrary",)),
        name="stage_a_sample" if sample else "stage_a_prompt",
    )(x, mod, w_in_b, ws, bs, gln, bln)


ATT_COLS = 256
BIAS_BLOCK = MAX_DISTANCE


def _attn_body(lam_ref, q_ref, k_ref, vt_ref, bias_ref, g_ref, o_ref, *scratch, tq, tk, between):
    n_chain = 2 * tq // ATT_COLS
    q2_refs, m_refs, l_refs, acc_refs = (scratch[i * n_chain:(i + 1) * n_chain] for i in range(4))
    qi = pl.program_id(2)
    between(0)
    for c in range(n_chain):
        q0 = (c * ATT_COLS) % tq
        q = q_ref[q0:q0 + ATT_COLS, :]
        lane = lax.broadcasted_iota(jnp.int32, q.shape, 1)
        keep = (lane < DK) if c < n_chain // 2 else (lane >= DK)
        q2_refs[c][...] = jnp.where(keep, q, jnp.zeros_like(q))
        m_refs[c][...] = jnp.full(m_refs[c].shape, NEG, F32)
        l_refs[c][...] = jnp.zeros(l_refs[c].shape, F32)
        acc_refs[c][...] = jnp.zeros(acc_refs[c].shape, F32)

    blk = BIAS_BLOCK

    kblocks = tk // blk

    def block_kinds(rel, q0):
        return [[(q0 // blk + b) - (rel + a) for b in range(ATT_COLS // blk)] for a in range(kblocks)]

    def keys_needed(rel, q0):
        if rel is None:
            return kblocks
        return sum(1 for row in block_kinds(rel, q0) if max(row) >= 0)

    def with_bias(s, rel, q0):
        if rel is None:
            return s
        kinds = block_kinds(rel, q0)[:s.shape[0] // blk]
        if all(d >= 2 for row in kinds for d in row):
            return s
        nxt, near = bias_ref[0, 0], bias_ref[0, 1]
        pick = lambda d: (jnp.full((blk, blk), NEG, F32) if d < 0 else near if d == 0 else nxt if d == 1
                          else jnp.zeros((blk, blk), F32))
        rows = []
        for a, row in enumerate(kinds):
            s_row = s[a * blk:(a + 1) * blk]
            if any(d < 2 for d in row):
                s_row = s_row + jnp.concatenate([pick(d) for d in row], axis=1)
            rows.append(s_row)
        return jnp.concatenate(rows, axis=0)

    def score(j, rel, c):
        nk = keys_needed(rel, (c * ATT_COLS) % tq) * blk
        if nk == 0:
            return None
        k = k_ref[pl.ds(pl.multiple_of(j * tk, tk), nk), :]
        s = lax.dot_general(k, q2_refs[c][...], (((1,), (1,)), ((), ())),
                            preferred_element_type=F32)
        return with_bias(s, rel, (c * ATT_COLS) % tq)

    def accumulate(j, c, s):
        vt = vt_ref[j, :, 0:s.shape[0]]
        m_old = m_refs[c][...]
        m_new = jnp.maximum(m_old, jnp.max(s, axis=0, keepdims=True))
        alpha = jnp.exp2(m_old - m_new)
        p = jnp.exp2(s - m_new)
        l_refs[c][...] = alpha * l_refs[c][...] + jnp.sum(p, axis=0, keepdims=True)
        acc_refs[c][...] = alpha * acc_refs[c][...] + jnp.dot(vt, p.astype(BF16), preferred_element_type=F32)
        m_refs[c][...] = m_new

    def tiles(*work):
        scores = [[score(j, rel, c) for c in range(n_chain)] for j, rel in work]
        for (j, _), tile_scores in zip(work, scores):
            for c, s in enumerate(tile_scores):
                if s is not None:
                    accumulate(j, c, s)

    ratio = tq // tk
    first_diag = qi * ratio
    n_plain = jnp.maximum(first_diag - 1, 0)

    def plain_pair(jj, carry):
        tiles((2 * jj, None), (2 * jj + 1, None))
        return carry

    diag = [(first_diag + r, r * kblocks) for r in range(ratio)]
    assert ratio % 2 == 0 and ratio >= 4
    half = ratio // 2

    @pl.when(qi == 0)
    def _():
        tiles(*diag[:half])

    @pl.when(qi >= 1)
    def _():
        lax.fori_loop(0, n_plain // 2, plain_pair, 0)

    between(1)

    @pl.when(qi == 0)
    def _():
        tiles(*diag[half:])

    @pl.when(qi >= 1)
    def _():
        tiles((first_diag - 2, None), (first_diag - 1, -kblocks))

    between(2)

    @pl.when(qi >= 1)
    def _():
        tiles(*diag)

    between(3)

    lam = lam_ref[0]
    o_all = jnp.concatenate([acc_refs[c][...] * (1.0 / l_refs[c][...]) for c in range(n_chain)],
                            axis=1)
    o = o_all[:, 0:tq] - lam * o_all[:, tq:2 * tq]
    ms = jnp.mean(o * o, axis=0, keepdims=True)
    on = (o * lax.rsqrt(ms + EPS)) * g_ref[...] * (1.0 - LAM_INIT)
    o_ref[...] = on.T.astype(BF16)


PAGE_ROWS = PAGE * N_HEADS
TAIL_TOKENS = 2 * PAGE


def _attn_fused_body(pt_ref, lam_ref, q_ref, k_ref, vt_ref, bias_ref, gcol_ref,
                     qs_ref, knew_ref, vnew_ref, bl_ref, bn_ref, grow_ref, ck_ref, cv_ref,
                     o_ref, os_ref, *scratch, tq, tk, pages, n_sub, n_steps):
    n_prompt_scratch = 4 * (2 * tq // ATT_COLS)
    kbuf, vbuf, sem, mask_ref, qm_ref, m_ref, l_ref, acc_ref = scratch[n_prompt_scratch:]
    step = (pl.program_id(0) * pl.num_programs(1) + pl.program_id(1)) * pl.num_programs(2) + pl.program_id(2)

    def page_copies(group, sl):
        base = group * pages
        out = []
        for i in range(pages):
            src = pl.ds(pl.multiple_of(pt_ref[base + i] * PAGE_ROWS, PAGE_ROWS), PAGE_ROWS)
            dst = pl.ds(i * PAGE_ROWS, PAGE_ROWS)
            out.append(pltpu.make_async_copy(ck_ref.at[src], kbuf.at[sl, dst], sem.at[sl, 0]))
            out.append(pltpu.make_async_copy(cv_ref.at[src], vbuf.at[sl, dst], sem.at[sl, 1]))
        return out

    def start_all(copies):
        for c in copies:
            c.start()

    def update(kb, vb, bias):
        sc = lax.dot_general(qm_ref[...], kb, (((1,), (1,)), ((), ())),
                             preferred_element_type=F32) + bias
        m_old = m_ref[...]
        m_new = jnp.maximum(m_old, jnp.max(sc, axis=1, keepdims=True))
        alpha = jnp.exp(m_old - m_new)
        p = jnp.exp(sc - m_new)
        l_ref[...] = alpha * l_ref[...] + jnp.sum(p, axis=1, keepdims=True)
        acc_ref[...] = alpha * acc_ref[...] + jnp.dot(p.astype(BF16), vb, preferred_element_type=F32)
        m_ref[...] = m_new

    def sample_group(u):
        group = step * n_sub + u
        slot = u % RING
        ahead = RING - 1
        if u == 0:
            @pl.when(step == 0)
            def _():
                for g0 in range(ahead):
                    start_all(page_copies(g0, g0))
                row = lax.broadcasted_iota(jnp.int32, mask_ref.shape, 0)
                col = lax.broadcasted_iota(jnp.int32, mask_ref.shape, 1)
                same_head = (col % N_HEADS) == (row // (2 * SUBLANES))
                mask_ref[...] = jnp.where(same_head, 0.0, NEG)
        if u + ahead < n_sub:
            start_all(page_copies(group + ahead, (u + ahead) % RING))
        else:
            @pl.when(step + 1 < n_steps)
            def _():
                start_all(page_copies(group + ahead, (u + ahead) % RING))
        for c in page_copies(group, slot):
            c.wait()
        if u == 0:
            m_ref[...] = jnp.full(m_ref.shape, NEG, F32)
            l_ref[...] = jnp.zeros(l_ref.shape, F32)
            acc_ref[...] = jnp.zeros(acc_ref.shape, F32)
            q = qs_ref[...]
            lane = lax.broadcasted_iota(jnp.int32, (SUBLANES, HEAD_W), 1)
            pieces = []
            for h in range(N_HEADS):
                qh = q[:, h * HEAD_W:(h + 1) * HEAD_W]
                pieces += [jnp.where(lane < DK, qh, 0.0), jnp.where(lane >= DK, qh, 0.0)]
            qm_ref[...] = jnp.concatenate(pieces, axis=0).astype(BF16)
        bias = mask_ref[...]
        if u == n_sub - 1:
            head_cols = mask_ref.shape[1] - bl_ref.shape[1]
            bias = jnp.concatenate([bias[:, :head_cols], bias[:, head_cols:] + bl_ref[...]], axis=1)
        update(kbuf[slot].astype(BF16), vbuf[slot].astype(BF16), bias)
        if u == n_sub - 1:
            update(knew_ref[0], vnew_ref[0], bn_ref[...])
            lam = lam_ref[0]
            o_all = acc_ref[...] * (1.0 / l_ref[...])
            for h in range(N_HEADS):
                r = h * 2 * SUBLANES
                o = o_all[r:r + SUBLANES] - lam * o_all[r + SUBLANES:r + 2 * SUBLANES]
                ms = jnp.mean(o * o, axis=-1, keepdims=True)
                os_ref[:, h * HEAD_W:(h + 1) * HEAD_W] = ((o * lax.rsqrt(ms + EPS)) * grow_ref[...]
                                                          * (1.0 - LAM_INIT))

    per_call = n_sub // 4

    def between(k):
        for u in range(per_call * k, per_call * (k + 1)):
            sample_group(u)

    _attn_body(lam_ref, q_ref, k_ref, vt_ref, bias_ref, gcol_ref, o_ref, *scratch[:n_prompt_scratch],
               tq=tq, tk=tk, between=between)


def _attn_fused(page_table_flat, lam, q, kb, vt, bias_t, g_col, q_s, knew, vnew, bias_last, bias_new, g_row,
                cache_k, cache_v, *, batch, seq, tq, tk, dec_b, n_pages):
    nq = seq // tq
    nk = seq // tk
    n = batch * seq
    n_chain = 2 * tq // ATT_COLS
    pages = PAGES_PER_GROUP
    n_sub = n_pages // pages
    n_steps = batch * N_HEADS * nq
    dec_t = q_s.shape[0] // dec_b
    n_rows = 2 * N_HEADS * dec_t
    step_rows = pages * PAGE_ROWS
    assert n_steps == dec_b and n_sub % 4 == 0 and n_sub % RING == 0 and nq == 2
    assert bias_last.shape[1] <= step_rows
    assert vt.shape == (batch * nk, QK_W, tk) and tq % tk == 0 and tk % BIAS_BLOCK == 0
    lin = lambda b, h, i: (b * N_HEADS + h) * nq + i
    grid_spec = pltpu.PrefetchScalarGridSpec(
        num_scalar_prefetch=1,
        grid=(batch, N_HEADS, nq),
        in_specs=[pl.BlockSpec(memory_space=pltpu.SMEM),
                  pl.BlockSpec((tq, HEAD_W), lambda b, h, i, pt: (b * nq + i, h)),
                  pl.BlockSpec((seq, HEAD_W), lambda b, h, i, pt: (b, h)),
                  pl.BlockSpec((nk, HEAD_W, tk), lambda b, h, i, pt: (b, h, 0)),
                  pl.BlockSpec((1, 2, BIAS_BLOCK, BIAS_BLOCK), lambda b, h, i, pt: (h, 0, 0, 0)),
                  pl.BlockSpec((HEAD_W, 1), lambda b, h, i, pt: (0, 0)),
                  pl.BlockSpec((dec_t, QK_W), lambda b, h, i, pt: (lin(b, h, i), 0)),
                  pl.BlockSpec((1, PAGE, HEAD_W), lambda b, h, i, pt: (lin(b, h, i), 0, 0)),
                  pl.BlockSpec((1, PAGE, HEAD_W), lambda b, h, i, pt: (lin(b, h, i), 0, 0)),
                  pl.BlockSpec(bias_last.shape, lambda b, h, i, pt: (0, 0)),
                  pl.BlockSpec(bias_new.shape, lambda b, h, i, pt: (0, 0)),
                  pl.BlockSpec((1, HEAD_W), lambda b, h, i, pt: (0, 0)),
                  pl.BlockSpec(memory_space=pl.ANY),
                  pl.BlockSpec(memory_space=pl.ANY)],
        out_specs=[pl.BlockSpec((tq, HEAD_W), lambda b, h, i, pt: (b * nq + i, h)),
                   pl.BlockSpec((dec_t, QK_W), lambda b, h, i, pt: (lin(b, h, i), 0))],
        scratch_shapes=([pltpu.VMEM((ATT_COLS, HEAD_W), BF16)] * n_chain
                        + [pltpu.VMEM((1, ATT_COLS), F32)] * (2 * n_chain)
                        + [pltpu.VMEM((HEAD_W, ATT_COLS), F32)] * n_chain
                        + [pltpu.VMEM((RING, step_rows, HEAD_W), F32),
                           pltpu.VMEM((RING, step_rows, HEAD_W), F32),
                           pltpu.SemaphoreType.DMA((RING, 2)),
                           pltpu.VMEM((n_rows, step_rows), F32),
                           pltpu.VMEM((n_rows, HEAD_W), BF16),
                           pltpu.VMEM((n_rows, 1), F32), pltpu.VMEM((n_rows, 1), F32),
                           pltpu.VMEM((n_rows, HEAD_W), F32)]))
    return pl.pallas_call(
        functools.partial(_attn_fused_body, tq=tq, tk=tk, pages=pages, n_sub=n_sub, n_steps=n_steps),
        grid_spec=grid_spec,
        out_shape=[jax.ShapeDtypeStruct((n, QK_W), BF16), jax.ShapeDtypeStruct(q_s.shape, F32)],
        compiler_params=_cparams(("arbitrary", "arbitrary", "arbitrary")),
        name="attn_fused",
    )(page_table_flat, lam, q, kb, vt, bias_t, g_col, q_s, knew, vnew, bias_last, bias_new, g_row,
      cache_k, cache_v)


INFO_W_A, INFO_W_B, INFO_CLS = 0, 1, 2
PAIR_A = (0, 0, 0, 1, 1, 3)
PAIR_B = (1, 2, 3, 3, 2, 2)


def _stage_c_body(o_ref, sg_ref, x_ref, mod_ref, wo_ref, wr_ref, br_ref, x1_ref, h2_ref, cls_ref, info_ref):
    x = x_ref[...]
    tm = x.shape[0]
    mod = mod_ref[0]
    g1 = mod[:, 2 * D_MODEL:3 * D_MODEL]
    sh2 = mod[:, 3 * D_MODEL:4 * D_MODEL]
    sc2 = mod[:, 4 * D_MODEL:5 * D_MODEL]
    mix = (jnp.dot(o_ref[...].astype(BF16), wo_ref[0:QK_W, :], preferred_element_type=F32)
           + jnp.dot(sg_ref[...], wo_ref[QK_W:2 * QK_W, :], preferred_element_type=F32))
    x1 = x + g1 * mix
    x1_ref[...] = x1
    ms = jnp.mean(x1 * x1, axis=-1, keepdims=True)
    h2 = ((x1 * lax.rsqrt(ms + EPS)) * (1.0 + sc2) + sh2).astype(BF16)
    h2_ref[...] = h2
    lg = lax.dot_general(wr_ref[...], h2, (((1,), (1,)), ((), ())),
                         preferred_element_type=F32) + br_ref[...]
    gl = [lg[i:i + 1, :] for i in range(N_EG)]
    el = [lg[N_EG + i:N_EG + i + 1, :] for i in range(N_EG * EPG)]
    gmax = jnp.maximum(jnp.maximum(gl[0], gl[1]), jnp.maximum(gl[2], gl[3]))
    gi = jnp.where(gl[0] == gmax, 0, jnp.where(gl[1] == gmax, 1, jnp.where(gl[2] == gmax, 2, 3)))
    gsum = (jnp.exp(gl[0] - gmax) + jnp.exp(gl[1] - gmax)
            + jnp.exp(gl[2] - gmax) + jnp.exp(gl[3] - gmax))
    gp = 1.0 / gsum
    sel = [jnp.where(gi == 0, el[j], jnp.where(gi == 1, el[EPG + j],
                                               jnp.where(gi == 2, el[2 * EPG + j], el[3 * EPG + j])))
           for j in range(EPG)]
    v0 = jnp.maximum(jnp.maximum(sel[0], sel[1]), jnp.maximum(sel[2], sel[3]))
    i0 = jnp.where(sel[0] == v0, 0, jnp.where(sel[1] == v0, 1, jnp.where(sel[2] == v0, 2, 3)))
    rest = [jnp.where(i0 == j, -3e38, sel[j]) for j in range(EPG)]
    v1 = jnp.maximum(jnp.maximum(rest[0], rest[1]), jnp.maximum(rest[2], rest[3]))
    i1 = jnp.where(rest[0] == v1, 0, jnp.where(rest[1] == v1, 1, jnp.where(rest[2] == v1, 2, 3)))
    e1 = jnp.exp(v1 - v0)
    den = 1.0 / (1.0 + e1)
    tw0 = den * gp
    tw1 = e1 * den * gp
    first_low = i0 < i1
    lo = jnp.where(first_low, i0, i1)
    hi = jnp.where(first_low, i1, i0)
    w_lo = jnp.where(first_low, tw0, tw1)
    w_hi = jnp.where(first_low, tw1, tw0)
    pair = jnp.where(lo == 0, hi - 1, jnp.where(lo == 2, 5, jnp.where(hi == 3, 3, 4)))
    swapped = pair == 5
    w_a = jnp.where(swapped, w_hi, w_lo)
    w_b = jnp.where(swapped, w_lo, w_hi)
    cls = gi * N_PAIRS + pair
    cls_ref[...] = jnp.broadcast_to(cls, cls_ref.shape).astype(jnp.int32)
    row = lax.broadcasted_iota(jnp.int32, (LANES, tm), 0)
    rec = jnp.where(row == INFO_W_A, w_a, jnp.where(row == INFO_W_B, w_b,
                                                    jnp.where(row == INFO_CLS, cls.astype(F32), 0.0)))
    info_ref[...] = rec.T


def _stage_c(o, sg, x, mod, wo_b, wr_t, br, *, tm, sample):
    n = x.shape[0]
    nt = n // tm
    mrows = mod.shape[1]
    blk = lambda w: pl.BlockSpec((tm, w), lambda i: (i, 0))
    full = lambda a: pl.BlockSpec(a.shape, lambda i: (0,) * a.ndim)
    if sample:
        mod_map = lambda i: (0, 0, 0)
    else:
        tiles_per_batch = 4096 // tm
        mod_map = lambda i: (i // tiles_per_batch, 0, 0)
    return pl.pallas_call(
        _stage_c_body,
        grid=(nt,),
        in_specs=[blk(QK_W), blk(QK_W), blk(D_MODEL),
                  pl.BlockSpec((1, mrows, 6 * D_MODEL), mod_map),
                  full(wo_b), full(wr_t), full(br)],
        out_specs=[blk(D_MODEL), blk(D_MODEL),
                   pl.BlockSpec((SUBLANES, tm), lambda i: (0, i)),
                   blk(LANES)],
        out_shape=[jax.ShapeDtypeStruct((n, D_MODEL), F32),
                   jax.ShapeDtypeStruct((n, D_MODEL), BF16),
                   jax.ShapeDtypeStruct((SUBLANES, n), jnp.int32),
                   jax.ShapeDtypeStruct((n, LANES), F32)],
        compiler_params=_cparams(("arbitrary",)),
        name="stage_c_sample" if sample else "stage_c_prompt",
    )(o, sg, x, mod, wo_b, wr_t, br)


def rows8(ref, start, count):
    scale = lambda v: v * SUBLANES if isinstance(v, int) else pl.multiple_of(v * SUBLANES, SUBLANES)
    return ref.at[pl.ds(scale(start), scale(count))]


def _perm_t(cls_col):
    n = cls_col.shape[0]
    lane = lax.broadcasted_iota(jnp.int32, (n, LANES), 1).astype(F32)
    onehot = (lane == cls_col).astype(BF16)
    r = lax.broadcasted_iota(jnp.int32, (n, n), 0)
    c = lax.broadcasted_iota(jnp.int32, (n, n), 1)
    before = (c < r).astype(BF16)
    rank = jnp.dot(before, onehot, preferred_element_type=F32)
    cnt = jnp.sum(onehot.astype(F32), axis=0, keepdims=True)
    cr = lax.broadcasted_iota(jnp.int32, (LANES, LANES), 0)
    cc = lax.broadcasted_iota(jnp.int32, (LANES, LANES), 1)
    lower_cls = (cr < cc).astype(BF16)
    base = jnp.dot(jnp.broadcast_to(cnt, (SUBLANES, LANES)).astype(BF16), lower_cls,
                   preferred_element_type=F32)[0:1, :]
    pos = jnp.sum(onehot.astype(F32) * (base + rank), axis=1, keepdims=True)
    dest = lax.broadcasted_iota(jnp.int32, (n, n), 1).astype(F32)
    return (dest == pos).astype(F32)


def _dispatch_body(soff_ref, slen_ref, poff_ref, plen_ref, ptot_ref, nact_ref, hp_ref, hs_ref, ip_ref, is_ref,
                   xs_ref, buf, zbuf, sem, zsem, *, tm, n_sub, n_tiles, n_slab_tiles):
    i = pl.program_id(0)
    slot = i % 2
    is_sample = i == n_tiles - 1

    def wait_tile(sl, tokens):
        pltpu.make_async_copy(rows8(buf.at[sl], 0, tokens), rows8(buf.at[sl], 0, tokens), sem.at[sl]).wait()

    @pl.when(i == 0)
    def _():
        zbuf[...] = jnp.zeros(zbuf.shape, F32)
        for c in range(N_CLASSES):
            @pl.when(plen_ref[c] > 0)
            def _():
                pltpu.make_async_copy(rows8(zbuf, 0, plen_ref[c]), rows8(xs_ref, poff_ref[c], plen_ref[c]),
                                      zsem).start()
        for j in range(n_slab_tiles - N_CLASSES, n_slab_tiles):
            @pl.when(j >= nact_ref[0])
            def _():
                pltpu.make_async_copy(zbuf, rows8(xs_ref, j * TM_E, TM_E), zsem).start()

    xp = []
    for u in range(n_sub):
        rows = slice(u * tm, (u + 1) * tm)
        x = hp_ref[rows, :]
        info = ip_ref[rows, :]
        if u == 0:
            x = jnp.where(is_sample, hs_ref[...], x)
            info = jnp.where(is_sample, is_ref[...], info)
        perm = _perm_t(info[:, INFO_CLS:INFO_CLS + 1]).T.astype(BF16)
        xp.append(jnp.dot(perm, x, preferred_element_type=F32))

    @pl.when(i >= 2)
    def _():
        wait_tile(slot, n_sub * tm)

    bs = buf.at[slot]
    for u in range(n_sub):
        for c in range(D_MODEL // LANES):
            bs[pl.ds(u * tm * SUBLANES + c, tm, stride=SUBLANES), :] = xp[u][:, c * LANES:(c + 1) * LANES]
    for u in range(n_sub):
        local = u * tm
        for c in range(N_CLASSES):
            k = (i * n_sub + u) * N_CLASSES + c
            n_rows = slen_ref[k]

            @pl.when(n_rows > 0)
            def _():
                pltpu.make_async_copy(rows8(bs, local, n_rows), rows8(xs_ref, soff_ref[k], n_rows),
                                      sem.at[slot]).start()
            local = local + n_rows

    @pl.when(i == n_tiles - 1)
    def _():
        wait_tile(slot, tm)
        if n_tiles >= 2:
            wait_tile(1 - slot, n_sub * tm)

        @pl.when(ptot_ref[0] > 0)
        def _():
            n = pl.multiple_of(ptot_ref[0] * SUBLANES, SUBLANES)
            pltpu.make_async_copy(xs_ref.at[pl.ds(0, n)], xs_ref.at[pl.ds(0, n)], zsem).wait()


def _dispatch(seg_off, seg_len, pad_off, pad_len, pad_tot, nact, h2_p, h2_s, info_p, info_s, *, tm, n_sub, n_slots):
    n_prompt_steps = h2_p.shape[0] // (n_sub * tm)
    n_tiles = n_prompt_steps + 1
    assert h2_s.shape[0] == tm and h2_p.shape[0] % (n_sub * tm) == 0
    assert seg_len.shape[0] == n_tiles * n_sub * N_CLASSES
    last_p = n_prompt_steps - 1
    grid_spec = pltpu.PrefetchScalarGridSpec(
        num_scalar_prefetch=6,
        grid=(n_tiles,),
        in_specs=[pl.BlockSpec((n_sub * tm, D_MODEL), lambda i, *_: (jnp.minimum(i, last_p), 0)),
                  pl.BlockSpec((tm, D_MODEL), lambda i, *_: (0, 0)),
                  pl.BlockSpec((n_sub * tm, LANES), lambda i, *_: (jnp.minimum(i, last_p), 0)),
                  pl.BlockSpec((tm, LANES), lambda i, *_: (0, 0))],
        out_specs=pl.BlockSpec(memory_space=pl.ANY),
        scratch_shapes=[pltpu.VMEM((2, n_sub * tm * SUBLANES, LANES), F32),
                        pltpu.VMEM((TM_E * SUBLANES, LANES), F32),
                        pltpu.SemaphoreType.DMA((2,)),
                        pltpu.SemaphoreType.DMA(())])
    return pl.pallas_call(
        functools.partial(_dispatch_body, tm=tm, n_sub=n_sub, n_tiles=n_tiles, n_slab_tiles=n_slots // TM_E),
        grid_spec=grid_spec,
        out_shape=jax.ShapeDtypeStruct((n_slots * SUBLANES, LANES), F32),
        compiler_params=_cparams(("arbitrary",)),
        name="moe_dispatch",
    )(seg_off, seg_len, pad_off, pad_len, pad_tot, nact, h2_p, h2_s, info_p, info_s)


def _moe_body(exp_ref, par_ref, first_ref, nxt_ref, nact_ref, x_ref, wg_ref, wu_ref, wd_ref,
              ya_ref, yb_ref, gbuf, ubuf, dbuf, sem, *, tm):
    i = pl.program_id(0)
    n_tiles = pl.num_programs(0)

    def weight_copies(slot, expert, buf):
        return [pltpu.make_async_copy(w_ref.at[expert], w_buf.at[slot, buf], sem.at[slot, buf])
                for w_ref, w_buf in ((wg_ref, gbuf), (wu_ref, ubuf), (wd_ref, dbuf))]

    @pl.when(i < nact_ref[0])
    def _():
        for slot in range(2):
            k = slot * n_tiles + i
            buf = par_ref[k]

            @pl.when(i == 0)
            def _():
                for c in weight_copies(slot, exp_ref[k], buf):
                    c.start()

            @pl.when(first_ref[k] == 1)
            def _():
                for c in weight_copies(slot, exp_ref[k], buf):
                    c.wait()

                @pl.when(nxt_ref[k] >= 0)
                def _():
                    for c in weight_copies(slot, nxt_ref[k], 1 - buf):
                        c.start()

        x = jnp.concatenate([x_ref[pl.ds(c, tm, stride=SUBLANES), :] for c in range(D_MODEL // LANES)],
                            axis=1).astype(BF16)

        wts = [w_buf[slot, par_ref[slot * n_tiles + i]].astype(BF16)
               for slot in range(2) for w_buf in (gbuf, ubuf, dbuf)]
        rows_p = tm // MOE_PARTS

        def hidden(xp, wg, wu):
            return (jnp.dot(xp, wg, preferred_element_type=F32), jnp.dot(xp, wu, preferred_element_type=F32))

        def down(gate_up, wd):
            gate, up = gate_up
            he = (gate * jax.nn.sigmoid(gate)) * up
            return jnp.dot(he.astype(BF16), wd, preferred_element_type=F32)

        hs = []
        for p in range(MOE_PARTS):
            xp = x[p * rows_p:(p + 1) * rows_p]
            hs.append((hidden(xp, wts[0], wts[1]), hidden(xp, wts[3], wts[4])))
        for p, (ha, hb) in enumerate(hs):
            for y_ref, h, wd in ((ya_ref, ha, wts[2]), (yb_ref, hb, wts[5])):
                y = down(h, wd)
                for c in range(D_MODEL // LANES):
                    y_ref[pl.ds(p * rows_p * SUBLANES + c, rows_p, stride=SUBLANES), :] = (
                        y[:, c * LANES:(c + 1) * LANES])

    @pl.when(i >= nact_ref[0])
    def _():
        ya_ref[...] = jnp.zeros(ya_ref.shape, ya_ref.dtype)
        yb_ref[...] = jnp.zeros(yb_ref.shape, yb_ref.dtype)


def _moe(tile_ea, tile_eb, nact, x_sorted, w_gate, w_up, w_down, *, tm, n_max):
    tiles = jnp.arange(n_max, dtype=jnp.int32)
    experts = jnp.stack([tile_ea, tile_eb])
    active = tiles[None, :] < nact
    first = active & ((tiles[None, :] == 0) | (experts != jnp.roll(experts, 1, axis=1)))
    parity = (jnp.cumsum(first.astype(jnp.int32), axis=1) - 1) % 2
    opener = jnp.where(first, tiles[None, :], n_max)
    next_open = lax.cummin(jnp.concatenate([opener[:, 1:], jnp.full((2, 1), n_max, jnp.int32)], axis=1),
                           axis=1, reverse=True)
    nxt = jnp.where(next_open < n_max, jnp.take_along_axis(experts, jnp.minimum(next_open, n_max - 1), axis=1), -1)
    flat = lambda a: a.astype(jnp.int32).reshape(-1)
    rows_in = lambda i, ex, pa, fi, nx, na: (jnp.minimum(i, na[0] - 1), 0)
    grid_spec = pltpu.PrefetchScalarGridSpec(
        num_scalar_prefetch=5,
        grid=(n_max,),
        in_specs=[pl.BlockSpec((tm * SUBLANES, LANES), rows_in),
                  pl.BlockSpec(memory_space=pl.ANY), pl.BlockSpec(memory_space=pl.ANY),
                  pl.BlockSpec(memory_space=pl.ANY)],
        out_specs=[pl.BlockSpec((tm * SUBLANES, LANES), lambda i, *_: (i, 0))] * 2,
        scratch_shapes=[pltpu.VMEM((2, 2, D_MODEL, D_EXPERT), F32),
                        pltpu.VMEM((2, 2, D_MODEL, D_EXPERT), F32),
                        pltpu.VMEM((2, 2, D_EXPERT, D_MODEL), F32),
                        pltpu.SemaphoreType.DMA((2, 2))])
    return pl.pallas_call(
        functools.partial(_moe_body, tm=tm),
        grid_spec=grid_spec,
        out_shape=[jax.ShapeDtypeStruct(x_sorted.shape, F32)] * 2,
        compiler_params=_cparams(("arbitrary",)),
        name="moe",
    )(flat(experts), flat(parity), flat(first), flat(nxt), nact, x_sorted, w_gate, w_up, w_down)


def _final_body(soff_ref, slen_ref, x1_ref, info_ref, mod_ref, gf_ref, ya_ref, yb_ref, y_ref, buf, sem,
                *, tm, n_sub, n_tiles, tile_base):
    i = pl.program_id(0)
    slot = i % 2
    slabs = (ya_ref, yb_ref)

    def fetch(step, sl):
        for u in range(n_sub):
            local = u * tm
            for c in range(N_CLASSES):
                k = (step * n_sub + u + tile_base) * N_CLASSES + c
                n_rows = slen_ref[k]

                @pl.when(n_rows > 0)
                def _():
                    for e in range(2):
                        pltpu.make_async_copy(rows8(slabs[e], soff_ref[k], n_rows),
                                              rows8(buf.at[sl, e], local, n_rows), sem.at[sl, e]).start()
                local = local + n_rows

    @pl.when(i == 0)
    def _():
        fetch(0, 0)

    @pl.when(i + 1 < n_tiles)
    def _():
        fetch(i + 1, 1 - slot)

    info = info_ref[...]
    perm_t = [_perm_t(info[u * tm:(u + 1) * tm, INFO_CLS:INFO_CLS + 1]).astype(BF16) for u in range(n_sub)]
    moe = None
    for e, lane_w in enumerate((INFO_W_A, INFO_W_B)):
        pltpu.make_async_copy(buf.at[slot, e], buf.at[slot, e], sem.at[slot, e]).wait()
        bs = buf.at[slot, e]
        parts = []
        for u in range(n_sub):
            ye = jnp.concatenate([bs[pl.ds(u * tm * SUBLANES + c, tm, stride=SUBLANES), :]
                                  for c in range(D_MODEL // LANES)], axis=1)
            parts.append(jnp.dot(perm_t[u], ye.astype(BF16), preferred_element_type=F32))
        term = info[:, lane_w:lane_w + 1] * jnp.concatenate(parts, axis=0)
        moe = term if moe is None else moe + term
    x1 = x1_ref[...]
    g2 = mod_ref[0][:, 5 * D_MODEL:6 * D_MODEL]
    x2 = x1 + g2 * moe
    ms = jnp.mean(x2 * x2, axis=-1, keepdims=True)
    y_ref[...] = (x2 * lax.rsqrt(ms + EPS)) * gf_ref[...]


def _final(seg_off, seg_len, x1, info, mod, g_final, ya_sorted, yb_sorted, *, tm, n_sub, sample, tile_base):
    n = x1.shape[0]
    rows = n_sub * tm
    nt = n // rows
    mrows = mod.shape[1]
    if sample:
        mod_map = lambda i, *_: (0, 0, 0)
    else:
        tiles_per_batch = 4096 // rows
        mod_map = lambda i, *_: (i // tiles_per_batch, 0, 0)
    grid_spec = pltpu.PrefetchScalarGridSpec(
        num_scalar_prefetch=2,
        grid=(nt,),
        in_specs=[pl.BlockSpec((rows, D_MODEL), lambda i, *_: (i, 0)),
                  pl.BlockSpec((rows, LANES), lambda i, *_: (i, 0)),
                  pl.BlockSpec((1, mrows, 6 * D_MODEL), mod_map),
                  pl.BlockSpec((1, D_MODEL), lambda i, *_: (0, 0)),
                  pl.BlockSpec(memory_space=pl.ANY),
                  pl.BlockSpec(memory_space=pl.ANY)],
        out_specs=pl.BlockSpec((rows, D_MODEL), lambda i, *_: (i, 0)),
        scratch_shapes=[pltpu.VMEM((2, 2, rows * SUBLANES, LANES), F32),
                        pltpu.SemaphoreType.DMA((2, 2))])
    return pl.pallas_call(
        functools.partial(_final_body, tm=tm, n_sub=n_sub, n_tiles=nt, tile_base=tile_base),
        grid_spec=grid_spec,
        out_shape=jax.ShapeDtypeStruct((n, D_MODEL), F32),
        compiler_params=_cparams(("arbitrary",)),
        name="final_sample" if sample else "final_prompt",
    )(seg_off, seg_len, x1, info, mod, g_final.reshape(1, D_MODEL), ya_sorted, yb_sorted)


def _bucket_table(n):
    d = np.arange(n)
    max_exact = N_BUCKETS // 2
    nf = np.maximum(d, 1).astype(np.float64)
    large = max_exact + (np.log(nf / max_exact) / math.log(MAX_DISTANCE / max_exact)
                         * (N_BUCKETS - max_exact)).astype(np.int64)
    large = np.minimum(large, N_BUCKETS - 1)
    return np.where(d < max_exact, d, large).astype(np.int32)


def _toeplitz(v, n_rows, n_cols):
    length = n_rows + n_cols - 1
    lead = v.shape[:-1]
    vp = jnp.concatenate([v, jnp.zeros(lead + (1,), v.dtype)], axis=-1)
    skew = jnp.tile(vp, (1,) * len(lead) + (n_rows,))[..., :n_rows * length].reshape(lead + (n_rows, length))
    return skew[..., n_rows - 1:n_rows - 1 + n_cols]


def kernel(x_prompt, x_sample, c_prompt, c_sample, cache_k, cache_v, page_table, w_ada, b_ada, w_in, w_o,
           lam_q1, lam_k1, lam_q2, lam_k2, g_subln, rel_bias, g_sg_ln, b_sg_ln, w_s, b_s, w_rg, b_rg,
           w_re, b_re, w_gate, w_up, w_down, g_final):
    batch, seq, _ = x_prompt.shape
    dec_b, dec_t, _ = x_sample.shape
    n_pages = page_table.shape[1]
    n_p = batch * seq
    n_s = dec_b * dec_t
    n_tot = n_p + n_s
    assert w_in.shape[0] == 1 and cache_k.shape[1] == 1 and seq % TQ_ATT == 0 and n_pages % PAGES_PER_GROUP == 0
    assert n_p % TM_TOK == 0 and n_p % n_s == 0 and n_tot % TM_E == 0 and dec_t == SUBLANES
    assert TAIL_TOKENS >= MAX_DISTANCE + dec_t and TAIL_TOKENS <= PAGES_PER_GROUP * PAGE
    assert n_s == TM_D and n_p % TM_D == 0 and N_CLASSES <= LANES

    w_in_b = w_in[0].astype(BF16)
    w_o_b = w_o[0].astype(BF16)
    wr_t = jnp.zeros((32, D_MODEL), F32).at[0:N_EG].set(w_rg[0].T).at[N_EG:N_EG + N_EG * EPG].set(w_re[0].T)
    wr_t = wr_t.astype(BF16)
    br = jnp.zeros((32, 1), F32).at[0:N_EG, 0].set(b_rg[0]).at[N_EG:N_EG + N_EG * EPG, 0].set(b_re[0])
    wg_b, wu_b, wd_b = w_gate[0], w_up[0], w_down[0]
    ws_tril = jnp.tril(w_s[0])
    ws_p = ws_tril.astype(BF16)
    bs_p = b_s[0][:, :, None]
    same_seq = np.kron(np.eye(dec_b, dtype=np.float32), np.ones((dec_t, dec_t), np.float32))
    rep = np.tile(np.eye(dec_t, dtype=np.float32), (dec_b, 1))
    ws_rep = jnp.einsum('ri,gij,cj->grc', rep, ws_tril[:, :dec_t, :dec_t], rep,
                        precision=lax.Precision.HIGHEST)
    ws_s = (ws_rep * same_seq).astype(BF16)
    bs_s = jnp.tile(b_s[0][:, :dec_t], (1, dec_b))[:, :, None]
    gln = g_sg_ln[0]
    bln = b_sg_ln[0]
    lam = (jnp.exp(jnp.sum(lam_q1[0] * lam_k1[0])) - jnp.exp(jnp.sum(lam_q2[0] * lam_k2[0]))
           + LAM_INIT).reshape(1).astype(F32)

    blk = BIAS_BLOCK
    n_dist = max(2 * blk, TAIL_TOKENS + dec_t)
    onehot = np.eye(N_BUCKETS, dtype=np.float32)[_bucket_table(n_dist)]
    ft = jnp.dot(onehot, rel_bias - rel_bias[N_BUCKETS - 1], precision=lax.Precision.HIGHEST).T
    neg = lambda n: jnp.full((N_HEADS, n), NEG, F32)
    bias_near = _toeplitz(jnp.concatenate([neg(blk - 1), ft[:, 0:blk]], axis=1), blk, blk)
    bias_next = _toeplitz(ft[:, 1:2 * blk], blk, blk)
    bias_t = jnp.stack([bias_next, bias_near], axis=1) * LOG2E
    bl = _toeplitz(jnp.flip(ft[:, 1:TAIL_TOKENS + dec_t], axis=1), dec_t, TAIL_TOKENS)
    head_eq = jnp.eye(N_HEADS, dtype=F32)
    bias_last = (bl[:, None, :, :, None] * head_eq[:, None, None, None, :])
    bias_last = jnp.broadcast_to(bias_last, (N_HEADS, 2, dec_t, TAIL_TOKENS, N_HEADS)).reshape(
        2 * N_HEADS * dec_t, TAIL_TOKENS * N_HEADS)
    bn = _toeplitz(jnp.concatenate([jnp.flip(ft[:, 0:dec_t], axis=1), neg(dec_t - 1)], axis=1),
                   dec_t, dec_t)
    bn = jnp.where(head_eq[:, None, None, :] > 0, bn[:, :, :, None], NEG)
    bn = jnp.broadcast_to(bn[:, None], (N_HEADS, 2, dec_t, dec_t, N_HEADS)).reshape(
        2 * N_HEADS * dec_t, dec_t * N_HEADS)
    bias_new = jnp.concatenate([bn, jnp.full((bn.shape[0], PAGE - bn.shape[1]), NEG, F32)], axis=1)

    c_all = jnp.concatenate([c_prompt, c_sample, jnp.zeros((4, D_MODEL), F32)], axis=0)
    mod_all = _ada(c_all, w_ada[0], b_ada[0])
    mod_p = mod_all[:batch].reshape(batch, 1, 6 * D_MODEL)
    mod_s = jnp.repeat(mod_all[batch:batch + dec_b], dec_t, axis=0).reshape(1, n_s, 6 * D_MODEL)

    xp = x_prompt.reshape(n_p, D_MODEL)
    xs = x_sample.reshape(n_s, D_MODEL)

    q_p, kf_p, kb_p, vf_p, vt_p, sg_p = _stage_a(xp, mod_p, w_in_b, ws_p, bs_p, gln, bln,
                                                 tm=TM_TOK, chunk=CHUNK, sample=False)
    q_s, kf_s, vf_s, sg_s, vsn_s = _stage_a(xs, mod_s, w_in_b, ws_s, bs_s, gln, bln,
                                            tm=n_s, chunk=n_s, sample=True)

    g_col = g_subln[0].reshape(HEAD_W, 1)
    g_row = g_subln[0].reshape(1, HEAD_W)
    pad = ((0, 0), (0, PAGE - dec_t * N_HEADS), (0, 0))
    knew = jnp.pad(kf_s.reshape(dec_b, dec_t * N_HEADS, HEAD_W), pad).astype(BF16)
    vnew = jnp.pad(vf_s.reshape(dec_b, dec_t * N_HEADS, HEAD_W), pad).astype(BF16)
    ck = cache_k.reshape(-1, HEAD_W)
    cv = cache_v.reshape(-1, HEAD_W)
    o_p, o_s = _attn_fused(page_table.reshape(-1), lam, q_p, kb_p, vt_p, bias_t, g_col,
                           q_s, knew, vnew, bias_last, bias_new, g_row, ck, cv,
                           batch=batch, seq=seq, tq=TQ_ATT, tk=TM_TOK, dec_b=dec_b, n_pages=n_pages)

    x1_p, h2_p, cls_p, info_p = _stage_c(o_p, sg_p, xp, mod_p, w_o_b, wr_t, br, tm=TM_TOK, sample=False)
    x1_s, h2_s, cls_s, info_s = _stage_c(o_s, sg_s, xs, mod_s, w_o_b, wr_t, br, tm=n_s, sample=True)

    tm_e = TM_E
    tm_d = TM_D
    n_max = n_tot // tm_e + N_CLASSES
    n_dt = n_tot // tm_d
    cls = jnp.concatenate([cls_p[0], cls_s[0]]).reshape(n_dt, tm_d)
    classes = jnp.arange(N_CLASSES, dtype=jnp.int32)
    seg_len = jnp.sum((cls[:, :, None] == classes).astype(jnp.int32), axis=1)
    counts = jnp.sum(seg_len, axis=0)
    ntile_c = (counts + tm_e - 1) // tm_e
    tile_end = jnp.cumsum(ntile_c)
    class_base = (tile_end - ntile_c) * tm_e
    nact = tile_end[-1]
    seg_off = class_base[None, :] + jnp.cumsum(seg_len, axis=0) - seg_len
    pad_off = class_base + counts
    pad_len = ntile_c * tm_e - counts
    tile_ids = jnp.arange(n_max, dtype=jnp.int32)
    tile_cls = jnp.sum((tile_ids[:, None] >= tile_end[None, :]).astype(jnp.int32), axis=1)
    last_cls = jnp.sum((nact - 1 >= tile_end).astype(jnp.int32))
    tile_cls = jnp.where(tile_ids < nact, tile_cls, last_cls)
    grp = tile_cls // N_PAIRS
    pidx = tile_cls % N_PAIRS
    pick = lambda table: sum(jnp.where(pidx == p, e, 0) for p, e in enumerate(table))
    tile_ea = (grp * EPG + pick(PAIR_A)).astype(jnp.int32)
    tile_eb = (grp * EPG + pick(PAIR_B)).astype(jnp.int32)
    empty = jnp.zeros((N_SUB - 1, N_CLASSES), jnp.int32)
    seg_off = jnp.concatenate([seg_off.astype(jnp.int32), empty]).reshape(-1)
    seg_len = jnp.concatenate([seg_len, empty]).reshape(-1)

    nact1 = nact.reshape(1).astype(jnp.int32)
    zero_rows = (jnp.sum(pad_len) + (n_max - nact) * tm_e).reshape(1).astype(jnp.int32)
    x_sorted = _dispatch(seg_off, seg_len, pad_off.astype(jnp.int32), pad_len.astype(jnp.int32), zero_rows,
                         nact1, h2_p, h2_s, info_p, info_s, tm=tm_d, n_sub=N_SUB, n_slots=n_max * tm_e)
    ya_sorted, yb_sorted = _moe(tile_ea, tile_eb, nact1, x_sorted, wg_b, wu_b, wd_b, tm=tm_e, n_max=n_max)

    y_p = _final(seg_off, seg_len, x1_p, info_p, mod_p, g_final, ya_sorted, yb_sorted,
                 tm=tm_d, n_sub=N_SUB, sample=False, tile_base=0)
    y_s = _final(seg_off, seg_len, x1_s, info_s, mod_s, g_final, ya_sorted, yb_sorted,
                 tm=tm_d, n_sub=1, sample=True, tile_base=n_p // tm_d)

    return (y_p.reshape(batch, seq, D_MODEL),
            y_s.reshape(dec_b, dec_t, D_MODEL),
            kf_p.reshape(batch, 1, seq, N_HEADS, HEAD_W),
            vf_p.reshape(batch, 1, seq, N_HEADS, HEAD_W),
            kf_s.reshape(dec_b, 1, dec_t, N_HEADS, HEAD_W),
            vf_s.reshape(dec_b, 1, dec_t, N_HEADS, HEAD_W),
            vsn_s.reshape(dec_b, 1, dec_t, N_GROUPS_SG, SG_CH))
```

```python
import functools
import math

import numpy as np
import jax
import jax.numpy as jnp
from jax import lax
from jax.experimental import pallas as pl
from jax.experimental.pallas import tpu as pltpu

F32 = jnp.float32
BF16 = jnp.bfloat16

D_MODEL = 1024
N_HEADS = 4
DK = 64
HEAD_W = 128
QK_W = N_HEADS * HEAD_W
N_GROUPS_SG = 4
SG_CH = 128
CHUNK = 128
PAGE = 128
N_BUCKETS = 32
MAX_DISTANCE = 128
N_EG = 4
EPG = 4
N_PAIRS = 6
N_CLASSES = N_EG * N_PAIRS
D_EXPERT = 512
EPS = 1e-6
LAM_INIT = 0.8 - 0.6 * math.exp(-0.3 * 0)
NEG = -1e30
LOG2E = math.log2(math.e)
LANES = 128
SUBLANES = 8

TM_TOK = 512
TM_C = 1024
STAGE_A_PARTS = 4
TQ_ATT = 2048
PAGES_PER_GROUP = 8
RING = 4
TM_E = 256
MOE_PARTS = 1
TM_D = 256
N_SUB = 4
VMEM_LIMIT = 56 * 1024 * 1024


def _cparams(sem):
    return pltpu.CompilerParams(dimension_semantics=sem, vmem_limit_bytes=VMEM_LIMIT)


def _ada_body(c_ref, w_ref, b_ref, o_ref):
    c = c_ref[...]
    a = (c * jax.nn.sigmoid(c)).astype(BF16)
    o_ref[...] = jnp.dot(a, w_ref[...].astype(BF16), preferred_element_type=F32) + b_ref[...]


def _ada(c_all, w_ada, b_ada):
    m = c_all.shape[0]
    n = w_ada.shape[1]
    tn = 1536
    return pl.pallas_call(
        _ada_body,
        grid=(n // tn,),
        in_specs=[pl.BlockSpec((m, D_MODEL), lambda j: (0, 0)),
                  pl.BlockSpec((D_MODEL, tn), lambda j: (0, j)),
                  pl.BlockSpec((1, tn), lambda j: (0, j))],
        out_specs=pl.BlockSpec((m, tn), lambda j: (0, j)),
        out_shape=jax.ShapeDtypeStruct((m, n), F32),
        compiler_params=_cparams(("arbitrary",)),
        name="adaln",
    )(c_all, w_ada, b_ada.reshape(1, n))


def _stage_a_body(x_ref, mod_ref, w_in_ref, ws_ref, bs_ref, gln_ref, bln_ref,
                  *out_refs, chunk, sample):
    if sample:
        q_ref, kf_ref, vf_ref, sg_ref, vsn_ref = out_refs
    else:
        q_ref, kf_ref, kb_ref, vf_ref, vt_ref, sg_ref = out_refs
    x = x_ref[...]
    tm = x.shape[0]
    mod = mod_ref[0]
    sh1 = mod[:, 0:D_MODEL]
    sc1 = mod[:, D_MODEL:2 * D_MODEL]
    ms = jnp.mean(x * x, axis=-1, keepdims=True)
    h = (x * lax.rsqrt(ms + EPS)) * (1.0 + sc1) + sh1
    hb = h.astype(BF16)
    n_parts = 1 if sample else STAGE_A_PARTS
    rows_p = tm // n_parts
    zs = [jnp.dot(hb[p * rows_p:(p + 1) * rows_p], w_in_ref[...], preferred_element_type=F32)
          for p in range(n_parts)]
    for p, z in enumerate(zs):
        a = p * rows_p
        q = z[:, 0:QK_W] * (DK ** -0.5)
        k = z[:, QK_W:2 * QK_W]
        v = z[:, 2 * QK_W:3 * QK_W]
        for hd in range(N_HEADS):
            kf_ref[pl.ds(a * N_HEADS + hd, rows_p, stride=N_HEADS), :] = k[:, hd * HEAD_W:(hd + 1) * HEAD_W]
            vf_ref[pl.ds(a * N_HEADS + hd, rows_p, stride=N_HEADS), :] = v[:, hd * HEAD_W:(hd + 1) * HEAD_W]
        if sample:
            q_ref[...] = q
        else:
            q_ref[a:a + rows_p, :] = (q * LOG2E).astype(BF16)
            kb_ref[a:a + rows_p, :] = k.astype(BF16)
            vt_ref[0, :, a:a + rows_p] = v.T.astype(BF16)
        u = z[:, 3 * QK_W:4 * QK_W]
        vs = z[:, 4 * QK_W:5 * QK_W]
        for g in range(N_GROUPS_SG):
            lo, hi = g * SG_CH, (g + 1) * SG_CH
            vg = vs[:, lo:hi]
            mu = jnp.mean(vg, axis=-1, keepdims=True)
            dv = vg - mu
            var = jnp.mean(dv * dv, axis=-1, keepdims=True)
            vn = (dv * lax.rsqrt(var + EPS)) * gln_ref[g:g + 1, :] + bln_ref[g:g + 1, :]
            if sample:
                vsn_ref[:, lo:hi] = vn
            vnb = vn.astype(BF16)
            for c in range(rows_p // chunk):
                r0, r1 = c * chunk, (c + 1) * chunk
                s = jnp.dot(ws_ref[g], vnb[r0:r1], preferred_element_type=F32) + bs_ref[g]
                sg_ref[a + r0:a + r1, lo:hi] = (u[r0:r1, lo:hi] * s).astype(BF16)


def _stage_a(x, mod, w_in_b, ws, bs, gln, bln, *, tm, chunk, sample):
    n = x.shape[0]
    nt = n // tm
    mrows = mod.shape[1]
    row = lambda w, dt: jax.ShapeDtypeStruct((n, w), dt)
    blk = lambda w: pl.BlockSpec((tm, w), lambda i: (i, 0))
    cache_shape = jax.ShapeDtypeStruct((n * N_HEADS, HEAD_W), F32)
    cache_blk = pl.BlockSpec((tm * N_HEADS, HEAD_W), lambda i: (i, 0))
    if sample:
        out_shape = [row(QK_W, F32), cache_shape, cache_shape, row(QK_W, BF16), row(QK_W, F32)]
        out_specs = [blk(QK_W), cache_blk, cache_blk, blk(QK_W), blk(QK_W)]
        mod_map = lambda i: (0, 0, 0)
    else:
        out_shape = [row(QK_W, BF16), cache_shape, row(QK_W, BF16), cache_shape,
                     jax.ShapeDtypeStruct((nt, QK_W, tm), BF16), row(QK_W, BF16)]
        out_specs = [blk(QK_W), cache_blk, blk(QK_W), cache_blk,
                     pl.BlockSpec((1, QK_W, tm), lambda i: (i, 0, 0)), blk(QK_W)]
        tiles_per_batch = 4096 // tm
        mod_map = lambda i: (i // tiles_per_batch, 0, 0)
    full = lambda a: pl.BlockSpec(a.shape, lambda i: (0,) * a.ndim)
    return pl.pallas_call(
        functools.partial(_stage_a_body, chunk=chunk, sample=sample),
        grid=(nt,),
        in_specs=[blk(D_MODEL),
                  pl.BlockSpec((1, mrows, 6 * D_MODEL), mod_map),
                  full(w_in_b), full(ws), full(bs), full(gln), full(bln)],
        out_specs=out_specs,
        out_shape=out_shape,
        compiler_params=_cparams(("arbitrary",)),
        name="stage_a_sample" if sample else "stage_a_prompt",
    )(x, mod, w_in_b, ws, bs, gln, bln)


ATT_COLS = 256
BIAS_BLOCK = MAX_DISTANCE


def _attn_body(lam_ref, q_ref, k_ref, vt_ref, bias_ref, g_ref, o_ref, *scratch, tq, tk, between):
    n_chain = 2 * tq // ATT_COLS
    q2_refs, m_refs, l_refs, acc_refs = (scratch[i * n_chain:(i + 1) * n_chain] for i in range(4))
    qi = pl.program_id(2)
    between(0)
    for c in range(n_chain):
        q0 = (c * ATT_COLS) % tq
        q = q_ref[q0:q0 + ATT_COLS, :]
        lane = lax.broadcasted_iota(jnp.int32, q.shape, 1)
        keep = (lane < DK) if c < n_chain // 2 else (lane >= DK)
        q2_refs[c][...] = jnp.where(keep, q, jnp.zeros_like(q))
        m_refs[c][...] = jnp.full(m_refs[c].shape, NEG, F32)
        l_refs[c][...] = jnp.zeros(l_refs[c].shape, F32)
        acc_refs[c][...] = jnp.zeros(acc_refs[c].shape, F32)

    blk = BIAS_BLOCK

    kblocks = tk // blk

    def block_kinds(rel, q0):
        return [[(q0 // blk + b) - (rel + a) for b in range(ATT_COLS // blk)] for a in range(kblocks)]

    def keys_needed(rel, q0):
        if rel is None:
            return kblocks
        return sum(1 for row in block_kinds(rel, q0) if max(row) >= 0)

    def with_bias(s, rel, q0):
        if rel is None:
            return s
        kinds = block_kinds(rel, q0)[:s.shape[0] // blk]
        if all(d >= 2 for row in kinds for d in row):
            return s
        nxt, near = bias_ref[0, 0], bias_ref[0, 1]
        pick = lambda d: (jnp.full((blk, blk), NEG, F32) if d < 0 else near if d == 0 else nxt if d == 1
                          else jnp.zeros((blk, blk), F32))
        rows = []
        for a, row in enumerate(kinds):
            s_row = s[a * blk:(a + 1) * blk]
            if any(d < 2 for d in row):
                s_row = s_row + jnp.concatenate([pick(d) for d in row], axis=1)
            rows.append(s_row)
        return jnp.concatenate(rows, axis=0)

    def score(j, rel, c):
        nk = keys_needed(rel, (c * ATT_COLS) % tq) * blk
        if nk == 0:
            return None
        k = k_ref[pl.ds(pl.multiple_of(j * tk, tk), nk), :]
        s = lax.dot_general(k, q2_refs[c][...], (((1,), (1,)), ((), ())),
                            preferred_element_type=F32)
        return with_bias(s, rel, (c * ATT_COLS) % tq)

    def accumulate(j, c, s):
        vt = vt_ref[j, :, 0:s.shape[0]]
        m_old = m_refs[c][...]
        m_new = jnp.maximum(m_old, jnp.max(s, axis=0, keepdims=True))
        alpha = jnp.exp2(m_old - m_new)
        p = jnp.exp2(s - m_new)
        l_refs[c][...] = alpha * l_refs[c][...] + jnp.sum(p, axis=0, keepdims=True)
        acc_refs[c][...] = alpha * acc_refs[c][...] + jnp.dot(vt, p.astype(BF16), preferred_element_type=F32)
        m_refs[c][...] = m_new

    def tiles(*work):
        scores = [[score(j, rel, c) for c in range(n_chain)] for j, rel in work]
        for (j, _), tile_scores in zip(work, scores):
            for c, s in enumerate(tile_scores):
                if s is not None:
                    accumulate(j, c, s)

    ratio = tq // tk
    first_diag = qi * ratio
    n_plain = jnp.maximum(first_diag - 1, 0)

    def plain_pair(jj, carry):
        tiles((2 * jj, None), (2 * jj + 1, None))
        return carry

    diag = [(first_diag + r, r * kblocks) for r in range(ratio)]
    assert ratio % 2 == 0 and ratio >= 4
    half = ratio // 2

    @pl.when(qi == 0)
    def _():
        tiles(*diag[:half])

    @pl.when(qi >= 1)
    def _():
        lax.fori_loop(0, n_plain // 2, plain_pair, 0)

    between(1)

    @pl.when(qi == 0)
    def _():
        tiles(*diag[half:])

    @pl.when(qi >= 1)
    def _():
        tiles((first_diag - 2, None), (first_diag - 1, -kblocks))

    between(2)

    @pl.when(qi >= 1)
    def _():
        tiles(*diag)

    between(3)

    lam = lam_ref[0]
    o_all = jnp.concatenate([acc_refs[c][...] * (1.0 / l_refs[c][...]) for c in range(n_chain)],
                            axis=1)
    o = o_all[:, 0:tq] - lam * o_all[:, tq:2 * tq]
    ms = jnp.mean(o * o, axis=0, keepdims=True)
    on = (o * lax.rsqrt(ms + EPS)) * g_ref[...] * (1.0 - LAM_INIT)
    o_ref[...] = on.T.astype(BF16)


PAGE_ROWS = PAGE * N_HEADS
TAIL_TOKENS = 2 * PAGE


def _attn_fused_body(pt_ref, lam_ref, q_ref, k_ref, vt_ref, bias_ref, gcol_ref,
                     qs_ref, knew_ref, vnew_ref, bl_ref, bn_ref, grow_ref, ck_ref, cv_ref,
                     o_ref, os_ref, *scratch, tq, tk, pages, n_sub, n_steps):
    n_prompt_scratch = 4 * (2 * tq // ATT_COLS)
    kbuf, vbuf, sem, mask_ref, qm_ref, m_ref, l_ref, acc_ref = scratch[n_prompt_scratch:]
    step = (pl.program_id(0) * pl.num_programs(1) + pl.program_id(1)) * pl.num_programs(2) + pl.program_id(2)

    def page_copies(group, sl):
        base = group * pages
        out = []
        for i in range(pages):
            src = pl.ds(pl.multiple_of(pt_ref[base + i] * PAGE_ROWS, PAGE_ROWS), PAGE_ROWS)
            dst = pl.ds(i * PAGE_ROWS, PAGE_ROWS)
            out.append(pltpu.make_async_copy(ck_ref.at[src], kbuf.at[sl, dst], sem.at[sl, 0]))
            out.append(pltpu.make_async_copy(cv_ref.at[src], vbuf.at[sl, dst], sem.at[sl, 1]))
        return out

    def start_all(copies):
        for c in copies:
            c.start()

    def update(kb, vb, bias):
        sc = lax.dot_general(qm_ref[...], kb, (((1,), (1,)), ((), ())),
                             preferred_element_type=F32) + bias
        m_old = m_ref[...]
        m_new = jnp.maximum(m_old, jnp.max(sc, axis=1, keepdims=True))
        alpha = jnp.exp(m_old - m_new)
        p = jnp.exp(sc - m_new)
        l_ref[...] = alpha * l_ref[...] + jnp.sum(p, axis=1, keepdims=True)
        acc_ref[...] = alpha * acc_ref[...] + jnp.dot(p.astype(BF16), vb, preferred_element_type=F32)
        m_ref[...] = m_new

    def sample_group(u):
        group = step * n_sub + u
        slot = u % RING
        ahead = RING - 1
        if u == 0:
            @pl.when(step == 0)
            def _():
                for g0 in range(ahead):
                    start_all(page_copies(g0, g0))
                row = lax.broadcasted_iota(jnp.int32, mask_ref.shape, 0)
                col = lax.broadcasted_iota(jnp.int32, mask_ref.shape, 1)
                same_head = (col % N_HEADS) == (row // (2 * SUBLANES))
                mask_ref[...] = jnp.where(same_head, 0.0, NEG)
        if u + ahead < n_sub:
            start_all(page_copies(group + ahead, (u + ahead) % RING))
        else:
            @pl.when(step + 1 < n_steps)
            def _():
                start_all(page_copies(group + ahead, (u + ahead) % RING))
        for c in page_copies(group, slot):
            c.wait()
        if u == 0:
            m_ref[...] = jnp.full(m_ref.shape, NEG, F32)
            l_ref[...] = jnp.zeros(l_ref.shape, F32)
            acc_ref[...] = jnp.zeros(acc_ref.shape, F32)
            q = qs_ref[...]
            lane = lax.broadcasted_iota(jnp.int32, (SUBLANES, HEAD_W), 1)
            pieces = []
            for h in range(N_HEADS):
                qh = q[:, h * HEAD_W:(h + 1) * HEAD_W]
                pieces += [jnp.where(lane < DK, qh, 0.0), jnp.where(lane >= DK, qh, 0.0)]
            qm_ref[...] = jnp.concatenate(pieces, axis=0).astype(BF16)
        bias = mask_ref[...]
        if u == n_sub - 1:
            head_cols = mask_ref.shape[1] - bl_ref.shape[1]
            bias = jnp.concatenate([bias[:, :head_cols], bias[:, head_cols:] + bl_ref[...]], axis=1)
        update(kbuf[slot].astype(BF16), vbuf[slot].astype(BF16), bias)
        if u == n_sub - 1:
            update(knew_ref[0], vnew_ref[0], bn_ref[...])
            lam = lam_ref[0]
            o_all = acc_ref[...] * (1.0 / l_ref[...])
            for h in range(N_HEADS):
                r = h * 2 * SUBLANES
                o = o_all[r:r + SUBLANES] - lam * o_all[r + SUBLANES:r + 2 * SUBLANES]
                ms = jnp.mean(o * o, axis=-1, keepdims=True)
                os_ref[:, h * HEAD_W:(h + 1) * HEAD_W] = ((o * lax.rsqrt(ms + EPS)) * grow_ref[...]
                                                          * (1.0 - LAM_INIT))

    per_call = n_sub // 4

    def between(k):
        for u in range(per_call * k, per_call * (k + 1)):
            sample_group(u)

    _attn_body(lam_ref, q_ref, k_ref, vt_ref, bias_ref, gcol_ref, o_ref, *scratch[:n_prompt_scratch],
               tq=tq, tk=tk, between=between)


def _attn_fused(page_table_flat, lam, q, kb, vt, bias_t, g_col, q_s, knew, vnew, bias_last, bias_new, g_row,
                cache_k, cache_v, *, batch, seq, tq, tk, dec_b, n_pages):
    nq = seq // tq
    nk = seq // tk
    n = batch * seq
    n_chain = 2 * tq // ATT_COLS
    pages = PAGES_PER_GROUP
    n_sub = n_pages // pages
    n_steps = batch * N_HEADS * nq
    dec_t = q_s.shape[0] // dec_b
    n_rows = 2 * N_HEADS * dec_t
    step_rows = pages * PAGE_ROWS
    assert n_steps == dec_b and n_sub % 4 == 0 and n_sub % RING == 0 and nq == 2
    assert bias_last.shape[1] <= step_rows
    assert vt.shape == (batch * nk, QK_W, tk) and tq % tk == 0 and tk % BIAS_BLOCK == 0
    lin = lambda b, h, i: (b * N_HEADS + h) * nq + i
    grid_spec = pltpu.PrefetchScalarGridSpec(
        num_scalar_prefetch=1,
        grid=(batch, N_HEADS, nq),
        in_specs=[pl.BlockSpec(memory_space=pltpu.SMEM),
                  pl.BlockSpec((tq, HEAD_W), lambda b, h, i, pt: (b * nq + i, h)),
                  pl.BlockSpec((seq, HEAD_W), lambda b, h, i, pt: (b, h)),
                  pl.BlockSpec((nk, HEAD_W, tk), lambda b, h, i, pt: (b, h, 0)),
                  pl.BlockSpec((1, 2, BIAS_BLOCK, BIAS_BLOCK), lambda b, h, i, pt: (h, 0, 0, 0)),
                  pl.BlockSpec((HEAD_W, 1), lambda b, h, i, pt: (0, 0)),
                  pl.BlockSpec((dec_t, QK_W), lambda b, h, i, pt: (lin(b, h, i), 0)),
                  pl.BlockSpec((1, PAGE, HEAD_W), lambda b, h, i, pt: (lin(b, h, i), 0, 0)),
                  pl.BlockSpec((1, PAGE, HEAD_W), lambda b, h, i, pt: (lin(b, h, i), 0, 0)),
                  pl.BlockSpec(bias_last.shape, lambda b, h, i, pt: (0, 0)),
                  pl.BlockSpec(bias_new.shape, lambda b, h, i, pt: (0, 0)),
                  pl.BlockSpec((1, HEAD_W), lambda b, h, i, pt: (0, 0)),
                  pl.BlockSpec(memory_space=pl.ANY),
                  pl.BlockSpec(memory_space=pl.ANY)],
        out_specs=[pl.BlockSpec((tq, HEAD_W), lambda b, h, i, pt: (b * nq + i, h)),
                   pl.BlockSpec((dec_t, QK_W), lambda b, h, i, pt: (lin(b, h, i), 0))],
        scratch_shapes=([pltpu.VMEM((ATT_COLS, HEAD_W), BF16)] * n_chain
                        + [pltpu.VMEM((1, ATT_COLS), F32)] * (2 * n_chain)
                        + [pltpu.VMEM((HEAD_W, ATT_COLS), F32)] * n_chain
                        + [pltpu.VMEM((RING, step_rows, HEAD_W), F32),
                           pltpu.VMEM((RING, step_rows, HEAD_W), F32),
                           pltpu.SemaphoreType.DMA((RING, 2)),
                           pltpu.VMEM((n_rows, step_rows), F32),
                           pltpu.VMEM((n_rows, HEAD_W), BF16),
                           pltpu.VMEM((n_rows, 1), F32), pltpu.VMEM((n_rows, 1), F32),
                           pltpu.VMEM((n_rows, HEAD_W), F32)]))
    return pl.pallas_call(
        functools.partial(_attn_fused_body, tq=tq, tk=tk, pages=pages, n_sub=n_sub, n_steps=n_steps),
        grid_spec=grid_spec,
        out_shape=[jax.ShapeDtypeStruct((n, QK_W), BF16), jax.ShapeDtypeStruct(q_s.shape, F32)],
        compiler_params=_cparams(("arbitrary", "arbitrary", "arbitrary")),
        name="attn_fused",
    )(page_table_flat, lam, q, kb, vt, bias_t, g_col, q_s, knew, vnew, bias_last, bias_new, g_row,
      cache_k, cache_v)


INFO_W_A, INFO_W_B, INFO_CLS = 0, 1, 2
PAIR_A = (0, 0, 0, 1, 1, 3)
PAIR_B = (1, 2, 3, 3, 2, 2)


def _stage_c_body(o_ref, sg_ref, x_ref, mod_ref, wo_ref, wr_ref, br_ref, x1_ref, h2_ref, cls_ref, info_ref):
    x = x_ref[...]
    tm = x.shape[0]
    mod = mod_ref[0]
    g1 = mod[:, 2 * D_MODEL:3 * D_MODEL]
    sh2 = mod[:, 3 * D_MODEL:4 * D_MODEL]
    sc2 = mod[:, 4 * D_MODEL:5 * D_MODEL]
    mix = (jnp.dot(o_ref[...].astype(BF16), wo_ref[0:QK_W, :], preferred_element_type=F32)
           + jnp.dot(sg_ref[...], wo_ref[QK_W:2 * QK_W, :], preferred_element_type=F32))
    x1 = x + g1 * mix
    x1_ref[...] = x1
    ms = jnp.mean(x1 * x1, axis=-1, keepdims=True)
    h2 = ((x1 * lax.rsqrt(ms + EPS)) * (1.0 + sc2) + sh2).astype(BF16)
    h2_ref[...] = h2
    lg = lax.dot_general(wr_ref[...], h2, (((1,), (1,)), ((), ())),
                         preferred_element_type=F32) + br_ref[...]
    gl = [lg[i:i + 1, :] for i in range(N_EG)]
    el = [lg[N_EG + i:N_EG + i + 1, :] for i in range(N_EG * EPG)]
    gmax = jnp.maximum(jnp.maximum(gl[0], gl[1]), jnp.maximum(gl[2], gl[3]))
    gi = jnp.where(gl[0] == gmax, 0, jnp.where(gl[1] == gmax, 1, jnp.where(gl[2] == gmax, 2, 3)))
    gsum = (jnp.exp(gl[0] - gmax) + jnp.exp(gl[1] - gmax)
            + jnp.exp(gl[2] - gmax) + jnp.exp(gl[3] - gmax))
    gp = 1.0 / gsum
    sel = [jnp.where(gi == 0, el[j], jnp.where(gi == 1, el[EPG + j],
                                               jnp.where(gi == 2, el[2 * EPG + j], el[3 * EPG + j])))
           for j in range(EPG)]
    v0 = jnp.maximum(jnp.maximum(sel[0], sel[1]), jnp.maximum(sel[2], sel[3]))
    i0 = jnp.where(sel[0] == v0, 0, jnp.where(sel[1] == v0, 1, jnp.where(sel[2] == v0, 2, 3)))
    rest = [jnp.where(i0 == j, -3e38, sel[j]) for j in range(EPG)]
    v1 = jnp.maximum(jnp.maximum(rest[0], rest[1]), jnp.maximum(rest[2], rest[3]))
    i1 = jnp.where(rest[0] == v1, 0, jnp.where(rest[1] == v1, 1, jnp.where(rest[2] == v1, 2, 3)))
    e1 = jnp.exp(v1 - v0)
    den = 1.0 / (1.0 + e1)
    tw0 = den * gp
    tw1 = e1 * den * gp
    first_low = i0 < i1
    lo = jnp.where(first_low, i0, i1)
    hi = jnp.where(first_low, i1, i0)
    w_lo = jnp.where(first_low, tw0, tw1)
    w_hi = jnp.where(first_low, tw1, tw0)
    pair = jnp.where(lo == 0, hi - 1, jnp.where(lo == 2, 5, jnp.where(hi == 3, 3, 4)))
    swapped = pair == 5
    w_a = jnp.where(swapped, w_hi, w_lo)
    w_b = jnp.where(swapped, w_lo, w_hi)
    cls = gi * N_PAIRS + pair
    cls_ref[...] = jnp.broadcast_to(cls, cls_ref.shape).astype(jnp.int32)
    row = lax.broadcasted_iota(jnp.int32, (LANES, tm), 0)
    rec = jnp.where(row == INFO_W_A, w_a, jnp.where(row == INFO_W_B, w_b,
                                                    jnp.where(row == INFO_CLS, cls.astype(F32), 0.0)))
    info_ref[...] = rec.T


def _stage_c(o, sg, x, mod, wo_b, wr_t, br, *, tm, sample):
    n = x.shape[0]
    nt = n // tm
    mrows = mod.shape[1]
    blk = lambda w: pl.BlockSpec((tm, w), lambda i: (i, 0))
    full = lambda a: pl.BlockSpec(a.shape, lambda i: (0,) * a.ndim)
    if sample:
        mod_map = lambda i: (0, 0, 0)
    else:
        tiles_per_batch = 4096 // tm
        mod_map = lambda i: (i // tiles_per_batch, 0, 0)
    return pl.pallas_call(
        _stage_c_body,
        grid=(nt,),
        in_specs=[blk(QK_W), blk(QK_W), blk(D_MODEL),
                  pl.BlockSpec((1, mrows, 6 * D_MODEL), mod_map),
                  full(wo_b), full(wr_t), full(br)],
        out_specs=[blk(D_MODEL), blk(D_MODEL),
                   pl.BlockSpec((SUBLANES, tm), lambda i: (0, i)),
                   blk(LANES)],
        out_shape=[jax.ShapeDtypeStruct((n, D_MODEL), F32),
                   jax.ShapeDtypeStruct((n, D_MODEL), BF16),
                   jax.ShapeDtypeStruct((SUBLANES, n), jnp.int32),
                   jax.ShapeDtypeStruct((n, LANES), F32)],
        compiler_params=_cparams(("arbitrary",)),
        name="stage_c_sample" if sample else "stage_c_prompt",
    )(o, sg, x, mod, wo_b, wr_t, br)


def rows8(ref, start, count):
    scale = lambda v: v * SUBLANES if isinstance(v, int) else pl.multiple_of(v * SUBLANES, SUBLANES)
    return ref.at[pl.ds(scale(start), scale(count))]


def _perm_t(cls_col):
    n = cls_col.shape[0]
    lane = lax.broadcasted_iota(jnp.int32, (n, LANES), 1).astype(F32)
    onehot = (lane == cls_col).astype(BF16)
    r = lax.broadcasted_iota(jnp.int32, (n, n), 0)
    c = lax.broadcasted_iota(jnp.int32, (n, n), 1)
    before = (c < r).astype(BF16)
    rank = jnp.dot(before, onehot, preferred_element_type=F32)
    cnt = jnp.sum(onehot.astype(F32), axis=0, keepdims=True)
    cr = lax.broadcasted_iota(jnp.int32, (LANES, LANES), 0)
    cc = lax.broadcasted_iota(jnp.int32, (LANES, LANES), 1)
    lower_cls = (cr < cc).astype(BF16)
    base = jnp.dot(jnp.broadcast_to(cnt, (SUBLANES, LANES)).astype(BF16), lower_cls,
                   preferred_element_type=F32)[0:1, :]
    pos = jnp.sum(onehot.astype(F32) * (base + rank), axis=1, keepdims=True)
    dest = lax.broadcasted_iota(jnp.int32, (n, n), 1).astype(F32)
    return (dest == pos).astype(F32)


def _dispatch_body(soff_ref, slen_ref, poff_ref, plen_ref, ptot_ref, nact_ref, hp_ref, hs_ref, ip_ref, is_ref,
                   xs_ref, buf, zbuf, sem, zsem, *, tm, n_sub, n_tiles, n_slab_tiles):
    i = pl.program_id(0)
    slot = i % 2
    is_sample = i == n_tiles - 1

    def wait_tile(sl, tokens):
        pltpu.make_async_copy(rows8(buf.at[sl], 0, tokens), rows8(buf.at[sl], 0, tokens), sem.at[sl]).wait()

    @pl.when(i == 0)
    def _():
        zbuf[...] = jnp.zeros(zbuf.shape, F32)
        for c in range(N_CLASSES):
            @pl.when(plen_ref[c] > 0)
            def _():
                pltpu.make_async_copy(rows8(zbuf, 0, plen_ref[c]), rows8(xs_ref, poff_ref[c], plen_ref[c]),
                                      zsem).start()
        for j in range(n_slab_tiles - N_CLASSES, n_slab_tiles):
            @pl.when(j >= nact_ref[0])
            def _():
                pltpu.make_async_copy(zbuf, rows8(xs_ref, j * TM_E, TM_E), zsem).start()

    xp = []
    for u in range(n_sub):
        rows = slice(u * tm, (u + 1) * tm)
        x = hp_ref[rows, :]
        info = ip_ref[rows, :]
        if u == 0:
            x = jnp.where(is_sample, hs_ref[...], x)
            info = jnp.where(is_sample, is_ref[...], info)
        perm = _perm_t(info[:, INFO_CLS:INFO_CLS + 1]).T.astype(BF16)
        xp.append(jnp.dot(perm, x, preferred_element_type=F32))

    @pl.when(i >= 2)
    def _():
        wait_tile(slot, n_sub * tm)

    bs = buf.at[slot]
    for u in range(n_sub):
        for c in range(D_MODEL // LANES):
            bs[pl.ds(u * tm * SUBLANES + c, tm, stride=SUBLANES), :] = xp[u][:, c * LANES:(c + 1) * LANES]
    for u in range(n_sub):
        local = u * tm
        for c in range(N_CLASSES):
            k = (i * n_sub + u) * N_CLASSES + c
            n_rows = slen_ref[k]

            @pl.when(n_rows > 0)
            def _():
                pltpu.make_async_copy(rows8(bs, local, n_rows), rows8(xs_ref, soff_ref[k], n_rows),
                                      sem.at[slot]).start()
            local = local + n_rows

    @pl.when(i == n_tiles - 1)
    def _():
        wait_tile(slot, tm)
        if n_tiles >= 2:
            wait_tile(1 - slot, n_sub * tm)

        @pl.when(ptot_ref[0] > 0)
        def _():
            n = pl.multiple_of(ptot_ref[0] * SUBLANES, SUBLANES)
            pltpu.make_async_copy(xs_ref.at[pl.ds(0, n)], xs_ref.at[pl.ds(0, n)], zsem).wait()


def _dispatch(seg_off, seg_len, pad_off, pad_len, pad_tot, nact, h2_p, h2_s, info_p, info_s, *, tm, n_sub, n_slots):
    n_prompt_steps = h2_p.shape[0] // (n_sub * tm)
    n_tiles = n_prompt_steps + 1
    assert h2_s.shape[0] == tm and h2_p.shape[0] % (n_sub * tm) == 0
    assert seg_len.shape[0] == n_tiles * n_sub * N_CLASSES
    last_p = n_prompt_steps - 1
    grid_spec = pltpu.PrefetchScalarGridSpec(
        num_scalar_prefetch=6,
        grid=(n_tiles,),
        in_specs=[pl.BlockSpec((n_sub * tm, D_MODEL), lambda i, *_: (jnp.minimum(i, last_p), 0)),
                  pl.BlockSpec((tm, D_MODEL), lambda i, *_: (0, 0)),
                  pl.BlockSpec((n_sub * tm, LANES), lambda i, *_: (jnp.minimum(i, last_p), 0)),
                  pl.BlockSpec((tm, LANES), lambda i, *_: (0, 0))],
        out_specs=pl.BlockSpec(memory_space=pl.ANY),
        scratch_shapes=[pltpu.VMEM((2, n_sub * tm * SUBLANES, LANES), F32),
                        pltpu.VMEM((TM_E * SUBLANES, LANES), F32),
                        pltpu.SemaphoreType.DMA((2,)),
                        pltpu.SemaphoreType.DMA(())])
    return pl.pallas_call(
        functools.partial(_dispatch_body, tm=tm, n_sub=n_sub, n_tiles=n_tiles, n_slab_tiles=n_slots // TM_E),
        grid_spec=grid_spec,
        out_shape=jax.ShapeDtypeStruct((n_slots * SUBLANES, LANES), F32),
        compiler_params=_cparams(("arbitrary",)),
        name="moe_dispatch",
    )(seg_off, seg_len, pad_off, pad_len, pad_tot, nact, h2_p, h2_s, info_p, info_s)


def _moe_body(exp_ref, par_ref, first_ref, nxt_ref, nact_ref, x_ref, wg_ref, wu_ref, wd_ref,
              ya_ref, yb_ref, gbuf, ubuf, dbuf, sem, *, tm):
    i = pl.program_id(0)
    n_tiles = pl.num_programs(0)

    def weight_copies(slot, expert, buf):
        return [pltpu.make_async_copy(w_ref.at[expert], w_buf.at[slot, buf], sem.at[slot, buf])
                for w_ref, w_buf in ((wg_ref, gbuf), (wu_ref, ubuf), (wd_ref, dbuf))]

    @pl.when(i < nact_ref[0])
    def _():
        for slot in range(2):
            k = slot * n_tiles + i
            buf = par_ref[k]

            @pl.when(i == 0)
            def _():
                for c in weight_copies(slot, exp_ref[k], buf):
                    c.start()

            @pl.when(first_ref[k] == 1)
            def _():
                for c in weight_copies(slot, exp_ref[k], buf):
                    c.wait()

                @pl.when(nxt_ref[k] >= 0)
                def _():
                    for c in weight_copies(slot, nxt_ref[k], 1 - buf):
                        c.start()

        x = jnp.concatenate([x_ref[pl.ds(c, tm, stride=SUBLANES), :] for c in range(D_MODEL // LANES)],
                            axis=1).astype(BF16)

        wts = [w_buf[slot, par_ref[slot * n_tiles + i]].astype(BF16)
               for slot in range(2) for w_buf in (gbuf, ubuf, dbuf)]
        rows_p = tm // MOE_PARTS

        def hidden(xp, wg, wu):
            return (jnp.dot(xp, wg, preferred_element_type=F32), jnp.dot(xp, wu, preferred_element_type=F32))

        def down(gate_up, wd):
            gate, up = gate_up
            he = (gate * jax.nn.sigmoid(gate)) * up
            return jnp.dot(he.astype(BF16), wd, preferred_element_type=F32)

        hs = []
        for p in range(MOE_PARTS):
            xp = x[p * rows_p:(p + 1) * rows_p]
            hs.append((hidden(xp, wts[0], wts[1]), hidden(xp, wts[3], wts[4])))
        for p, (ha, hb) in enumerate(hs):
            for y_ref, h, wd in ((ya_ref, ha, wts[2]), (yb_ref, hb, wts[5])):
                y = down(h, wd)
                for c in range(D_MODEL // LANES):
                    y_ref[pl.ds(p * rows_p * SUBLANES + c, rows_p, stride=SUBLANES), :] = (
                        y[:, c * LANES:(c + 1) * LANES])

    @pl.when(i >= nact_ref[0])
    def _():
        ya_ref[...] = jnp.zeros(ya_ref.shape, ya_ref.dtype)
        yb_ref[...] = jnp.zeros(yb_ref.shape, yb_ref.dtype)


def _moe(tile_ea, tile_eb, nact, x_sorted, w_gate, w_up, w_down, *, tm, n_max):
    tiles = jnp.arange(n_max, dtype=jnp.int32)
    experts = jnp.stack([tile_ea, tile_eb])
    active = tiles[None, :] < nact
    first = active & ((tiles[None, :] == 0) | (experts != jnp.roll(experts, 1, axis=1)))
    parity = (jnp.cumsum(first.astype(jnp.int32), axis=1) - 1) % 2
    opener = jnp.where(first, tiles[None, :], n_max)
    next_open = lax.cummin(jnp.concatenate([opener[:, 1:], jnp.full((2, 1), n_max, jnp.int32)], axis=1),
                           axis=1, reverse=True)
    nxt = jnp.where(next_open < n_max, jnp.take_along_axis(experts, jnp.minimum(next_open, n_max - 1), axis=1), -1)
    flat = lambda a: a.astype(jnp.int32).reshape(-1)
    rows_in = lambda i, ex, pa, fi, nx, na: (jnp.minimum(i, na[0] - 1), 0)
    grid_spec = pltpu.PrefetchScalarGridSpec(
        num_scalar_prefetch=5,
        grid=(n_max,),
        in_specs=[pl.BlockSpec((tm * SUBLANES, LANES), rows_in),
                  pl.BlockSpec(memory_space=pl.ANY), pl.BlockSpec(memory_space=pl.ANY),
                  pl.BlockSpec(memory_space=pl.ANY)],
        out_specs=[pl.BlockSpec((tm * SUBLANES, LANES), lambda i, *_: (i, 0))] * 2,
        scratch_shapes=[pltpu.VMEM((2, 2, D_MODEL, D_EXPERT), F32),
                        pltpu.VMEM((2, 2, D_MODEL, D_EXPERT), F32),
                        pltpu.VMEM((2, 2, D_EXPERT, D_MODEL), F32),
                        pltpu.SemaphoreType.DMA((2, 2))])
    return pl.pallas_call(
        functools.partial(_moe_body, tm=tm),
        grid_spec=grid_spec,
        out_shape=[jax.ShapeDtypeStruct(x_sorted.shape, F32)] * 2,
        compiler_params=_cparams(("arbitrary",)),
        name="moe",
    )(flat(experts), flat(parity), flat(first), flat(nxt), nact, x_sorted, w_gate, w_up, w_down)


def _final_body(soff_ref, slen_ref, x1_ref, info_ref, mod_ref, gf_ref, ya_ref, yb_ref, y_ref, buf, sem,
                *, tm, n_sub, n_tiles, tile_base):
    i = pl.program_id(0)
    slot = i % 2
    slabs = (ya_ref, yb_ref)

    def fetch(step, sl):
        for u in range(n_sub):
            local = u * tm
            for c in range(N_CLASSES):
                k = (step * n_sub + u + tile_base) * N_CLASSES + c
                n_rows = slen_ref[k]

                @pl.when(n_rows > 0)
                def _():
                    for e in range(2):
                        pltpu.make_async_copy(rows8(slabs[e], soff_ref[k], n_rows),
                                              rows8(buf.at[sl, e], local, n_rows), sem.at[sl, e]).start()
                local = local + n_rows

    @pl.when(i == 0)
    def _():
        fetch(0, 0)

    @pl.when(i + 1 < n_tiles)
    def _():
        fetch(i + 1, 1 - slot)

    info = info_ref[...]
    perm_t = [_perm_t(info[u * tm:(u + 1) * tm, INFO_CLS:INFO_CLS + 1]).astype(BF16) for u in range(n_sub)]
    moe = None
    for e, lane_w in enumerate((INFO_W_A, INFO_W_B)):
        pltpu.make_async_copy(buf.at[slot, e], buf.at[slot, e], sem.at[slot, e]).wait()
        bs = buf.at[slot, e]
        parts = []
        for u in range(n_sub):
            ye = jnp.concatenate([bs[pl.ds(u * tm * SUBLANES + c, tm, stride=SUBLANES), :]
                                  for c in range(D_MODEL // LANES)], axis=1)
            parts.append(jnp.dot(perm_t[u], ye.astype(BF16), preferred_element_type=F32))
        term = info[:, lane_w:lane_w + 1] * jnp.concatenate(parts, axis=0)
        moe = term if moe is None else moe + term
    x1 = x1_ref[...]
    g2 = mod_ref[0][:, 5 * D_MODEL:6 * D_MODEL]
    x2 = x1 + g2 * moe
    ms = jnp.mean(x2 * x2, axis=-1, keepdims=True)
    y_ref[...] = (x2 * lax.rsqrt(ms + EPS)) * gf_ref[...]


def _final(seg_off, seg_len, x1, info, mod, g_final, ya_sorted, yb_sorted, *, tm, n_sub, sample, tile_base):
    n = x1.shape[0]
    rows = n_sub * tm
    nt = n // rows
    mrows = mod.shape[1]
    if sample:
        mod_map = lambda i, *_: (0, 0, 0)
    else:
        tiles_per_batch = 4096 // rows
        mod_map = lambda i, *_: (i // tiles_per_batch, 0, 0)
    grid_spec = pltpu.PrefetchScalarGridSpec(
        num_scalar_prefetch=2,
        grid=(nt,),
        in_specs=[pl.BlockSpec((rows, D_MODEL), lambda i, *_: (i, 0)),
                  pl.BlockSpec((rows, LANES), lambda i, *_: (i, 0)),
                  pl.BlockSpec((1, mrows, 6 * D_MODEL), mod_map),
                  pl.BlockSpec((1, D_MODEL), lambda i, *_: (0, 0)),
                  pl.BlockSpec(memory_space=pl.ANY),
                  pl.BlockSpec(memory_space=pl.ANY)],
        out_specs=pl.BlockSpec((rows, D_MODEL), lambda i, *_: (i, 0)),
        scratch_shapes=[pltpu.VMEM((2, 2, rows * SUBLANES, LANES), F32),
                        pltpu.SemaphoreType.DMA((2, 2))])
    return pl.pallas_call(
        functools.partial(_final_body, tm=tm, n_sub=n_sub, n_tiles=nt, tile_base=tile_base),
        grid_spec=grid_spec,
        out_shape=jax.ShapeDtypeStruct((n, D_MODEL), F32),
        compiler_params=_cparams(("arbitrary",)),
        name="final_sample" if sample else "final_prompt",
    )(seg_off, seg_len, x1, info, mod, g_final.reshape(1, D_MODEL), ya_sorted, yb_sorted)


def _bucket_table(n):
    d = np.arange(n)
    max_exact = N_BUCKETS // 2
    nf = np.maximum(d, 1).astype(np.float64)
    large = max_exact + (np.log(nf / max_exact) / math.log(MAX_DISTANCE / max_exact)
                         * (N_BUCKETS - max_exact)).astype(np.int64)
    large = np.minimum(large, N_BUCKETS - 1)
    return np.where(d < max_exact, d, large).astype(np.int32)


def _toeplitz(v, n_rows, n_cols):
    length = n_rows + n_cols - 1
    lead = v.shape[:-1]
    vp = jnp.concatenate([v, jnp.zeros(lead + (1,), v.dtype)], axis=-1)
    skew = jnp.tile(vp, (1,) * len(lead) + (n_rows,))[..., :n_rows * length].reshape(lead + (n_rows, length))
    return skew[..., n_rows - 1:n_rows - 1 + n_cols]


def kernel(x_prompt, x_sample, c_prompt, c_sample, cache_k, cache_v, page_table, w_ada, b_ada, w_in, w_o,
           lam_q1, lam_k1, lam_q2, lam_k2, g_subln, rel_bias, g_sg_ln, b_sg_ln, w_s, b_s, w_rg, b_rg,
           w_re, b_re, w_gate, w_up, w_down, g_final):
    batch, seq, _ = x_prompt.shape
    dec_b, dec_t, _ = x_sample.shape
    n_pages = page_table.shape[1]
    n_p = batch * seq
    n_s = dec_b * dec_t
    n_tot = n_p + n_s
    assert w_in.shape[0] == 1 and cache_k.shape[1] == 1 and seq % TQ_ATT == 0 and n_pages % PAGES_PER_GROUP == 0
    assert n_p % TM_TOK == 0 and n_p % n_s == 0 and n_tot % TM_E == 0 and dec_t == SUBLANES
    assert TAIL_TOKENS >= MAX_DISTANCE + dec_t and TAIL_TOKENS <= PAGES_PER_GROUP * PAGE
    assert n_s == TM_D and n_p % TM_D == 0 and N_CLASSES <= LANES

    w_in_b = w_in[0].astype(BF16)
    w_o_b = w_o[0].astype(BF16)
    wr_t = jnp.zeros((32, D_MODEL), F32).at[0:N_EG].set(w_rg[0].T).at[N_EG:N_EG + N_EG * EPG].set(w_re[0].T)
    wr_t = wr_t.astype(BF16)
    br = jnp.zeros((32, 1), F32).at[0:N_EG, 0].set(b_rg[0]).at[N_EG:N_EG + N_EG * EPG, 0].set(b_re[0])
    wg_b, wu_b, wd_b = w_gate[0], w_up[0], w_down[0]
    ws_tril = jnp.tril(w_s[0])
    ws_p = ws_tril.astype(BF16)
    bs_p = b_s[0][:, :, None]
    same_seq = np.kron(np.eye(dec_b, dtype=np.float32), np.ones((dec_t, dec_t), np.float32))
    rep = np.tile(np.eye(dec_t, dtype=np.float32), (dec_b, 1))
    ws_rep = jnp.einsum('ri,gij,cj->grc', rep, ws_tril[:, :dec_t, :dec_t], rep,
                        precision=lax.Precision.HIGHEST)
    ws_s = (ws_rep * same_seq).astype(BF16)
    bs_s = jnp.tile(b_s[0][:, :dec_t], (1, dec_b))[:, :, None]
    gln = g_sg_ln[0]
    bln = b_sg_ln[0]
    lam = (jnp.exp(jnp.sum(lam_q1[0] * lam_k1[0])) - jnp.exp(jnp.sum(lam_q2[0] * lam_k2[0]))
           + LAM_INIT).reshape(1).astype(F32)

    blk = BIAS_BLOCK
    n_dist = max(2 * blk, TAIL_TOKENS + dec_t)
    onehot = np.eye(N_BUCKETS, dtype=np.float32)[_bucket_table(n_dist)]
    ft = jnp.dot(onehot, rel_bias - rel_bias[N_BUCKETS - 1], precision=lax.Precision.HIGHEST).T
    neg = lambda n: jnp.full((N_HEADS, n), NEG, F32)
    bias_near = _toeplitz(jnp.concatenate([neg(blk - 1), ft[:, 0:blk]], axis=1), blk, blk)
    bias_next = _toeplitz(ft[:, 1:2 * blk], blk, blk)
    bias_t = jnp.stack([bias_next, bias_near], axis=1) * LOG2E
    bl = _toeplitz(jnp.flip(ft[:, 1:TAIL_TOKENS + dec_t], axis=1), dec_t, TAIL_TOKENS)
    head_eq = jnp.eye(N_HEADS, dtype=F32)
    bias_last = (bl[:, None, :, :, None] * head_eq[:, None, None, None, :])
    bias_last = jnp.broadcast_to(bias_last, (N_HEADS, 2, dec_t, TAIL_TOKENS, N_HEADS)).reshape(
        2 * N_HEADS * dec_t, TAIL_TOKENS * N_HEADS)
    bn = _toeplitz(jnp.concatenate([jnp.flip(ft[:, 0:dec_t], axis=1), neg(dec_t - 1)], axis=1),
                   dec_t, dec_t)
    bn = jnp.where(head_eq[:, None, None, :] > 0, bn[:, :, :, None], NEG)
    bn = jnp.broadcast_to(bn[:, None], (N_HEADS, 2, dec_t, dec_t, N_HEADS)).reshape(
        2 * N_HEADS * dec_t, dec_t * N_HEADS)
    bias_new = jnp.concatenate([bn, jnp.full((bn.shape[0], PAGE - bn.shape[1]), NEG, F32)], axis=1)

    c_all = jnp.concatenate([c_prompt, c_sample, jnp.zeros((4, D_MODEL), F32)], axis=0)
    mod_all = _ada(c_all, w_ada[0], b_ada[0])
    mod_p = mod_all[:batch].reshape(batch, 1, 6 * D_MODEL)
    mod_s = jnp.repeat(mod_all[batch:batch + dec_b], dec_t, axis=0).reshape(1, n_s, 6 * D_MODEL)

    xp = x_prompt.reshape(n_p, D_MODEL)
    xs = x_sample.reshape(n_s, D_MODEL)

    q_p, kf_p, kb_p, vf_p, vt_p, sg_p = _stage_a(xp, mod_p, w_in_b, ws_p, bs_p, gln, bln,
                                                 tm=TM_TOK, chunk=CHUNK, sample=False)
    q_s, kf_s, vf_s, sg_s, vsn_s = _stage_a(xs, mod_s, w_in_b, ws_s, bs_s, gln, bln,
                                            tm=n_s, chunk=n_s, sample=True)

    g_col = g_subln[0].reshape(HEAD_W, 1)
    g_row = g_subln[0].reshape(1, HEAD_W)
    pad = ((0, 0), (0, PAGE - dec_t * N_HEADS), (0, 0))
    knew = jnp.pad(kf_s.reshape(dec_b, dec_t * N_HEADS, HEAD_W), pad).astype(BF16)
    vnew = jnp.pad(vf_s.reshape(dec_b, dec_t * N_HEADS, HEAD_W), pad).astype(BF16)
    ck = cache_k.reshape(-1, HEAD_W)
    cv = cache_v.reshape(-1, HEAD_W)
    o_p, o_s = _attn_fused(page_table.reshape(-1), lam, q_p, kb_p, vt_p, bias_t, g_col,
                           q_s, knew, vnew, bias_last, bias_new, g_row, ck, cv,
                           batch=batch, seq=seq, tq=TQ_ATT, tk=TM_TOK, dec_b=dec_b, n_pages=n_pages)

    x1_p, h2_p, cls_p, info_p = _stage_c(o_p, sg_p, xp, mod_p, w_o_b, wr_t, br, tm=TM_C, sample=False)
    x1_s, h2_s, cls_s, info_s = _stage_c(o_s, sg_s, xs, mod_s, w_o_b, wr_t, br, tm=n_s, sample=True)

    tm_e = TM_E
    tm_d = TM_D
    n_max = n_tot // tm_e + N_CLASSES
    n_dt = n_tot // tm_d
    cls = jnp.concatenate([cls_p[0], cls_s[0]]).reshape(n_dt, tm_d)
    classes = jnp.arange(N_CLASSES, dtype=jnp.int32)
    seg_len = jnp.sum((cls[:, :, None] == classes).astype(jnp.int32), axis=1)
    counts = jnp.sum(seg_len, axis=0)
    ntile_c = (counts + tm_e - 1) // tm_e
    tile_end = jnp.cumsum(ntile_c)
    class_base = (tile_end - ntile_c) * tm_e
    nact = tile_end[-1]
    seg_off = class_base[None, :] + jnp.cumsum(seg_len, axis=0) - seg_len
    pad_off = class_base + counts
    pad_len = ntile_c * tm_e - counts
    tile_ids = jnp.arange(n_max, dtype=jnp.int32)
    tile_cls = jnp.sum((tile_ids[:, None] >= tile_end[None, :]).astype(jnp.int32), axis=1)
    last_cls = jnp.sum((nact - 1 >= tile_end).astype(jnp.int32))
    tile_cls = jnp.where(tile_ids < nact, tile_cls, last_cls)
    grp = tile_cls // N_PAIRS
    pidx = tile_cls % N_PAIRS
    pick = lambda table: sum(jnp.where(pidx == p, e, 0) for p, e in enumerate(table))
    tile_ea = (grp * EPG + pick(PAIR_A)).astype(jnp.int32)
    tile_eb = (grp * EPG + pick(PAIR_B)).astype(jnp.int32)
    empty = jnp.zeros((N_SUB - 1, N_CLASSES), jnp.int32)
    seg_off = jnp.concatenate([seg_off.astype(jnp.int32), empty]).reshape(-1)
    seg_len = jnp.concatenate([seg_len, empty]).reshape(-1)

    nact1 = nact.reshape(1).astype(jnp.int32)
    zero_rows = (jnp.sum(pad_len) + (n_max - nact) * tm_e).reshape(1).astype(jnp.int32)
    x_sorted = _dispatch(seg_off, seg_len, pad_off.astype(jnp.int32), pad_len.astype(jnp.int32), zero_rows,
                         nact1, h2_p, h2_s, info_p, info_s, tm=tm_d, n_sub=N_SUB, n_slots=n_max * tm_e)
    ya_sorted, yb_sorted = _moe(tile_ea, tile_eb, nact1, x_sorted, wg_b, wu_b, wd_b, tm=tm_e, n_max=n_max)

    y_p = _final(seg_off, seg_len, x1_p, info_p, mod_p, g_final, ya_sorted, yb_sorted,
                 tm=tm_d, n_sub=N_SUB, sample=False, tile_base=0)
    y_s = _final(seg_off, seg_len, x1_s, info_s, mod_s, g_final, ya_sorted, yb_sorted,
                 tm=tm_d, n_sub=1, sample=True, tile_base=n_p // tm_d)

    return (y_p.reshape(batch, seq, D_MODEL),
            y_s.reshape(dec_b, dec_t, D_MODEL),
            kf_p.reshape(batch, 1, seq, N_HEADS, HEAD_W),
            vf_p.reshape(batch, 1, seq, N_HEADS, HEAD_W),
            kf_s.reshape(dec_b, 1, dec_t, N_HEADS, HEAD_W),
            vf_s.reshape(dec_b, 1, dec_t, N_HEADS, HEAD_W),
            vsn_s.reshape(dec_b, 1, dec_t, N_GROUPS_SG, SG_CH))
```
